```python
import jax, jax.numpy as jnp
from jax import lax
import numpy as np

D_MODEL = 1024
BATCH = 2
SEQ = 16384
DEPTH = 2

MIX_WIDTH = D_MODEL
POOL_WIDTH = D_MODEL // 4
POOL_WINDOWS = (2, 4, 8, 16)
POOL_GROUP = POOL_WIDTH // len(POOL_WINDOWS)
LRU_WIDTH = D_MODEL // 4
LRU_HEADS = 4
LRU_HEAD_DIM = LRU_WIDTH // LRU_HEADS
LRU_C = 8.0
CONV_WIDTH = 4
HEAD_DIM = 64
ATTN_WIDTH = MIX_WIDTH - POOL_WIDTH - LRU_WIDTH
N_HEADS = ATTN_WIDTH // HEAD_DIM
N_KV_HEADS = 2
GQA_GROUP = N_HEADS // N_KV_HEADS
KV_WIDTH = N_KV_HEADS * HEAD_DIM
N_BRANCH = 3
CMP_LEN = 32
CMP_STRIDE = 16
SEL_BLOCK = 64
SEL_TOPK = 16
WINDOW = 512
Q_BLOCK = 128
ROPE_THETA = 10000.0
D_FF = 2816
NORM_EPS = 1e-6
NEG_INF = -1e30
BIG_SCORE = 1e9
IN_SIZES = (POOL_WIDTH, LRU_WIDTH, LRU_WIDTH, ATTN_WIDTH,
            KV_WIDTH, KV_WIDTH, KV_WIDTH, KV_WIDTH, KV_WIDTH, KV_WIDTH,
            N_BRANCH * N_HEADS)
IN_COLS = sum(IN_SIZES)

kernel_name = "hymba_pool_rglru_nsa_macaron"


def rms_norm(x, g):
    x32 = x.astype(jnp.float32)
    y = x32 * lax.rsqrt(jnp.mean(x32 * x32, axis=-1, keepdims=True) + NORM_EPS)
    return (y * g.astype(jnp.float32)).astype(x.dtype)


def swiglu(x, w_gate, w_up, w_down):
    return (jax.nn.silu(x @ w_gate) * (x @ w_up)) @ w_down


def rope_tables(positions):
    inv = ROPE_THETA ** (-jnp.arange(0, HEAD_DIM, 2, dtype=jnp.float32) / HEAD_DIM)
    ang = positions.astype(jnp.float32)[..., None] * inv
    return jnp.cos(ang), jnp.sin(ang)


def apply_rope(x, cos, sin):
    x32 = x.astype(jnp.float32)
    x1, x2 = jnp.split(x32, 2, axis=-1)
    c = cos[:, :, None, :]
    s = sin[:, :, None, :]
    return jnp.concatenate([x1 * c - x2 * s, x2 * c + x1 * s], axis=-1).astype(x.dtype)


def pool_mixer(xp, pool_w, pool_scale):
    B, S, _ = xp.shape
    xg = xp.astype(jnp.float32).reshape(B, S, len(POOL_WINDOWS), POOL_GROUP)
    c = jnp.concatenate([jnp.zeros((B, 1) + xg.shape[2:], jnp.float32),
                         jnp.cumsum(xg, axis=1)], axis=1)
    t = jnp.arange(S)
    means = []
    for g, w in enumerate(POOL_WINDOWS):
        cg = c[:, :, g]
        lo = jnp.concatenate([jnp.zeros((B, w - 1, POOL_GROUP), jnp.float32),
                              cg[:, :S + 1 - w]], axis=1)
        cnt = jnp.minimum(t + 1, w).astype(jnp.float32)
        means.append((cg[:, 1:] - lo) / cnt[None, :, None])
    pooled = jnp.stack(means, axis=2)
    y = jnp.einsum('bsgi,gij->bsgj', (pooled - xg).astype(xp.dtype), pool_w)
    return y.reshape(B, S, POOL_WIDTH) * pool_scale


def rglru_mixer(xl, gate, conv_w, conv_b, w_r, b_r, w_i, b_i, lam):
    B, S, C = xl.shape
    xc = lax.conv_general_dilated(xl, conv_w[:, None, :], window_strides=(1,),
                                  padding=[(CONV_WIDTH - 1, 0)],
                                  dimension_numbers=('NWC', 'WIO', 'NWC'),
                                  feature_group_count=C) + conv_b
    xh = xc.reshape(B, S, LRU_HEADS, LRU_HEAD_DIM)
    r = jax.nn.sigmoid(jnp.einsum('bshi,hij->bshj', xh, w_r).reshape(B, S, C) + b_r)
    i = jax.nn.sigmoid(jnp.einsum('bshi,hij->bshj', xh, w_i).reshape(B, S, C) + b_i)
    log_a = -LRU_C * r.astype(jnp.float32) * jax.nn.softplus(-lam.astype(jnp.float32))
    a = jnp.exp(log_a)
    mult = jnp.sqrt(-jnp.expm1(2.0 * log_a))
    b = mult * (i * xc).astype(jnp.float32)

    def combine(left, right):
        a1, b1 = left
        a2, b2 = right
        return a1 * a2, a2 * b1 + b2

    _, h = lax.associative_scan(combine, (a, b), axis=1)
    return h.astype(xl.dtype) * jax.nn.gelu(gate)


def masked_softmax(s, mask, scale):
    s = jnp.where(mask, s.astype(jnp.float32) * scale, NEG_INF)
    p = jax.nn.softmax(s, axis=-1)
    return jnp.where(mask, p, 0.0)


def nsa_mixer(q, kc, vc, ks, vs, kw, vw, g, cos, sin, cmp_w_k, cmp_w_v, cmp_pe):
    B, S, _ = q.shape
    dt = q.dtype
    kv_heads = lambda t: t.reshape(B, S, N_KV_HEADS, HEAD_DIM)
    q = apply_rope(q.reshape(B, S, N_HEADS, HEAD_DIM), cos, sin)
    kc = apply_rope(kv_heads(kc), cos, sin)
    ks = apply_rope(kv_heads(ks), cos, sin)
    kw = apply_rope(kv_heads(kw), cos, sin)
    vc, vs, vw = kv_heads(vc), kv_heads(vs), kv_heads(vw)
    gates = jax.nn.sigmoid(g.astype(jnp.float32)).astype(dt).reshape(
        B, S, N_KV_HEADS, GQA_GROUP, N_BRANCH)
    qg = q.reshape(B, S, N_KV_HEADS, GQA_GROUP, HEAD_DIM)
    scale = HEAD_DIM ** -0.5

    n_cmp = (S - CMP_LEN) // CMP_STRIDE + 1
    cidx = jnp.arange(n_cmp)[:, None] * CMP_STRIDE + jnp.arange(CMP_LEN)[None, :]
    pe = cmp_pe[None, None, :, None, :]
    k_cmp = jnp.einsum('bnlhd,lde->bnhe', kc[:, cidx] + pe, cmp_w_k)
    v_cmp = jnp.einsum('bnlhd,lde->bnhe', vc[:, cidx] + pe, cmp_w_v)
    cmp_start = jnp.arange(n_cmp) * CMP_STRIDE
    cmp_end = cmp_start + CMP_LEN - 1

    n_sel = S // SEL_BLOCK
    n_top = min(SEL_TOPK, n_sel)
    sel_start = jnp.arange(n_sel) * SEL_BLOCK
    overlap = ((cmp_start[:, None] < sel_start[None, :] + SEL_BLOCK) &
               (cmp_start[:, None] + CMP_LEN > sel_start[None, :])).astype(jnp.float32)
    ks_blk = ks.reshape(B, n_sel, SEL_BLOCK, N_KV_HEADS, HEAD_DIM).transpose(0, 3, 1, 2, 4)
    vs_blk = vs.reshape(B, n_sel, SEL_BLOCK, N_KV_HEADS, HEAD_DIM).transpose(0, 3, 1, 2, 4)
    gather = jax.vmap(jax.vmap(lambda kb, ix: kb[ix]))

    pad = jnp.zeros((B, WINDOW, N_KV_HEADS, HEAD_DIM), dt)
    kw_pad = jnp.concatenate([pad, kw], axis=1)
    vw_pad = jnp.concatenate([pad, vw], axis=1)
    j_sel = jnp.arange(n_sel)

    def query_block(qb):
        start = qb * Q_BLOCK
        tq = start + jnp.arange(Q_BLOCK)
        qblk = lax.dynamic_slice_in_dim(qg, start, Q_BLOCK, axis=1)
        gblk = lax.dynamic_slice_in_dim(gates, start, Q_BLOCK, axis=1)

        s_c = jnp.einsum('bqhgd,bnhd->bhgqn', qblk, k_cmp)
        p_c = masked_softmax(s_c, cmp_end[None, :] <= tq[:, None], scale)
        o_c = jnp.einsum('bhgqn,bnhd->bqhgd', p_c.astype(dt), v_cmp)

        imp = jnp.einsum('bhqn,nj->bhqj', p_c.sum(axis=2), overlap)
        cur = tq // SEL_BLOCK
        valid = j_sel[None, :] <= cur[:, None]
        forced = ((j_sel[None, :] == 0) | (j_sel[None, :] == cur[:, None]) |
                  (j_sel[None, :] == cur[:, None] - 1))
        score = jnp.where(valid, jnp.where(forced, BIG_SCORE, imp), -BIG_SCORE)
        _, idx = lax.top_k(score, n_top)
        k_sel = gather(ks_blk, idx).reshape(B, N_KV_HEADS, Q_BLOCK, n_top * SEL_BLOCK, HEAD_DIM)
        v_sel = gather(vs_blk, idx).reshape(B, N_KV_HEADS, Q_BLOCK, n_top * SEL_BLOCK, HEAD_DIM)
        kpos = (idx[..., None] * SEL_BLOCK + jnp.arange(SEL_BLOCK)).reshape(
            B, N_KV_HEADS, Q_BLOCK, n_top * SEL_BLOCK)
        mask_s = (kpos <= tq[None, None, :, None])[:, :, None]
        s_s = jnp.einsum('bqhgd,bhqkd->bhgqk', qblk, k_sel)
        p_s = masked_softmax(s_s, mask_s, scale)
        o_s = jnp.einsum('bhgqk,bhqkd->bqhgd', p_s.astype(dt), v_sel)

        kwb = lax.dynamic_slice_in_dim(kw_pad, start, Q_BLOCK + WINDOW, axis=1)
        vwb = lax.dynamic_slice_in_dim(vw_pad, start, Q_BLOCK + WINDOW, axis=1)
        kp = start - WINDOW + jnp.arange(Q_BLOCK + WINDOW)
        diff = tq[:, None] - kp[None, :]
        mask_w = (diff >= 0) & (diff < WINDOW) & (kp[None, :] >= 0)
        s_w = jnp.einsum('bqhgd,bkhd->bhgqk', qblk, kwb)
        p_w = masked_softmax(s_w, mask_w, scale)
        o_w = jnp.einsum('bhgqk,bkhd->bqhgd', p_w.astype(dt), vwb)

        o = gblk[..., 0:1] * o_c + gblk[..., 1:2] * o_s + gblk[..., 2:3] * o_w
        return o.reshape(B, Q_BLOCK, ATTN_WIDTH)

    out = lax.map(query_block, jnp.arange(S // Q_BLOCK))
    return out.transpose(1, 0, 2, 3).reshape(B, S, ATTN_WIDTH)


def token_mixer(h, cos, sin, w_in, w_out, pool_w, pool_scale, conv_w, conv_b,
                lru_w_r, lru_b_r, lru_w_i, lru_b_i, lru_lambda, cmp_w_k, cmp_w_v, cmp_pe):
    proj = h @ w_in
    bounds = []
    acc = 0
    for size in IN_SIZES[:-1]:
        acc += size
        bounds.append(acc)
    xp, xl, gl, q, kc, vc, ks, vs, kw, vw, g = jnp.split(proj, bounds, axis=-1)
    y_pool = pool_mixer(xp, pool_w, pool_scale)
    y_lru = rglru_mixer(xl, gl, conv_w, conv_b, lru_w_r, lru_b_r, lru_w_i, lru_b_i, lru_lambda)
    y_attn = nsa_mixer(q, kc, vc, ks, vs, kw, vw, g, cos, sin, cmp_w_k, cmp_w_v, cmp_pe)
    return jnp.concatenate([y_pool, y_lru, y_attn], axis=-1) @ w_out


def setup_inputs(seed: int = 0) -> dict:
    key = jax.random.key(seed)
    ks = jax.random.split(key, 32)
    f32 = jnp.float32
    nrm = lambda k, shape, s: jax.random.normal(k, shape, f32) * s
    gain = lambda k: 1.0 + nrm(k, (DEPTH, D_MODEL), 0.05)
    x = jax.random.normal(ks[0], (BATCH, SEQ, D_MODEL), f32)
    offset = jax.random.randint(ks[1], (BATCH, 1), 0, 1024, dtype=jnp.int32)
    positions = (offset + jnp.arange(SEQ, dtype=jnp.int32)[None, :]).astype(jnp.int32)
    u = jax.random.uniform(ks[2], (DEPTH, LRU_WIDTH), f32, minval=0.9, maxval=0.999)
    return {
        "x": x,
        "positions": positions,
        "ffn1_pre_g": gain(ks[3]),
        "ffn1_post_g": gain(ks[4]),
        "ffn1_w_gate": nrm(ks[5], (DEPTH, D_MODEL, D_FF), D_MODEL ** -0.5),
        "ffn1_w_up": nrm(ks[6], (DEPTH, D_MODEL, D_FF), D_MODEL ** -0.5),
        "ffn1_w_down": nrm(ks[7], (DEPTH, D_FF, D_MODEL), D_FF ** -0.5),
        "mix_pre_g": gain(ks[8]),
        "mix_post_g": gain(ks[9]),
        "w_in": nrm(ks[10], (DEPTH, D_MODEL, IN_COLS), D_MODEL ** -0.5),
        "w_out": nrm(ks[11], (DEPTH, MIX_WIDTH, D_MODEL), MIX_WIDTH ** -0.5),
        "pool_w": nrm(ks[12], (DEPTH, len(POOL_WINDOWS), POOL_GROUP, POOL_GROUP), POOL_GROUP ** -0.5),
        "pool_scale": 1.0 + nrm(ks[13], (DEPTH, POOL_WIDTH), 0.1),
        "conv_w": nrm(ks[14], (DEPTH, CONV_WIDTH, LRU_WIDTH), CONV_WIDTH ** -0.5),
        "conv_b": nrm(ks[15], (DEPTH, LRU_WIDTH), 0.01),
        "lru_w_r": nrm(ks[16], (DEPTH, LRU_HEADS, LRU_HEAD_DIM, LRU_HEAD_DIM), LRU_HEAD_DIM ** -0.5),
        "lru_b_r": nrm(ks[17], (DEPTH, LRU_WIDTH), 0.01),
        "lru_w_i": nrm(ks[18], (DEPTH, LRU_HEADS, LRU_HEAD_DIM, LRU_HEAD_DIM), LRU_HEAD_DIM ** -0.5),
        "lru_b_i": nrm(ks[19], (DEPTH, LRU_WIDTH), 0.01),
        "lru_lambda": jnp.log(u) - jnp.log1p(-u),
        "cmp_w_k": nrm(ks[20], (DEPTH, CMP_LEN, HEAD_DIM, HEAD_DIM), (CMP_LEN * HEAD_DIM) ** -0.5),
        "cmp_w_v": nrm(ks[21], (DEPTH, CMP_LEN, HEAD_DIM, HEAD_DIM), (CMP_LEN * HEAD_DIM) ** -0.5),
        "cmp_pe": nrm(ks[22], (DEPTH, CMP_LEN, HEAD_DIM), 0.02),
        "ffn2_pre_g": gain(ks[23]),
        "ffn2_post_g": gain(ks[24]),
        "ffn2_w_gate": nrm(ks[25], (DEPTH, D_MODEL, D_FF), D_MODEL ** -0.5),
        "ffn2_w_up": nrm(ks[26], (DEPTH, D_MODEL, D_FF), D_MODEL ** -0.5),
        "ffn2_w_down": nrm(ks[27], (DEPTH, D_FF, D_MODEL), D_FF ** -0.5),
    }


def reference(x, positions, ffn1_pre_g, ffn1_post_g, ffn1_w_gate, ffn1_w_up, ffn1_w_down,
              mix_pre_g, mix_post_g, w_in, w_out, pool_w, pool_scale, conv_w, conv_b,
              lru_w_r, lru_b_r, lru_w_i, lru_b_i, lru_lambda, cmp_w_k, cmp_w_v, cmp_pe,
              ffn2_pre_g, ffn2_post_g, ffn2_w_gate, ffn2_w_up, ffn2_w_down):
    cos, sin = rope_tables(positions)
    h = x
    for l in range(DEPTH):
        f1 = swiglu(rms_norm(h, ffn1_pre_g[l]), ffn1_w_gate[l], ffn1_w_up[l], ffn1_w_down[l])
        h = h + 0.5 * rms_norm(f1, ffn1_post_g[l])
        m = token_mixer(rms_norm(h, mix_pre_g[l]), cos, sin, w_in[l], w_out[l],
                        pool_w[l], pool_scale[l], conv_w[l], conv_b[l],
                        lru_w_r[l], lru_b_r[l], lru_w_i[l], lru_b_i[l], lru_lambda[l],
                        cmp_w_k[l], cmp_w_v[l], cmp_pe[l])
        h = h + rms_norm(m, mix_post_g[l])
        f2 = swiglu(rms_norm(h, ffn2_pre_g[l]), ffn2_w_gate[l], ffn2_w_up[l], ffn2_w_down[l])
        h = h + 0.5 * rms_norm(f2, ffn2_post_g[l])
    return h
```

```python
import functools

import jax
import jax.numpy as jnp
from jax import lax
from jax.experimental import pallas as pl
from jax.experimental.pallas import tpu as pltpu

F32 = jnp.float32
MXU_DTYPE = jnp.bfloat16

POOL_WINDOWS = (2, 4, 8, 16)
POOL_GROUP = 64
LRU_HEAD_DIM = 64
LRU_C = 8.0
CONV_WIDTH = 4
HEAD_DIM = 64
N_KV_HEADS = 2
GQA_GROUP = 4
N_HEADS = N_KV_HEADS * GQA_GROUP
N_BRANCH = 3
CMP_LEN = 32
CMP_STRIDE = 16
SEL_BLOCK = 64
SEL_TOPK = 16
WINDOW = 512
ROPE_THETA = 10000.0
NORM_EPS = 1e-6
NEG_INF = -1e30
BIG_SCORE = 1e9

V7X_LANES = 128
V7X_VMEM_LIMIT_BYTES = 56 * 1024 * 1024

ROW_TILE = 512
SEQ_TILE = 512
Q_TILE = 128
K_TILE = 256
SEL_GROUP = 128
V_ROWS = 80
M_INIT = -1e20


def _params(semantics):
    return pltpu.CompilerParams(dimension_semantics=semantics, vmem_limit_bytes=V7X_VMEM_LIMIT_BYTES)


def _rms(x, g):
    return x * lax.rsqrt(jnp.mean(x * x, axis=-1, keepdims=True) + NORM_EPS) * g


def _sigmoid(x):
    return 1.0 / (1.0 + jnp.exp(-x))


def _mm(a, b):
    return jnp.dot(a, b, preferred_element_type=F32)


def _const_spec(shape):
    zeros = (0,) * len(shape)
    return pl.BlockSpec(shape, lambda *_: zeros, pipeline_mode=pl.Buffered(1))


def _rope_body(pos_ref, inv_ref, cos_ref, sin_ref):
    ang = pos_ref[0].astype(F32) * inv_ref[...]
    lane = lax.broadcasted_iota(jnp.int32, ang.shape, 1)
    cos_ref[0] = jnp.cos(ang)
    sin_ref[0] = jnp.where((lane & (HEAD_DIM - 1)) < HEAD_DIM // 2, -jnp.sin(ang), jnp.sin(ang))


def _rope_tables(positions):
    b, s = positions.shape
    inv = ROPE_THETA ** (-jnp.arange(0, HEAD_DIM, 2, dtype=F32) / HEAD_DIM)
    inv_row = jnp.tile(inv, V7X_LANES // (HEAD_DIM // 2))[None, :]
    ts = min(SEQ_TILE, s)
    out = jax.ShapeDtypeStruct((b, s, V7X_LANES), F32)
    return pl.pallas_call(
        _rope_body,
        grid=(b, s // ts),
        in_specs=[pl.BlockSpec((1, ts, 1), lambda i, j: (i, j, 0)),
                  pl.BlockSpec((1, V7X_LANES), lambda i, j: (0, 0))],
        out_specs=[pl.BlockSpec((1, ts, V7X_LANES), lambda i, j: (i, j, 0))] * 2,
        out_shape=[out, out],
        compiler_params=_params(("parallel", "parallel")),
        name="rope_tables",
    )(positions[:, :, None], inv_row)


def _ffn_body(h_ref, gpre_ref, gpost_ref, wg_ref, wu_ref, wd_ref, o_ref):
    h = h_ref[...]
    xn = _rms(h, gpre_ref[...]).astype(MXU_DTYPE)
    gate = _mm(xn, wg_ref[...])
    up = _mm(xn, wu_ref[...])
    act = (gate * _sigmoid(gate) * up).astype(MXU_DTYPE)
    f = _mm(act, wd_ref[...])
    o_ref[...] = h + 0.5 * _rms(f, gpost_ref[...])


def _ffn(h2, g_pre, g_post, w_gate, w_up, w_down):
    t, d = h2.shape
    dff = w_gate.shape[1]
    tm = min(ROW_TILE, t)
    return pl.pallas_call(
        _ffn_body,
        grid=(t // tm,),
        in_specs=[pl.BlockSpec((tm, d), lambda i: (i, 0)),
                  _const_spec((1, d)), _const_spec((1, d)),
                  _const_spec((d, dff)), _const_spec((d, dff)), _const_spec((dff, d))],
        out_specs=pl.BlockSpec((tm, d), lambda i: (i, 0)),
        out_shape=jax.ShapeDtypeStruct((t, d), F32),
        compiler_params=_params(("parallel",)),
        name="ffn",
    )(h2, g_pre[None, :], g_post[None, :], w_gate, w_up, w_down)


def _swap_halves(x):
    n = x.shape[1]
    lane = lax.broadcasted_iota(jnp.int32, x.shape, 1)
    first_half = (lane & (HEAD_DIM - 1)) < HEAD_DIM // 2
    return jnp.where(first_half, pltpu.roll(x, n - HEAD_DIM // 2, 1), pltpu.roll(x, HEAD_DIM // 2, 1))


def _inproj_body(cols, h_ref, g_ref, w_ref, cos_ref, sin_ref,
                 xpl_ref, q_ref, kc_ref, vc_ref, ks_ref, vs_ref, kw_ref, vw_ref, gate_ref):
    xn = _rms(h_ref[...], g_ref[...]).astype(MXU_DTYPE)
    proj = _mm(xn, w_ref[...])
    cos = cos_ref[...]
    sin = sin_ref[...]

    def rope(x):
        rep = x.shape[1] // V7X_LANES
        c = jnp.concatenate([cos] * rep, axis=1) if rep > 1 else cos
        s = jnp.concatenate([sin] * rep, axis=1) if rep > 1 else sin
        return x * c + _swap_halves(x) * s

    def seg(name):
        lo, hi = cols[name]
        return proj[:, lo:hi]

    xpl_ref[...] = seg("xpl")
    q_ref[...] = (rope(seg("q")) * (HEAD_DIM ** -0.5)).astype(q_ref.dtype)
    kc_ref[...] = rope(seg("kc"))
    vc_ref[...] = seg("vc")
    ks_ref[...] = rope(seg("ks")).astype(ks_ref.dtype)
    vs_ref[...] = seg("vs").astype(vs_ref.dtype)
    kw_ref[...] = rope(seg("kw")).astype(kw_ref.dtype)
    vw_ref[...] = seg("vw").astype(vw_ref.dtype)
    gate_ref[...] = _sigmoid(seg("g"))


def _inproj(h2, g_pre, w_all, cols, cos2, sin2):
    t, d = h2.shape
    tm = min(ROW_TILE, t)
    ncol = w_all.shape[1]
    widths = [("xpl", F32), ("q", MXU_DTYPE), ("kc", F32), ("vc", F32), ("ks", MXU_DTYPE),
              ("vs", MXU_DTYPE), ("kw", MXU_DTYPE), ("vw", MXU_DTYPE), ("g", F32)]
    row = lambda w: pl.BlockSpec((tm, w), lambda i: (i, 0))
    out_shape = [jax.ShapeDtypeStruct((t, cols[n][1] - cols[n][0]), dt) for n, dt in widths]
    out_specs = [row(cols[n][1] - cols[n][0]) for n, _ in widths]
    return pl.pallas_call(
        functools.partial(_inproj_body, cols),
        grid=(t // tm,),
        in_specs=[row(d), _const_spec((1, d)), _const_spec((d, ncol)), row(V7X_LANES), row(V7X_LANES)],
        out_specs=out_specs,
        out_shape=out_shape,
        compiler_params=_params(("parallel",)),
        name="mixer_in_proj",
    )(h2, g_pre[None, :], w_all, cos2, sin2)


def _poollru_body(x_ref, pw_ref, pscale_ref, cw_ref, cb_ref, wri_ref, bri_ref, lam_ref,
                  y_ref, pool_carry, conv_carry, h_carry):
    si = pl.program_id(1)
    ts = x_ref.shape[1]
    width = pw_ref.shape[0]
    halo_p = pool_carry.shape[0]
    halo_c = conv_carry.shape[0]

    @pl.when(si == 0)
    def _():
        pool_carry[...] = jnp.zeros_like(pool_carry)
        conv_carry[...] = jnp.zeros_like(conv_carry)
        h_carry[...] = jnp.zeros_like(h_carry)

    x = x_ref[0]
    xp = x[:, :width]
    xl = x[:, width:2 * width]
    gl = x[:, 2 * width:]

    ext = jnp.concatenate([pool_carry[...], xp], axis=0)
    sums = [ext]
    shift = 1
    for _ in POOL_WINDOWS:
        sums.append(sums[-1] + pltpu.roll(sums[-1], shift, 0))
        shift *= 2
    lane = lax.broadcasted_iota(jnp.int32, (1, width), 1)
    grp = lax.shift_right_logical(lane, POOL_GROUP.bit_length() - 1)
    win_sum = sums[len(POOL_WINDOWS)]
    win = jnp.full((1, width), float(POOL_WINDOWS[-1]), F32)
    for gi in range(len(POOL_WINDOWS) - 2, -1, -1):
        win_sum = jnp.where(grp == gi, sums[gi + 1], win_sum)
        win = jnp.where(grp == gi, float(POOL_WINDOWS[gi]), win)
    win_sum = win_sum[halo_p:]
    t_abs = si * ts + lax.broadcasted_iota(jnp.int32, (ts, 1), 0)
    cnt = jnp.minimum((t_abs + 1).astype(F32), win)
    pooled = win_sum / cnt
    y_pool = _mm((pooled - xp).astype(MXU_DTYPE), pw_ref[...]) * pscale_ref[...]
    pool_carry[...] = xp[ts - halo_p:]

    extc = jnp.concatenate([conv_carry[...], xl], axis=0)
    xc = extc * cw_ref[CONV_WIDTH - 1:CONV_WIDTH, :]
    for k in range(1, CONV_WIDTH):
        xc = xc + pltpu.roll(extc, k, 0) * cw_ref[CONV_WIDTH - 1 - k:CONV_WIDTH - k, :]
    xc = xc[halo_c:] + cb_ref[...]
    conv_carry[...] = xl[ts - halo_c:]

    ri = _mm(xc.astype(MXU_DTYPE), wri_ref[...]) + bri_ref[...]
    r = _sigmoid(ri[:, :width])
    i_gate = _sigmoid(ri[:, width:])
    neg_lam = -lam_ref[...]
    softplus = jnp.maximum(neg_lam, 0.0) + jnp.log1p(jnp.exp(-jnp.abs(neg_lam)))
    log_a = -LRU_C * r * softplus
    a = jnp.exp(log_a)
    b = jnp.sqrt(-jnp.tanh(log_a) * (a * a + 1.0)) * (i_gate * xc)

    row = lax.broadcasted_iota(jnp.int32, (ts, 1), 0)
    k = 1
    while k < ts:
        keep = row >= k
        a_prev = jnp.where(keep, pltpu.roll(a, k, 0), 1.0)
        b_prev = jnp.where(keep, pltpu.roll(b, k, 0), 0.0)
        b = a * b_prev + b
        a = a * a_prev
        k *= 2
    h = a * h_carry[0:1, :] + b
    h_carry[...] = jnp.broadcast_to(h[ts - 1:ts, :], h_carry.shape)

    gelu = 0.5 * gl * (1.0 + jnp.tanh(0.7978845608028654 * (gl + 0.044715 * gl * gl * gl)))
    y_ref[0] = jnp.concatenate([y_pool, h * gelu], axis=1).astype(y_ref.dtype)


def _poollru(xpl, pool_w_bd, pool_scale, conv_w, conv_b, w_ri_bd, b_ri, lam):
    b, s, w3 = xpl.shape
    width = w3 // 3
    ts = min(SEQ_TILE, s)
    return pl.pallas_call(
        _poollru_body,
        grid=(b, s // ts),
        in_specs=[pl.BlockSpec((1, ts, w3), lambda i, j: (i, j, 0)),
                  _const_spec((width, width)), _const_spec((1, width)),
                  _const_spec((CONV_WIDTH, width)), _const_spec((1, width)),
                  _const_spec((width, 2 * width)), _const_spec((1, 2 * width)), _const_spec((1, width))],
        out_specs=pl.BlockSpec((1, ts, 2 * width), lambda i, j: (i, j, 0)),
        out_shape=jax.ShapeDtypeStruct((b, s, 2 * width), MXU_DTYPE),
        scratch_shapes=[pltpu.VMEM((POOL_WINDOWS[-1], width), F32),
                        pltpu.VMEM((8, width), F32),
                        pltpu.VMEM((8, width), F32)],
        compiler_params=_params(("parallel", "arbitrary")),
        name="pool_rglru",
    )(xpl, pool_w_bd, pool_scale[None, :], conv_w, conv_b[None, :], w_ri_bd, b_ri[None, :], lam[None, :])


def _compress_body(r_ref, pea_ref, peb_ref, wa_ref, wb_ref, o_ref):
    r = r_ref[0]
    first = _mm((r + pea_ref[...]).astype(MXU_DTYPE), wa_ref[...])
    second = _mm((r + peb_ref[...]).astype(MXU_DTYPE), wb_ref[...])
    n1 = r.shape[0]
    o_ref[0] = (first + pltpu.roll(second, n1 - 1, 0)).astype(o_ref.dtype)


def _compress(rows, pe_a, pe_b, w_a, w_b):
    b, n1, k = rows.shape
    n_out = w_a.shape[1]
    return pl.pallas_call(
        _compress_body,
        grid=(b,),
        in_specs=[pl.BlockSpec((1, n1, k), lambda i: (i, 0, 0)),
                  _const_spec((1, k)), _const_spec((1, k)), _const_spec((k, n_out)), _const_spec((k, n_out))],
        out_specs=pl.BlockSpec((1, n1, n_out), lambda i: (i, 0, 0)),
        out_shape=jax.ShapeDtypeStruct((b, n1, n_out), MXU_DTYPE),
        compiler_params=_params(("parallel",)),
        name="compress_kv",
    )(rows, pe_a, pe_b, w_a, w_b)


def _stack_heads(q_ref):
    return jnp.concatenate([q_ref[0, g] for g in range(GQA_GROUP)], axis=1)


def _query_positions(start, tq):
    lane = lax.broadcasted_iota(jnp.int32, (1, GQA_GROUP * tq), 1)
    return start + (lane & (tq - 1))


def _cmp_body(n_sel, q_ref, kc_ref, vct_ref, ovt_ref, oc_ref, mb_ref):
    tq = q_ref.shape[3]
    start = pl.program_id(2) * tq
    qt = _stack_heads(q_ref)
    kc = kc_ref[0, 0]
    n1 = kc.shape[0]
    t_row = _query_positions(start, tq)

    s = _mm(kc, qt)
    n_idx = lax.broadcasted_iota(jnp.int32, (n1, 1), 0)
    mask = n_idx * CMP_STRIDE + (CMP_LEN - 1) <= t_row
    s = jnp.where(mask, s, NEG_INF)
    m = jnp.max(s, axis=0, keepdims=True)
    p = jnp.where(mask, jnp.exp(s - m), 0.0)
    den = jnp.sum(p, axis=0, keepdims=True)
    pn = p * (1.0 / jnp.where(den > 0.0, den, 1.0))
    oc = _mm(vct_ref[0, 0], pn.astype(MXU_DTYPE))
    psum = pn[:, 0:tq]
    for g in range(GQA_GROUP):
        oc_ref[0, g] = oc[:, g * tq:(g + 1) * tq]
        if g:
            psum = psum + pn[:, g * tq:(g + 1) * tq]

    ovt = ovt_ref[...]
    hi = psum.astype(MXU_DTYPE)
    rem = psum - hi.astype(F32)
    mid = rem.astype(MXU_DTYPE)
    low = (rem - mid.astype(F32)).astype(MXU_DTYPE)
    imp = _mm(ovt, hi) + _mm(ovt, mid) + _mm(ovt, low)

    n_pad = imp.shape[0]
    j = lax.broadcasted_iota(jnp.int32, (n_pad, 1), 0).astype(F32)
    t_q = start + lax.broadcasted_iota(jnp.int32, (1, tq), 1)
    cur = lax.shift_right_logical(t_q, SEL_BLOCK.bit_length() - 1).astype(F32)
    forced = (j == 0.0) | (j == cur) | (j == cur - 1.0)
    score = jnp.where(j <= cur, jnp.where(forced, BIG_SCORE, imp), -BIG_SCORE)
    score = jnp.where(j < float(n_sel), score, -jnp.inf)
    bias = jnp.full(score.shape, NEG_INF, F32)
    for _ in range(min(SEL_TOPK, n_sel)):
        best = jnp.max(score, axis=0, keepdims=True)
        first = jnp.min(jnp.where(score == best, j, float(n_pad)), axis=0, keepdims=True)
        pick = j == first
        bias = jnp.where(pick, 0.0, bias)
        score = jnp.where(pick, -jnp.inf, score)
    mb_ref[0, 0] = bias.astype(mb_ref.dtype)


def _cmp_attention(qt, k_cmp, v_cmp_t, ov_t, n_sel):
    b, _, hd, s = qt.shape
    n1 = k_cmp.shape[2]
    n_pad = ov_t.shape[0]
    tq = min(Q_TILE, s)
    head_blk = pl.BlockSpec((1, GQA_GROUP, hd, tq), lambda i, h, j: (i, h, 0, j))
    return pl.pallas_call(
        functools.partial(_cmp_body, n_sel),
        grid=(b, N_KV_HEADS, s // tq),
        in_specs=[head_blk,
                  pl.BlockSpec((1, 1, n1, hd), lambda i, h, j: (i, h, 0, 0)),
                  pl.BlockSpec((1, 1, hd, n1), lambda i, h, j: (i, h, 0, 0)),
                  _const_spec((n_pad, n1))],
        out_specs=[head_blk, pl.BlockSpec((1, 1, n_pad, tq), lambda i, h, j: (i, h, 0, j))],
        out_shape=[jax.ShapeDtypeStruct((b, N_HEADS, hd, s), F32),
                   jax.ShapeDtypeStruct((b, N_KV_HEADS, n_pad, s), MXU_DTYPE)],
        compiler_params=_params(("parallel", "parallel", "parallel")),
        name="cmp_attention_select",
    )(qt, k_cmp, v_cmp_t, ov_t)


def _selwin_body(q_ref, mb_ref, ks_ref, vs_ref, kw_ref, vw_ref, oc_ref, gate_ref, y_ref):
    tq = q_ref.shape[3]
    tk = vs_ref.shape[4]
    qi = pl.program_id(2)
    start = qi * tq
    rows = GQA_GROUP * tq
    qt = _stack_heads(q_ref)
    t_row = _query_positions(start, tq)
    tiles_per_group = SEL_GROUP * SEL_BLOCK // tk
    n_groups = mb_ref.shape[2] // SEL_GROUP

    def q_aug(bias_rows):
        return jnp.concatenate([qt, jnp.concatenate([bias_rows] * GQA_GROUP, axis=1)], axis=0)

    def step(kt, carry, qa, causal):
        m, acc = carry
        k = ks_ref[0, 0, pl.ds(pl.multiple_of(kt * tk, tk), tk), :]
        s = _mm(k, qa)
        if causal:
            kp = kt * tk + lax.broadcasted_iota(jnp.int32, (tk, 1), 0)
            s = jnp.where(kp <= t_row, s, NEG_INF)
        m_new = jnp.maximum(m, jnp.max(s, axis=0, keepdims=True))
        alpha = jnp.exp(m - m_new)
        p = jnp.exp(s - m_new).astype(MXU_DTYPE)
        acc = alpha * acc + _mm(vs_ref[0, 0, kt], p)
        return m_new, acc

    carry = (jnp.full((1, rows), M_INIT, F32), jnp.zeros((V_ROWS, rows), F32))
    n_full = start // tk
    for grp in range(n_groups):
        qa = q_aug(mb_ref[0, 0, grp * SEL_GROUP:(grp + 1) * SEL_GROUP, :])
        lo = grp * tiles_per_group
        hi = jnp.clip(n_full, lo, lo + tiles_per_group)
        carry = lax.fori_loop(lo, hi, functools.partial(step, qa=qa, causal=False), carry)
    grp_d = n_full // tiles_per_group
    qa_d = q_aug(mb_ref[0, 0, pl.ds(pl.multiple_of(grp_d * SEL_GROUP, SEL_GROUP), SEL_GROUP), :])
    _, acc = step(n_full, carry, qa_d, True)
    o_sel = acc[:HEAD_DIM] * (1.0 / acc[HEAD_DIM:HEAD_DIM + 1])

    span = WINDOW + tq
    kwin = kw_ref[0, 0, pl.ds(pl.multiple_of(start, tq), span), :]
    s = _mm(kwin, qt)
    kp = start - WINDOW + lax.broadcasted_iota(jnp.int32, (span, 1), 0)
    diff = t_row - kp
    mask = (diff >= 0) & (diff < WINDOW) & (kp >= 0)
    s = jnp.where(mask, s, NEG_INF)
    m = jnp.max(s, axis=0, keepdims=True)
    p = jnp.where(mask, jnp.exp(s - m), 0.0).astype(MXU_DTYPE)
    vwin = jnp.concatenate([vw_ref[0, 0, qi + i] for i in range(span // tq)], axis=1)
    accw = _mm(vwin, p)
    o_win = accw[:HEAD_DIM] * (1.0 / accw[HEAD_DIM:HEAD_DIM + 1])

    for g in range(GQA_GROUP):
        cols = slice(g * tq, (g + 1) * tq)
        gates = [gate_ref[0, 0, N_BRANCH * g + br:N_BRANCH * g + br + 1, :] for br in range(N_BRANCH)]
        y = gates[0] * oc_ref[0, g] + gates[1] * o_sel[:, cols] + gates[2] * o_win[:, cols]
        y_ref[0, g] = y.astype(y_ref.dtype)


def _sel_win_attention(qt, mb_t, ks_aug, vs_tiles, kw_pad, vw_tiles, oc_t, gates_t):
    b, _, hd, s = qt.shape
    tq = min(Q_TILE, s)
    n_pad = mb_t.shape[2]
    head_blk = pl.BlockSpec((1, GQA_GROUP, hd, tq), lambda i, h, j: (i, h, 0, j))
    whole = lambda a: pl.BlockSpec((1, 1) + a.shape[2:], lambda i, h, j: (i, h) + (0,) * (a.ndim - 2),
                                   pipeline_mode=pl.Buffered(1))
    return pl.pallas_call(
        _selwin_body,
        grid=(b, N_KV_HEADS, s // tq),
        in_specs=[head_blk,
                  pl.BlockSpec((1, 1, n_pad, tq), lambda i, h, j: (i, h, 0, j)),
                  whole(ks_aug), whole(vs_tiles), whole(kw_pad), whole(vw_tiles),
                  head_blk,
                  pl.BlockSpec((1, 1, GQA_GROUP * N_BRANCH, tq), lambda i, h, j: (i, h, 0, j))],
        out_specs=head_blk,
        out_shape=jax.ShapeDtypeStruct((b, N_HEADS, hd, s), MXU_DTYPE),
        compiler_params=_params(("parallel", "parallel", "arbitrary")),
        name="sel_win_attention",
    )(qt, mb_t, ks_aug, vs_tiles, kw_pad, vw_tiles, oc_t, gates_t)


def _outproj_body(h_ref, ypl_ref, yat_ref, w1_ref, w2_ref, g_ref, o_ref):
    m = _mm(ypl_ref[...], w1_ref[...]) + _mm(yat_ref[...], w2_ref[...])
    o_ref[...] = h_ref[...] + _rms(m, g_ref[...])


def _outproj(h2, ypl, yat, w1, w2, g_post):
    t, d = h2.shape
    tm = min(ROW_TILE, t)
    row = lambda w: pl.BlockSpec((tm, w), lambda i: (i, 0))
    return pl.pallas_call(
        _outproj_body,
        grid=(t // tm,),
        in_specs=[row(d), row(ypl.shape[1]), row(yat.shape[1]),
                  _const_spec(w1.shape), _const_spec(w2.shape), _const_spec((1, d))],
        out_specs=row(d),
        out_shape=jax.ShapeDtypeStruct((t, d), F32),
        compiler_params=_params(("parallel",)),
        name="mixer_out_proj",
    )(h2, ypl, yat, w1, w2, g_post[None, :])


def _block_diag(blocks):
    n, a, b = blocks.shape
    eye = jnp.eye(n, dtype=blocks.dtype)
    return jnp.einsum("nab,nm->namb", blocks, eye).reshape(n * a, n * b)


def _compress_weights(w, pe):
    eye = jnp.eye(N_KV_HEADS, dtype=w.dtype)
    halves = []
    for part in range(CMP_LEN // CMP_STRIDE):
        wl = w[part * CMP_STRIDE:(part + 1) * CMP_STRIDE]
        pel = pe[part * CMP_STRIDE:(part + 1) * CMP_STRIDE]
        wm = jnp.einsum("lde,hg->lhdge", wl, eye).reshape(CMP_STRIDE * N_KV_HEADS * HEAD_DIM,
                                                           N_KV_HEADS * HEAD_DIM)
        pm = jnp.tile(pel[:, None, :], (1, N_KV_HEADS, 1)).reshape(1, -1)
        halves.append((wm.astype(MXU_DTYPE), pm))
    return halves


def _split_heads_t(x, b, s):
    return x.reshape(b, s, N_KV_HEADS, HEAD_DIM).transpose(0, 2, 3, 1)


def _value_tiles(v_t, tile):
    b, kv, hd, n = v_t.shape
    ones = jnp.ones((b, kv, 1, n), v_t.dtype)
    zeros = jnp.zeros((b, kv, V_ROWS - hd - 1, n), v_t.dtype)
    full = jnp.concatenate([v_t, ones, zeros], axis=2)
    return full.reshape(b, kv, V_ROWS, n // tile, tile).transpose(0, 1, 3, 2, 4)


def kernel(x, positions, ffn1_pre_g, ffn1_post_g, ffn1_w_gate, ffn1_w_up, ffn1_w_down, mix_pre_g, mix_post_g,
           w_in, w_out, pool_w, pool_scale, conv_w, conv_b, lru_w_r, lru_b_r, lru_w_i, lru_b_i, lru_lambda,
           cmp_w_k, cmp_w_v, cmp_pe, ffn2_pre_g, ffn2_post_g, ffn2_w_gate, ffn2_w_up, ffn2_w_down):
    b, s, d = x.shape
    depth = w_in.shape[0]
    t = b * s
    pool_width = pool_w.shape[1] * pool_w.shape[2]
    lru_width = lru_w_r.shape[1] * lru_w_r.shape[2]
    attn_width = N_HEADS * HEAD_DIM
    kv_width = N_KV_HEADS * HEAD_DIM
    assert pool_width == lru_width and s % Q_TILE == 0 and s % (CMP_STRIDE * 8) == 0
    assert Q_TILE == V7X_LANES and WINDOW % Q_TILE == 0 and K_TILE % Q_TILE == 0

    sizes = [("xpl", pool_width + 2 * lru_width), ("q", attn_width), ("kc", kv_width), ("vc", kv_width),
             ("ks", kv_width), ("vs", kv_width), ("kw", kv_width), ("vw", kv_width), ("g", V7X_LANES)]
    cols, off = {}, 0
    for name, width in sizes:
        cols[name] = (off, off + width)
        off += width
    n_gate = N_BRANCH * N_HEADS

    n_sel = s // SEL_BLOCK
    n_pad = -(-n_sel // SEL_GROUP) * SEL_GROUP
    n1 = s // CMP_STRIDE
    n_cmp = (s - CMP_LEN) // CMP_STRIDE + 1
    cmp_start = jnp.arange(n1) * CMP_STRIDE
    sel_start = jnp.arange(n_pad) * SEL_BLOCK
    overlap_t = ((cmp_start[None, :] < sel_start[:, None] + SEL_BLOCK) &
                 (cmp_start[None, :] + CMP_LEN > sel_start[:, None]) &
                 (jnp.arange(n1)[None, :] < n_cmp) & (jnp.arange(n_pad)[:, None] < n_sel)).astype(MXU_DTYPE)
    key_blk = (jnp.arange(s) // SEL_BLOCK) % SEL_GROUP
    key_onehot = (key_blk[:, None] == jnp.arange(SEL_GROUP)[None, :]).astype(MXU_DTYPE)

    cos, sin = _rope_tables(positions)
    cos2 = cos.reshape(t, V7X_LANES)
    sin2 = sin.reshape(t, V7X_LANES)

    h = x.reshape(t, d)
    for l in range(depth):
        cast = lambda w: w[l].astype(MXU_DTYPE)
        h = _ffn(h, ffn1_pre_g[l], ffn1_post_g[l], cast(ffn1_w_gate), cast(ffn1_w_up), cast(ffn1_w_down))

        w_all = jnp.pad(w_in[l], ((0, 0), (0, off - w_in.shape[2]))).astype(MXU_DTYPE)
        xpl, q, kc, vc, ks, vs, kw, vw, gates = _inproj(h, mix_pre_g[l], w_all, cols, cos2, sin2)

        w_ri = jnp.concatenate([_block_diag(lru_w_r[l]), _block_diag(lru_w_i[l])], axis=1).astype(MXU_DTYPE)
        ypl = _poollru(xpl.reshape(b, s, -1), _block_diag(pool_w[l]).astype(MXU_DTYPE), pool_scale[l],
                       conv_w[l], conv_b[l], w_ri, jnp.concatenate([lru_b_r[l], lru_b_i[l]]), lru_lambda[l])

        (wk_a, pe_a), (wk_b, pe_b) = _compress_weights(cmp_w_k[l], cmp_pe[l])
        (wv_a, _), (wv_b, _) = _compress_weights(cmp_w_v[l], cmp_pe[l])
        k_cmp = _compress(kc.reshape(b, n1, -1), pe_a, pe_b, wk_a, wk_b)
        v_cmp = _compress(vc.reshape(b, n1, -1), pe_a, pe_b, wv_a, wv_b)
        k_cmp = k_cmp.reshape(b, n1, N_KV_HEADS, HEAD_DIM).transpose(0, 2, 1, 3)
        v_cmp_t = _split_heads_t(v_cmp, b, n1)

        qt = q.reshape(b, s, N_HEADS, HEAD_DIM).transpose(0, 2, 3, 1)
        oc_t, mb_t = _cmp_attention(qt, k_cmp, v_cmp_t, overlap_t, n_sel)

        ks_h = ks.reshape(b, s, N_KV_HEADS, HEAD_DIM).transpose(0, 2, 1, 3)
        ks_aug = jnp.concatenate(
            [ks_h, jnp.broadcast_to(key_onehot, (b, N_KV_HEADS, s, SEL_GROUP))], axis=3)
        vs_tiles = _value_tiles(_split_heads_t(vs, b, s), K_TILE)
        front = ((0, 0), (0, 0), (WINDOW, 0), (0, 0))
        kw_pad = jnp.pad(kw.reshape(b, s, N_KV_HEADS, HEAD_DIM).transpose(0, 2, 1, 3), front)
        vw_tiles = _value_tiles(jnp.pad(_split_heads_t(vw, b, s), ((0, 0), (0, 0), (0, 0), (WINDOW, 0))),
                                Q_TILE)
        gates_t = gates.reshape(b, s, -1)[:, :, :n_gate].transpose(0, 2, 1).reshape(
            b, N_KV_HEADS, GQA_GROUP * N_BRANCH, s)
        y_t = _sel_win_attention(qt, mb_t, ks_aug, vs_tiles, kw_pad, vw_tiles, oc_t, gates_t)
        yat = y_t.transpose(0, 3, 1, 2).reshape(t, attn_width)

        w_o = w_out[l].astype(MXU_DTYPE)
        split = pool_width + lru_width
        h = _outproj(h, ypl.reshape(t, -1), yat, w_o[:split], w_o[split:], mix_post_g[l])

        h = _ffn(h, ffn2_pre_g[l], ffn2_post_g[l], cast(ffn2_w_gate), cast(ffn2_w_up), cast(ffn2_w_down))
    return h.reshape(b, s, d)
```

```python
import functools

import jax
import jax.numpy as jnp
from jax import lax
from jax.experimental import pallas as pl
from jax.experimental.pallas import tpu as pltpu

F32 = jnp.float32
MXU_DTYPE = jnp.bfloat16

POOL_WINDOWS = (2, 4, 8, 16)
POOL_GROUP = 64
LRU_HEAD_DIM = 64
LRU_C = 8.0
CONV_WIDTH = 4
HEAD_DIM = 64
N_KV_HEADS = 2
GQA_GROUP = 4
N_HEADS = N_KV_HEADS * GQA_GROUP
N_BRANCH = 3
CMP_LEN = 32
CMP_STRIDE = 16
SEL_BLOCK = 64
SEL_TOPK = 16
WINDOW = 512
ROPE_THETA = 10000.0
NORM_EPS = 1e-6
NEG_INF = -1e30
BIG_SCORE = 1e9

V7X_LANES = 128
V7X_VMEM_LIMIT_BYTES = 56 * 1024 * 1024

ROW_TILE = 512
SEQ_TILE = 512
CMP_Q_TILE = 128
SEL_TILE = 256
SEL_GROUP = 128
V_ROWS = 80
M_INIT = -1e20
Q_SCALE = HEAD_DIM ** -0.5 * 1.4426950408889634


def _params(semantics):
    return pltpu.CompilerParams(dimension_semantics=semantics, vmem_limit_bytes=V7X_VMEM_LIMIT_BYTES)


def _rms(x, g):
    return x * lax.rsqrt(jnp.mean(x * x, axis=-1, keepdims=True) + NORM_EPS) * g


def _sigmoid(x):
    return 1.0 / (1.0 + jnp.exp(-x))


def _mm(a, b):
    return jnp.dot(a, b, preferred_element_type=F32)


def _const_spec(shape):
    zeros = (0,) * len(shape)
    return pl.BlockSpec(shape, lambda *_: zeros, pipeline_mode=pl.Buffered(1))


def _rope_body(pos_ref, inv_ref, cos_ref, sin_ref):
    ang = pos_ref[0].astype(F32) * inv_ref[...]
    lane = lax.broadcasted_iota(jnp.int32, ang.shape, 1)
    cos_ref[0] = jnp.cos(ang)
    sin_ref[0] = jnp.where((lane & (HEAD_DIM - 1)) < HEAD_DIM // 2, -jnp.sin(ang), jnp.sin(ang))


def _rope_tables(positions):
    b, s = positions.shape
    inv = ROPE_THETA ** (-jnp.arange(0, HEAD_DIM, 2, dtype=F32) / HEAD_DIM)
    inv_row = jnp.tile(inv, V7X_LANES // (HEAD_DIM // 2))[None, :]
    ts = min(SEQ_TILE, s)
    out = jax.ShapeDtypeStruct((b, s, V7X_LANES), F32)
    return pl.pallas_call(
        _rope_body,
        grid=(b, s // ts),
        in_specs=[pl.BlockSpec((1, ts, 1), lambda i, j: (i, j, 0)),
                  pl.BlockSpec((1, V7X_LANES), lambda i, j: (0, 0))],
        out_specs=[pl.BlockSpec((1, ts, V7X_LANES), lambda i, j: (i, j, 0))] * 2,
        out_shape=[out, out],
        compiler_params=_params(("parallel", "parallel")),
        name="rope_tables",
    )(positions[:, :, None], inv_row)


def _ffn_body(h_ref, gpre_ref, gpost_ref, wg_ref, wu_ref, wd_ref, o_ref):
    h = h_ref[...]
    xn = _rms(h, gpre_ref[...]).astype(MXU_DTYPE)
    gate = _mm(xn, wg_ref[...])
    up = _mm(xn, wu_ref[...])
    act = (gate * _sigmoid(gate) * up).astype(MXU_DTYPE)
    f = _mm(act, wd_ref[...])
    o_ref[...] = h + 0.5 * _rms(f, gpost_ref[...])


def _ffn(h2, g_pre, g_post, w_gate, w_up, w_down):
    t, d = h2.shape
    dff = w_gate.shape[1]
    tm = min(ROW_TILE, t)
    return pl.pallas_call(
        _ffn_body,
        grid=(t // tm,),
        in_specs=[pl.BlockSpec((tm, d), lambda i: (i, 0)),
                  _const_spec((1, d)), _const_spec((1, d)),
                  _const_spec((d, dff)), _const_spec((d, dff)), _const_spec((dff, d))],
        out_specs=pl.BlockSpec((tm, d), lambda i: (i, 0)),
        out_shape=jax.ShapeDtypeStruct((t, d), F32),
        compiler_params=_params(("parallel",)),
        name="ffn",
    )(h2, g_pre[None, :], g_post[None, :], w_gate, w_up, w_down)


def _swap_halves(x):
    n = x.shape[1]
    lane = lax.broadcasted_iota(jnp.int32, x.shape, 1)
    first_half = (lane & (HEAD_DIM - 1)) < HEAD_DIM // 2
    return jnp.where(first_half, pltpu.roll(x, n - HEAD_DIM // 2, 1), pltpu.roll(x, HEAD_DIM // 2, 1))


def _inproj_body(cols, h_ref, g_ref, w_ref, cos_ref, sin_ref,
                 xpl_ref, q_ref, kc_ref, vc_ref, ks_ref, vs_ref, kw_ref, vw_ref, gate_ref):
    xn = _rms(h_ref[...], g_ref[...]).astype(MXU_DTYPE)
    proj = _mm(xn, w_ref[...])
    cos = cos_ref[...]
    sin = sin_ref[...]

    def rope(x):
        rep = x.shape[1] // V7X_LANES
        c = jnp.concatenate([cos] * rep, axis=1) if rep > 1 else cos
        s = jnp.concatenate([sin] * rep, axis=1) if rep > 1 else sin
        return x * c + _swap_halves(x) * s

    def seg(name):
        lo, hi = cols[name]
        return proj[:, lo:hi]

    xpl_ref[...] = seg("xpl")
    q_ref[...] = (rope(seg("q")) * Q_SCALE).astype(q_ref.dtype)
    kc_ref[...] = rope(seg("kc"))
    vc_ref[...] = seg("vc")
    ks_ref[...] = rope(seg("ks")).astype(ks_ref.dtype)
    vs_ref[...] = seg("vs").astype(vs_ref.dtype)
    kw_ref[...] = rope(seg("kw")).astype(kw_ref.dtype)
    vw_ref[...] = seg("vw").astype(vw_ref.dtype)
    gate_ref[...] = _sigmoid(seg("g"))


def _inproj(h2, g_pre, w_all, cols, cos2, sin2):
    t, d = h2.shape
    tm = min(ROW_TILE, t)
    ncol = w_all.shape[1]
    widths = [("xpl", F32), ("q", MXU_DTYPE), ("kc", F32), ("vc", F32), ("ks", MXU_DTYPE),
              ("vs", MXU_DTYPE), ("kw", MXU_DTYPE), ("vw", MXU_DTYPE), ("g", F32)]
    row = lambda w: pl.BlockSpec((tm, w), lambda i: (i, 0))
    out_shape = [jax.ShapeDtypeStruct((t, cols[n][1] - cols[n][0]), dt) for n, dt in widths]
    out_specs = [row(cols[n][1] - cols[n][0]) for n, _ in widths]
    return pl.pallas_call(
        functools.partial(_inproj_body, cols),
        grid=(t // tm,),
        in_specs=[row(d), _const_spec((1, d)), _const_spec((d, ncol)), row(V7X_LANES), row(V7X_LANES)],
        out_specs=out_specs,
        out_shape=out_shape,
        compiler_params=_params(("parallel",)),
        name="mixer_in_proj",
    )(h2, g_pre[None, :], w_all, cos2, sin2)


def _poollru_body(x_ref, pw_ref, pscale_ref, cw_ref, cb_ref, wri_ref, bri_ref, lam_ref,
                  y_ref, pool_carry, conv_carry, h_carry):
    si = pl.program_id(1)
    ts = x_ref.shape[1]
    width = pw_ref.shape[0]
    halo_p = pool_carry.shape[0]
    halo_c = conv_carry.shape[0]

    @pl.when(si == 0)
    def _():
        pool_carry[...] = jnp.zeros_like(pool_carry)
        conv_carry[...] = jnp.zeros_like(conv_carry)
        h_carry[...] = jnp.zeros_like(h_carry)

    x = x_ref[0]
    xp = x[:, :width]
    xl = x[:, width:2 * width]
    gl = x[:, 2 * width:]

    ext = jnp.concatenate([pool_carry[...], xp], axis=0)
    sums = [ext]
    shift = 1
    for _ in POOL_WINDOWS:
        sums.append(sums[-1] + pltpu.roll(sums[-1], shift, 0))
        shift *= 2
    lane = lax.broadcasted_iota(jnp.int32, (1, width), 1)
    grp = lax.shift_right_logical(lane, POOL_GROUP.bit_length() - 1)
    win_sum = sums[len(POOL_WINDOWS)]
    win = jnp.full((1, width), float(POOL_WINDOWS[-1]), F32)
    for gi in range(len(POOL_WINDOWS) - 2, -1, -1):
        win_sum = jnp.where(grp == gi, sums[gi + 1], win_sum)
        win = jnp.where(grp == gi, float(POOL_WINDOWS[gi]), win)
    win_sum = win_sum[halo_p:]
    t_abs = si * ts + lax.broadcasted_iota(jnp.int32, (ts, 1), 0)
    cnt = jnp.minimum((t_abs + 1).astype(F32), win)
    pooled = win_sum / cnt
    y_pool = _mm((pooled - xp).astype(MXU_DTYPE), pw_ref[...]) * pscale_ref[...]
    pool_carry[...] = xp[ts - halo_p:]

    extc = jnp.concatenate([conv_carry[...], xl], axis=0)
    xc = extc * cw_ref[CONV_WIDTH - 1:CONV_WIDTH, :]
    for k in range(1, CONV_WIDTH):
        xc = xc + pltpu.roll(extc, k, 0) * cw_ref[CONV_WIDTH - 1 - k:CONV_WIDTH - k, :]
    xc = xc[halo_c:] + cb_ref[...]
    conv_carry[...] = xl[ts - halo_c:]

    ri = _mm(xc.astype(MXU_DTYPE), wri_ref[...]) + bri_ref[...]
    r = _sigmoid(ri[:, :width])
    i_gate = _sigmoid(ri[:, width:])
    neg_lam = -lam_ref[...]
    softplus = jnp.maximum(neg_lam, 0.0) + jnp.log1p(jnp.exp(-jnp.abs(neg_lam)))
    log_a = -LRU_C * r * softplus
    a = jnp.exp(log_a)
    b = jnp.sqrt(-jnp.tanh(log_a) * (a * a + 1.0)) * (i_gate * xc)

    row = lax.broadcasted_iota(jnp.int32, (ts, 1), 0)
    k = 1
    while k < ts:
        keep = row >= k
        a_prev = jnp.where(keep, pltpu.roll(a, k, 0), 1.0)
        b_prev = jnp.where(keep, pltpu.roll(b, k, 0), 0.0)
        b = a * b_prev + b
        a = a * a_prev
        k *= 2
    h = a * h_carry[0:1, :] + b
    h_carry[...] = jnp.broadcast_to(h[ts - 1:ts, :], h_carry.shape)

    gelu = 0.5 * gl * (1.0 + jnp.tanh(0.7978845608028654 * (gl + 0.044715 * gl * gl * gl)))
    y_ref[0] = jnp.concatenate([y_pool, h * gelu], axis=1).astype(y_ref.dtype)


def _poollru(xpl, pool_w_bd, pool_scale, conv_w, conv_b, w_ri_bd, b_ri, lam):
    b, s, w3 = xpl.shape
    width = w3 // 3
    ts = min(SEQ_TILE, s)
    return pl.pallas_call(
        _poollru_body,
        grid=(b, s // ts),
        in_specs=[pl.BlockSpec((1, ts, w3), lambda i, j: (i, j, 0)),
                  _const_spec((width, width)), _const_spec((1, width)),
                  _const_spec((CONV_WIDTH, width)), _const_spec((1, width)),
                  _const_spec((width, 2 * width)), _const_spec((1, 2 * width)), _const_spec((1, width))],
        out_specs=pl.BlockSpec((1, ts, 2 * width), lambda i, j: (i, j, 0)),
        out_shape=jax.ShapeDtypeStruct((b, s, 2 * width), MXU_DTYPE),
        scratch_shapes=[pltpu.VMEM((POOL_WINDOWS[-1], width), F32),
                        pltpu.VMEM((8, width), F32),
                        pltpu.VMEM((8, width), F32)],
        compiler_params=_params(("parallel", "arbitrary")),
        name="pool_rglru",
    )(xpl, pool_w_bd, pool_scale[None, :], conv_w, conv_b[None, :], w_ri_bd, b_ri[None, :], lam[None, :])


def _compress_body(r_ref, pea_ref, peb_ref, wa_ref, wb_ref, o_ref):
    r = r_ref[0]
    first = _mm((r + pea_ref[...]).astype(MXU_DTYPE), wa_ref[...])
    second = _mm((r + peb_ref[...]).astype(MXU_DTYPE), wb_ref[...])
    n1 = r.shape[0]
    o_ref[0] = (first + pltpu.roll(second, n1 - 1, 0)).astype(o_ref.dtype)


def _compress(rows, pe_a, pe_b, w_a, w_b):
    b, n1, k = rows.shape
    n_out = w_a.shape[1]
    return pl.pallas_call(
        _compress_body,
        grid=(b,),
        in_specs=[pl.BlockSpec((1, n1, k), lambda i: (i, 0, 0)),
                  _const_spec((1, k)), _const_spec((1, k)), _const_spec((k, n_out)), _const_spec((k, n_out))],
        out_specs=pl.BlockSpec((1, n1, n_out), lambda i: (i, 0, 0)),
        out_shape=jax.ShapeDtypeStruct((b, n1, n_out), MXU_DTYPE),
        compiler_params=_params(("parallel",)),
        name="compress_kv",
    )(rows, pe_a, pe_b, w_a, w_b)


def _stack_heads(q_ref):
    return jnp.concatenate([q_ref[0, g] for g in range(GQA_GROUP)], axis=1)


def _query_positions(start, tq):
    lane = lax.broadcasted_iota(jnp.int32, (1, GQA_GROUP * tq), 1)
    return start + (lane & (tq - 1))


def _cmp_body(n_sel, q_ref, kc_ref, vct_ref, ovt_ref, oc_ref, mb_ref):
    tq = q_ref.shape[3]
    start = pl.program_id(2) * tq
    qt = _stack_heads(q_ref)
    kc = kc_ref[0, 0]
    n1 = kc.shape[0]
    t_row = _query_positions(start, tq)

    s = _mm(kc, qt)
    n_idx = lax.broadcasted_iota(jnp.int32, (n1, 1), 0)
    mask = n_idx * CMP_STRIDE + (CMP_LEN - 1) <= t_row
    s = jnp.where(mask, s, NEG_INF)
    m = jnp.max(s, axis=0, keepdims=True)
    p = jnp.where(mask, jnp.exp2(s - m), 0.0)
    den = jnp.sum(p, axis=0, keepdims=True)
    pn = p * (1.0 / jnp.where(den > 0.0, den, 1.0))
    oc = _mm(vct_ref[0, 0], pn.astype(MXU_DTYPE))
    psum = pn[:, 0:tq]
    for g in range(GQA_GROUP):
        oc_ref[0, g] = oc[:, g * tq:(g + 1) * tq]
        if g:
            psum = psum + pn[:, g * tq:(g + 1) * tq]

    ovt = ovt_ref[...]
    hi = psum.astype(MXU_DTYPE)
    rem = psum - hi.astype(F32)
    mid = rem.astype(MXU_DTYPE)
    low = (rem - mid.astype(F32)).astype(MXU_DTYPE)
    imp = _mm(ovt, hi) + _mm(ovt, mid) + _mm(ovt, low)

    n_pad = imp.shape[0]
    j = lax.broadcasted_iota(jnp.int32, (n_pad, 1), 0).astype(F32)
    t_q = start + lax.broadcasted_iota(jnp.int32, (1, tq), 1)
    cur = lax.shift_right_logical(t_q, SEL_BLOCK.bit_length() - 1).astype(F32)
    forced = (j == 0.0) | (j == cur) | (j == cur - 1.0)
    score = jnp.where(j <= cur, jnp.where(forced, BIG_SCORE, imp), -BIG_SCORE)
    score = jnp.where(j < float(n_sel), score, -jnp.inf)
    bias = jnp.full(score.shape, NEG_INF, F32)
    for _ in range(min(SEL_TOPK, n_sel)):
        best = jnp.max(score, axis=0, keepdims=True)
        first = jnp.min(jnp.where(score == best, j, float(n_pad)), axis=0, keepdims=True)
        pick = j == first
        bias = jnp.where(pick, 0.0, bias)
        score = jnp.where(pick, -jnp.inf, score)
    mb_ref[0, 0] = bias.astype(mb_ref.dtype)


def _cmp_attention(qt, k_cmp, v_cmp_t, ov_t, n_sel):
    b, _, hd, s = qt.shape
    n1 = k_cmp.shape[2]
    n_pad = ov_t.shape[0]
    tq = min(CMP_Q_TILE, s)
    head_blk = pl.BlockSpec((1, GQA_GROUP, hd, tq), lambda i, h, j: (i, h, 0, j))
    return pl.pallas_call(
        functools.partial(_cmp_body, n_sel),
        grid=(b, N_KV_HEADS, s // tq),
        in_specs=[head_blk,
                  pl.BlockSpec((1, 1, n1, hd), lambda i, h, j: (i, h, 0, 0)),
                  pl.BlockSpec((1, 1, hd, n1), lambda i, h, j: (i, h, 0, 0)),
                  _const_spec((n_pad, n1))],
        out_specs=[head_blk, pl.BlockSpec((1, 1, n_pad, tq), lambda i, h, j: (i, h, 0, j))],
        out_shape=[jax.ShapeDtypeStruct((b, N_HEADS, hd, s), F32),
                   jax.ShapeDtypeStruct((b, N_KV_HEADS, n_pad, s), MXU_DTYPE)],
        compiler_params=_params(("parallel", "parallel", "parallel")),
        name="cmp_attention_select",
    )(qt, k_cmp, v_cmp_t, ov_t)


def _selwin_body(q_ref, mb_ref, ks_ref, vs_ref, kw_ref, vw_ref, oc_ref, gate_ref, y_ref, qa_ref):
    tq = q_ref.shape[3]
    tk = vs_ref.shape[4]
    qi = pl.program_id(2)
    start = qi * tq
    rows = GQA_GROUP * tq
    qt = _stack_heads(q_ref)
    t_row = _query_positions(start, tq)
    group_shift = (SEL_GROUP * SEL_BLOCK // tk).bit_length() - 1
    n_groups = mb_ref.shape[2] // SEL_GROUP

    for grp in range(n_groups):
        bias_rows = mb_ref[0, 0, grp * SEL_GROUP:(grp + 1) * SEL_GROUP, :]
        qa_ref[grp] = jnp.concatenate([qt, jnp.concatenate([bias_rows] * GQA_GROUP, axis=1)], axis=0)

    def tile(kt, causal):
        k = ks_ref[0, 0, pl.ds(pl.multiple_of(kt * tk, tk), tk), :]
        s = _mm(k, qa_ref[lax.shift_right_logical(kt, group_shift)])
        if causal:
            kp = kt * tk + lax.broadcasted_iota(jnp.int32, (tk, 1), 0)
            s = jnp.where(kp <= t_row, s, NEG_INF)
        m_t = jnp.max(s, axis=0, keepdims=True)
        p = jnp.exp2(s - m_t).astype(MXU_DTYPE)
        return m_t, _mm(vs_ref[0, 0, kt], p)

    def merge(carry, parts):
        m, acc = carry
        m_new = m
        for m_t, _ in parts:
            m_new = jnp.maximum(m_new, m_t)
        acc = acc * jnp.exp2(m - m_new)
        for m_t, o_t in parts:
            acc = acc + o_t * jnp.exp2(m_t - m_new)
        return m_new, acc

    carry = (jnp.full((1, rows), M_INIT, F32), jnp.zeros((V_ROWS, rows), F32))
    n_full = qi
    carry = lax.fori_loop(0, lax.shift_right_logical(n_full, 1),
                          lambda i, c: merge(c, [tile(2 * i, False), tile(2 * i + 1, False)]), carry)
    carry = lax.cond((n_full & 1) == 1, lambda c: merge(c, [tile(n_full - 1, False)]), lambda c: c, carry)
    _, acc = merge(carry, [tile(n_full, True)])
    o_sel = acc[:HEAD_DIM] * (1.0 / acc[HEAD_DIM:HEAD_DIM + 1])

    span = WINDOW + tq
    kwin = kw_ref[0, 0, pl.ds(pl.multiple_of(start, tq), span), :]
    s = _mm(kwin, qt)
    kp = start - WINDOW + lax.broadcasted_iota(jnp.int32, (span, 1), 0)
    diff = t_row - kp
    mask = (diff >= 0) & (diff < WINDOW) & (kp >= 0)
    s = jnp.where(mask, s, NEG_INF)
    m = jnp.max(s, axis=0, keepdims=True)
    p = jnp.where(mask, jnp.exp2(s - m), 0.0).astype(MXU_DTYPE)
    vwin = jnp.concatenate([vw_ref[0, 0, qi + i] for i in range(span // tq)], axis=1)
    accw = _mm(vwin, p)
    o_win = accw[:HEAD_DIM] * (1.0 / accw[HEAD_DIM:HEAD_DIM + 1])

    for g in range(GQA_GROUP):
        cols = slice(g * tq, (g + 1) * tq)
        gates = [gate_ref[0, 0, N_BRANCH * g + br:N_BRANCH * g + br + 1, :] for br in range(N_BRANCH)]
        y = gates[0] * oc_ref[0, g] + gates[1] * o_sel[:, cols] + gates[2] * o_win[:, cols]
        y_ref[0, g] = y.astype(y_ref.dtype)


def _sel_win_attention(qt, mb_t, ks_aug, vs_tiles, kw_pad, vw_tiles, oc_t, gates_t):
    b, _, hd, s = qt.shape
    tq = vs_tiles.shape[4]
    n_pad = mb_t.shape[2]
    head_blk = pl.BlockSpec((1, GQA_GROUP, hd, tq), lambda i, h, j: (i, h, 0, j))
    whole = lambda a: pl.BlockSpec((1, 1) + a.shape[2:], lambda i, h, j: (i, h) + (0,) * (a.ndim - 2),
                                   pipeline_mode=pl.Buffered(1))
    qa_scratch = pltpu.VMEM((n_pad // SEL_GROUP, hd + SEL_GROUP, GQA_GROUP * tq), MXU_DTYPE)
    return pl.pallas_call(
        _selwin_body,
        grid=(b, N_KV_HEADS, s // tq),
        in_specs=[head_blk,
                  pl.BlockSpec((1, 1, n_pad, tq), lambda i, h, j: (i, h, 0, j)),
                  whole(ks_aug), whole(vs_tiles), whole(kw_pad), whole(vw_tiles),
                  head_blk,
                  pl.BlockSpec((1, 1, GQA_GROUP * N_BRANCH, tq), lambda i, h, j: (i, h, 0, j))],
        out_specs=head_blk,
        out_shape=jax.ShapeDtypeStruct((b, N_HEADS, hd, s), MXU_DTYPE),
        scratch_shapes=[qa_scratch],
        compiler_params=_params(("parallel", "parallel", "arbitrary")),
        name="sel_win_attention",
    )(qt, mb_t, ks_aug, vs_tiles, kw_pad, vw_tiles, oc_t, gates_t)


def _outproj_body(h_ref, ypl_ref, yat_ref, w1_ref, w2_ref, g_ref, o_ref):
    m = _mm(ypl_ref[...], w1_ref[...]) + _mm(yat_ref[...], w2_ref[...])
    o_ref[...] = h_ref[...] + _rms(m, g_ref[...])


def _outproj(h2, ypl, yat, w1, w2, g_post):
    t, d = h2.shape
    tm = min(ROW_TILE, t)
    row = lambda w: pl.BlockSpec((tm, w), lambda i: (i, 0))
    return pl.pallas_call(
        _outproj_body,
        grid=(t // tm,),
        in_specs=[row(d), row(ypl.shape[1]), row(yat.shape[1]),
                  _const_spec(w1.shape), _const_spec(w2.shape), _const_spec((1, d))],
        out_specs=row(d),
        out_shape=jax.ShapeDtypeStruct((t, d), F32),
        compiler_params=_params(("parallel",)),
        name="mixer_out_proj",
    )(h2, ypl, yat, w1, w2, g_post[None, :])


def _block_diag(blocks):
    n, a, b = blocks.shape
    eye = jnp.eye(n, dtype=blocks.dtype)
    return jnp.einsum("nab,nm->namb", blocks, eye).reshape(n * a, n * b)


def _compress_weights(w, pe):
    eye = jnp.eye(N_KV_HEADS, dtype=w.dtype)
    halves = []
    for part in range(CMP_LEN // CMP_STRIDE):
        wl = w[part * CMP_STRIDE:(part + 1) * CMP_STRIDE]
        pel = pe[part * CMP_STRIDE:(part + 1) * CMP_STRIDE]
        wm = jnp.einsum("lde,hg->lhdge", wl, eye).reshape(CMP_STRIDE * N_KV_HEADS * HEAD_DIM,
                                                           N_KV_HEADS * HEAD_DIM)
        pm = jnp.tile(pel[:, None, :], (1, N_KV_HEADS, 1)).reshape(1, -1)
        halves.append((wm.astype(MXU_DTYPE), pm))
    return halves


def _split_heads_t(x, b, s):
    return x.reshape(b, s, N_KV_HEADS, HEAD_DIM).transpose(0, 2, 3, 1)


def _value_tiles(v_t, tile):
    b, kv, hd, n = v_t.shape
    ones = jnp.ones((b, kv, 1, n), v_t.dtype)
    zeros = jnp.zeros((b, kv, V_ROWS - hd - 1, n), v_t.dtype)
    full = jnp.concatenate([v_t, ones, zeros], axis=2)
    return full.reshape(b, kv, V_ROWS, n // tile, tile).transpose(0, 1, 3, 2, 4)


def kernel(x, positions, ffn1_pre_g, ffn1_post_g, ffn1_w_gate, ffn1_w_up, ffn1_w_down, mix_pre_g, mix_post_g,
           w_in, w_out, pool_w, pool_scale, conv_w, conv_b, lru_w_r, lru_b_r, lru_w_i, lru_b_i, lru_lambda,
           cmp_w_k, cmp_w_v, cmp_pe, ffn2_pre_g, ffn2_post_g, ffn2_w_gate, ffn2_w_up, ffn2_w_down):
    b, s, d = x.shape
    depth = w_in.shape[0]
    t = b * s
    pool_width = pool_w.shape[1] * pool_w.shape[2]
    lru_width = lru_w_r.shape[1] * lru_w_r.shape[2]
    attn_width = N_HEADS * HEAD_DIM
    kv_width = N_KV_HEADS * HEAD_DIM
    assert pool_width == lru_width and s % SEL_TILE == 0 and s % (CMP_STRIDE * 8) == 0
    assert WINDOW % SEL_TILE == 0 and (SEL_GROUP * SEL_BLOCK) % (2 * SEL_TILE) == 0

    sizes = [("xpl", pool_width + 2 * lru_width), ("q", attn_width), ("kc", kv_width), ("vc", kv_width),
             ("ks", kv_width), ("vs", kv_width), ("kw", kv_width), ("vw", kv_width), ("g", V7X_LANES)]
    cols, off = {}, 0
    for name, width in sizes:
        cols[name] = (off, off + width)
        off += width
    n_gate = N_BRANCH * N_HEADS

    n_sel = s // SEL_BLOCK
    n_pad = -(-n_sel // SEL_GROUP) * SEL_GROUP
    n1 = s // CMP_STRIDE
    n_cmp = (s - CMP_LEN) // CMP_STRIDE + 1
    cmp_start = jnp.arange(n1) * CMP_STRIDE
    sel_start = jnp.arange(n_pad) * SEL_BLOCK
    overlap_t = ((cmp_start[None, :] < sel_start[:, None] + SEL_BLOCK) &
                 (cmp_start[None, :] + CMP_LEN > sel_start[:, None]) &
                 (jnp.arange(n1)[None, :] < n_cmp) & (jnp.arange(n_pad)[:, None] < n_sel)).astype(MXU_DTYPE)
    key_blk = (jnp.arange(s) // SEL_BLOCK) % SEL_GROUP
    key_onehot = (key_blk[:, None] == jnp.arange(SEL_GROUP)[None, :]).astype(MXU_DTYPE)

    cos, sin = _rope_tables(positions)
    cos2 = cos.reshape(t, V7X_LANES)
    sin2 = sin.reshape(t, V7X_LANES)

    h = x.reshape(t, d)
    for l in range(depth):
        cast = lambda w: w[l].astype(MXU_DTYPE)
        h = _ffn(h, ffn1_pre_g[l], ffn1_post_g[l], cast(ffn1_w_gate), cast(ffn1_w_up), cast(ffn1_w_down))

        w_all = jnp.pad(w_in[l], ((0, 0), (0, off - w_in.shape[2]))).astype(MXU_DTYPE)
        xpl, q, kc, vc, ks, vs, kw, vw, gates = _inproj(h, mix_pre_g[l], w_all, cols, cos2, sin2)

        w_ri = jnp.concatenate([_block_diag(lru_w_r[l]), _block_diag(lru_w_i[l])], axis=1).astype(MXU_DTYPE)
        ypl = _poollru(xpl.reshape(b, s, -1), _block_diag(pool_w[l]).astype(MXU_DTYPE), pool_scale[l],
                       conv_w[l], conv_b[l], w_ri, jnp.concatenate([lru_b_r[l], lru_b_i[l]]), lru_lambda[l])

        (wk_a, pe_a), (wk_b, pe_b) = _compress_weights(cmp_w_k[l], cmp_pe[l])
        (wv_a, _), (wv_b, _) = _compress_weights(cmp_w_v[l], cmp_pe[l])
        k_cmp = _compress(kc.reshape(b, n1, -1), pe_a, pe_b, wk_a, wk_b)
        v_cmp = _compress(vc.reshape(b, n1, -1), pe_a, pe_b, wv_a, wv_b)
        k_cmp = k_cmp.reshape(b, n1, N_KV_HEADS, HEAD_DIM).transpose(0, 2, 1, 3)
        v_cmp_t = _split_heads_t(v_cmp, b, n1)

        qt = q.reshape(b, s, N_HEADS, HEAD_DIM).transpose(0, 2, 3, 1)
        oc_t, mb_t = _cmp_attention(qt, k_cmp, v_cmp_t, overlap_t, n_sel)

        ks_h = ks.reshape(b, s, N_KV_HEADS, HEAD_DIM).transpose(0, 2, 1, 3)
        ks_aug = jnp.concatenate(
            [ks_h, jnp.broadcast_to(key_onehot, (b, N_KV_HEADS, s, SEL_GROUP))], axis=3)
        vs_tiles = _value_tiles(_split_heads_t(vs, b, s), SEL_TILE)
        front = ((0, 0), (0, 0), (WINDOW, 0), (0, 0))
        kw_pad = jnp.pad(kw.reshape(b, s, N_KV_HEADS, HEAD_DIM).transpose(0, 2, 1, 3), front)
        vw_tiles = _value_tiles(jnp.pad(_split_heads_t(vw, b, s), ((0, 0), (0, 0), (0, 0), (WINDOW, 0))),
                                SEL_TILE)
        gates_t = gates.reshape(b, s, -1)[:, :, :n_gate].transpose(0, 2, 1).reshape(
            b, N_KV_HEADS, GQA_GROUP * N_BRANCH, s)
        y_t = _sel_win_attention(qt, mb_t, ks_aug, vs_tiles, kw_pad, vw_tiles, oc_t, gates_t)
        yat = y_t.transpose(0, 3, 1, 2).reshape(t, attn_width)

        w_o = w_out[l].astype(MXU_DTYPE)
        split = pool_width + lru_width
        h = _outproj(h, ypl.reshape(t, -1), yat, w_o[:split], w_o[split:], mix_post_g[l])

        h = _ffn(h, ffn2_pre_g[l], ffn2_post_g[l], cast(ffn2_w_gate), cast(ffn2_w_up), cast(ffn2_w_down))
    return h.reshape(b, s, d)
```

```python
import functools

import jax
import jax.numpy as jnp
from jax import lax
from jax.experimental import pallas as pl
from jax.experimental.pallas import tpu as pltpu

F32 = jnp.float32
MXU_DTYPE = jnp.bfloat16

POOL_WINDOWS = (2, 4, 8, 16)
POOL_GROUP = 64
LRU_HEAD_DIM = 64
LRU_C = 8.0
CONV_WIDTH = 4
HEAD_DIM = 64
N_KV_HEADS = 2
GQA_GROUP = 4
N_HEADS = N_KV_HEADS * GQA_GROUP
N_BRANCH = 3
CMP_LEN = 32
CMP_STRIDE = 16
SEL_BLOCK = 64
SEL_TOPK = 16
WINDOW = 512
ROPE_THETA = 10000.0
NORM_EPS = 1e-6
NEG_INF = -1e30
BIG_SCORE = 1e9

V7X_LANES = 128
V7X_VMEM_LIMIT_BYTES = 56 * 1024 * 1024

ROW_TILE = 512
SEQ_TILE = 512
CMP_Q_TILE = 128
SEL_TILE = 256
SEL_GROUP = 128
V_ROWS = 80
M_INIT = -1e20
STREAM_TILES = 4
EXP_HEADROOM = 64.0
Q_SCALE = HEAD_DIM ** -0.5 * 1.4426950408889634


def _params(semantics):
    return pltpu.CompilerParams(dimension_semantics=semantics, vmem_limit_bytes=V7X_VMEM_LIMIT_BYTES)


def _rms(x, g):
    return x * lax.rsqrt(jnp.mean(x * x, axis=-1, keepdims=True) + NORM_EPS) * g


def _sigmoid(x):
    return 1.0 / (1.0 + jnp.exp(-x))


def _mm(a, b):
    return jnp.dot(a, b, preferred_element_type=F32)


def _const_spec(shape):
    zeros = (0,) * len(shape)
    return pl.BlockSpec(shape, lambda *_: zeros, pipeline_mode=pl.Buffered(1))


def _rope_body(pos_ref, inv_ref, cos_ref, sin_ref):
    ang = pos_ref[0].astype(F32) * inv_ref[...]
    lane = lax.broadcasted_iota(jnp.int32, ang.shape, 1)
    cos_ref[0] = jnp.cos(ang)
    sin_ref[0] = jnp.where((lane & (HEAD_DIM - 1)) < HEAD_DIM // 2, -jnp.sin(ang), jnp.sin(ang))


def _rope_tables(positions):
    b, s = positions.shape
    inv = ROPE_THETA ** (-jnp.arange(0, HEAD_DIM, 2, dtype=F32) / HEAD_DIM)
    inv_row = jnp.tile(inv, V7X_LANES // (HEAD_DIM // 2))[None, :]
    ts = min(SEQ_TILE, s)
    out = jax.ShapeDtypeStruct((b, s, V7X_LANES), F32)
    return pl.pallas_call(
        _rope_body,
        grid=(b, s // ts),
        in_specs=[pl.BlockSpec((1, ts, 1), lambda i, j: (i, j, 0)),
                  pl.BlockSpec((1, V7X_LANES), lambda i, j: (0, 0))],
        out_specs=[pl.BlockSpec((1, ts, V7X_LANES), lambda i, j: (i, j, 0))] * 2,
        out_shape=[out, out],
        compiler_params=_params(("parallel", "parallel")),
        name="rope_tables",
    )(positions[:, :, None], inv_row)


def _ffn_body(h_ref, gpre_ref, gpost_ref, wg_ref, wu_ref, wd_ref, o_ref):
    h = h_ref[...]
    xn = _rms(h, gpre_ref[...]).astype(MXU_DTYPE)
    gate = _mm(xn, wg_ref[...])
    up = _mm(xn, wu_ref[...])
    act = (gate * _sigmoid(gate) * up).astype(MXU_DTYPE)
    f = _mm(act, wd_ref[...])
    o_ref[...] = h + 0.5 * _rms(f, gpost_ref[...])


def _ffn(h2, g_pre, g_post, w_gate, w_up, w_down):
    t, d = h2.shape
    dff = w_gate.shape[1]
    tm = min(ROW_TILE, t)
    return pl.pallas_call(
        _ffn_body,
        grid=(t // tm,),
        in_specs=[pl.BlockSpec((tm, d), lambda i: (i, 0)),
                  _const_spec((1, d)), _const_spec((1, d)),
                  _const_spec((d, dff)), _const_spec((d, dff)), _const_spec((dff, d))],
        out_specs=pl.BlockSpec((tm, d), lambda i: (i, 0)),
        out_shape=jax.ShapeDtypeStruct((t, d), F32),
        compiler_params=_params(("parallel",)),
        name="ffn",
    )(h2, g_pre[None, :], g_post[None, :], w_gate, w_up, w_down)


def _swap_halves(x):
    n = x.shape[1]
    lane = lax.broadcasted_iota(jnp.int32, x.shape, 1)
    first_half = (lane & (HEAD_DIM - 1)) < HEAD_DIM // 2
    return jnp.where(first_half, pltpu.roll(x, n - HEAD_DIM // 2, 1), pltpu.roll(x, HEAD_DIM // 2, 1))


def _inproj_body(cols, h_ref, g_ref, w_ref, cos_ref, sin_ref,
                 xpl_ref, q_ref, kc_ref, vc_ref, ks_ref, vs_ref, kw_ref, vw_ref, gate_ref):
    xn = _rms(h_ref[...], g_ref[...]).astype(MXU_DTYPE)
    proj = _mm(xn, w_ref[...])
    cos = cos_ref[...]
    sin = sin_ref[...]

    def rope(x):
        rep = x.shape[1] // V7X_LANES
        c = jnp.concatenate([cos] * rep, axis=1) if rep > 1 else cos
        s = jnp.concatenate([sin] * rep, axis=1) if rep > 1 else sin
        return x * c + _swap_halves(x) * s

    def seg(name):
        lo, hi = cols[name]
        return proj[:, lo:hi]

    xpl_ref[...] = seg("xpl")
    q_ref[...] = (rope(seg("q")) * Q_SCALE).astype(q_ref.dtype)
    kc_ref[...] = rope(seg("kc"))
    vc_ref[...] = seg("vc")
    ks_ref[...] = rope(seg("ks")).astype(ks_ref.dtype)
    vs_ref[...] = seg("vs").astype(vs_ref.dtype)
    kw_ref[...] = rope(seg("kw")).astype(kw_ref.dtype)
    vw_ref[...] = seg("vw").astype(vw_ref.dtype)
    gate_ref[...] = _sigmoid(seg("g"))


def _inproj(h2, g_pre, w_all, cols, cos2, sin2):
    t, d = h2.shape
    tm = min(ROW_TILE, t)
    ncol = w_all.shape[1]
    widths = [("xpl", F32), ("q", MXU_DTYPE), ("kc", F32), ("vc", F32), ("ks", MXU_DTYPE),
              ("vs", MXU_DTYPE), ("kw", MXU_DTYPE), ("vw", MXU_DTYPE), ("g", F32)]
    row = lambda w: pl.BlockSpec((tm, w), lambda i: (i, 0))
    out_shape = [jax.ShapeDtypeStruct((t, cols[n][1] - cols[n][0]), dt) for n, dt in widths]
    out_specs = [row(cols[n][1] - cols[n][0]) for n, _ in widths]
    return pl.pallas_call(
        functools.partial(_inproj_body, cols),
        grid=(t // tm,),
        in_specs=[row(d), _const_spec((1, d)), _const_spec((d, ncol)), row(V7X_LANES), row(V7X_LANES)],
        out_specs=out_specs,
        out_shape=out_shape,
        compiler_params=_params(("parallel",)),
        name="mixer_in_proj",
    )(h2, g_pre[None, :], w_all, cos2, sin2)


def _poollru_body(x_ref, pw_ref, pscale_ref, cw_ref, cb_ref, wri_ref, bri_ref, lam_ref,
                  y_ref, pool_carry, conv_carry, h_carry):
    si = pl.program_id(1)
    ts = x_ref.shape[1]
    width = pw_ref.shape[0]
    halo_p = pool_carry.shape[0]
    halo_c = conv_carry.shape[0]

    @pl.when(si == 0)
    def _():
        pool_carry[...] = jnp.zeros_like(pool_carry)
        conv_carry[...] = jnp.zeros_like(conv_carry)
        h_carry[...] = jnp.zeros_like(h_carry)

    x = x_ref[0]
    xp = x[:, :width]
    xl = x[:, width:2 * width]
    gl = x[:, 2 * width:]

    ext = jnp.concatenate([pool_carry[...], xp], axis=0)
    sums = [ext]
    shift = 1
    for _ in POOL_WINDOWS:
        sums.append(sums[-1] + pltpu.roll(sums[-1], shift, 0))
        shift *= 2
    lane = lax.broadcasted_iota(jnp.int32, (1, width), 1)
    grp = lax.shift_right_logical(lane, POOL_GROUP.bit_length() - 1)
    win_sum = sums[len(POOL_WINDOWS)]
    win = jnp.full((1, width), float(POOL_WINDOWS[-1]), F32)
    for gi in range(len(POOL_WINDOWS) - 2, -1, -1):
        win_sum = jnp.where(grp == gi, sums[gi + 1], win_sum)
        win = jnp.where(grp == gi, float(POOL_WINDOWS[gi]), win)
    win_sum = win_sum[halo_p:]
    t_abs = si * ts + lax.broadcasted_iota(jnp.int32, (ts, 1), 0)
    cnt = jnp.minimum((t_abs + 1).astype(F32), win)
    pooled = win_sum / cnt
    y_pool = _mm((pooled - xp).astype(MXU_DTYPE), pw_ref[...]) * pscale_ref[...]
    pool_carry[...] = xp[ts - halo_p:]

    extc = jnp.concatenate([conv_carry[...], xl], axis=0)
    xc = extc * cw_ref[CONV_WIDTH - 1:CONV_WIDTH, :]
    for k in range(1, CONV_WIDTH):
        xc = xc + pltpu.roll(extc, k, 0) * cw_ref[CONV_WIDTH - 1 - k:CONV_WIDTH - k, :]
    xc = xc[halo_c:] + cb_ref[...]
    conv_carry[...] = xl[ts - halo_c:]

    ri = _mm(xc.astype(MXU_DTYPE), wri_ref[...]) + bri_ref[...]
    r = _sigmoid(ri[:, :width])
    i_gate = _sigmoid(ri[:, width:])
    neg_lam = -lam_ref[...]
    softplus = jnp.maximum(neg_lam, 0.0) + jnp.log1p(jnp.exp(-jnp.abs(neg_lam)))
    log_a = -LRU_C * r * softplus
    a = jnp.exp(log_a)
    b = jnp.sqrt(-jnp.tanh(log_a) * (a * a + 1.0)) * (i_gate * xc)

    row = lax.broadcasted_iota(jnp.int32, (ts, 1), 0)
    k = 1
    while k < ts:
        keep = row >= k
        a_prev = jnp.where(keep, pltpu.roll(a, k, 0), 1.0)
        b_prev = jnp.where(keep, pltpu.roll(b, k, 0), 0.0)
        b = a * b_prev + b
        a = a * a_prev
        k *= 2
    h = a * h_carry[0:1, :] + b
    h_carry[...] = jnp.broadcast_to(h[ts - 1:ts, :], h_carry.shape)

    gelu = 0.5 * gl * (1.0 + jnp.tanh(0.7978845608028654 * (gl + 0.044715 * gl * gl * gl)))
    y_ref[0] = jnp.concatenate([y_pool, h * gelu], axis=1).astype(y_ref.dtype)


def _poollru(xpl, pool_w_bd, pool_scale, conv_w, conv_b, w_ri_bd, b_ri, lam):
    b, s, w3 = xpl.shape
    width = w3 // 3
    ts = min(SEQ_TILE, s)
    return pl.pallas_call(
        _poollru_body,
        grid=(b, s // ts),
        in_specs=[pl.BlockSpec((1, ts, w3), lambda i, j: (i, j, 0)),
                  _const_spec((width, width)), _const_spec((1, width)),
                  _const_spec((CONV_WIDTH, width)), _const_spec((1, width)),
                  _const_spec((width, 2 * width)), _const_spec((1, 2 * width)), _const_spec((1, width))],
        out_specs=pl.BlockSpec((1, ts, 2 * width), lambda i, j: (i, j, 0)),
        out_shape=jax.ShapeDtypeStruct((b, s, 2 * width), MXU_DTYPE),
        scratch_shapes=[pltpu.VMEM((POOL_WINDOWS[-1], width), F32),
                        pltpu.VMEM((8, width), F32),
                        pltpu.VMEM((8, width), F32)],
        compiler_params=_params(("parallel", "arbitrary")),
        name="pool_rglru",
    )(xpl, pool_w_bd, pool_scale[None, :], conv_w, conv_b[None, :], w_ri_bd, b_ri[None, :], lam[None, :])


def _compress_body(r_ref, pea_ref, peb_ref, wa_ref, wb_ref, o_ref):
    r = r_ref[0]
    first = _mm((r + pea_ref[...]).astype(MXU_DTYPE), wa_ref[...])
    second = _mm((r + peb_ref[...]).astype(MXU_DTYPE), wb_ref[...])
    n1 = r.shape[0]
    o_ref[0] = (first + pltpu.roll(second, n1 - 1, 0)).astype(o_ref.dtype)


def _compress(rows, pe_a, pe_b, w_a, w_b):
    b, n1, k = rows.shape
    n_out = w_a.shape[1]
    return pl.pallas_call(
        _compress_body,
        grid=(b,),
        in_specs=[pl.BlockSpec((1, n1, k), lambda i: (i, 0, 0)),
                  _const_spec((1, k)), _const_spec((1, k)), _const_spec((k, n_out)), _const_spec((k, n_out))],
        out_specs=pl.BlockSpec((1, n1, n_out), lambda i: (i, 0, 0)),
        out_shape=jax.ShapeDtypeStruct((b, n1, n_out), MXU_DTYPE),
        compiler_params=_params(("parallel",)),
        name="compress_kv",
    )(rows, pe_a, pe_b, w_a, w_b)


def _stack_heads(q_ref):
    return jnp.concatenate([q_ref[0, g] for g in range(GQA_GROUP)], axis=1)


def _query_positions(start, tq):
    lane = lax.broadcasted_iota(jnp.int32, (1, GQA_GROUP * tq), 1)
    return start + (lane & (tq - 1))


def _cmp_body(n_sel, q_ref, kc_ref, vct_ref, ovt_ref, oc_ref, mb_ref):
    tq = q_ref.shape[3]
    start = pl.program_id(2) * tq
    qt = _stack_heads(q_ref)
    kc = kc_ref[0, 0]
    n1 = kc.shape[0]
    t_row = _query_positions(start, tq)

    s = _mm(kc, qt)
    n_idx = lax.broadcasted_iota(jnp.int32, (n1, 1), 0)
    mask = n_idx * CMP_STRIDE + (CMP_LEN - 1) <= t_row
    s = jnp.where(mask, s, NEG_INF)
    m = jnp.max(s, axis=0, keepdims=True)
    p = jnp.exp2(s - jnp.where(m > 0.5 * NEG_INF, m, 0.0))
    den = jnp.sum(p, axis=0, keepdims=True)
    pn = p * (1.0 / jnp.where(den > 0.0, den, 1.0))
    oc = _mm(vct_ref[0, 0], pn.astype(MXU_DTYPE))
    psum = pn[:, 0:tq]
    for g in range(GQA_GROUP):
        oc_ref[0, g] = oc[:, g * tq:(g + 1) * tq]
        if g:
            psum = psum + pn[:, g * tq:(g + 1) * tq]

    ovt = ovt_ref[...]
    hi = psum.astype(MXU_DTYPE)
    rem = psum - hi.astype(F32)
    mid = rem.astype(MXU_DTYPE)
    low = (rem - mid.astype(F32)).astype(MXU_DTYPE)
    imp = _mm(ovt, hi) + _mm(ovt, mid) + _mm(ovt, low)

    n_pad = imp.shape[0]
    j = lax.broadcasted_iota(jnp.int32, (n_pad, 1), 0).astype(F32)
    t_q = start + lax.broadcasted_iota(jnp.int32, (1, tq), 1)
    cur = lax.shift_right_logical(t_q, SEL_BLOCK.bit_length() - 1).astype(F32)
    forced = (j == 0.0) | (j == cur) | (j == cur - 1.0)
    score = jnp.where(j <= cur, jnp.where(forced, BIG_SCORE, imp), -BIG_SCORE)
    score = jnp.where(j < float(n_sel), score, -jnp.inf)
    bias = jnp.full(score.shape, NEG_INF, F32)
    for _ in range(min(SEL_TOPK, n_sel)):
        best = jnp.max(score, axis=0, keepdims=True)
        first = jnp.min(jnp.where(score == best, j, float(n_pad)), axis=0, keepdims=True)
        pick = j == first
        bias = jnp.where(pick, 0.0, bias)
        score = jnp.where(pick, -jnp.inf, score)
    mb_ref[0, 0] = bias.astype(mb_ref.dtype)


def _cmp_attention(qt, k_cmp, v_cmp_t, ov_t, n_sel):
    b, _, hd, s = qt.shape
    n1 = k_cmp.shape[2]
    n_pad = ov_t.shape[0]
    tq = min(CMP_Q_TILE, s)
    head_blk = pl.BlockSpec((1, GQA_GROUP, hd, tq), lambda i, h, j: (i, h, 0, j))
    return pl.pallas_call(
        functools.partial(_cmp_body, n_sel),
        grid=(b, N_KV_HEADS, s // tq),
        in_specs=[head_blk,
                  pl.BlockSpec((1, 1, n1, hd), lambda i, h, j: (i, h, 0, 0)),
                  pl.BlockSpec((1, 1, hd, n1), lambda i, h, j: (i, h, 0, 0)),
                  _const_spec((n_pad, n1))],
        out_specs=[head_blk, pl.BlockSpec((1, 1, n_pad, tq), lambda i, h, j: (i, h, 0, j))],
        out_shape=[jax.ShapeDtypeStruct((b, N_HEADS, hd, s), F32),
                   jax.ShapeDtypeStruct((b, N_KV_HEADS, n_pad, s), MXU_DTYPE)],
        compiler_params=_params(("parallel", "parallel", "parallel")),
        name="cmp_attention_select",
    )(qt, k_cmp, v_cmp_t, ov_t)


def _selwin_body(q_ref, mb_ref, ks_ref, vs_ref, kw_ref, vw_ref, oc_ref, gate_ref, y_ref, qa_ref, acc_ref):
    tq = q_ref.shape[3]
    tk = vs_ref.shape[4]
    qi = pl.program_id(2)
    start = qi * tq
    rows = GQA_GROUP * tq
    qt = _stack_heads(q_ref)
    t_row = _query_positions(start, tq)
    group_shift = (SEL_GROUP * SEL_BLOCK // tk).bit_length() - 1
    n_groups = mb_ref.shape[2] // SEL_GROUP

    for grp in range(n_groups):
        bias_rows = mb_ref[0, 0, grp * SEL_GROUP:(grp + 1) * SEL_GROUP, :]
        qa_ref[grp] = jnp.concatenate([qt, jnp.concatenate([bias_rows] * GQA_GROUP, axis=1)], axis=0)

    def scores(kt, n_sub, causal):
        k = ks_ref[0, 0, pl.ds(pl.multiple_of(kt * tk, tk), n_sub * tk), :]
        s = _mm(k, qa_ref[lax.shift_right_logical(kt, group_shift)])
        if causal:
            kp = kt * tk + lax.broadcasted_iota(jnp.int32, (n_sub * tk, 1), 0)
            s = jnp.where(kp <= t_row, s, NEG_INF)
        return s

    def values(kt, n_sub):
        return jnp.concatenate([vs_ref[0, 0, kt + i] for i in range(n_sub)], axis=1)

    def col_max(s):
        return jnp.max(s, axis=0, keepdims=True)

    n_past = qi

    s_diag = scores(qi, 1, True)
    ref = col_max(s_diag)
    acc_ref[...] = _mm(values(qi, 1), jnp.exp2(s_diag - ref).astype(MXU_DTYPE))

    def stream(kt, n_sub, seen_max):
        s = scores(kt, n_sub, False)
        acc_ref[...] += _mm(values(kt, n_sub), jnp.exp2(s - ref).astype(MXU_DTYPE))
        return jnp.maximum(seen_max, col_max(s))

    n_big = lax.shift_right_logical(n_past, STREAM_TILES.bit_length() - 1)
    seen_max = lax.fori_loop(0, n_big, lambda i, c: stream(i * STREAM_TILES, STREAM_TILES, c), ref)
    seen_max = lax.fori_loop(n_big * STREAM_TILES, n_past, lambda i, c: stream(i, 1, c), seen_max)
    overflow = jnp.max(seen_max - ref) > EXP_HEADROOM

    @pl.when(overflow)
    def _():
        def tile(kt, causal):
            s = scores(kt, 1, causal)
            m_t = col_max(s)
            return m_t, _mm(values(kt, 1), jnp.exp2(s - m_t).astype(MXU_DTYPE))

        def merge(carry, part):
            m, acc = carry
            m_t, o_t = part
            m_new = jnp.maximum(m, m_t)
            return m_new, acc * jnp.exp2(m - m_new) + o_t * jnp.exp2(m_t - m_new)

        carry = (jnp.full((1, rows), M_INIT, F32), jnp.zeros((V_ROWS, rows), F32))
        carry = lax.fori_loop(0, n_past, lambda i, c: merge(c, tile(i, False)), carry)
        acc_ref[...] = merge(carry, tile(qi, True))[1]

    acc = acc_ref[...]
    o_sel = acc[:HEAD_DIM] * (1.0 / acc[HEAD_DIM:HEAD_DIM + 1])

    span = WINDOW + tq
    kwin = kw_ref[0, 0, pl.ds(pl.multiple_of(start, tq), span), :]
    s = _mm(kwin, qt)
    kp = start - WINDOW + lax.broadcasted_iota(jnp.int32, (span, 1), 0)
    oldest = jnp.maximum(t_row - WINDOW, -1)
    s = jnp.concatenate([jnp.where(kp[:WINDOW] > oldest, s[:WINDOW], NEG_INF),
                         jnp.where(kp[WINDOW:] <= t_row, s[WINDOW:], NEG_INF)], axis=0)
    m = jnp.max(s, axis=0, keepdims=True)
    p = jnp.exp2(s - m).astype(MXU_DTYPE)
    vwin = jnp.concatenate([vw_ref[0, 0, qi + i] for i in range(span // tq)], axis=1)
    accw = _mm(vwin, p)
    o_win = accw[:HEAD_DIM] * (1.0 / accw[HEAD_DIM:HEAD_DIM + 1])

    for g in range(GQA_GROUP):
        cols = slice(g * tq, (g + 1) * tq)
        gates = [gate_ref[0, 0, N_BRANCH * g + br:N_BRANCH * g + br + 1, :] for br in range(N_BRANCH)]
        y = gates[0] * oc_ref[0, g] + gates[1] * o_sel[:, cols] + gates[2] * o_win[:, cols]
        y_ref[0, g] = y.astype(y_ref.dtype)


def _sel_win_attention(qt, mb_t, ks_aug, vs_tiles, kw_pad, vw_tiles, oc_t, gates_t):
    b, _, hd, s = qt.shape
    tq = vs_tiles.shape[4]
    n_pad = mb_t.shape[2]
    head_blk = pl.BlockSpec((1, GQA_GROUP, hd, tq), lambda i, h, j: (i, h, 0, j))
    whole = lambda a: pl.BlockSpec((1, 1) + a.shape[2:], lambda i, h, j: (i, h) + (0,) * (a.ndim - 2),
                                   pipeline_mode=pl.Buffered(1))
    qa_scratch = pltpu.VMEM((n_pad // SEL_GROUP, hd + SEL_GROUP, GQA_GROUP * tq), MXU_DTYPE)
    return pl.pallas_call(
        _selwin_body,
        grid=(b, N_KV_HEADS, s // tq),
        in_specs=[head_blk,
                  pl.BlockSpec((1, 1, n_pad, tq), lambda i, h, j: (i, h, 0, j)),
                  whole(ks_aug), whole(vs_tiles), whole(kw_pad), whole(vw_tiles),
                  head_blk,
                  pl.BlockSpec((1, 1, GQA_GROUP * N_BRANCH, tq), lambda i, h, j: (i, h, 0, j))],
        out_specs=head_blk,
        out_shape=jax.ShapeDtypeStruct((b, N_HEADS, hd, s), MXU_DTYPE),
        scratch_shapes=[qa_scratch, pltpu.VMEM((V_ROWS, GQA_GROUP * tq), F32)],
        compiler_params=_params(("parallel", "parallel", "arbitrary")),
        name="sel_win_attention",
    )(qt, mb_t, ks_aug, vs_tiles, kw_pad, vw_tiles, oc_t, gates_t)


def _outproj_body(h_ref, ypl_ref, yat_ref, w1_ref, w2_ref, g_ref, o_ref):
    m = _mm(ypl_ref[...], w1_ref[...]) + _mm(yat_ref[...], w2_ref[...])
    o_ref[...] = h_ref[...] + _rms(m, g_ref[...])


def _outproj(h2, ypl, yat, w1, w2, g_post):
    t, d = h2.shape
    tm = min(ROW_TILE, t)
    row = lambda w: pl.BlockSpec((tm, w), lambda i: (i, 0))
    return pl.pallas_call(
        _outproj_body,
        grid=(t // tm,),
        in_specs=[row(d), row(ypl.shape[1]), row(yat.shape[1]),
                  _const_spec(w1.shape), _const_spec(w2.shape), _const_spec((1, d))],
        out_specs=row(d),
        out_shape=jax.ShapeDtypeStruct((t, d), F32),
        compiler_params=_params(("parallel",)),
        name="mixer_out_proj",
    )(h2, ypl, yat, w1, w2, g_post[None, :])


def _block_diag(blocks):
    n, a, b = blocks.shape
    eye = jnp.eye(n, dtype=blocks.dtype)
    return jnp.einsum("nab,nm->namb", blocks, eye).reshape(n * a, n * b)


def _compress_weights(w, pe):
    eye = jnp.eye(N_KV_HEADS, dtype=w.dtype)
    halves = []
    for part in range(CMP_LEN // CMP_STRIDE):
        wl = w[part * CMP_STRIDE:(part + 1) * CMP_STRIDE]
        pel = pe[part * CMP_STRIDE:(part + 1) * CMP_STRIDE]
        wm = jnp.einsum("lde,hg->lhdge", wl, eye).reshape(CMP_STRIDE * N_KV_HEADS * HEAD_DIM,
                                                           N_KV_HEADS * HEAD_DIM)
        pm = jnp.tile(pel[:, None, :], (1, N_KV_HEADS, 1)).reshape(1, -1)
        halves.append((wm.astype(MXU_DTYPE), pm))
    return halves


def _split_heads_t(x, b, s):
    return x.reshape(b, s, N_KV_HEADS, HEAD_DIM).transpose(0, 2, 3, 1)


def _value_tiles(v_t, tile):
    b, kv, hd, n = v_t.shape
    ones = jnp.ones((b, kv, 1, n), v_t.dtype)
    zeros = jnp.zeros((b, kv, V_ROWS - hd - 1, n), v_t.dtype)
    full = jnp.concatenate([v_t, ones, zeros], axis=2)
    return full.reshape(b, kv, V_ROWS, n // tile, tile).transpose(0, 1, 3, 2, 4)


def kernel(x, positions, ffn1_pre_g, ffn1_post_g, ffn1_w_gate, ffn1_w_up, ffn1_w_down, mix_pre_g, mix_post_g,
           w_in, w_out, pool_w, pool_scale, conv_w, conv_b, lru_w_r, lru_b_r, lru_w_i, lru_b_i, lru_lambda,
           cmp_w_k, cmp_w_v, cmp_pe, ffn2_pre_g, ffn2_post_g, ffn2_w_gate, ffn2_w_up, ffn2_w_down):
    b, s, d = x.shape
    depth = w_in.shape[0]
    t = b * s
    pool_width = pool_w.shape[1] * pool_w.shape[2]
    lru_width = lru_w_r.shape[1] * lru_w_r.shape[2]
    attn_width = N_HEADS * HEAD_DIM
    kv_width = N_KV_HEADS * HEAD_DIM
    assert pool_width == lru_width and s % SEL_TILE == 0 and s % (CMP_STRIDE * 8) == 0
    assert WINDOW % SEL_TILE == 0 and (SEL_GROUP * SEL_BLOCK) % (STREAM_TILES * SEL_TILE) == 0

    sizes = [("xpl", pool_width + 2 * lru_width), ("q", attn_width), ("kc", kv_width), ("vc", kv_width),
             ("ks", kv_width), ("vs", kv_width), ("kw", kv_width), ("vw", kv_width), ("g", V7X_LANES)]
    cols, off = {}, 0
    for name, width in sizes:
        cols[name] = (off, off + width)
        off += width
    n_gate = N_BRANCH * N_HEADS

    n_sel = s // SEL_BLOCK
    n_pad = -(-n_sel // SEL_GROUP) * SEL_GROUP
    n1 = s // CMP_STRIDE
    n_cmp = (s - CMP_LEN) // CMP_STRIDE + 1
    cmp_start = jnp.arange(n1) * CMP_STRIDE
    sel_start = jnp.arange(n_pad) * SEL_BLOCK
    overlap_t = ((cmp_start[None, :] < sel_start[:, None] + SEL_BLOCK) &
                 (cmp_start[None, :] + CMP_LEN > sel_start[:, None]) &
                 (jnp.arange(n1)[None, :] < n_cmp) & (jnp.arange(n_pad)[:, None] < n_sel)).astype(MXU_DTYPE)
    key_blk = (jnp.arange(s) // SEL_BLOCK) % SEL_GROUP
    key_onehot = (key_blk[:, None] == jnp.arange(SEL_GROUP)[None, :]).astype(MXU_DTYPE)

    cos, sin = _rope_tables(positions)
    cos2 = cos.reshape(t, V7X_LANES)
    sin2 = sin.reshape(t, V7X_LANES)

    h = x.reshape(t, d)
    for l in range(depth):
        cast = lambda w: w[l].astype(MXU_DTYPE)
        h = _ffn(h, ffn1_pre_g[l], ffn1_post_g[l], cast(ffn1_w_gate), cast(ffn1_w_up), cast(ffn1_w_down))

        w_all = jnp.pad(w_in[l], ((0, 0), (0, off - w_in.shape[2]))).astype(MXU_DTYPE)
        xpl, q, kc, vc, ks, vs, kw, vw, gates = _inproj(h, mix_pre_g[l], w_all, cols, cos2, sin2)

        w_ri = jnp.concatenate([_block_diag(lru_w_r[l]), _block_diag(lru_w_i[l])], axis=1).astype(MXU_DTYPE)
        ypl = _poollru(xpl.reshape(b, s, -1), _block_diag(pool_w[l]).astype(MXU_DTYPE), pool_scale[l],
                       conv_w[l], conv_b[l], w_ri, jnp.concatenate([lru_b_r[l], lru_b_i[l]]), lru_lambda[l])

        (wk_a, pe_a), (wk_b, pe_b) = _compress_weights(cmp_w_k[l], cmp_pe[l])
        (wv_a, _), (wv_b, _) = _compress_weights(cmp_w_v[l], cmp_pe[l])
        k_cmp = _compress(kc.reshape(b, n1, -1), pe_a, pe_b, wk_a, wk_b)
        v_cmp = _compress(vc.reshape(b, n1, -1), pe_a, pe_b, wv_a, wv_b)
        k_cmp = k_cmp.reshape(b, n1, N_KV_HEADS, HEAD_DIM).transpose(0, 2, 1, 3)
        v_cmp_t = _split_heads_t(v_cmp, b, n1)

        qt = q.reshape(b, s, N_HEADS, HEAD_DIM).transpose(0, 2, 3, 1)
        oc_t, mb_t = _cmp_attention(qt, k_cmp, v_cmp_t, overlap_t, n_sel)

        ks_h = ks.reshape(b, s, N_KV_HEADS, HEAD_DIM).transpose(0, 2, 1, 3)
        ks_aug = jnp.concatenate(
            [ks_h, jnp.broadcast_to(key_onehot, (b, N_KV_HEADS, s, SEL_GROUP))], axis=3)
        vs_tiles = _value_tiles(_split_heads_t(vs, b, s), SEL_TILE)
        front = ((0, 0), (0, 0), (WINDOW, 0), (0, 0))
        kw_pad = jnp.pad(kw.reshape(b, s, N_KV_HEADS, HEAD_DIM).transpose(0, 2, 1, 3), front)
        vw_tiles = _value_tiles(jnp.pad(_split_heads_t(vw, b, s), ((0, 0), (0, 0), (0, 0), (WINDOW, 0))),
                                SEL_TILE)
        gates_t = gates.reshape(b, s, -1)[:, :, :n_gate].transpose(0, 2, 1).reshape(
            b, N_KV_HEADS, GQA_GROUP * N_BRANCH, s)
        y_t = _sel_win_attention(qt, mb_t, ks_aug, vs_tiles, kw_pad, vw_tiles, oc_t, gates_t)
        yat = y_t.transpose(0, 3, 1, 2).reshape(t, attn_width)

        w_o = w_out[l].astype(MXU_DTYPE)
        split = pool_width + lru_width
        h = _outproj(h, ypl.reshape(t, -1), yat, w_o[:split], w_o[split:], mix_post_g[l])

        h = _ffn(h, ffn2_pre_g[l], ffn2_post_g[l], cast(ffn2_w_gate), cast(ffn2_w_up), cast(ffn2_w_down))
    return h.reshape(b, s, d)
```

```python
import functools

import jax
import jax.numpy as jnp
from jax import lax
from jax.experimental import pallas as pl
from jax.experimental.pallas import tpu as pltpu

F32 = jnp.float32
MXU_DTYPE = jnp.bfloat16

POOL_WINDOWS = (2, 4, 8, 16)
POOL_GROUP = 64
LRU_HEAD_DIM = 64
LRU_C = 8.0
CONV_WIDTH = 4
HEAD_DIM = 64
N_KV_HEADS = 2
GQA_GROUP = 4
N_HEADS = N_KV_HEADS * GQA_GROUP
KV_WIDTH = N_KV_HEADS * HEAD_DIM
N_BRANCH = 3
CMP_LEN = 32
CMP_STRIDE = 16
SEL_BLOCK = 64
SEL_TOPK = 16
WINDOW = 512
ROPE_THETA = 10000.0
NORM_EPS = 1e-6
NEG_INF = -1e30
BIG_SCORE = 1e9

V7X_LANES = 128
V7X_VMEM_LIMIT_BYTES = 56 * 1024 * 1024

ROW_TILE = 512
SEQ_TILE = 512
CMP_Q_TILE = 128
CMP_CHUNK = 256
SEL_TILE = 256
SEL_GROUP = 128
ONES_ROWS = 16
M_INIT = -1e20
STREAM_TILES = 4
EXP_HEADROOM = 64.0
Q_SCALE = HEAD_DIM ** -0.5 * 1.4426950408889634


def _params(semantics):
    return pltpu.CompilerParams(dimension_semantics=semantics, vmem_limit_bytes=V7X_VMEM_LIMIT_BYTES)


def _rms(x, g):
    return x * lax.rsqrt(jnp.mean(x * x, axis=-1, keepdims=True) + NORM_EPS) * g


def _sigmoid(x):
    return 1.0 / (1.0 + jnp.exp(-x))


def _mm(a, b):
    return jnp.dot(a, b, preferred_element_type=F32)


def _mm_tn(a_t, b):
    return lax.dot_general(a_t, b, (((0,), (0,)), ((), ())), preferred_element_type=F32)


def _log2(n):
    assert n & (n - 1) == 0
    return n.bit_length() - 1


def _const_spec(shape):
    zeros = (0,) * len(shape)
    return pl.BlockSpec(shape, lambda *_: zeros, pipeline_mode=pl.Buffered(1))


def _rope_body(pos_ref, inv_ref, cos_ref, sin_ref):
    ang = pos_ref[0].astype(F32) * inv_ref[...]
    lane = lax.broadcasted_iota(jnp.int32, ang.shape, 1)
    cos_ref[0] = jnp.cos(ang)
    sin_ref[0] = jnp.where((lane & (HEAD_DIM - 1)) < HEAD_DIM // 2, -jnp.sin(ang), jnp.sin(ang))


def _rope_tables(positions):
    b, s = positions.shape
    inv = ROPE_THETA ** (-jnp.arange(0, HEAD_DIM, 2, dtype=F32) / HEAD_DIM)
    inv_row = jnp.tile(inv, V7X_LANES // (HEAD_DIM // 2))[None, :]
    ts = min(SEQ_TILE, s)
    out = jax.ShapeDtypeStruct((b, s, V7X_LANES), F32)
    return pl.pallas_call(
        _rope_body,
        grid=(b, s // ts),
        in_specs=[pl.BlockSpec((1, ts, 1), lambda i, j: (i, j, 0)),
                  pl.BlockSpec((1, V7X_LANES), lambda i, j: (0, 0))],
        out_specs=[pl.BlockSpec((1, ts, V7X_LANES), lambda i, j: (i, j, 0))] * 2,
        out_shape=[out, out],
        compiler_params=_params(("parallel", "parallel")),
        name="rope_tables",
    )(positions[:, :, None], inv_row)


def _ffn_body(h_ref, gpre_ref, gpost_ref, wg_ref, wu_ref, wd_ref, o_ref):
    h = h_ref[...]
    xn = _rms(h, gpre_ref[...]).astype(MXU_DTYPE)
    gate = _mm(xn, wg_ref[...])
    up = _mm(xn, wu_ref[...])
    act = (gate * _sigmoid(gate) * up).astype(MXU_DTYPE)
    f = _mm(act, wd_ref[...])
    o_ref[...] = h + 0.5 * _rms(f, gpost_ref[...])


def _ffn(h2, g_pre, g_post, w_gate, w_up, w_down):
    t, d = h2.shape
    dff = w_gate.shape[1]
    tm = min(ROW_TILE, t)
    return pl.pallas_call(
        _ffn_body,
        grid=(t // tm,),
        in_specs=[pl.BlockSpec((tm, d), lambda i: (i, 0)),
                  _const_spec((1, d)), _const_spec((1, d)),
                  _const_spec((d, dff)), _const_spec((d, dff)), _const_spec((dff, d))],
        out_specs=pl.BlockSpec((tm, d), lambda i: (i, 0)),
        out_shape=jax.ShapeDtypeStruct((t, d), F32),
        compiler_params=_params(("parallel",)),
        name="ffn",
    )(h2, g_pre[None, :], g_post[None, :], w_gate, w_up, w_down)


def _swap_halves(x):
    n = x.shape[1]
    lane = lax.broadcasted_iota(jnp.int32, x.shape, 1)
    first_half = (lane & (HEAD_DIM - 1)) < HEAD_DIM // 2
    return jnp.where(first_half, pltpu.roll(x, n - HEAD_DIM // 2, 1), pltpu.roll(x, HEAD_DIM // 2, 1))


def _inproj_body(cols, h_ref, g_ref, w_ref, cos_ref, sin_ref,
                 xpl_ref, qt_ref, kc_ref, vc_ref, ks_ref, vst_ref, kw_ref, vwt_ref, gatet_ref):
    xn = _rms(h_ref[0], g_ref[...]).astype(MXU_DTYPE)
    proj = _mm(xn, w_ref[...])
    cos = cos_ref[0]
    sin = sin_ref[0]

    def rope(x):
        rep = x.shape[1] // V7X_LANES
        c = jnp.concatenate([cos] * rep, axis=1) if rep > 1 else cos
        s = jnp.concatenate([sin] * rep, axis=1) if rep > 1 else sin
        return x * c + _swap_halves(x) * s

    def seg(name):
        lo, hi = cols[name]
        return proj[:, lo:hi]

    xpl_ref[0] = seg("xpl")
    qt_ref[0] = (rope(seg("q")) * Q_SCALE).T.astype(qt_ref.dtype)
    kc_ref[0] = rope(seg("kc"))
    vc_ref[0] = seg("vc")
    ks_ref[0] = rope(seg("ks")).astype(ks_ref.dtype)
    vst_ref[0] = seg("vs").T.astype(vst_ref.dtype)
    kw_ref[0] = rope(seg("kw")).astype(kw_ref.dtype)
    vwt_ref[0] = seg("vw").T.astype(vwt_ref.dtype)
    gatet_ref[0] = _sigmoid(seg("g")).T[:gatet_ref.shape[1]]


def _inproj(h3, g_pre, w_all, cols, cos, sin):
    b, s, d = h3.shape
    tm = min(ROW_TILE, s)
    ncol = w_all.shape[1]
    width = lambda n: cols[n][1] - cols[n][0]
    row = lambda w: pl.BlockSpec((1, tm, w), lambda i, j: (i, j, 0))
    col = lambda w: pl.BlockSpec((1, w, tm), lambda i, j: (i, 0, j))
    tok = lambda n, dt: (jax.ShapeDtypeStruct((b, s, width(n)), dt), row(width(n)))
    chan = lambda w, dt: (jax.ShapeDtypeStruct((b, w, s), dt), col(w))
    outs = [tok("xpl", F32), chan(width("q"), MXU_DTYPE), tok("kc", F32), tok("vc", F32),
            tok("ks", MXU_DTYPE), chan(width("vs"), MXU_DTYPE), tok("kw", MXU_DTYPE),
            chan(width("vw"), MXU_DTYPE), chan(N_BRANCH * N_HEADS, F32)]
    return pl.pallas_call(
        functools.partial(_inproj_body, cols),
        grid=(b, s // tm),
        in_specs=[row(d), _const_spec((1, d)), _const_spec((d, ncol)), row(V7X_LANES), row(V7X_LANES)],
        out_specs=[o[1] for o in outs],
        out_shape=[o[0] for o in outs],
        compiler_params=_params(("parallel", "parallel")),
        name="mixer_in_proj",
    )(h3, g_pre[None, :], w_all, cos, sin)


def _poollru_body(x_ref, pw_ref, pscale_ref, cw_ref, cb_ref, wri_ref, bri_ref, lam_ref,
                  y_ref, pool_carry, conv_carry, h_carry):
    si = pl.program_id(1)
    ts = x_ref.shape[1]
    width = pw_ref.shape[0]
    halo_p = pool_carry.shape[0]
    halo_c = conv_carry.shape[0]

    @pl.when(si == 0)
    def _():
        pool_carry[...] = jnp.zeros_like(pool_carry)
        conv_carry[...] = jnp.zeros_like(conv_carry)
        h_carry[...] = jnp.zeros_like(h_carry)

    x = x_ref[0]
    xp = x[:, :width]
    xl = x[:, width:2 * width]
    gl = x[:, 2 * width:]

    ext = jnp.concatenate([pool_carry[...], xp], axis=0)
    sums = [ext]
    shift = 1
    for _ in POOL_WINDOWS:
        sums.append(sums[-1] + pltpu.roll(sums[-1], shift, 0))
        shift *= 2
    lane = lax.broadcasted_iota(jnp.int32, (1, width), 1)
    grp = lax.shift_right_logical(lane, _log2(POOL_GROUP))
    win_sum = sums[len(POOL_WINDOWS)]
    win = jnp.full((1, width), float(POOL_WINDOWS[-1]), F32)
    for gi in range(len(POOL_WINDOWS) - 2, -1, -1):
        win_sum = jnp.where(grp == gi, sums[gi + 1], win_sum)
        win = jnp.where(grp == gi, float(POOL_WINDOWS[gi]), win)
    win_sum = win_sum[halo_p:]
    t_abs = si * ts + lax.broadcasted_iota(jnp.int32, (ts, 1), 0)
    cnt = jnp.minimum((t_abs + 1).astype(F32), win)
    pooled = win_sum / cnt
    y_pool = _mm((pooled - xp).astype(MXU_DTYPE), pw_ref[...]) * pscale_ref[...]
    pool_carry[...] = xp[ts - halo_p:]

    extc = jnp.concatenate([conv_carry[...], xl], axis=0)
    xc = extc * cw_ref[CONV_WIDTH - 1:CONV_WIDTH, :]
    for k in range(1, CONV_WIDTH):
        xc = xc + pltpu.roll(extc, k, 0) * cw_ref[CONV_WIDTH - 1 - k:CONV_WIDTH - k, :]
    xc = xc[halo_c:] + cb_ref[...]
    conv_carry[...] = xl[ts - halo_c:]

    ri = _mm(xc.astype(MXU_DTYPE), wri_ref[...]) + bri_ref[...]
    r = _sigmoid(ri[:, :width])
    i_gate = _sigmoid(ri[:, width:])
    neg_lam = -lam_ref[...]
    softplus = jnp.maximum(neg_lam, 0.0) + jnp.log1p(jnp.exp(-jnp.abs(neg_lam)))
    log_a = -LRU_C * r * softplus
    a = jnp.exp(log_a)
    b = jnp.sqrt(-jnp.tanh(log_a) * (a * a + 1.0)) * (i_gate * xc)

    row = lax.broadcasted_iota(jnp.int32, (ts, 1), 0)
    k = 1
    while k < ts:
        keep = row >= k
        a_prev = jnp.where(keep, pltpu.roll(a, k, 0), 1.0)
        b_prev = jnp.where(keep, pltpu.roll(b, k, 0), 0.0)
        b = a * b_prev + b
        a = a * a_prev
        k *= 2
    h = a * h_carry[0:1, :] + b
    h_carry[...] = jnp.broadcast_to(h[ts - 1:ts, :], h_carry.shape)

    gelu = 0.5 * gl * (1.0 + jnp.tanh(0.7978845608028654 * (gl + 0.044715 * gl * gl * gl)))
    y_ref[0] = jnp.concatenate([y_pool, h * gelu], axis=1).astype(y_ref.dtype)


def _poollru(xpl, pool_w_bd, pool_scale, conv_w, conv_b, w_ri_bd, b_ri, lam):
    b, s, w3 = xpl.shape
    width = w3 // 3
    ts = min(SEQ_TILE, s)
    return pl.pallas_call(
        _poollru_body,
        grid=(b, s // ts),
        in_specs=[pl.BlockSpec((1, ts, w3), lambda i, j: (i, j, 0)),
                  _const_spec((width, width)), _const_spec((1, width)),
                  _const_spec((CONV_WIDTH, width)), _const_spec((1, width)),
                  _const_spec((width, 2 * width)), _const_spec((1, 2 * width)), _const_spec((1, width))],
        out_specs=pl.BlockSpec((1, ts, 2 * width), lambda i, j: (i, j, 0)),
        out_shape=jax.ShapeDtypeStruct((b, s, 2 * width), MXU_DTYPE),
        scratch_shapes=[pltpu.VMEM((POOL_WINDOWS[-1], width), F32),
                        pltpu.VMEM((8, width), F32),
                        pltpu.VMEM((8, width), F32)],
        compiler_params=_params(("parallel", "arbitrary")),
        name="pool_rglru",
    )(xpl, pool_w_bd, pool_scale[None, :], conv_w, conv_b[None, :], w_ri_bd, b_ri[None, :], lam[None, :])


def _compress_body(transpose_out, r_ref, pea_ref, peb_ref, wa_ref, wb_ref, o_ref):
    r = r_ref[0]
    first = _mm((r + pea_ref[...]).astype(MXU_DTYPE), wa_ref[...])
    second = _mm((r + peb_ref[...]).astype(MXU_DTYPE), wb_ref[...])
    n1 = r.shape[0]
    out = first + pltpu.roll(second, n1 - 1, 0)
    o_ref[0] = (out.T if transpose_out else out).astype(o_ref.dtype)


def _compress(rows, pe_a, pe_b, w_a, w_b, transpose_out):
    b, n1, k = rows.shape
    n_out = w_a.shape[1]
    out_dims = (n_out, n1) if transpose_out else (n1, n_out)
    return pl.pallas_call(
        functools.partial(_compress_body, transpose_out),
        grid=(b,),
        in_specs=[pl.BlockSpec((1, n1, k), lambda i: (i, 0, 0)),
                  _const_spec((1, k)), _const_spec((1, k)), _const_spec((k, n_out)), _const_spec((k, n_out))],
        out_specs=pl.BlockSpec((1,) + out_dims, lambda i: (i, 0, 0)),
        out_shape=jax.ShapeDtypeStruct((b,) + out_dims, MXU_DTYPE),
        compiler_params=_params(("parallel",)),
        name="compress_kv",
    )(rows, pe_a, pe_b, w_a, w_b)


def _stack_heads(q_ref):
    return jnp.concatenate([q_ref[0, g * HEAD_DIM:(g + 1) * HEAD_DIM, :] for g in range(GQA_GROUP)], axis=1)


def _place_in_kv_half(qt, kvh):
    zeros = jnp.zeros_like(qt)
    parts = [jnp.where(kvh == h, qt, zeros) for h in range(N_KV_HEADS)]
    return jnp.concatenate(parts, axis=0)


def _query_positions(start, tq):
    lane = lax.broadcasted_iota(jnp.int32, (1, GQA_GROUP * tq), 1)
    return start + (lane & (tq - 1))


def _with_ones(v_t):
    return jnp.concatenate([v_t, jnp.ones((ONES_ROWS, v_t.shape[1]), v_t.dtype)], axis=0)


def _normalize(acc):
    return acc[:HEAD_DIM] * (1.0 / acc[HEAD_DIM:HEAD_DIM + 1])


def _cmp_body(n_sel, q_ref, kc_ref, vct_ref, ovt_ref, oc_ref, mb_ref, imp_ref):
    tq = q_ref.shape[2]
    kvh = pl.program_id(1)
    start = pl.program_id(2) * tq
    qt = _stack_heads(q_ref)
    qp = _place_in_kv_half(qt, kvh)
    n1 = kc_ref.shape[1]
    t_row = _query_positions(start, tq)
    assert CMP_STRIDE * CMP_CHUNK >= tq + CMP_LEN

    def attend(nrows):
        s = _mm(kc_ref[0, :nrows, :], qp)
        lo = max(0, nrows - 2 * CMP_CHUNK)
        n_idx = lo + lax.broadcasted_iota(jnp.int32, (nrows - lo, 1), 0)
        tail = jnp.where(n_idx * CMP_STRIDE + (CMP_LEN - 1) <= t_row, s[lo:], NEG_INF)
        s = jnp.concatenate([s[:lo], tail], axis=0) if lo else tail
        m = jnp.max(s, axis=0, keepdims=True)
        p = jnp.exp2(s - jnp.where(m > 0.5 * NEG_INF, m, 0.0))
        den = jnp.sum(p, axis=0, keepdims=True)
        pn = p * (1.0 / jnp.where(den > 0.0, den, 1.0))
        oc = _mm(vct_ref[0, :, :nrows], pn.astype(MXU_DTYPE))
        psum = pn[:, 0:tq]
        for g in range(GQA_GROUP):
            oc_ref[0, g * HEAD_DIM:(g + 1) * HEAD_DIM, :] = oc[:, g * tq:(g + 1) * tq]
            if g:
                psum = psum + pn[:, g * tq:(g + 1) * tq]
        ovt = ovt_ref[:, :nrows]
        hi = psum.astype(MXU_DTYPE)
        rem = psum - hi.astype(F32)
        mid = rem.astype(MXU_DTYPE)
        low = (rem - mid.astype(F32)).astype(MXU_DTYPE)
        imp_ref[...] = _mm(ovt, hi) + _mm(ovt, mid) + _mm(ovt, low)

    n_need = lax.shift_right_logical(start + tq, _log2(CMP_STRIDE)) - 1
    chunks = lax.shift_right_logical(n_need + CMP_CHUNK - 1, _log2(CMP_CHUNK))
    n_variants = -(-n1 // CMP_CHUNK)
    for c in range(1, n_variants + 1):
        pl.when(chunks == c)(functools.partial(attend, min(c * CMP_CHUNK, n1)))
    imp = imp_ref[...]

    n_pad = imp.shape[0]
    j = lax.broadcasted_iota(jnp.int32, (n_pad, 1), 0).astype(F32)
    t_q = start + lax.broadcasted_iota(jnp.int32, (1, tq), 1)
    cur = lax.shift_right_logical(t_q, _log2(SEL_BLOCK)).astype(F32)
    forced = (j == 0.0) | (j == cur) | (j == cur - 1.0)
    score = jnp.where(j <= cur, jnp.where(forced, BIG_SCORE, imp), -BIG_SCORE)
    score = jnp.where(j < float(n_sel), score, -jnp.inf)
    bias = jnp.full(score.shape, NEG_INF, F32)
    for _ in range(min(SEL_TOPK, n_sel)):
        best = jnp.max(score, axis=0, keepdims=True)
        first = jnp.min(jnp.where(score == best, j, float(n_pad)), axis=0, keepdims=True)
        pick = j == first
        bias = jnp.where(pick, 0.0, bias)
        score = jnp.where(pick, -jnp.inf, score)
    mb_ref[0, 0] = bias.astype(mb_ref.dtype)


def _cmp_attention(qt, k_cmp, v_cmp_t, ov_t, n_sel):
    b, _, s = qt.shape
    n1 = k_cmp.shape[1]
    n_pad = ov_t.shape[0]
    tq = min(CMP_Q_TILE, s)
    group_rows = GQA_GROUP * HEAD_DIM
    head_blk = pl.BlockSpec((1, group_rows, tq), lambda i, h, j: (i, h, j))
    return pl.pallas_call(
        functools.partial(_cmp_body, n_sel),
        grid=(b, N_KV_HEADS, s // tq),
        in_specs=[head_blk,
                  pl.BlockSpec((1, n1, KV_WIDTH), lambda i, h, j: (i, 0, 0)),
                  pl.BlockSpec((1, HEAD_DIM, n1), lambda i, h, j: (i, h, 0)),
                  _const_spec((n_pad, n1))],
        out_specs=[head_blk, pl.BlockSpec((1, 1, n_pad, tq), lambda i, h, j: (i, h, 0, j))],
        out_shape=[jax.ShapeDtypeStruct((b, N_HEADS * HEAD_DIM, s), F32),
                   jax.ShapeDtypeStruct((b, N_KV_HEADS, n_pad, s), MXU_DTYPE)],
        scratch_shapes=[pltpu.VMEM((n_pad, tq), F32)],
        compiler_params=_params(("parallel", "parallel", "parallel")),
        name="cmp_attention_select",
    )(qt, k_cmp, v_cmp_t, ov_t)


def _selwin_body(q_ref, mb_ref, ks_ref, oh_ref, vst_ref, kw_ref, vwt_ref, oc_ref, gate_ref, y_ref,
                 qa_ref, acc_ref, win_ref):
    tq = q_ref.shape[2]
    tk = tq
    kvh = pl.program_id(1)
    qi = pl.program_id(2)
    start = qi * tq
    rows = GQA_GROUP * tq
    qt = _stack_heads(q_ref)
    qp = _place_in_kv_half(qt, kvh)
    t_row = _query_positions(start, tq)
    group_shift = _log2(SEL_GROUP * SEL_BLOCK // tk)
    n_groups = mb_ref.shape[2] // SEL_GROUP

    for grp in range(n_groups):
        bias_rows = mb_ref[0, 0, grp * SEL_GROUP:(grp + 1) * SEL_GROUP, :]
        qa_ref[grp] = jnp.concatenate([jnp.concatenate([bias_rows] * GQA_GROUP, axis=1), qp], axis=0)

    def scores(kt, n_sub, causal):
        key_rows = pl.ds(pl.multiple_of(kt * tk, tk), n_sub * tk)
        k = jnp.concatenate([oh_ref[key_rows, :], ks_ref[0, key_rows, :]], axis=1)
        s = _mm(k, qa_ref[lax.shift_right_logical(kt, group_shift)])
        if causal:
            kp = kt * tk + lax.broadcasted_iota(jnp.int32, (n_sub * tk, 1), 0)
            s = jnp.where(kp <= t_row, s, NEG_INF)
        return s

    def values(kt, n_sub):
        return _with_ones(vst_ref[0, :, pl.ds(pl.multiple_of(kt * tk, tk), n_sub * tk)])

    def col_max(s):
        return jnp.max(s, axis=0, keepdims=True)

    n_past = qi

    s_diag = scores(qi, 1, True)
    ref = col_max(s_diag)
    acc_ref[...] = _mm(values(qi, 1), jnp.exp2(s_diag - ref).astype(MXU_DTYPE))

    def stream(kt, n_sub, seen_max):
        s = scores(kt, n_sub, False)
        acc_ref[...] += _mm(values(kt, n_sub), jnp.exp2(s - ref).astype(MXU_DTYPE))
        return jnp.maximum(seen_max, col_max(s))

    n_big = lax.shift_right_logical(n_past, _log2(STREAM_TILES))
    seen_max = lax.fori_loop(0, n_big, lambda i, c: stream(i * STREAM_TILES, STREAM_TILES, c), ref)
    seen_max = lax.fori_loop(n_big * STREAM_TILES, n_past, lambda i, c: stream(i, 1, c), seen_max)
    overflow = jnp.max(seen_max - ref) > EXP_HEADROOM

    @pl.when(overflow)
    def _():
        def tile(kt, causal):
            s = scores(kt, 1, causal)
            m_t = col_max(s)
            return m_t, _mm(values(kt, 1), jnp.exp2(s - m_t).astype(MXU_DTYPE))

        def merge(carry, part):
            m, acc = carry
            m_t, o_t = part
            m_new = jnp.maximum(m, m_t)
            return m_new, acc * jnp.exp2(m - m_new) + o_t * jnp.exp2(m_t - m_new)

        carry = (jnp.full((1, rows), M_INIT, F32), jnp.zeros(acc_ref.shape, F32))
        carry = lax.fori_loop(0, n_past, lambda i, c: merge(c, tile(i, False)), carry)
        acc_ref[...] = merge(carry, tile(qi, True))[1]

    o_sel = _normalize(acc_ref[...])

    span = WINDOW + tq
    oldest = jnp.maximum(t_row - WINDOW, -1)

    def window(clamped):
        first = 0 if clamped else pl.multiple_of(start - WINDOW, tq)
        s = _mm(kw_ref[0, pl.ds(first, span), :], qp)
        kp = first + lax.broadcasted_iota(jnp.int32, (span, 1), 0)
        if clamped:
            s = jnp.where(kp > oldest, jnp.where(kp <= t_row, s, NEG_INF), NEG_INF)
        else:
            s = jnp.concatenate([jnp.where(kp[:WINDOW] > oldest, s[:WINDOW], NEG_INF),
                                 jnp.where(kp[WINDOW:] <= t_row, s[WINDOW:], NEG_INF)], axis=0)
        p = jnp.exp2(s - col_max(s)).astype(MXU_DTYPE)
        win_ref[...] = _mm(_with_ones(vwt_ref[0, :, pl.ds(first, span)]), p)

    first_full = WINDOW // tq
    pl.when(qi >= first_full)(functools.partial(window, False))
    pl.when(qi < first_full)(functools.partial(window, True))
    o_win = _normalize(win_ref[...])

    gate_row = kvh * (GQA_GROUP * N_BRANCH)
    for g in range(GQA_GROUP):
        cols = slice(g * tq, (g + 1) * tq)
        head = slice(g * HEAD_DIM, (g + 1) * HEAD_DIM)
        gates = [gate_ref[0, pl.ds(gate_row + N_BRANCH * g + br, 1), :] for br in range(N_BRANCH)]
        y = gates[0] * oc_ref[0, head, :] + gates[1] * o_sel[:, cols] + gates[2] * o_win[:, cols]
        y_ref[0, head, :] = y.astype(y_ref.dtype)


def _sel_win_attention(qt, mb_t, ks, key_onehot, vs_t, kw, vw_t, oc_t, gates_t):
    b, _, s = qt.shape
    tq = min(SEL_TILE, s)
    n_pad = mb_t.shape[2]
    n_gate = gates_t.shape[1]
    group_rows = GQA_GROUP * HEAD_DIM
    head_blk = pl.BlockSpec((1, group_rows, tq), lambda i, h, j: (i, h, j))
    once = pl.Buffered(1)
    keys = pl.BlockSpec((1, s, KV_WIDTH), lambda i, h, j: (i, 0, 0), pipeline_mode=once)
    vals = pl.BlockSpec((1, HEAD_DIM, s), lambda i, h, j: (i, h, 0), pipeline_mode=once)
    acc = pltpu.VMEM((HEAD_DIM + ONES_ROWS, GQA_GROUP * tq), F32)
    return pl.pallas_call(
        _selwin_body,
        grid=(b, N_KV_HEADS, s // tq),
        in_specs=[head_blk,
                  pl.BlockSpec((1, 1, n_pad, tq), lambda i, h, j: (i, h, 0, j)),
                  keys, _const_spec(key_onehot.shape), vals, keys, vals,
                  head_blk,
                  pl.BlockSpec((1, n_gate, tq), lambda i, h, j: (i, 0, j))],
        out_specs=head_blk,
        out_shape=jax.ShapeDtypeStruct((b, N_HEADS * HEAD_DIM, s), MXU_DTYPE),
        scratch_shapes=[pltpu.VMEM((n_pad // SEL_GROUP, SEL_GROUP + KV_WIDTH, GQA_GROUP * tq), MXU_DTYPE),
                        acc, acc],
        compiler_params=_params(("parallel", "parallel", "arbitrary")),
        name="sel_win_attention",
    )(qt, mb_t, ks, key_onehot, vs_t, kw, vw_t, oc_t, gates_t)


def _outproj_body(h_ref, ypl_ref, yat_ref, w1_ref, w2_ref, g_ref, o_ref):
    m = _mm(ypl_ref[0], w1_ref[...]) + _mm_tn(yat_ref[0], w2_ref[...])
    o_ref[0] = h_ref[0] + _rms(m, g_ref[...])


def _outproj(h3, ypl, yat_t, w1, w2, g_post):
    b, s, d = h3.shape
    tm = min(ROW_TILE, s)
    row = lambda w: pl.BlockSpec((1, tm, w), lambda i, j: (i, j, 0))
    return pl.pallas_call(
        _outproj_body,
        grid=(b, s // tm),
        in_specs=[row(d), row(ypl.shape[2]),
                  pl.BlockSpec((1, yat_t.shape[1], tm), lambda i, j: (i, 0, j)),
                  _const_spec(w1.shape), _const_spec(w2.shape), _const_spec((1, d))],
        out_specs=row(d),
        out_shape=jax.ShapeDtypeStruct((b, s, d), F32),
        compiler_params=_params(("parallel", "parallel")),
        name="mixer_out_proj",
    )(h3, ypl, yat_t, w1, w2, g_post[None, :])


def _block_diag(blocks):
    n, a, b = blocks.shape
    eye = jnp.eye(n, dtype=blocks.dtype)
    return jnp.einsum("nab,nm->namb", blocks, eye).reshape(n * a, n * b)


def _compress_weights(w, pe):
    eye = jnp.eye(N_KV_HEADS, dtype=w.dtype)
    halves = []
    for part in range(CMP_LEN // CMP_STRIDE):
        wl = w[part * CMP_STRIDE:(part + 1) * CMP_STRIDE]
        pel = pe[part * CMP_STRIDE:(part + 1) * CMP_STRIDE]
        wm = jnp.einsum("lde,hg->lhdge", wl, eye).reshape(CMP_STRIDE * KV_WIDTH, KV_WIDTH)
        pm = jnp.tile(pel[:, None, :], (1, N_KV_HEADS, 1)).reshape(1, -1)
        halves.append((wm.astype(MXU_DTYPE), pm))
    return halves


def kernel(x, positions, ffn1_pre_g, ffn1_post_g, ffn1_w_gate, ffn1_w_up, ffn1_w_down, mix_pre_g, mix_post_g,
           w_in, w_out, pool_w, pool_scale, conv_w, conv_b, lru_w_r, lru_b_r, lru_w_i, lru_b_i, lru_lambda,
           cmp_w_k, cmp_w_v, cmp_pe, ffn2_pre_g, ffn2_post_g, ffn2_w_gate, ffn2_w_up, ffn2_w_down):
    b, s, d = x.shape
    depth = w_in.shape[0]
    t = b * s
    pool_width = pool_w.shape[1] * pool_w.shape[2]
    lru_width = lru_w_r.shape[1] * lru_w_r.shape[2]
    attn_width = N_HEADS * HEAD_DIM
    assert pool_width == lru_width and s % SEL_TILE == 0 and s % (CMP_STRIDE * 8) == 0
    assert WINDOW % SEL_TILE == 0 and (SEL_GROUP * SEL_BLOCK) % (STREAM_TILES * SEL_TILE) == 0
    assert s >= WINDOW + SEL_TILE

    sizes = [("xpl", pool_width + 2 * lru_width), ("q", attn_width), ("kc", KV_WIDTH), ("vc", KV_WIDTH),
             ("ks", KV_WIDTH), ("vs", KV_WIDTH), ("kw", KV_WIDTH), ("vw", KV_WIDTH), ("g", V7X_LANES)]
    cols, off = {}, 0
    for name, width in sizes:
        cols[name] = (off, off + width)
        off += width

    n_sel = s // SEL_BLOCK
    n_pad = -(-n_sel // SEL_GROUP) * SEL_GROUP
    n1 = s // CMP_STRIDE
    n_cmp = (s - CMP_LEN) // CMP_STRIDE + 1
    cmp_start = jnp.arange(n1) * CMP_STRIDE
    sel_start = jnp.arange(n_pad) * SEL_BLOCK
    overlap_t = ((cmp_start[None, :] < sel_start[:, None] + SEL_BLOCK) &
                 (cmp_start[None, :] + CMP_LEN > sel_start[:, None]) &
                 (jnp.arange(n1)[None, :] < n_cmp) & (jnp.arange(n_pad)[:, None] < n_sel)).astype(MXU_DTYPE)
    key_blk = (jnp.arange(s) // SEL_BLOCK) % SEL_GROUP
    key_onehot = (key_blk[:, None] == jnp.arange(SEL_GROUP)[None, :]).astype(MXU_DTYPE)

    cos, sin = _rope_tables(positions)

    h = x
    for l in range(depth):
        cast = lambda w: w[l].astype(MXU_DTYPE)
        h = _ffn(h.reshape(t, d), ffn1_pre_g[l], ffn1_post_g[l],
                 cast(ffn1_w_gate), cast(ffn1_w_up), cast(ffn1_w_down)).reshape(b, s, d)

        w_all = jnp.pad(w_in[l], ((0, 0), (0, off - w_in.shape[2]))).astype(MXU_DTYPE)
        xpl, qt, kc, vc, ks, vs_t, kw, vw_t, gates_t = _inproj(h, mix_pre_g[l], w_all, cols, cos, sin)

        w_ri = jnp.concatenate([_block_diag(lru_w_r[l]), _block_diag(lru_w_i[l])], axis=1).astype(MXU_DTYPE)
        ypl = _poollru(xpl, _block_diag(pool_w[l]).astype(MXU_DTYPE), pool_scale[l],
                       conv_w[l], conv_b[l], w_ri, jnp.concatenate([lru_b_r[l], lru_b_i[l]]), lru_lambda[l])

        (wk_a, pe_a), (wk_b, pe_b) = _compress_weights(cmp_w_k[l], cmp_pe[l])
        (wv_a, _), (wv_b, _) = _compress_weights(cmp_w_v[l], cmp_pe[l])
        k_cmp = _compress(kc.reshape(b, n1, -1), pe_a, pe_b, wk_a, wk_b, False)
        v_cmp_t = _compress(vc.reshape(b, n1, -1), pe_a, pe_b, wv_a, wv_b, True)

        oc_t, mb_t = _cmp_attention(qt, k_cmp, v_cmp_t, overlap_t, n_sel)
        y_t = _sel_win_attention(qt, mb_t, ks, key_onehot, vs_t, kw, vw_t, oc_t, gates_t)

        w_o = w_out[l].astype(MXU_DTYPE)
        split = pool_width + lru_width
        h = _outproj(h, ypl, y_t, w_o[:split], w_o[split:], mix_post_g[l])

        h = _ffn(h.reshape(t, d), ffn2_pre_g[l], ffn2_post_g[l],
                 cast(ffn2_w_gate), cast(ffn2_w_up), cast(ffn2_w_down)).reshape(b, s, d)
    return h
```

```python
import functools

import jax
import jax.numpy as jnp
from jax import lax
from jax.experimental import pallas as pl
from jax.experimental.pallas import tpu as pltpu

F32 = jnp.float32
MXU_DTYPE = jnp.bfloat16

POOL_WINDOWS = (2, 4, 8, 16)
POOL_GROUP = 64
LRU_HEAD_DIM = 64
LRU_C = 8.0
CONV_WIDTH = 4
HEAD_DIM = 64
N_KV_HEADS = 2
GQA_GROUP = 4
N_HEADS = N_KV_HEADS * GQA_GROUP
KV_WIDTH = N_KV_HEADS * HEAD_DIM
N_BRANCH = 3
CMP_LEN = 32
CMP_STRIDE = 16
SEL_BLOCK = 64
SEL_TOPK = 16
WINDOW = 512
ROPE_THETA = 10000.0
NORM_EPS = 1e-6
NEG_INF = -1e30
BIG_SCORE = 1e9

V7X_LANES = 128
V7X_VMEM_LIMIT_BYTES = 56 * 1024 * 1024

ROW_TILE = 512
SEQ_TILE = 512
CMP_Q_TILE = 256
CMP_CHUNK = 256
SEL_TILE = 512
SEL_GROUP = 128
ONES_ROWS = 16
M_INIT = -1e20
STREAM_TILES = 2
EXP_HEADROOM = 64.0
Q_SCALE = HEAD_DIM ** -0.5 * 1.4426950408889634


def _params(semantics):
    return pltpu.CompilerParams(dimension_semantics=semantics, vmem_limit_bytes=V7X_VMEM_LIMIT_BYTES)


def _rms(x, g):
    return x * lax.rsqrt(jnp.mean(x * x, axis=-1, keepdims=True) + NORM_EPS) * g


def _sigmoid(x):
    return 1.0 / (1.0 + jnp.exp(-x))


def _mm(a, b):
    return jnp.dot(a, b, preferred_element_type=F32)


def _mm_tn(a_t, b):
    return lax.dot_general(a_t, b, (((0,), (0,)), ((), ())), preferred_element_type=F32)


def _log2(n):
    assert n & (n - 1) == 0
    return n.bit_length() - 1


def _const_spec(shape):
    zeros = (0,) * len(shape)
    return pl.BlockSpec(shape, lambda *_: zeros, pipeline_mode=pl.Buffered(1))


def _rope_body(pos_ref, inv_ref, cos_ref, sin_ref):
    ang = pos_ref[0].astype(F32) * inv_ref[...]
    lane = lax.broadcasted_iota(jnp.int32, ang.shape, 1)
    cos_ref[0] = jnp.cos(ang)
    sin_ref[0] = jnp.where((lane & (HEAD_DIM - 1)) < HEAD_DIM // 2, -jnp.sin(ang), jnp.sin(ang))


def _rope_tables(positions):
    b, s = positions.shape
    inv = ROPE_THETA ** (-jnp.arange(0, HEAD_DIM, 2, dtype=F32) / HEAD_DIM)
    inv_row = jnp.tile(inv, V7X_LANES // (HEAD_DIM // 2))[None, :]
    ts = min(SEQ_TILE, s)
    out = jax.ShapeDtypeStruct((b, s, V7X_LANES), F32)
    return pl.pallas_call(
        _rope_body,
        grid=(b, s // ts),
        in_specs=[pl.BlockSpec((1, ts, 1), lambda i, j: (i, j, 0)),
                  pl.BlockSpec((1, V7X_LANES), lambda i, j: (0, 0))],
        out_specs=[pl.BlockSpec((1, ts, V7X_LANES), lambda i, j: (i, j, 0))] * 2,
        out_shape=[out, out],
        compiler_params=_params(("parallel", "parallel")),
        name="rope_tables",
    )(positions[:, :, None], inv_row)


def _ffn_body(h_ref, gpre_ref, gpost_ref, wg_ref, wu_ref, wd_ref, o_ref):
    h = h_ref[...]
    xn = _rms(h, gpre_ref[...]).astype(MXU_DTYPE)
    gate = _mm(xn, wg_ref[...])
    up = _mm(xn, wu_ref[...])
    act = (gate * _sigmoid(gate) * up).astype(MXU_DTYPE)
    f = _mm(act, wd_ref[...])
    o_ref[...] = h + 0.5 * _rms(f, gpost_ref[...])


def _ffn(h2, g_pre, g_post, w_gate, w_up, w_down):
    t, d = h2.shape
    dff = w_gate.shape[1]
    tm = min(ROW_TILE, t)
    return pl.pallas_call(
        _ffn_body,
        grid=(t // tm,),
        in_specs=[pl.BlockSpec((tm, d), lambda i: (i, 0)),
                  _const_spec((1, d)), _const_spec((1, d)),
                  _const_spec((d, dff)), _const_spec((d, dff)), _const_spec((dff, d))],
        out_specs=pl.BlockSpec((tm, d), lambda i: (i, 0)),
        out_shape=jax.ShapeDtypeStruct((t, d), F32),
        compiler_params=_params(("parallel",)),
        name="ffn",
    )(h2, g_pre[None, :], g_post[None, :], w_gate, w_up, w_down)


def _swap_halves(x):
    n = x.shape[1]
    lane = lax.broadcasted_iota(jnp.int32, x.shape, 1)
    first_half = (lane & (HEAD_DIM - 1)) < HEAD_DIM // 2
    return jnp.where(first_half, pltpu.roll(x, n - HEAD_DIM // 2, 1), pltpu.roll(x, HEAD_DIM // 2, 1))


def _inproj_body(cols, h_ref, g_ref, w_ref, cos_ref, sin_ref,
                 xpl_ref, qt_ref, kc_ref, vc_ref, ks_ref, vst_ref, kw_ref, vwt_ref, gatet_ref):
    xn = _rms(h_ref[0], g_ref[...]).astype(MXU_DTYPE)
    proj = _mm(xn, w_ref[...])
    cos = cos_ref[0]
    sin = sin_ref[0]

    def rope(x):
        rep = x.shape[1] // V7X_LANES
        c = jnp.concatenate([cos] * rep, axis=1) if rep > 1 else cos
        s = jnp.concatenate([sin] * rep, axis=1) if rep > 1 else sin
        return x * c + _swap_halves(x) * s

    def seg(name):
        lo, hi = cols[name]
        return proj[:, lo:hi]

    xpl_ref[0] = seg("xpl")
    qt_ref[0] = (rope(seg("q")) * Q_SCALE).T.astype(qt_ref.dtype)
    kc_ref[0] = rope(seg("kc"))
    vc_ref[0] = seg("vc")
    ks_ref[0] = rope(seg("ks")).astype(ks_ref.dtype)
    vst_ref[0] = seg("vs").T.astype(vst_ref.dtype)
    kw_ref[0] = rope(seg("kw")).astype(kw_ref.dtype)
    vwt_ref[0] = seg("vw").T.astype(vwt_ref.dtype)
    gatet_ref[0] = _sigmoid(seg("g")).T[:gatet_ref.shape[1]]


def _inproj(h3, g_pre, w_all, cols, cos, sin):
    b, s, d = h3.shape
    tm = min(ROW_TILE, s)
    ncol = w_all.shape[1]
    width = lambda n: cols[n][1] - cols[n][0]
    row = lambda w: pl.BlockSpec((1, tm, w), lambda i, j: (i, j, 0))
    col = lambda w: pl.BlockSpec((1, w, tm), lambda i, j: (i, 0, j))
    tok = lambda n, dt: (jax.ShapeDtypeStruct((b, s, width(n)), dt), row(width(n)))
    chan = lambda w, dt: (jax.ShapeDtypeStruct((b, w, s), dt), col(w))
    outs = [tok("xpl", F32), chan(width("q"), MXU_DTYPE), tok("kc", F32), tok("vc", F32),
            tok("ks", MXU_DTYPE), chan(width("vs"), MXU_DTYPE), tok("kw", MXU_DTYPE),
            chan(width("vw"), MXU_DTYPE), chan(N_BRANCH * N_HEADS, F32)]
    return pl.pallas_call(
        functools.partial(_inproj_body, cols),
        grid=(b, s // tm),
        in_specs=[row(d), _const_spec((1, d)), _const_spec((d, ncol)), row(V7X_LANES), row(V7X_LANES)],
        out_specs=[o[1] for o in outs],
        out_shape=[o[0] for o in outs],
        compiler_params=_params(("parallel", "parallel")),
        name="mixer_in_proj",
    )(h3, g_pre[None, :], w_all, cos, sin)


def _poollru_body(x_ref, pw_ref, pscale_ref, cw_ref, cb_ref, wri_ref, bri_ref, lam_ref,
                  y_ref, pool_carry, conv_carry, h_carry):
    si = pl.program_id(1)
    ts = x_ref.shape[1]
    width = pw_ref.shape[0]
    halo_p = pool_carry.shape[0]
    halo_c = conv_carry.shape[0]

    @pl.when(si == 0)
    def _():
        pool_carry[...] = jnp.zeros_like(pool_carry)
        conv_carry[...] = jnp.zeros_like(conv_carry)
        h_carry[...] = jnp.zeros_like(h_carry)

    x = x_ref[0]
    xp = x[:, :width]
    xl = x[:, width:2 * width]
    gl = x[:, 2 * width:]

    ext = jnp.concatenate([pool_carry[...], xp], axis=0)
    sums = [ext]
    shift = 1
    for _ in POOL_WINDOWS:
        sums.append(sums[-1] + pltpu.roll(sums[-1], shift, 0))
        shift *= 2
    lane = lax.broadcasted_iota(jnp.int32, (1, width), 1)
    grp = lax.shift_right_logical(lane, _log2(POOL_GROUP))
    win_sum = sums[len(POOL_WINDOWS)]
    win = jnp.full((1, width), float(POOL_WINDOWS[-1]), F32)
    for gi in range(len(POOL_WINDOWS) - 2, -1, -1):
        win_sum = jnp.where(grp == gi, sums[gi + 1], win_sum)
        win = jnp.where(grp == gi, float(POOL_WINDOWS[gi]), win)
    win_sum = win_sum[halo_p:]
    t_abs = si * ts + lax.broadcasted_iota(jnp.int32, (ts, 1), 0)
    cnt = jnp.minimum((t_abs + 1).astype(F32), win)
    pooled = win_sum / cnt
    y_pool = _mm((pooled - xp).astype(MXU_DTYPE), pw_ref[...]) * pscale_ref[...]
    pool_carry[...] = xp[ts - halo_p:]

    extc = jnp.concatenate([conv_carry[...], xl], axis=0)
    xc = extc * cw_ref[CONV_WIDTH - 1:CONV_WIDTH, :]
    for k in range(1, CONV_WIDTH):
        xc = xc + pltpu.roll(extc, k, 0) * cw_ref[CONV_WIDTH - 1 - k:CONV_WIDTH - k, :]
    xc = xc[halo_c:] + cb_ref[...]
    conv_carry[...] = xl[ts - halo_c:]

    ri = _mm(xc.astype(MXU_DTYPE), wri_ref[...]) + bri_ref[...]
    r = _sigmoid(ri[:, :width])
    i_gate = _sigmoid(ri[:, width:])
    neg_lam = -lam_ref[...]
    softplus = jnp.maximum(neg_lam, 0.0) + jnp.log1p(jnp.exp(-jnp.abs(neg_lam)))
    log_a = -LRU_C * r * softplus
    a = jnp.exp(log_a)
    b = jnp.sqrt(-jnp.tanh(log_a) * (a * a + 1.0)) * (i_gate * xc)

    row = lax.broadcasted_iota(jnp.int32, (ts, 1), 0)
    k = 1
    while k < ts:
        keep = row >= k
        a_prev = jnp.where(keep, pltpu.roll(a, k, 0), 1.0)
        b_prev = jnp.where(keep, pltpu.roll(b, k, 0), 0.0)
        b = a * b_prev + b
        a = a * a_prev
        k *= 2
    h = a * h_carry[0:1, :] + b
    h_carry[...] = jnp.broadcast_to(h[ts - 1:ts, :], h_carry.shape)

    gelu = 0.5 * gl * (1.0 + jnp.tanh(0.7978845608028654 * (gl + 0.044715 * gl * gl * gl)))
    y_ref[0] = jnp.concatenate([y_pool, h * gelu], axis=1).astype(y_ref.dtype)


def _poollru(xpl, pool_w_bd, pool_scale, conv_w, conv_b, w_ri_bd, b_ri, lam):
    b, s, w3 = xpl.shape
    width = w3 // 3
    ts = min(SEQ_TILE, s)
    return pl.pallas_call(
        _poollru_body,
        grid=(b, s // ts),
        in_specs=[pl.BlockSpec((1, ts, w3), lambda i, j: (i, j, 0)),
                  _const_spec((width, width)), _const_spec((1, width)),
                  _const_spec((CONV_WIDTH, width)), _const_spec((1, width)),
                  _const_spec((width, 2 * width)), _const_spec((1, 2 * width)), _const_spec((1, width))],
        out_specs=pl.BlockSpec((1, ts, 2 * width), lambda i, j: (i, j, 0)),
        out_shape=jax.ShapeDtypeStruct((b, s, 2 * width), MXU_DTYPE),
        scratch_shapes=[pltpu.VMEM((POOL_WINDOWS[-1], width), F32),
                        pltpu.VMEM((8, width), F32),
                        pltpu.VMEM((8, width), F32)],
        compiler_params=_params(("parallel", "arbitrary")),
        name="pool_rglru",
    )(xpl, pool_w_bd, pool_scale[None, :], conv_w, conv_b[None, :], w_ri_bd, b_ri[None, :], lam[None, :])


def _compress_body(transpose_out, r_ref, pea_ref, peb_ref, wa_ref, wb_ref, o_ref):
    r = r_ref[0]
    first = _mm((r + pea_ref[...]).astype(MXU_DTYPE), wa_ref[...])
    second = _mm((r + peb_ref[...]).astype(MXU_DTYPE), wb_ref[...])
    n1 = r.shape[0]
    out = first + pltpu.roll(second, n1 - 1, 0)
    o_ref[0] = (out.T if transpose_out else out).astype(o_ref.dtype)


def _compress(rows, pe_a, pe_b, w_a, w_b, transpose_out):
    b, n1, k = rows.shape
    n_out = w_a.shape[1]
    out_dims = (n_out, n1) if transpose_out else (n1, n_out)
    return pl.pallas_call(
        functools.partial(_compress_body, transpose_out),
        grid=(b,),
        in_specs=[pl.BlockSpec((1, n1, k), lambda i: (i, 0, 0)),
                  _const_spec((1, k)), _const_spec((1, k)), _const_spec((k, n_out)), _const_spec((k, n_out))],
        out_specs=pl.BlockSpec((1,) + out_dims, lambda i: (i, 0, 0)),
        out_shape=jax.ShapeDtypeStruct((b,) + out_dims, MXU_DTYPE),
        compiler_params=_params(("parallel",)),
        name="compress_kv",
    )(rows, pe_a, pe_b, w_a, w_b)


def _stack_heads(q_ref):
    return jnp.concatenate([q_ref[0, g * HEAD_DIM:(g + 1) * HEAD_DIM, :] for g in range(GQA_GROUP)], axis=1)


def _place_in_kv_half(qt, kvh):
    zeros = jnp.zeros_like(qt)
    parts = [jnp.where(kvh == h, qt, zeros) for h in range(N_KV_HEADS)]
    return jnp.concatenate(parts, axis=0)


def _query_positions(start, tq):
    lane = lax.broadcasted_iota(jnp.int32, (1, GQA_GROUP * tq), 1)
    return start + (lane & (tq - 1))


def _with_ones(v_t):
    return jnp.concatenate([v_t, jnp.ones((ONES_ROWS, v_t.shape[1]), v_t.dtype)], axis=0)


def _normalize(acc):
    return acc[:HEAD_DIM] * (1.0 / acc[HEAD_DIM:HEAD_DIM + 1])


def _cmp_body(n_sel, q_ref, kc_ref, vct_ref, ovt_ref, oc_ref, mb_ref, imp_ref):
    tq = q_ref.shape[2]
    kvh = pl.program_id(1)
    start = pl.program_id(2) * tq
    qt = _stack_heads(q_ref)
    qp = _place_in_kv_half(qt, kvh)
    n1 = kc_ref.shape[1]
    t_row = _query_positions(start, tq)
    assert CMP_STRIDE * CMP_CHUNK >= tq + CMP_LEN

    def attend(nrows):
        s = _mm(kc_ref[0, :nrows, :], qp)
        lo = max(0, nrows - 2 * CMP_CHUNK)
        n_idx = lo + lax.broadcasted_iota(jnp.int32, (nrows - lo, 1), 0)
        tail = jnp.where(n_idx * CMP_STRIDE + (CMP_LEN - 1) <= t_row, s[lo:], NEG_INF)
        s = jnp.concatenate([s[:lo], tail], axis=0) if lo else tail
        m = jnp.max(s, axis=0, keepdims=True)
        p = jnp.exp2(s - jnp.where(m > 0.5 * NEG_INF, m, 0.0))
        den = jnp.sum(p, axis=0, keepdims=True)
        pn = p * (1.0 / jnp.where(den > 0.0, den, 1.0))
        oc = _mm(vct_ref[0, :, :nrows], pn.astype(MXU_DTYPE))
        psum = pn[:, 0:tq]
        for g in range(GQA_GROUP):
            oc_ref[0, g * HEAD_DIM:(g + 1) * HEAD_DIM, :] = oc[:, g * tq:(g + 1) * tq]
            if g:
                psum = psum + pn[:, g * tq:(g + 1) * tq]
        ovt = ovt_ref[:, :nrows]
        hi = psum.astype(MXU_DTYPE)
        rem = psum - hi.astype(F32)
        mid = rem.astype(MXU_DTYPE)
        low = (rem - mid.astype(F32)).astype(MXU_DTYPE)
        imp_ref[...] = _mm(ovt, hi) + _mm(ovt, mid) + _mm(ovt, low)

    n_need = lax.shift_right_logical(start + tq, _log2(CMP_STRIDE)) - 1
    chunks = lax.shift_right_logical(n_need + CMP_CHUNK - 1, _log2(CMP_CHUNK))
    n_variants = -(-n1 // CMP_CHUNK)
    for c in range(1, n_variants + 1):
        pl.when(chunks == c)(functools.partial(attend, min(c * CMP_CHUNK, n1)))
    imp = imp_ref[...]

    n_pad = imp.shape[0]
    j = lax.broadcasted_iota(jnp.int32, (n_pad, 1), 0).astype(F32)
    t_q = start + lax.broadcasted_iota(jnp.int32, (1, tq), 1)
    cur = lax.shift_right_logical(t_q, _log2(SEL_BLOCK)).astype(F32)
    forced = (j == 0.0) | (j == cur) | (j == cur - 1.0)
    valid = j <= cur
    base = jnp.where(valid, jnp.where(forced, BIG_SCORE, imp), -BIG_SCORE)
    base = jnp.where(j < float(n_sel), base, -jnp.inf)
    assert n_sel >= SEL_TOPK

    score = jnp.where(forced, -jnp.inf, base)
    bias = jnp.where(forced, 0.0, NEG_INF)
    for _ in range(SEL_TOPK - 3):
        pick = score == jnp.max(score, axis=0, keepdims=True)
        bias = jnp.where(pick, 0.0, bias)
        score = jnp.where(pick, -jnp.inf, score)
    mb_ref[0, 0] = bias.astype(mb_ref.dtype)
    taken = jnp.sum(jnp.where(valid, jnp.where(bias == 0.0, 1.0, 0.0), 0.0), axis=0, keepdims=True)
    tie = jnp.max(jnp.abs(taken - jnp.minimum(cur + 1.0, float(SEL_TOPK)))) > 0.0

    @pl.when(tie)
    def _():
        score = base
        bias = jnp.full(score.shape, NEG_INF, F32)
        for _ in range(SEL_TOPK):
            best = jnp.max(score, axis=0, keepdims=True)
            first = jnp.min(jnp.where(score == best, j, float(n_pad)), axis=0, keepdims=True)
            pick = j == first
            bias = jnp.where(pick, 0.0, bias)
            score = jnp.where(pick, -jnp.inf, score)
        mb_ref[0, 0] = bias.astype(mb_ref.dtype)


def _cmp_attention(qt, k_cmp, v_cmp_t, ov_t, n_sel):
    b, _, s = qt.shape
    n1 = k_cmp.shape[1]
    n_pad = ov_t.shape[0]
    tq = min(CMP_Q_TILE, s)
    group_rows = GQA_GROUP * HEAD_DIM
    head_blk = pl.BlockSpec((1, group_rows, tq), lambda i, h, j: (i, h, j))
    return pl.pallas_call(
        functools.partial(_cmp_body, n_sel),
        grid=(b, N_KV_HEADS, s // tq),
        in_specs=[head_blk,
                  pl.BlockSpec((1, n1, KV_WIDTH), lambda i, h, j: (i, 0, 0)),
                  pl.BlockSpec((1, HEAD_DIM, n1), lambda i, h, j: (i, h, 0)),
                  _const_spec((n_pad, n1))],
        out_specs=[head_blk, pl.BlockSpec((1, 1, n_pad, tq), lambda i, h, j: (i, h, 0, j))],
        out_shape=[jax.ShapeDtypeStruct((b, N_HEADS * HEAD_DIM, s), F32),
                   jax.ShapeDtypeStruct((b, N_KV_HEADS, n_pad, s), MXU_DTYPE)],
        scratch_shapes=[pltpu.VMEM((n_pad, tq), F32)],
        compiler_params=_params(("parallel", "parallel", "parallel")),
        name="cmp_attention_select",
    )(qt, k_cmp, v_cmp_t, ov_t)


def _selwin_body(q_ref, mb_ref, ks_ref, oh_ref, vst_ref, kw_ref, vwt_ref, oc_ref, gate_ref, y_ref,
                 qa_ref, acc_ref, win_ref):
    tq = q_ref.shape[2]
    tk = tq
    kvh = pl.program_id(1)
    qi = pl.program_id(2)
    start = qi * tq
    rows = GQA_GROUP * tq
    qt = _stack_heads(q_ref)
    qp = _place_in_kv_half(qt, kvh)
    t_row = _query_positions(start, tq)
    group_shift = _log2(SEL_GROUP * SEL_BLOCK // tk)
    n_groups = mb_ref.shape[2] // SEL_GROUP

    for grp in range(n_groups):
        bias_rows = mb_ref[0, 0, grp * SEL_GROUP:(grp + 1) * SEL_GROUP, :]
        qa_ref[grp] = jnp.concatenate([jnp.concatenate([bias_rows] * GQA_GROUP, axis=1), qp], axis=0)

    def scores(kt, n_sub, causal):
        key_rows = pl.ds(pl.multiple_of(kt * tk, tk), n_sub * tk)
        k = jnp.concatenate([oh_ref[key_rows, :], ks_ref[0, key_rows, :]], axis=1)
        s = _mm(k, qa_ref[lax.shift_right_logical(kt, group_shift)])
        if causal:
            kp = kt * tk + lax.broadcasted_iota(jnp.int32, (n_sub * tk, 1), 0)
            s = jnp.where(kp <= t_row, s, NEG_INF)
        return s

    def values(kt, n_sub):
        return _with_ones(vst_ref[0, :, pl.ds(pl.multiple_of(kt * tk, tk), n_sub * tk)])

    def col_max(s):
        return jnp.max(s, axis=0, keepdims=True)

    n_past = qi

    own_rows = pl.ds(pl.multiple_of(start, tq), tq)
    old_first = pl.multiple_of(jnp.maximum(start - WINDOW, 0), tq)
    old_rows = pl.ds(old_first, WINDOW)
    oldest = jnp.maximum(t_row - WINDOW, -1)

    def window_scores_own():
        kp = start + lax.broadcasted_iota(jnp.int32, (tq, 1), 0)
        return jnp.where(kp <= t_row, _mm(kw_ref[0, own_rows, :], qp), NEG_INF)

    def window_scores_old():
        kp = old_first + lax.broadcasted_iota(jnp.int32, (WINDOW, 1), 0)
        kp = jnp.where(kp < start, kp, -2)
        return jnp.where(kp > oldest, _mm(kw_ref[0, old_rows, :], qp), NEG_INF)

    def pv(v_t, s, ref):
        return _mm(_with_ones(v_t), jnp.exp2(s - ref).astype(MXU_DTYPE))

    s_diag = scores(qi, 1, True)
    ref = col_max(s_diag)
    acc_ref[...] = _mm(values(qi, 1), jnp.exp2(s_diag - ref).astype(MXU_DTYPE))
    s_own = window_scores_own()
    ref_w = col_max(s_own)
    s_old = window_scores_old()
    win_ref[...] = pv(vwt_ref[0, :, own_rows], s_own, ref_w) + pv(vwt_ref[0, :, old_rows], s_old, ref_w)
    excess_w = jnp.max(col_max(s_old) - ref_w)

    def stream(kt, n_sub, seen_max):
        s = scores(kt, n_sub, False)
        acc_ref[...] += _mm(values(kt, n_sub), jnp.exp2(s - ref).astype(MXU_DTYPE))
        return jnp.maximum(seen_max, col_max(s))

    n_big = lax.shift_right_logical(n_past, _log2(STREAM_TILES))
    seen_max = lax.fori_loop(0, n_big, lambda i, c: stream(i * STREAM_TILES, STREAM_TILES, c), ref)
    seen_max = lax.fori_loop(n_big * STREAM_TILES, n_past, lambda i, c: stream(i, 1, c), seen_max)
    overflow = jnp.maximum(jnp.max(seen_max - ref), excess_w) > EXP_HEADROOM

    @pl.when(overflow)
    def _():
        def tile(kt, causal):
            s = scores(kt, 1, causal)
            m_t = col_max(s)
            return m_t, _mm(values(kt, 1), jnp.exp2(s - m_t).astype(MXU_DTYPE))

        def merge(carry, part):
            m, acc = carry
            m_t, o_t = part
            m_new = jnp.maximum(m, m_t)
            return m_new, acc * jnp.exp2(m - m_new) + o_t * jnp.exp2(m_t - m_new)

        carry = (jnp.full((1, rows), M_INIT, F32), jnp.zeros(acc_ref.shape, F32))
        carry = lax.fori_loop(0, n_past, lambda i, c: merge(c, tile(i, False)), carry)
        acc_ref[...] = merge(carry, tile(qi, True))[1]

        x_own = window_scores_own()
        x_old = window_scores_old()
        m_w = jnp.maximum(col_max(x_own), col_max(x_old))
        win_ref[...] = pv(vwt_ref[0, :, own_rows], x_own, m_w) + pv(vwt_ref[0, :, old_rows], x_old, m_w)

    o_sel = _normalize(acc_ref[...])
    o_win = _normalize(win_ref[...])

    gate_row = kvh * (GQA_GROUP * N_BRANCH)
    for g in range(GQA_GROUP):
        cols = slice(g * tq, (g + 1) * tq)
        head = slice(g * HEAD_DIM, (g + 1) * HEAD_DIM)
        gates = [gate_ref[0, pl.ds(gate_row + N_BRANCH * g + br, 1), :] for br in range(N_BRANCH)]
        y = gates[0] * oc_ref[0, head, :] + gates[1] * o_sel[:, cols] + gates[2] * o_win[:, cols]
        y_ref[0, head, :] = y.astype(y_ref.dtype)


def _sel_win_attention(qt, mb_t, ks, key_onehot, vs_t, kw, vw_t, oc_t, gates_t):
    b, _, s = qt.shape
    tq = min(SEL_TILE, s)
    n_pad = mb_t.shape[2]
    n_gate = gates_t.shape[1]
    group_rows = GQA_GROUP * HEAD_DIM
    head_blk = pl.BlockSpec((1, group_rows, tq), lambda i, h, j: (i, h, j))
    once = pl.Buffered(1)
    keys = pl.BlockSpec((1, s, KV_WIDTH), lambda i, h, j: (i, 0, 0), pipeline_mode=once)
    vals = pl.BlockSpec((1, HEAD_DIM, s), lambda i, h, j: (i, h, 0), pipeline_mode=once)
    acc = pltpu.VMEM((HEAD_DIM + ONES_ROWS, GQA_GROUP * tq), F32)
    return pl.pallas_call(
        _selwin_body,
        grid=(b, N_KV_HEADS, s // tq),
        in_specs=[head_blk,
                  pl.BlockSpec((1, 1, n_pad, tq), lambda i, h, j: (i, h, 0, j)),
                  keys, _const_spec(key_onehot.shape), vals, keys, vals,
                  head_blk,
                  pl.BlockSpec((1, n_gate, tq), lambda i, h, j: (i, 0, j))],
        out_specs=head_blk,
        out_shape=jax.ShapeDtypeStruct((b, N_HEADS * HEAD_DIM, s), MXU_DTYPE),
        scratch_shapes=[pltpu.VMEM((n_pad // SEL_GROUP, SEL_GROUP + KV_WIDTH, GQA_GROUP * tq), MXU_DTYPE),
                        acc, acc],
        compiler_params=_params(("parallel", "parallel", "arbitrary")),
        name="sel_win_attention",
    )(qt, mb_t, ks, key_onehot, vs_t, kw, vw_t, oc_t, gates_t)


def _outproj_body(h_ref, ypl_ref, yat_ref, w1_ref, w2_ref, g_ref, o_ref):
    m = _mm(ypl_ref[0], w1_ref[...]) + _mm_tn(yat_ref[0], w2_ref[...])
    o_ref[0] = h_ref[0] + _rms(m, g_ref[...])


def _outproj(h3, ypl, yat_t, w1, w2, g_post):
    b, s, d = h3.shape
    tm = min(ROW_TILE, s)
    row = lambda w: pl.BlockSpec((1, tm, w), lambda i, j: (i, j, 0))
    return pl.pallas_call(
        _outproj_body,
        grid=(b, s // tm),
        in_specs=[row(d), row(ypl.shape[2]),
                  pl.BlockSpec((1, yat_t.shape[1], tm), lambda i, j: (i, 0, j)),
                  _const_spec(w1.shape), _const_spec(w2.shape), _const_spec((1, d))],
        out_specs=row(d),
        out_shape=jax.ShapeDtypeStruct((b, s, d), F32),
        compiler_params=_params(("parallel", "parallel")),
        name="mixer_out_proj",
    )(h3, ypl, yat_t, w1, w2, g_post[None, :])


def _block_diag(blocks):
    n, a, b = blocks.shape
    eye = jnp.eye(n, dtype=blocks.dtype)
    return jnp.einsum("nab,nm->namb", blocks, eye).reshape(n * a, n * b)


def _compress_weights(w, pe):
    eye = jnp.eye(N_KV_HEADS, dtype=w.dtype)
    halves = []
    for part in range(CMP_LEN // CMP_STRIDE):
        wl = w[part * CMP_STRIDE:(part + 1) * CMP_STRIDE]
        pel = pe[part * CMP_STRIDE:(part + 1) * CMP_STRIDE]
        wm = jnp.einsum("lde,hg->lhdge", wl, eye).reshape(CMP_STRIDE * KV_WIDTH, KV_WIDTH)
        pm = jnp.tile(pel[:, None, :], (1, N_KV_HEADS, 1)).reshape(1, -1)
        halves.append((wm.astype(MXU_DTYPE), pm))
    return halves


def kernel(x, positions, ffn1_pre_g, ffn1_post_g, ffn1_w_gate, ffn1_w_up, ffn1_w_down, mix_pre_g, mix_post_g,
           w_in, w_out, pool_w, pool_scale, conv_w, conv_b, lru_w_r, lru_b_r, lru_w_i, lru_b_i, lru_lambda,
           cmp_w_k, cmp_w_v, cmp_pe, ffn2_pre_g, ffn2_post_g, ffn2_w_gate, ffn2_w_up, ffn2_w_down):
    b, s, d = x.shape
    depth = w_in.shape[0]
    t = b * s
    pool_width = pool_w.shape[1] * pool_w.shape[2]
    lru_width = lru_w_r.shape[1] * lru_w_r.shape[2]
    attn_width = N_HEADS * HEAD_DIM
    assert pool_width == lru_width and s % SEL_TILE == 0 and s % (CMP_STRIDE * 8) == 0
    assert WINDOW % SEL_TILE == 0 and (SEL_GROUP * SEL_BLOCK) % (STREAM_TILES * SEL_TILE) == 0
    assert s >= WINDOW + SEL_TILE

    sizes = [("xpl", pool_width + 2 * lru_width), ("q", attn_width), ("kc", KV_WIDTH), ("vc", KV_WIDTH),
             ("ks", KV_WIDTH), ("vs", KV_WIDTH), ("kw", KV_WIDTH), ("vw", KV_WIDTH), ("g", V7X_LANES)]
    cols, off = {}, 0
    for name, width in sizes:
        cols[name] = (off, off + width)
        off += width

    n_sel = s // SEL_BLOCK
    n_pad = -(-n_sel // SEL_GROUP) * SEL_GROUP
    n1 = s // CMP_STRIDE
    n_cmp = (s - CMP_LEN) // CMP_STRIDE + 1
    cmp_start = jnp.arange(n1) * CMP_STRIDE
    sel_start = jnp.arange(n_pad) * SEL_BLOCK
    overlap_t = ((cmp_start[None, :] < sel_start[:, None] + SEL_BLOCK) &
                 (cmp_start[None, :] + CMP_LEN > sel_start[:, None]) &
                 (jnp.arange(n1)[None, :] < n_cmp) & (jnp.arange(n_pad)[:, None] < n_sel)).astype(MXU_DTYPE)
    key_blk = (jnp.arange(s) // SEL_BLOCK) % SEL_GROUP
    key_onehot = (key_blk[:, None] == jnp.arange(SEL_GROUP)[None, :]).astype(MXU_DTYPE)

    cos, sin = _rope_tables(positions)

    h = x
    for l in range(depth):
        cast = lambda w: w[l].astype(MXU_DTYPE)
        h = _ffn(h.reshape(t, d), ffn1_pre_g[l], ffn1_post_g[l],
                 cast(ffn1_w_gate), cast(ffn1_w_up), cast(ffn1_w_down)).reshape(b, s, d)

        w_all = jnp.pad(w_in[l], ((0, 0), (0, off - w_in.shape[2]))).astype(MXU_DTYPE)
        xpl, qt, kc, vc, ks, vs_t, kw, vw_t, gates_t = _inproj(h, mix_pre_g[l], w_all, cols, cos, sin)

        w_ri = jnp.concatenate([_block_diag(lru_w_r[l]), _block_diag(lru_w_i[l])], axis=1).astype(MXU_DTYPE)
        ypl = _poollru(xpl, _block_diag(pool_w[l]).astype(MXU_DTYPE), pool_scale[l],
                       conv_w[l], conv_b[l], w_ri, jnp.concatenate([lru_b_r[l], lru_b_i[l]]), lru_lambda[l])

        (wk_a, pe_a), (wk_b, pe_b) = _compress_weights(cmp_w_k[l], cmp_pe[l])
        (wv_a, _), (wv_b, _) = _compress_weights(cmp_w_v[l], cmp_pe[l])
        k_cmp = _compress(kc.reshape(b, n1, -1), pe_a, pe_b, wk_a, wk_b, False)
        v_cmp_t = _compress(vc.reshape(b, n1, -1), pe_a, pe_b, wv_a, wv_b, True)

        oc_t, mb_t = _cmp_attention(qt, k_cmp, v_cmp_t, overlap_t, n_sel)
        y_t = _sel_win_attention(qt, mb_t, ks, key_onehot, vs_t, kw, vw_t, oc_t, gates_t)

        w_o = w_out[l].astype(MXU_DTYPE)
        split = pool_width + lru_width
        h = _outproj(h, ypl, y_t, w_o[:split], w_o[split:], mix_post_g[l])

        h = _ffn(h.reshape(t, d), ffn2_pre_g[l], ffn2_post_g[l],
                 cast(ffn2_w_gate), cast(ffn2_w_up), cast(ffn2_w_down)).reshape(b, s, d)
    return h
```

```python
import functools

import jax
import jax.numpy as jnp
from jax import lax
from jax.experimental import pallas as pl
from jax.experimental.pallas import tpu as pltpu

F32 = jnp.float32
MXU_DTYPE = jnp.bfloat16

POOL_WINDOWS = (2, 4, 8, 16)
POOL_GROUP = 64
LRU_HEAD_DIM = 64
LRU_C = 8.0
CONV_WIDTH = 4
HEAD_DIM = 64
N_KV_HEADS = 2
GQA_GROUP = 4
N_HEADS = N_KV_HEADS * GQA_GROUP
KV_WIDTH = N_KV_HEADS * HEAD_DIM
N_BRANCH = 3
CMP_LEN = 32
CMP_STRIDE = 16
SEL_BLOCK = 64
SEL_TOPK = 16
WINDOW = 512
ROPE_THETA = 10000.0
NORM_EPS = 1e-6
NEG_INF = -1e30
BIG_SCORE = 1e9

V7X_LANES = 128
V7X_VMEM_LIMIT_BYTES = 56 * 1024 * 1024

ROW_TILE = 512
SEQ_TILE = 512
CMP_Q_TILE = 256
CMP_CHUNK = 256
SEL_TILE = 512
SEL_GROUP = 128
ONES_ROWS = 16
M_INIT = -1e20
STREAM_TILES = 2
EXP_HEADROOM = 64.0
Q_SCALE = HEAD_DIM ** -0.5 * 1.4426950408889634


def _params(semantics):
    return pltpu.CompilerParams(dimension_semantics=semantics, vmem_limit_bytes=V7X_VMEM_LIMIT_BYTES)


def _rms(x, g):
    return x * lax.rsqrt(jnp.mean(x * x, axis=-1, keepdims=True) + NORM_EPS) * g


def _sigmoid(x):
    return 1.0 / (1.0 + jnp.exp(-x))


def _mm(a, b):
    return jnp.dot(a, b, preferred_element_type=F32)


def _mm_tn(a_t, b):
    return lax.dot_general(a_t, b, (((0,), (0,)), ((), ())), preferred_element_type=F32)


def _log2(n):
    assert n & (n - 1) == 0
    return n.bit_length() - 1


def _const_spec(shape):
    zeros = (0,) * len(shape)
    return pl.BlockSpec(shape, lambda *_: zeros, pipeline_mode=pl.Buffered(1))


def _rope_body(pos_ref, inv_ref, cos_ref, sin_ref):
    ang = pos_ref[0].astype(F32) * inv_ref[...]
    lane = lax.broadcasted_iota(jnp.int32, ang.shape, 1)
    cos_ref[0] = jnp.cos(ang)
    sin_ref[0] = jnp.where((lane & (HEAD_DIM - 1)) < HEAD_DIM // 2, -jnp.sin(ang), jnp.sin(ang))


def _rope_tables(positions):
    b, s = positions.shape
    inv = ROPE_THETA ** (-jnp.arange(0, HEAD_DIM, 2, dtype=F32) / HEAD_DIM)
    inv_row = jnp.tile(inv, V7X_LANES // (HEAD_DIM // 2))[None, :]
    ts = min(SEQ_TILE, s)
    out = jax.ShapeDtypeStruct((b, s, V7X_LANES), F32)
    return pl.pallas_call(
        _rope_body,
        grid=(b, s // ts),
        in_specs=[pl.BlockSpec((1, ts, 1), lambda i, j: (i, j, 0)),
                  pl.BlockSpec((1, V7X_LANES), lambda i, j: (0, 0))],
        out_specs=[pl.BlockSpec((1, ts, V7X_LANES), lambda i, j: (i, j, 0))] * 2,
        out_shape=[out, out],
        compiler_params=_params(("parallel", "parallel")),
        name="rope_tables",
    )(positions[:, :, None], inv_row)


def _ffn_math(h, ffn_refs):
    gpre_ref, gpost_ref, wg_ref, wu_ref, wd_ref = ffn_refs
    xn = _rms(h, gpre_ref[...]).astype(MXU_DTYPE)
    gate = _mm(xn, wg_ref[...])
    up = _mm(xn, wu_ref[...])
    act = (gate * _sigmoid(gate) * up).astype(MXU_DTYPE)
    f = _mm(act, wd_ref[...])
    return h + 0.5 * _rms(f, gpost_ref[...])


def _ffn_operands(g_pre, g_post, w_gate, w_up, w_down):
    d, dff = w_gate.shape
    specs = [_const_spec((1, d)), _const_spec((1, d)),
             _const_spec((d, dff)), _const_spec((d, dff)), _const_spec((dff, d))]
    return specs, (g_pre[None, :], g_post[None, :], w_gate, w_up, w_down)


N_FFN_OPERANDS = 5


def _swap_halves(x):
    n = x.shape[1]
    lane = lax.broadcasted_iota(jnp.int32, x.shape, 1)
    first_half = (lane & (HEAD_DIM - 1)) < HEAD_DIM // 2
    return jnp.where(first_half, pltpu.roll(x, n - HEAD_DIM // 2, 1), pltpu.roll(x, HEAD_DIM // 2, 1))


def _ffn_inproj_body(cols, h_ref, *refs):
    ffn_refs, (g_ref, w_ref, cos_ref, sin_ref) = refs[:N_FFN_OPERANDS], refs[N_FFN_OPERANDS:N_FFN_OPERANDS + 4]
    h_out_ref, xpl_ref, qt_ref, kc_ref, vc_ref, ks_ref, vst_ref, kw_ref, vwt_ref, gatet_ref = \
        refs[N_FFN_OPERANDS + 4:]
    h = _ffn_math(h_ref[0], ffn_refs)
    h_out_ref[0] = h
    xn = _rms(h, g_ref[...]).astype(MXU_DTYPE)
    proj = _mm(xn, w_ref[...])
    cos = cos_ref[0]
    sin = sin_ref[0]

    def rope(x):
        rep = x.shape[1] // V7X_LANES
        c = jnp.concatenate([cos] * rep, axis=1) if rep > 1 else cos
        s = jnp.concatenate([sin] * rep, axis=1) if rep > 1 else sin
        return x * c + _swap_halves(x) * s

    def seg(name):
        lo, hi = cols[name]
        return proj[:, lo:hi]

    xpl_ref[0] = seg("xpl")
    qt_ref[0] = (rope(seg("q")) * Q_SCALE).T.astype(qt_ref.dtype)
    kc_ref[0] = rope(seg("kc"))
    vc_ref[0] = seg("vc")
    ks_ref[0] = rope(seg("ks")).astype(ks_ref.dtype)
    vst_ref[0] = seg("vs").T.astype(vst_ref.dtype)
    kw_ref[0] = rope(seg("kw")).astype(kw_ref.dtype)
    vwt_ref[0] = seg("vw").T.astype(vwt_ref.dtype)
    gatet_ref[0] = _sigmoid(seg("g")).T[:gatet_ref.shape[1]]


def _ffn_inproj(h3, ffn_weights, g_pre, w_all, cols, cos, sin):
    b, s, d = h3.shape
    tm = min(ROW_TILE, s)
    ncol = w_all.shape[1]
    width = lambda n: cols[n][1] - cols[n][0]
    row = lambda w: pl.BlockSpec((1, tm, w), lambda i, j: (i, j, 0))
    col = lambda w: pl.BlockSpec((1, w, tm), lambda i, j: (i, 0, j))
    tok = lambda n, dt: (jax.ShapeDtypeStruct((b, s, width(n)), dt), row(width(n)))
    chan = lambda w, dt: (jax.ShapeDtypeStruct((b, w, s), dt), col(w))
    outs = [(jax.ShapeDtypeStruct((b, s, d), F32), row(d)),
            tok("xpl", F32), chan(width("q"), MXU_DTYPE), tok("kc", F32), tok("vc", F32),
            tok("ks", MXU_DTYPE), chan(width("vs"), MXU_DTYPE), tok("kw", MXU_DTYPE),
            chan(width("vw"), MXU_DTYPE), chan(N_BRANCH * N_HEADS, F32)]
    ffn_specs, ffn_args = _ffn_operands(*ffn_weights)
    return pl.pallas_call(
        functools.partial(_ffn_inproj_body, cols),
        grid=(b, s // tm),
        in_specs=[row(d)] + ffn_specs +
                 [_const_spec((1, d)), _const_spec((d, ncol)), row(V7X_LANES), row(V7X_LANES)],
        out_specs=[o[1] for o in outs],
        out_shape=[o[0] for o in outs],
        compiler_params=_params(("parallel", "parallel")),
        name="ffn_mixer_in_proj",
    )(h3, *ffn_args, g_pre[None, :], w_all, cos, sin)


def _poollru_body(x_ref, pw_ref, pscale_ref, cw_ref, cb_ref, wri_ref, bri_ref, lam_ref,
                  y_ref, pool_carry, conv_carry, h_carry):
    si = pl.program_id(1)
    ts = x_ref.shape[1]
    width = pw_ref.shape[0]
    halo_p = pool_carry.shape[0]
    halo_c = conv_carry.shape[0]

    @pl.when(si == 0)
    def _():
        pool_carry[...] = jnp.zeros_like(pool_carry)
        conv_carry[...] = jnp.zeros_like(conv_carry)
        h_carry[...] = jnp.zeros_like(h_carry)

    x = x_ref[0]
    xp = x[:, :width]
    xl = x[:, width:2 * width]
    gl = x[:, 2 * width:]

    ext = jnp.concatenate([pool_carry[...], xp], axis=0)
    sums = [ext]
    shift = 1
    for _ in POOL_WINDOWS:
        sums.append(sums[-1] + pltpu.roll(sums[-1], shift, 0))
        shift *= 2
    lane = lax.broadcasted_iota(jnp.int32, (1, width), 1)
    grp = lax.shift_right_logical(lane, _log2(POOL_GROUP))
    win_sum = sums[len(POOL_WINDOWS)]
    win = jnp.full((1, width), float(POOL_WINDOWS[-1]), F32)
    for gi in range(len(POOL_WINDOWS) - 2, -1, -1):
        win_sum = jnp.where(grp == gi, sums[gi + 1], win_sum)
        win = jnp.where(grp == gi, float(POOL_WINDOWS[gi]), win)
    win_sum = win_sum[halo_p:]
    t_abs = si * ts + lax.broadcasted_iota(jnp.int32, (ts, 1), 0)
    cnt = jnp.minimum((t_abs + 1).astype(F32), win)
    pooled = win_sum / cnt
    y_pool = _mm((pooled - xp).astype(MXU_DTYPE), pw_ref[...]) * pscale_ref[...]
    pool_carry[...] = xp[ts - halo_p:]

    extc = jnp.concatenate([conv_carry[...], xl], axis=0)
    xc = extc * cw_ref[CONV_WIDTH - 1:CONV_WIDTH, :]
    for k in range(1, CONV_WIDTH):
        xc = xc + pltpu.roll(extc, k, 0) * cw_ref[CONV_WIDTH - 1 - k:CONV_WIDTH - k, :]
    xc = xc[halo_c:] + cb_ref[...]
    conv_carry[...] = xl[ts - halo_c:]

    ri = _mm(xc.astype(MXU_DTYPE), wri_ref[...]) + bri_ref[...]
    r = _sigmoid(ri[:, :width])
    i_gate = _sigmoid(ri[:, width:])
    neg_lam = -lam_ref[...]
    softplus = jnp.maximum(neg_lam, 0.0) + jnp.log1p(jnp.exp(-jnp.abs(neg_lam)))
    log_a = -LRU_C * r * softplus
    a = jnp.exp(log_a)
    b = jnp.sqrt(-jnp.tanh(log_a) * (a * a + 1.0)) * (i_gate * xc)

    row = lax.broadcasted_iota(jnp.int32, (ts, 1), 0)
    k = 1
    while k < ts:
        keep = row >= k
        a_prev = jnp.where(keep, pltpu.roll(a, k, 0), 1.0)
        b_prev = jnp.where(keep, pltpu.roll(b, k, 0), 0.0)
        b = a * b_prev + b
        a = a * a_prev
        k *= 2
    h = a * h_carry[0:1, :] + b
    h_carry[...] = jnp.broadcast_to(h[ts - 1:ts, :], h_carry.shape)

    gelu = 0.5 * gl * (1.0 + jnp.tanh(0.7978845608028654 * (gl + 0.044715 * gl * gl * gl)))
    y_ref[0] = jnp.concatenate([y_pool, h * gelu], axis=1).astype(y_ref.dtype)


def _poollru(xpl, pool_w_bd, pool_scale, conv_w, conv_b, w_ri_bd, b_ri, lam):
    b, s, w3 = xpl.shape
    width = w3 // 3
    ts = min(SEQ_TILE, s)
    return pl.pallas_call(
        _poollru_body,
        grid=(b, s // ts),
        in_specs=[pl.BlockSpec((1, ts, w3), lambda i, j: (i, j, 0)),
                  _const_spec((width, width)), _const_spec((1, width)),
                  _const_spec((CONV_WIDTH, width)), _const_spec((1, width)),
                  _const_spec((width, 2 * width)), _const_spec((1, 2 * width)), _const_spec((1, width))],
        out_specs=pl.BlockSpec((1, ts, 2 * width), lambda i, j: (i, j, 0)),
        out_shape=jax.ShapeDtypeStruct((b, s, 2 * width), MXU_DTYPE),
        scratch_shapes=[pltpu.VMEM((POOL_WINDOWS[-1], width), F32),
                        pltpu.VMEM((8, width), F32),
                        pltpu.VMEM((8, width), F32)],
        compiler_params=_params(("parallel", "arbitrary")),
        name="pool_rglru",
    )(xpl, pool_w_bd, pool_scale[None, :], conv_w, conv_b[None, :], w_ri_bd, b_ri[None, :], lam[None, :])


def _compress_body(transpose_out, r_ref, pea_ref, peb_ref, wa_ref, wb_ref, o_ref):
    r = r_ref[0]
    first = _mm((r + pea_ref[...]).astype(MXU_DTYPE), wa_ref[...])
    second = _mm((r + peb_ref[...]).astype(MXU_DTYPE), wb_ref[...])
    n1 = r.shape[0]
    out = first + pltpu.roll(second, n1 - 1, 0)
    o_ref[0] = (out.T if transpose_out else out).astype(o_ref.dtype)


def _compress(rows, pe_a, pe_b, w_a, w_b, transpose_out):
    b, n1, k = rows.shape
    n_out = w_a.shape[1]
    out_dims = (n_out, n1) if transpose_out else (n1, n_out)
    return pl.pallas_call(
        functools.partial(_compress_body, transpose_out),
        grid=(b,),
        in_specs=[pl.BlockSpec((1, n1, k), lambda i: (i, 0, 0)),
                  _const_spec((1, k)), _const_spec((1, k)), _const_spec((k, n_out)), _const_spec((k, n_out))],
        out_specs=pl.BlockSpec((1,) + out_dims, lambda i: (i, 0, 0)),
        out_shape=jax.ShapeDtypeStruct((b,) + out_dims, MXU_DTYPE),
        compiler_params=_params(("parallel",)),
        name="compress_kv",
    )(rows, pe_a, pe_b, w_a, w_b)


def _stack_heads(q_ref):
    return jnp.concatenate([q_ref[0, g * HEAD_DIM:(g + 1) * HEAD_DIM, :] for g in range(GQA_GROUP)], axis=1)


def _place_in_kv_half(qt, kvh):
    zeros = jnp.zeros_like(qt)
    parts = [jnp.where(kvh == h, qt, zeros) for h in range(N_KV_HEADS)]
    return jnp.concatenate(parts, axis=0)


def _query_positions(start, tq):
    lane = lax.broadcasted_iota(jnp.int32, (1, GQA_GROUP * tq), 1)
    return start + (lane & (tq - 1))


def _with_ones(v_t):
    return jnp.concatenate([v_t, jnp.ones((ONES_ROWS, v_t.shape[1]), v_t.dtype)], axis=0)


def _normalize(acc):
    return acc[:HEAD_DIM] * (1.0 / acc[HEAD_DIM:HEAD_DIM + 1])


def _cmp_body(n_sel, q_ref, kc_ref, vct_ref, ovt_ref, oc_ref, mb_ref, imp_ref):
    tq = q_ref.shape[2]
    kvh = pl.program_id(1)
    start = pl.program_id(2) * tq
    qt = _stack_heads(q_ref)
    qp = _place_in_kv_half(qt, kvh)
    n1 = kc_ref.shape[1]
    t_row = _query_positions(start, tq)
    assert CMP_STRIDE * CMP_CHUNK >= tq + CMP_LEN

    def attend(nrows):
        s = _mm(kc_ref[0, :nrows, :], qp)
        lo = max(0, nrows - 2 * CMP_CHUNK)
        n_idx = lo + lax.broadcasted_iota(jnp.int32, (nrows - lo, 1), 0)
        tail = jnp.where(n_idx * CMP_STRIDE + (CMP_LEN - 1) <= t_row, s[lo:], NEG_INF)
        s = jnp.concatenate([s[:lo], tail], axis=0) if lo else tail
        m = jnp.max(s, axis=0, keepdims=True)
        p = jnp.exp2(s - jnp.where(m > 0.5 * NEG_INF, m, 0.0))
        den = jnp.sum(p, axis=0, keepdims=True)
        pn = p * (1.0 / jnp.where(den > 0.0, den, 1.0))
        oc = _mm(vct_ref[0, :, :nrows], pn.astype(MXU_DTYPE))
        psum = pn[:, 0:tq]
        for g in range(GQA_GROUP):
            oc_ref[0, g * HEAD_DIM:(g + 1) * HEAD_DIM, :] = oc[:, g * tq:(g + 1) * tq]
            if g:
                psum = psum + pn[:, g * tq:(g + 1) * tq]
        ovt = ovt_ref[:, :nrows]
        hi = psum.astype(MXU_DTYPE)
        rem = psum - hi.astype(F32)
        mid = rem.astype(MXU_DTYPE)
        low = (rem - mid.astype(F32)).astype(MXU_DTYPE)
        imp_ref[...] = _mm(ovt, hi) + _mm(ovt, mid) + _mm(ovt, low)

    n_need = lax.shift_right_logical(start + tq, _log2(CMP_STRIDE)) - 1
    chunks = lax.shift_right_logical(n_need + CMP_CHUNK - 1, _log2(CMP_CHUNK))
    n_variants = -(-n1 // CMP_CHUNK)
    for c in range(1, n_variants + 1):
        pl.when(chunks == c)(functools.partial(attend, min(c * CMP_CHUNK, n1)))
    imp = imp_ref[...]

    n_pad = imp.shape[0]
    j = lax.broadcasted_iota(jnp.int32, (n_pad, 1), 0).astype(F32)
    t_q = start + lax.broadcasted_iota(jnp.int32, (1, tq), 1)
    cur = lax.shift_right_logical(t_q, _log2(SEL_BLOCK)).astype(F32)
    forced = (j == 0.0) | (j == cur) | (j == cur - 1.0)
    valid = j <= cur
    base = jnp.where(valid, jnp.where(forced, BIG_SCORE, imp), -BIG_SCORE)
    base = jnp.where(j < float(n_sel), base, -jnp.inf)
    assert n_sel >= SEL_TOPK

    score = jnp.where(forced, -jnp.inf, base)
    for _ in range(SEL_TOPK - 3):
        score = jnp.where(score == jnp.max(score, axis=0, keepdims=True), -jnp.inf, score)
    bias = jnp.where(score == -jnp.inf, 0.0, NEG_INF)
    mb_ref[0, 0] = bias.astype(mb_ref.dtype)
    taken = jnp.sum(jnp.where(valid, jnp.where(bias == 0.0, 1.0, 0.0), 0.0), axis=0, keepdims=True)
    tie = jnp.max(jnp.abs(taken - jnp.minimum(cur + 1.0, float(SEL_TOPK)))) > 0.0

    @pl.when(tie)
    def _():
        score = base
        bias = jnp.full(score.shape, NEG_INF, F32)
        for _ in range(SEL_TOPK):
            best = jnp.max(score, axis=0, keepdims=True)
            first = jnp.min(jnp.where(score == best, j, float(n_pad)), axis=0, keepdims=True)
            pick = j == first
            bias = jnp.where(pick, 0.0, bias)
            score = jnp.where(pick, -jnp.inf, score)
        mb_ref[0, 0] = bias.astype(mb_ref.dtype)


def _cmp_attention(qt, k_cmp, v_cmp_t, ov_t, n_sel):
    b, _, s = qt.shape
    n1 = k_cmp.shape[1]
    n_pad = ov_t.shape[0]
    tq = min(CMP_Q_TILE, s)
    group_rows = GQA_GROUP * HEAD_DIM
    head_blk = pl.BlockSpec((1, group_rows, tq), lambda i, h, j: (i, h, j))
    return pl.pallas_call(
        functools.partial(_cmp_body, n_sel),
        grid=(b, N_KV_HEADS, s // tq),
        in_specs=[head_blk,
                  pl.BlockSpec((1, n1, KV_WIDTH), lambda i, h, j: (i, 0, 0)),
                  pl.BlockSpec((1, HEAD_DIM, n1), lambda i, h, j: (i, h, 0)),
                  _const_spec((n_pad, n1))],
        out_specs=[head_blk, pl.BlockSpec((1, 1, n_pad, tq), lambda i, h, j: (i, h, 0, j))],
        out_shape=[jax.ShapeDtypeStruct((b, N_HEADS * HEAD_DIM, s), F32),
                   jax.ShapeDtypeStruct((b, N_KV_HEADS, n_pad, s), MXU_DTYPE)],
        scratch_shapes=[pltpu.VMEM((n_pad, tq), F32)],
        compiler_params=_params(("parallel", "parallel", "parallel")),
        name="cmp_attention_select",
    )(qt, k_cmp, v_cmp_t, ov_t)


def _selwin_body(q_ref, mb_ref, ks_ref, oh_ref, vst_ref, kw_ref, vwt_ref, oc_ref, gate_ref, y_ref,
                 qa_ref, acc_ref, win_ref):
    tq = q_ref.shape[2]
    tk = tq
    kvh = pl.program_id(1)
    qi = pl.program_id(2)
    start = qi * tq
    rows = GQA_GROUP * tq
    qt = _stack_heads(q_ref)
    qp = _place_in_kv_half(qt, kvh)
    t_row = _query_positions(start, tq)
    group_shift = _log2(SEL_GROUP * SEL_BLOCK // tk)
    n_groups = mb_ref.shape[2] // SEL_GROUP

    for grp in range(n_groups):
        bias_rows = mb_ref[0, 0, grp * SEL_GROUP:(grp + 1) * SEL_GROUP, :]
        qa_ref[grp] = jnp.concatenate([jnp.concatenate([bias_rows] * GQA_GROUP, axis=1), qp], axis=0)

    def scores(kt, n_sub, causal):
        key_rows = pl.ds(pl.multiple_of(kt * tk, tk), n_sub * tk)
        k = jnp.concatenate([oh_ref[key_rows, :], ks_ref[0, key_rows, :]], axis=1)
        s = _mm(k, qa_ref[lax.shift_right_logical(kt, group_shift)])
        if causal:
            kp = kt * tk + lax.broadcasted_iota(jnp.int32, (n_sub * tk, 1), 0)
            s = jnp.where(kp <= t_row, s, NEG_INF)
        return s

    def values(kt, n_sub):
        return _with_ones(vst_ref[0, :, pl.ds(pl.multiple_of(kt * tk, tk), n_sub * tk)])

    def col_max(s):
        return jnp.max(s, axis=0, keepdims=True)

    n_past = qi

    own_rows = pl.ds(pl.multiple_of(start, tq), tq)
    old_first = pl.multiple_of(jnp.maximum(start - WINDOW, 0), tq)
    old_rows = pl.ds(old_first, WINDOW)
    oldest = jnp.maximum(t_row - WINDOW, -1)

    def window_scores_own():
        kp = start + lax.broadcasted_iota(jnp.int32, (tq, 1), 0)
        return jnp.where(kp <= t_row, _mm(kw_ref[0, own_rows, :], qp), NEG_INF)

    def window_scores_old():
        kp = old_first + lax.broadcasted_iota(jnp.int32, (WINDOW, 1), 0)
        kp = jnp.where(kp < start, kp, -2)
        return jnp.where(kp > oldest, _mm(kw_ref[0, old_rows, :], qp), NEG_INF)

    def pv(v_t, s, ref):
        return _mm(_with_ones(v_t), jnp.exp2(s - ref).astype(MXU_DTYPE))

    def self_scores(k_ref):
        k_t = k_ref[0, own_rows, :].astype(F32).T
        k_own = k_t[:HEAD_DIM]
        for h in range(1, N_KV_HEADS):
            k_own = jnp.where(kvh == h, k_t[h * HEAD_DIM:(h + 1) * HEAD_DIM], k_own)
        parts = [jnp.sum(qt[:, g * tq:(g + 1) * tq].astype(F32) * k_own, axis=0, keepdims=True)
                 for g in range(GQA_GROUP)]
        return jnp.concatenate(parts, axis=1)

    ref = self_scores(ks_ref)
    ref_w = self_scores(kw_ref)
    s_own = window_scores_own()
    s_old = window_scores_old()
    win_ref[...] = pv(vwt_ref[0, :, own_rows], s_own, ref_w) + pv(vwt_ref[0, :, old_rows], s_old, ref_w)
    excess_w = jnp.max(jnp.maximum(col_max(s_own), col_max(s_old)) - ref_w)

    def stream(kt, n_sub, seen_max, causal=False):
        s = scores(kt, n_sub, causal)
        acc_ref[...] += _mm(values(kt, n_sub), jnp.exp2(s - ref).astype(MXU_DTYPE))
        return jnp.maximum(seen_max, col_max(s))

    acc_ref[...] = jnp.zeros_like(acc_ref)
    n_big = lax.shift_right_logical(n_past, _log2(STREAM_TILES))
    seen_max = lax.fori_loop(0, n_big, lambda i, c: stream(i * STREAM_TILES, STREAM_TILES, c), ref)
    seen_max = lax.fori_loop(n_big * STREAM_TILES, n_past, lambda i, c: stream(i, 1, c), seen_max)
    seen_max = stream(qi, 1, seen_max, causal=True)
    overflow = jnp.maximum(jnp.max(seen_max - ref), excess_w) > EXP_HEADROOM

    @pl.when(overflow)
    def _():
        def tile(kt, causal):
            s = scores(kt, 1, causal)
            m_t = col_max(s)
            return m_t, _mm(values(kt, 1), jnp.exp2(s - m_t).astype(MXU_DTYPE))

        def merge(carry, part):
            m, acc = carry
            m_t, o_t = part
            m_new = jnp.maximum(m, m_t)
            return m_new, acc * jnp.exp2(m - m_new) + o_t * jnp.exp2(m_t - m_new)

        carry = (jnp.full((1, rows), M_INIT, F32), jnp.zeros(acc_ref.shape, F32))
        carry = lax.fori_loop(0, n_past, lambda i, c: merge(c, tile(i, False)), carry)
        acc_ref[...] = merge(carry, tile(qi, True))[1]

        x_own = window_scores_own()
        x_old = window_scores_old()
        m_w = jnp.maximum(col_max(x_own), col_max(x_old))
        win_ref[...] = pv(vwt_ref[0, :, own_rows], x_own, m_w) + pv(vwt_ref[0, :, old_rows], x_old, m_w)

    o_sel = _normalize(acc_ref[...])
    o_win = _normalize(win_ref[...])

    gate_row = kvh * (GQA_GROUP * N_BRANCH)
    for g in range(GQA_GROUP):
        cols = slice(g * tq, (g + 1) * tq)
        head = slice(g * HEAD_DIM, (g + 1) * HEAD_DIM)
        gates = [gate_ref[0, pl.ds(gate_row + N_BRANCH * g + br, 1), :] for br in range(N_BRANCH)]
        y = gates[0] * oc_ref[0, head, :] + gates[1] * o_sel[:, cols] + gates[2] * o_win[:, cols]
        y_ref[0, head, :] = y.astype(y_ref.dtype)


def _sel_win_attention(qt, mb_t, ks, key_onehot, vs_t, kw, vw_t, oc_t, gates_t):
    b, _, s = qt.shape
    tq = min(SEL_TILE, s)
    n_pad = mb_t.shape[2]
    n_gate = gates_t.shape[1]
    group_rows = GQA_GROUP * HEAD_DIM
    head_blk = pl.BlockSpec((1, group_rows, tq), lambda i, h, j: (i, h, j))
    once = pl.Buffered(1)
    keys = pl.BlockSpec((1, s, KV_WIDTH), lambda i, h, j: (i, 0, 0), pipeline_mode=once)
    vals = pl.BlockSpec((1, HEAD_DIM, s), lambda i, h, j: (i, h, 0), pipeline_mode=once)
    acc = pltpu.VMEM((HEAD_DIM + ONES_ROWS, GQA_GROUP * tq), F32)
    return pl.pallas_call(
        _selwin_body,
        grid=(b, N_KV_HEADS, s // tq),
        in_specs=[head_blk,
                  pl.BlockSpec((1, 1, n_pad, tq), lambda i, h, j: (i, h, 0, j)),
                  keys, _const_spec(key_onehot.shape), vals, keys, vals,
                  head_blk,
                  pl.BlockSpec((1, n_gate, tq), lambda i, h, j: (i, 0, j))],
        out_specs=head_blk,
        out_shape=jax.ShapeDtypeStruct((b, N_HEADS * HEAD_DIM, s), MXU_DTYPE),
        scratch_shapes=[pltpu.VMEM((n_pad // SEL_GROUP, SEL_GROUP + KV_WIDTH, GQA_GROUP * tq), MXU_DTYPE),
                        acc, acc],
        compiler_params=_params(("parallel", "parallel", "arbitrary")),
        name="sel_win_attention",
    )(qt, mb_t, ks, key_onehot, vs_t, kw, vw_t, oc_t, gates_t)


def _outproj_ffn_body(h_ref, ypl_ref, yat_ref, w1_ref, w2_ref, g_ref, *refs):
    ffn_refs, o_ref = refs[:N_FFN_OPERANDS], refs[N_FFN_OPERANDS]
    m = _mm(ypl_ref[0], w1_ref[...]) + _mm_tn(yat_ref[0], w2_ref[...])
    o_ref[0] = _ffn_math(h_ref[0] + _rms(m, g_ref[...]), ffn_refs)


def _outproj_ffn(h3, ypl, yat_t, w1, w2, g_post, ffn_weights):
    b, s, d = h3.shape
    tm = min(ROW_TILE, s)
    row = lambda w: pl.BlockSpec((1, tm, w), lambda i, j: (i, j, 0))
    ffn_specs, ffn_args = _ffn_operands(*ffn_weights)
    return pl.pallas_call(
        _outproj_ffn_body,
        grid=(b, s // tm),
        in_specs=[row(d), row(ypl.shape[2]),
                  pl.BlockSpec((1, yat_t.shape[1], tm), lambda i, j: (i, 0, j)),
                  _const_spec(w1.shape), _const_spec(w2.shape), _const_spec((1, d))] + ffn_specs,
        out_specs=row(d),
        out_shape=jax.ShapeDtypeStruct((b, s, d), F32),
        compiler_params=_params(("parallel", "parallel")),
        name="mixer_out_proj_ffn",
    )(h3, ypl, yat_t, w1, w2, g_post[None, :], *ffn_args)


def _block_diag(blocks):
    n, a, b = blocks.shape
    eye = jnp.eye(n, dtype=blocks.dtype)
    return jnp.einsum("nab,nm->namb", blocks, eye).reshape(n * a, n * b)


def _compress_weights(w, pe):
    eye = jnp.eye(N_KV_HEADS, dtype=w.dtype)
    halves = []
    for part in range(CMP_LEN // CMP_STRIDE):
        wl = w[part * CMP_STRIDE:(part + 1) * CMP_STRIDE]
        pel = pe[part * CMP_STRIDE:(part + 1) * CMP_STRIDE]
        wm = jnp.einsum("lde,hg->lhdge", wl, eye).reshape(CMP_STRIDE * KV_WIDTH, KV_WIDTH)
        pm = jnp.tile(pel[:, None, :], (1, N_KV_HEADS, 1)).reshape(1, -1)
        halves.append((wm.astype(MXU_DTYPE), pm))
    return halves


def kernel(x, positions, ffn1_pre_g, ffn1_post_g, ffn1_w_gate, ffn1_w_up, ffn1_w_down, mix_pre_g, mix_post_g,
           w_in, w_out, pool_w, pool_scale, conv_w, conv_b, lru_w_r, lru_b_r, lru_w_i, lru_b_i, lru_lambda,
           cmp_w_k, cmp_w_v, cmp_pe, ffn2_pre_g, ffn2_post_g, ffn2_w_gate, ffn2_w_up, ffn2_w_down):
    b, s, d = x.shape
    depth = w_in.shape[0]
    t = b * s
    pool_width = pool_w.shape[1] * pool_w.shape[2]
    lru_width = lru_w_r.shape[1] * lru_w_r.shape[2]
    attn_width = N_HEADS * HEAD_DIM
    assert pool_width == lru_width and s % SEL_TILE == 0 and s % (CMP_STRIDE * 8) == 0
    assert WINDOW % SEL_TILE == 0 and (SEL_GROUP * SEL_BLOCK) % (STREAM_TILES * SEL_TILE) == 0
    assert s >= WINDOW + SEL_TILE

    sizes = [("xpl", pool_width + 2 * lru_width), ("q", attn_width), ("kc", KV_WIDTH), ("vc", KV_WIDTH),
             ("ks", KV_WIDTH), ("vs", KV_WIDTH), ("kw", KV_WIDTH), ("vw", KV_WIDTH), ("g", V7X_LANES)]
    cols, off = {}, 0
    for name, width in sizes:
        cols[name] = (off, off + width)
        off += width

    n_sel = s // SEL_BLOCK
    n_pad = -(-n_sel // SEL_GROUP) * SEL_GROUP
    n1 = s // CMP_STRIDE
    n_cmp = (s - CMP_LEN) // CMP_STRIDE + 1
    cmp_start = jnp.arange(n1) * CMP_STRIDE
    sel_start = jnp.arange(n_pad) * SEL_BLOCK
    overlap_t = ((cmp_start[None, :] < sel_start[:, None] + SEL_BLOCK) &
                 (cmp_start[None, :] + CMP_LEN > sel_start[:, None]) &
                 (jnp.arange(n1)[None, :] < n_cmp) & (jnp.arange(n_pad)[:, None] < n_sel)).astype(MXU_DTYPE)
    key_blk = (jnp.arange(s) // SEL_BLOCK) % SEL_GROUP
    key_onehot = (key_blk[:, None] == jnp.arange(SEL_GROUP)[None, :]).astype(MXU_DTYPE)

    cos, sin = _rope_tables(positions)

    h = x
    for l in range(depth):
        cast = lambda w: w[l].astype(MXU_DTYPE)
        ffn1 = (ffn1_pre_g[l], ffn1_post_g[l], cast(ffn1_w_gate), cast(ffn1_w_up), cast(ffn1_w_down))
        ffn2 = (ffn2_pre_g[l], ffn2_post_g[l], cast(ffn2_w_gate), cast(ffn2_w_up), cast(ffn2_w_down))

        w_all = jnp.pad(w_in[l], ((0, 0), (0, off - w_in.shape[2]))).astype(MXU_DTYPE)
        h, xpl, qt, kc, vc, ks, vs_t, kw, vw_t, gates_t = _ffn_inproj(
            h, ffn1, mix_pre_g[l], w_all, cols, cos, sin)

        w_ri = jnp.concatenate([_block_diag(lru_w_r[l]), _block_diag(lru_w_i[l])], axis=1).astype(MXU_DTYPE)
        ypl = _poollru(xpl, _block_diag(pool_w[l]).astype(MXU_DTYPE), pool_scale[l],
                       conv_w[l], conv_b[l], w_ri, jnp.concatenate([lru_b_r[l], lru_b_i[l]]), lru_lambda[l])

        (wk_a, pe_a), (wk_b, pe_b) = _compress_weights(cmp_w_k[l], cmp_pe[l])
        (wv_a, _), (wv_b, _) = _compress_weights(cmp_w_v[l], cmp_pe[l])
        k_cmp = _compress(kc.reshape(b, n1, -1), pe_a, pe_b, wk_a, wk_b, False)
        v_cmp_t = _compress(vc.reshape(b, n1, -1), pe_a, pe_b, wv_a, wv_b, True)

        oc_t, mb_t = _cmp_attention(qt, k_cmp, v_cmp_t, overlap_t, n_sel)
        y_t = _sel_win_attention(qt, mb_t, ks, key_onehot, vs_t, kw, vw_t, oc_t, gates_t)

        w_o = w_out[l].astype(MXU_DTYPE)
        split = pool_width + lru_width
        h = _outproj_ffn(h, ypl, y_t, w_o[:split], w_o[split:], mix_post_g[l], ffn2)
    return h
```

```python
import functools

import jax
import jax.numpy as jnp
from jax import lax
from jax.experimental import pallas as pl
from jax.experimental.pallas import tpu as pltpu

F32 = jnp.float32
MXU_DTYPE = jnp.bfloat16

POOL_WINDOWS = (2, 4, 8, 16)
POOL_GROUP = 64
LRU_HEAD_DIM = 64
LRU_C = 8.0
CONV_WIDTH = 4
HEAD_DIM = 64
N_KV_HEADS = 2
GQA_GROUP = 4
N_HEADS = N_KV_HEADS * GQA_GROUP
KV_WIDTH = N_KV_HEADS * HEAD_DIM
N_BRANCH = 3
CMP_LEN = 32
CMP_STRIDE = 16
SEL_BLOCK = 64
SEL_TOPK = 16
WINDOW = 512
ROPE_THETA = 10000.0
NORM_EPS = 1e-6
NEG_INF = -1e30
BIG_SCORE = 1e9

V7X_LANES = 128
V7X_VMEM_LIMIT_BYTES = 56 * 1024 * 1024

ROW_TILE = 512
SEQ_TILE = 512
CMP_Q_TILE = 512
CMP_CHUNK = 256
SEL_TILE = 512
SEL_GROUP = 128
ONES_ROWS = 16
M_INIT = -1e20
STREAM_TILES = 2
EXP_HEADROOM = 64.0
Q_SCALE = HEAD_DIM ** -0.5 * 1.4426950408889634


def _params(semantics):
    return pltpu.CompilerParams(dimension_semantics=semantics, vmem_limit_bytes=V7X_VMEM_LIMIT_BYTES)


def _rms(x, g):
    return x * lax.rsqrt(jnp.mean(x * x, axis=-1, keepdims=True) + NORM_EPS) * g


def _sigmoid(x):
    return 1.0 / (1.0 + jnp.exp(-x))


def _mm(a, b):
    return jnp.dot(a, b, preferred_element_type=F32)


def _mm_tn(a_t, b):
    return lax.dot_general(a_t, b, (((0,), (0,)), ((), ())), preferred_element_type=F32)


def _log2(n):
    assert n & (n - 1) == 0
    return n.bit_length() - 1


def _const_spec(shape):
    zeros = (0,) * len(shape)
    return pl.BlockSpec(shape, lambda *_: zeros, pipeline_mode=pl.Buffered(1))


def _rope_body(pos_ref, inv_ref, cos_ref, sin_ref):
    ang = pos_ref[0].astype(F32) * inv_ref[...]
    lane = lax.broadcasted_iota(jnp.int32, ang.shape, 1)
    cos_ref[0] = jnp.cos(ang)
    sin_ref[0] = jnp.where((lane & (HEAD_DIM - 1)) < HEAD_DIM // 2, -jnp.sin(ang), jnp.sin(ang))


def _rope_tables(positions):
    b, s = positions.shape
    inv = ROPE_THETA ** (-jnp.arange(0, HEAD_DIM, 2, dtype=F32) / HEAD_DIM)
    inv_row = jnp.tile(inv, V7X_LANES // (HEAD_DIM // 2))[None, :]
    ts = min(SEQ_TILE, s)
    out = jax.ShapeDtypeStruct((b, s, V7X_LANES), F32)
    return pl.pallas_call(
        _rope_body,
        grid=(b, s // ts),
        in_specs=[pl.BlockSpec((1, ts, 1), lambda i, j: (i, j, 0)),
                  pl.BlockSpec((1, V7X_LANES), lambda i, j: (0, 0))],
        out_specs=[pl.BlockSpec((1, ts, V7X_LANES), lambda i, j: (i, j, 0))] * 2,
        out_shape=[out, out],
        compiler_params=_params(("parallel", "parallel")),
        name="rope_tables",
    )(positions[:, :, None], inv_row)


def _ffn_math(h, ffn_refs):
    gpre_ref, gpost_ref, wg_ref, wu_ref, wd_ref = ffn_refs
    xn = _rms(h, gpre_ref[...]).astype(MXU_DTYPE)
    gate = _mm(xn, wg_ref[...])
    up = _mm(xn, wu_ref[...])
    act = (gate * _sigmoid(gate) * up).astype(MXU_DTYPE)
    f = _mm(act, wd_ref[...])
    return h + 0.5 * _rms(f, gpost_ref[...])


def _ffn_operands(g_pre, g_post, w_gate, w_up, w_down):
    d, dff = w_gate.shape
    specs = [_const_spec((1, d)), _const_spec((1, d)),
             _const_spec((d, dff)), _const_spec((d, dff)), _const_spec((dff, d))]
    return specs, (g_pre[None, :], g_post[None, :], w_gate, w_up, w_down)


N_FFN_OPERANDS = 5


def _ffn_body(h_ref, *refs):
    refs[N_FFN_OPERANDS][0] = _ffn_math(h_ref[0], refs[:N_FFN_OPERANDS])


def _ffn(h3, ffn_weights):
    b, s, d = h3.shape
    tm = min(ROW_TILE, s)
    row = pl.BlockSpec((1, tm, d), lambda i, j: (i, j, 0))
    ffn_specs, ffn_args = _ffn_operands(*ffn_weights)
    return pl.pallas_call(
        _ffn_body,
        grid=(b, s // tm),
        in_specs=[row] + ffn_specs,
        out_specs=row,
        out_shape=jax.ShapeDtypeStruct((b, s, d), F32),
        compiler_params=_params(("parallel", "parallel")),
        name="ffn",
    )(h3, *ffn_args)


def _swap_halves(x):
    n = x.shape[1]
    lane = lax.broadcasted_iota(jnp.int32, x.shape, 1)
    first_half = (lane & (HEAD_DIM - 1)) < HEAD_DIM // 2
    return jnp.where(first_half, pltpu.roll(x, n - HEAD_DIM // 2, 1), pltpu.roll(x, HEAD_DIM // 2, 1))


def _inproj_body(cols, h_ref, g_ref, w_ref, cos_ref, sin_ref,
                 xpl_ref, qt_ref, kc_ref, vc_ref, ks_ref, vst_ref, kw_ref, vwt_ref, gatet_ref):
    xn = _rms(h_ref[0], g_ref[...]).astype(MXU_DTYPE)
    proj = _mm(xn, w_ref[...])
    cos = cos_ref[0]
    sin = sin_ref[0]

    def rope(x):
        rep = x.shape[1] // V7X_LANES
        c = jnp.concatenate([cos] * rep, axis=1) if rep > 1 else cos
        s = jnp.concatenate([sin] * rep, axis=1) if rep > 1 else sin
        return x * c + _swap_halves(x) * s

    def seg(name):
        lo, hi = cols[name]
        return proj[:, lo:hi]

    xpl_ref[0] = seg("xpl")
    qt_ref[0] = (rope(seg("q")) * Q_SCALE).T.astype(qt_ref.dtype)
    kc_ref[0] = rope(seg("kc"))
    vc_ref[0] = seg("vc")
    ks_ref[0] = rope(seg("ks")).astype(ks_ref.dtype)
    vst_ref[0] = seg("vs").T.astype(vst_ref.dtype)
    kw_ref[0] = rope(seg("kw")).astype(kw_ref.dtype)
    vwt_ref[0] = seg("vw").T.astype(vwt_ref.dtype)
    gatet_ref[0] = _sigmoid(seg("g")).T[:gatet_ref.shape[1]]


def _inproj(h3, g_pre, w_all, cols, cos, sin):
    b, s, d = h3.shape
    tm = min(ROW_TILE, s)
    ncol = w_all.shape[1]
    width = lambda n: cols[n][1] - cols[n][0]
    row = lambda w: pl.BlockSpec((1, tm, w), lambda i, j: (i, j, 0))
    col = lambda w: pl.BlockSpec((1, w, tm), lambda i, j: (i, 0, j))
    tok = lambda n, dt: (jax.ShapeDtypeStruct((b, s, width(n)), dt), row(width(n)))
    chan = lambda w, dt: (jax.ShapeDtypeStruct((b, w, s), dt), col(w))
    outs = [tok("xpl", F32), chan(width("q"), MXU_DTYPE), tok("kc", F32), tok("vc", F32),
            tok("ks", MXU_DTYPE), chan(width("vs"), MXU_DTYPE), tok("kw", MXU_DTYPE),
            chan(width("vw"), MXU_DTYPE), chan(N_BRANCH * N_HEADS, F32)]
    return pl.pallas_call(
        functools.partial(_inproj_body, cols),
        grid=(b, s // tm),
        in_specs=[row(d), _const_spec((1, d)), _const_spec((d, ncol)), row(V7X_LANES), row(V7X_LANES)],
        out_specs=[o[1] for o in outs],
        out_shape=[o[0] for o in outs],
        compiler_params=_params(("parallel", "parallel")),
        name="mixer_in_proj",
    )(h3, g_pre[None, :], w_all, cos, sin)


def _poollru_body(x_ref, pw_ref, pscale_ref, cw_ref, cb_ref, wri_ref, bri_ref, lam_ref,
                  y_ref, pool_carry, conv_carry, h_carry):
    si = pl.program_id(1)
    ts = x_ref.shape[1]
    width = pw_ref.shape[0]
    halo_p = pool_carry.shape[0]
    halo_c = conv_carry.shape[0]

    @pl.when(si == 0)
    def _():
        pool_carry[...] = jnp.zeros_like(pool_carry)
        conv_carry[...] = jnp.zeros_like(conv_carry)
        h_carry[...] = jnp.zeros_like(h_carry)

    x = x_ref[0]
    xp = x[:, :width]
    xl = x[:, width:2 * width]
    gl = x[:, 2 * width:]

    ext = jnp.concatenate([pool_carry[...], xp], axis=0)
    sums = [ext]
    shift = 1
    for _ in POOL_WINDOWS:
        sums.append(sums[-1] + pltpu.roll(sums[-1], shift, 0))
        shift *= 2
    lane = lax.broadcasted_iota(jnp.int32, (1, width), 1)
    grp = lax.shift_right_logical(lane, _log2(POOL_GROUP))
    win_sum = sums[len(POOL_WINDOWS)]
    win = jnp.full((1, width), float(POOL_WINDOWS[-1]), F32)
    for gi in range(len(POOL_WINDOWS) - 2, -1, -1):
        win_sum = jnp.where(grp == gi, sums[gi + 1], win_sum)
        win = jnp.where(grp == gi, float(POOL_WINDOWS[gi]), win)
    win_sum = win_sum[halo_p:]
    t_abs = si * ts + lax.broadcasted_iota(jnp.int32, (ts, 1), 0)
    cnt = jnp.minimum((t_abs + 1).astype(F32), win)
    pooled = win_sum / cnt
    y_pool = _mm((pooled - xp).astype(MXU_DTYPE), pw_ref[...]) * pscale_ref[...]
    pool_carry[...] = xp[ts - halo_p:]

    extc = jnp.concatenate([conv_carry[...], xl], axis=0)
    xc = extc * cw_ref[CONV_WIDTH - 1:CONV_WIDTH, :]
    for k in range(1, CONV_WIDTH):
        xc = xc + pltpu.roll(extc, k, 0) * cw_ref[CONV_WIDTH - 1 - k:CONV_WIDTH - k, :]
    xc = xc[halo_c:] + cb_ref[...]
    conv_carry[...] = xl[ts - halo_c:]

    ri = _mm(xc.astype(MXU_DTYPE), wri_ref[...]) + bri_ref[...]
    r = _sigmoid(ri[:, :width])
    i_gate = _sigmoid(ri[:, width:])
    neg_lam = -lam_ref[...]
    softplus = jnp.maximum(neg_lam, 0.0) + jnp.log1p(jnp.exp(-jnp.abs(neg_lam)))
    log_a = -LRU_C * r * softplus
    a = jnp.exp(log_a)
    b = jnp.sqrt(-jnp.tanh(log_a) * (a * a + 1.0)) * (i_gate * xc)

    row = lax.broadcasted_iota(jnp.int32, (ts, 1), 0)
    k = 1
    while k < ts:
        keep = row >= k
        a_prev = jnp.where(keep, pltpu.roll(a, k, 0), 1.0)
        b_prev = jnp.where(keep, pltpu.roll(b, k, 0), 0.0)
        b = a * b_prev + b
        a = a * a_prev
        k *= 2
    h = a * h_carry[0:1, :] + b
    h_carry[...] = jnp.broadcast_to(h[ts - 1:ts, :], h_carry.shape)

    gelu = 0.5 * gl * (1.0 + jnp.tanh(0.7978845608028654 * (gl + 0.044715 * gl * gl * gl)))
    y_ref[0] = jnp.concatenate([y_pool, h * gelu], axis=1).astype(y_ref.dtype)


def _poollru(xpl, pool_w_bd, pool_scale, conv_w, conv_b, w_ri_bd, b_ri, lam):
    b, s, w3 = xpl.shape
    width = w3 // 3
    ts = min(SEQ_TILE, s)
    return pl.pallas_call(
        _poollru_body,
        grid=(b, s // ts),
        in_specs=[pl.BlockSpec((1, ts, w3), lambda i, j: (i, j, 0)),
                  _const_spec((width, width)), _const_spec((1, width)),
                  _const_spec((CONV_WIDTH, width)), _const_spec((1, width)),
                  _const_spec((width, 2 * width)), _const_spec((1, 2 * width)), _const_spec((1, width))],
        out_specs=pl.BlockSpec((1, ts, 2 * width), lambda i, j: (i, j, 0)),
        out_shape=jax.ShapeDtypeStruct((b, s, 2 * width), MXU_DTYPE),
        scratch_shapes=[pltpu.VMEM((POOL_WINDOWS[-1], width), F32),
                        pltpu.VMEM((8, width), F32),
                        pltpu.VMEM((8, width), F32)],
        compiler_params=_params(("parallel", "arbitrary")),
        name="pool_rglru",
    )(xpl, pool_w_bd, pool_scale[None, :], conv_w, conv_b[None, :], w_ri_bd, b_ri[None, :], lam[None, :])


def _compress_body(transpose_out, r_ref, pea_ref, peb_ref, wa_ref, wb_ref, o_ref):
    r = r_ref[0]
    first = _mm((r + pea_ref[...]).astype(MXU_DTYPE), wa_ref[...])
    second = _mm((r + peb_ref[...]).astype(MXU_DTYPE), wb_ref[...])
    n1 = r.shape[0]
    out = first + pltpu.roll(second, n1 - 1, 0)
    o_ref[0] = (out.T if transpose_out else out).astype(o_ref.dtype)


def _compress(rows, pe_a, pe_b, w_a, w_b, transpose_out):
    b, n1, k = rows.shape
    n_out = w_a.shape[1]
    out_dims = (n_out, n1) if transpose_out else (n1, n_out)
    return pl.pallas_call(
        functools.partial(_compress_body, transpose_out),
        grid=(b,),
        in_specs=[pl.BlockSpec((1, n1, k), lambda i: (i, 0, 0)),
                  _const_spec((1, k)), _const_spec((1, k)), _const_spec((k, n_out)), _const_spec((k, n_out))],
        out_specs=pl.BlockSpec((1,) + out_dims, lambda i: (i, 0, 0)),
        out_shape=jax.ShapeDtypeStruct((b,) + out_dims, MXU_DTYPE),
        compiler_params=_params(("parallel",)),
        name="compress_kv",
    )(rows, pe_a, pe_b, w_a, w_b)


def _stack_heads(q_ref):
    return jnp.concatenate([q_ref[0, g * HEAD_DIM:(g + 1) * HEAD_DIM, :] for g in range(GQA_GROUP)], axis=1)


def _place_in_kv_half(qt, kvh):
    zeros = jnp.zeros_like(qt)
    parts = [jnp.where(kvh == h, qt, zeros) for h in range(N_KV_HEADS)]
    return jnp.concatenate(parts, axis=0)


def _query_positions(start, tq):
    lane = lax.broadcasted_iota(jnp.int32, (1, GQA_GROUP * tq), 1)
    return start + (lane & (tq - 1))


def _with_ones(v_t):
    return jnp.concatenate([v_t, jnp.ones((ONES_ROWS, v_t.shape[1]), v_t.dtype)], axis=0)


def _normalize(acc):
    return acc[:HEAD_DIM] * (1.0 / acc[HEAD_DIM:HEAD_DIM + 1])


def _cmp_body(n_sel, q_ref, kc_ref, vct_ref, ovt_ref, oc_ref, mb_ref, imp_ref):
    tq = q_ref.shape[2]
    kvh = pl.program_id(1)
    start = pl.program_id(2) * tq
    qt = _stack_heads(q_ref)
    qp = _place_in_kv_half(qt, kvh)
    n1 = kc_ref.shape[1]
    t_row = _query_positions(start, tq)
    assert CMP_STRIDE * CMP_CHUNK >= tq + CMP_LEN

    def attend(nrows):
        s = _mm(kc_ref[0, :nrows, :], qp)
        lo = max(0, nrows - 2 * CMP_CHUNK)
        n_idx = lo + lax.broadcasted_iota(jnp.int32, (nrows - lo, 1), 0)
        tail = jnp.where(n_idx * CMP_STRIDE + (CMP_LEN - 1) <= t_row, s[lo:], NEG_INF)
        s = jnp.concatenate([s[:lo], tail], axis=0) if lo else tail
        m = jnp.max(s, axis=0, keepdims=True)
        p = jnp.exp2(s - jnp.where(m > 0.5 * NEG_INF, m, 0.0))
        den = jnp.sum(p, axis=0, keepdims=True)
        pn = p * (1.0 / jnp.where(den > 0.0, den, 1.0))
        oc = _mm(vct_ref[0, :, :nrows], pn.astype(MXU_DTYPE))
        psum = pn[:, 0:tq]
        for g in range(GQA_GROUP):
            oc_ref[0, g * HEAD_DIM:(g + 1) * HEAD_DIM, :] = oc[:, g * tq:(g + 1) * tq]
            if g:
                psum = psum + pn[:, g * tq:(g + 1) * tq]
        ovt = ovt_ref[:, :nrows]
        hi = psum.astype(MXU_DTYPE)
        rem = psum - hi.astype(F32)
        mid = rem.astype(MXU_DTYPE)
        low = (rem - mid.astype(F32)).astype(MXU_DTYPE)
        imp_ref[...] = _mm(ovt, hi) + _mm(ovt, mid) + _mm(ovt, low)

    n_need = lax.shift_right_logical(start + tq, _log2(CMP_STRIDE)) - 1
    chunks = lax.shift_right_logical(n_need + CMP_CHUNK - 1, _log2(CMP_CHUNK))
    n_variants = -(-n1 // CMP_CHUNK)
    for c in range(1, n_variants + 1):
        pl.when(chunks == c)(functools.partial(attend, min(c * CMP_CHUNK, n1)))
    imp = imp_ref[...]

    n_pad = imp.shape[0]
    j = lax.broadcasted_iota(jnp.int32, (n_pad, 1), 0).astype(F32)
    t_q = start + lax.broadcasted_iota(jnp.int32, (1, tq), 1)
    cur = lax.shift_right_logical(t_q, _log2(SEL_BLOCK)).astype(F32)
    forced = (j == 0.0) | (j == cur) | (j == cur - 1.0)
    valid = j <= cur
    base = jnp.where(valid, jnp.where(forced, BIG_SCORE, imp), -BIG_SCORE)
    base = jnp.where(j < float(n_sel), base, -jnp.inf)
    assert n_sel >= SEL_TOPK

    score = jnp.where(forced, -jnp.inf, base)
    for _ in range(SEL_TOPK - 3):
        score = jnp.where(score == jnp.max(score, axis=0, keepdims=True), -jnp.inf, score)
    bias = jnp.where(score == -jnp.inf, 0.0, NEG_INF)
    mb_ref[0, 0] = bias.astype(mb_ref.dtype)
    taken = jnp.sum(jnp.where(valid, jnp.where(bias == 0.0, 1.0, 0.0), 0.0), axis=0, keepdims=True)
    tie = jnp.max(jnp.abs(taken - jnp.minimum(cur + 1.0, float(SEL_TOPK)))) > 0.0

    @pl.when(tie)
    def _():
        score = base
        bias = jnp.full(score.shape, NEG_INF, F32)
        for _ in range(SEL_TOPK):
            best = jnp.max(score, axis=0, keepdims=True)
            first = jnp.min(jnp.where(score == best, j, float(n_pad)), axis=0, keepdims=True)
            pick = j == first
            bias = jnp.where(pick, 0.0, bias)
            score = jnp.where(pick, -jnp.inf, score)
        mb_ref[0, 0] = bias.astype(mb_ref.dtype)


def _cmp_attention(qt, k_cmp, v_cmp_t, ov_t, n_sel):
    b, _, s = qt.shape
    n1 = k_cmp.shape[1]
    n_pad = ov_t.shape[0]
    tq = min(CMP_Q_TILE, s)
    group_rows = GQA_GROUP * HEAD_DIM
    head_blk = pl.BlockSpec((1, group_rows, tq), lambda i, h, j: (i, h, j))
    return pl.pallas_call(
        functools.partial(_cmp_body, n_sel),
        grid=(b, N_KV_HEADS, s // tq),
        in_specs=[head_blk,
                  pl.BlockSpec((1, n1, KV_WIDTH), lambda i, h, j: (i, 0, 0)),
                  pl.BlockSpec((1, HEAD_DIM, n1), lambda i, h, j: (i, h, 0)),
                  _const_spec((n_pad, n1))],
        out_specs=[head_blk, pl.BlockSpec((1, 1, n_pad, tq), lambda i, h, j: (i, h, 0, j))],
        out_shape=[jax.ShapeDtypeStruct((b, N_HEADS * HEAD_DIM, s), F32),
                   jax.ShapeDtypeStruct((b, N_KV_HEADS, n_pad, s), MXU_DTYPE)],
        scratch_shapes=[pltpu.VMEM((n_pad, tq), F32)],
        compiler_params=_params(("parallel", "parallel", "parallel")),
        name="cmp_attention_select",
    )(qt, k_cmp, v_cmp_t, ov_t)


def _selwin_body(q_ref, mb_ref, ks_ref, oh_ref, vst_ref, kw_ref, vwt_ref, oc_ref, gate_ref, y_ref,
                 qa_ref, acc_ref, win_ref, seen_ref):
    tq = q_ref.shape[2]
    tk = tq
    kvh = pl.program_id(1)
    qi = pl.program_id(2)
    start = qi * tq
    rows = GQA_GROUP * tq
    qt = _stack_heads(q_ref)
    qp = _place_in_kv_half(qt, kvh)
    t_row = _query_positions(start, tq)
    group_shift = _log2(SEL_GROUP * SEL_BLOCK // tk)
    n_groups = mb_ref.shape[2] // SEL_GROUP

    for grp in range(n_groups):
        bias_rows = mb_ref[0, 0, grp * SEL_GROUP:(grp + 1) * SEL_GROUP, :]
        qa_ref[grp] = jnp.concatenate([jnp.concatenate([bias_rows] * GQA_GROUP, axis=1), qp], axis=0)

    def scores(kt, n_sub, causal):
        key_rows = pl.ds(pl.multiple_of(kt * tk, tk), n_sub * tk)
        oh_rows = pl.ds(pl.multiple_of(lax.rem(kt * tk, oh_ref.shape[0]), tk), n_sub * tk)
        k = jnp.concatenate([oh_ref[oh_rows, :], ks_ref[0, key_rows, :]], axis=1)
        s = _mm(k, qa_ref[lax.shift_right_logical(kt, group_shift)])
        if causal:
            kp = kt * tk + lax.broadcasted_iota(jnp.int32, (n_sub * tk, 1), 0)
            s = jnp.where(kp <= t_row, s, NEG_INF)
        return s

    def values(kt, n_sub):
        return _with_ones(vst_ref[0, :, pl.ds(pl.multiple_of(kt * tk, tk), n_sub * tk)])

    def col_max(s):
        return jnp.max(s, axis=0, keepdims=True)

    n_past = qi

    own_rows = pl.ds(pl.multiple_of(start, tq), tq)
    old_first = pl.multiple_of(jnp.maximum(start - WINDOW, 0), tq)
    old_rows = pl.ds(old_first, WINDOW)
    oldest = jnp.maximum(t_row - WINDOW, -1)

    def window_scores_own():
        kp = start + lax.broadcasted_iota(jnp.int32, (tq, 1), 0)
        return jnp.where(kp <= t_row, _mm(kw_ref[0, own_rows, :], qp), NEG_INF)

    def window_scores_old():
        kp = old_first + lax.broadcasted_iota(jnp.int32, (WINDOW, 1), 0)
        kp = jnp.where(kp < start, kp, -2)
        return jnp.where(kp > oldest, _mm(kw_ref[0, old_rows, :], qp), NEG_INF)

    def pv(v_t, s, ref):
        return _mm(_with_ones(v_t), jnp.exp2(s - ref).astype(MXU_DTYPE))

    def self_scores(k_ref):
        k_t = k_ref[0, own_rows, :].astype(F32).T
        k_own = k_t[:HEAD_DIM]
        for h in range(1, N_KV_HEADS):
            k_own = jnp.where(kvh == h, k_t[h * HEAD_DIM:(h + 1) * HEAD_DIM], k_own)
        parts = [jnp.sum(qt[:, g * tq:(g + 1) * tq].astype(F32) * k_own, axis=0, keepdims=True)
                 for g in range(GQA_GROUP)]
        return jnp.concatenate(parts, axis=1)

    ref = self_scores(ks_ref)
    ref_w = self_scores(kw_ref)
    s_own = window_scores_own()
    s_old = window_scores_old()
    win_ref[...] = pv(vwt_ref[0, :, own_rows], s_own, ref_w) + pv(vwt_ref[0, :, old_rows], s_old, ref_w)
    excess_w = jnp.max(jnp.maximum(col_max(s_own), col_max(s_old)) - ref_w)

    def consume(s, kt, n_sub):
        acc_ref[...] += _mm(values(kt, n_sub), jnp.exp2(s - ref).astype(MXU_DTYPE))
        seen_ref[...] = jnp.maximum(seen_ref[...], col_max(s))

    def stream(kt, n_sub, causal=False):
        consume(scores(kt, n_sub, causal), kt, n_sub)

    acc_ref[...] = jnp.zeros_like(acc_ref)
    seen_ref[...] = ref
    n_big = lax.shift_right_logical(n_past, _log2(STREAM_TILES))
    lax.fori_loop(0, n_big, lambda i, c: (stream(i * STREAM_TILES, STREAM_TILES), c)[1], 0)
    lax.fori_loop(n_big * STREAM_TILES, n_past, lambda i, c: (stream(i, 1), c)[1], 0)
    stream(qi, 1, causal=True)
    overflow = jnp.maximum(jnp.max(seen_ref[...] - ref), excess_w) > EXP_HEADROOM

    @pl.when(overflow)
    def _():
        def tile(kt, causal):
            s = scores(kt, 1, causal)
            m_t = col_max(s)
            return m_t, _mm(values(kt, 1), jnp.exp2(s - m_t).astype(MXU_DTYPE))

        def merge(carry, part):
            m, acc = carry
            m_t, o_t = part
            m_new = jnp.maximum(m, m_t)
            return m_new, acc * jnp.exp2(m - m_new) + o_t * jnp.exp2(m_t - m_new)

        carry = (jnp.full((1, rows), M_INIT, F32), jnp.zeros(acc_ref.shape, F32))
        carry = lax.fori_loop(0, n_past, lambda i, c: merge(c, tile(i, False)), carry)
        acc_ref[...] = merge(carry, tile(qi, True))[1]

        x_own = window_scores_own()
        x_old = window_scores_old()
        m_w = jnp.maximum(col_max(x_own), col_max(x_old))
        win_ref[...] = pv(vwt_ref[0, :, own_rows], x_own, m_w) + pv(vwt_ref[0, :, old_rows], x_old, m_w)

    o_sel = _normalize(acc_ref[...])
    o_win = _normalize(win_ref[...])

    gate_row = kvh * (GQA_GROUP * N_BRANCH)
    for g in range(GQA_GROUP):
        cols = slice(g * tq, (g + 1) * tq)
        head = slice(g * HEAD_DIM, (g + 1) * HEAD_DIM)
        gates = [gate_ref[0, pl.ds(gate_row + N_BRANCH * g + br, 1), :] for br in range(N_BRANCH)]
        y = gates[0] * oc_ref[0, head, :] + gates[1] * o_sel[:, cols] + gates[2] * o_win[:, cols]
        y_ref[0, head, :] = y.astype(y_ref.dtype)


def _sel_win_attention(qt, mb_t, ks, key_onehot, vs_t, kw, vw_t, oc_t, gates_t):
    b, _, s = qt.shape
    tq = min(SEL_TILE, s)
    n_pad = mb_t.shape[2]
    n_gate = gates_t.shape[1]
    group_rows = GQA_GROUP * HEAD_DIM
    head_blk = pl.BlockSpec((1, group_rows, tq), lambda i, h, j: (i, h, j))
    once = pl.Buffered(1)
    keys = pl.BlockSpec((1, s, KV_WIDTH), lambda i, h, j: (i, 0, 0), pipeline_mode=once)
    vals = pl.BlockSpec((1, HEAD_DIM, s), lambda i, h, j: (i, h, 0), pipeline_mode=once)
    acc = pltpu.VMEM((HEAD_DIM + ONES_ROWS, GQA_GROUP * tq), F32)
    return pl.pallas_call(
        _selwin_body,
        grid=(b, N_KV_HEADS, s // tq),
        in_specs=[head_blk,
                  pl.BlockSpec((1, 1, n_pad, tq), lambda i, h, j: (i, h, 0, j)),
                  keys, _const_spec(key_onehot.shape), vals, keys, vals,
                  head_blk,
                  pl.BlockSpec((1, n_gate, tq), lambda i, h, j: (i, 0, j))],
        out_specs=head_blk,
        out_shape=jax.ShapeDtypeStruct((b, N_HEADS * HEAD_DIM, s), MXU_DTYPE),
        scratch_shapes=[pltpu.VMEM((n_pad // SEL_GROUP, SEL_GROUP + KV_WIDTH, GQA_GROUP * tq), MXU_DTYPE),
                        acc, acc, pltpu.VMEM((1, GQA_GROUP * tq), F32)],
        compiler_params=_params(("parallel", "parallel", "arbitrary")),
        name="sel_win_attention",
    )(qt, mb_t, ks, key_onehot, vs_t, kw, vw_t, oc_t, gates_t)


def _outproj_ffn_body(h_ref, ypl_ref, yat_ref, w1_ref, w2_ref, g_ref, *refs):
    ffn_refs, o_ref = refs[:N_FFN_OPERANDS], refs[N_FFN_OPERANDS]
    m = _mm(ypl_ref[0], w1_ref[...]) + _mm_tn(yat_ref[0], w2_ref[...])
    o_ref[0] = _ffn_math(h_ref[0] + _rms(m, g_ref[...]), ffn_refs)


def _outproj_ffn(h3, ypl, yat_t, w1, w2, g_post, ffn_weights):
    b, s, d = h3.shape
    tm = min(ROW_TILE, s)
    row = lambda w: pl.BlockSpec((1, tm, w), lambda i, j: (i, j, 0))
    ffn_specs, ffn_args = _ffn_operands(*ffn_weights)
    return pl.pallas_call(
        _outproj_ffn_body,
        grid=(b, s // tm),
        in_specs=[row(d), row(ypl.shape[2]),
                  pl.BlockSpec((1, yat_t.shape[1], tm), lambda i, j: (i, 0, j)),
                  _const_spec(w1.shape), _const_spec(w2.shape), _const_spec((1, d))] + ffn_specs,
        out_specs=row(d),
        out_shape=jax.ShapeDtypeStruct((b, s, d), F32),
        compiler_params=_params(("parallel", "parallel")),
        name="mixer_out_proj_ffn",
    )(h3, ypl, yat_t, w1, w2, g_post[None, :], *ffn_args)


def _block_diag(blocks):
    n, a, b = blocks.shape
    eye = jnp.eye(n, dtype=blocks.dtype)
    return jnp.einsum("nab,nm->namb", blocks, eye).reshape(n * a, n * b)


def _compress_weights(w, pe):
    eye = jnp.eye(N_KV_HEADS, dtype=w.dtype)
    halves = []
    for part in range(CMP_LEN // CMP_STRIDE):
        wl = w[part * CMP_STRIDE:(part + 1) * CMP_STRIDE]
        pel = pe[part * CMP_STRIDE:(part + 1) * CMP_STRIDE]
        wm = jnp.einsum("lde,hg->lhdge", wl, eye).reshape(CMP_STRIDE * KV_WIDTH, KV_WIDTH)
        pm = jnp.tile(pel[:, None, :], (1, N_KV_HEADS, 1)).reshape(1, -1)
        halves.append((wm.astype(MXU_DTYPE), pm))
    return halves


def kernel(x, positions, ffn1_pre_g, ffn1_post_g, ffn1_w_gate, ffn1_w_up, ffn1_w_down, mix_pre_g, mix_post_g,
           w_in, w_out, pool_w, pool_scale, conv_w, conv_b, lru_w_r, lru_b_r, lru_w_i, lru_b_i, lru_lambda,
           cmp_w_k, cmp_w_v, cmp_pe, ffn2_pre_g, ffn2_post_g, ffn2_w_gate, ffn2_w_up, ffn2_w_down):
    b, s, d = x.shape
    depth = w_in.shape[0]
    t = b * s
    pool_width = pool_w.shape[1] * pool_w.shape[2]
    lru_width = lru_w_r.shape[1] * lru_w_r.shape[2]
    attn_width = N_HEADS * HEAD_DIM
    assert pool_width == lru_width and s % SEL_TILE == 0 and s % (CMP_STRIDE * 8) == 0
    assert WINDOW % SEL_TILE == 0 and (SEL_GROUP * SEL_BLOCK) % (STREAM_TILES * SEL_TILE) == 0
    assert s >= WINDOW + SEL_TILE

    sizes = [("xpl", pool_width + 2 * lru_width), ("q", attn_width), ("kc", KV_WIDTH), ("vc", KV_WIDTH),
             ("ks", KV_WIDTH), ("vs", KV_WIDTH), ("kw", KV_WIDTH), ("vw", KV_WIDTH), ("g", V7X_LANES)]
    cols, off = {}, 0
    for name, width in sizes:
        cols[name] = (off, off + width)
        off += width

    n_sel = s // SEL_BLOCK
    n_pad = -(-n_sel // SEL_GROUP) * SEL_GROUP
    n1 = s // CMP_STRIDE
    n_cmp = (s - CMP_LEN) // CMP_STRIDE + 1
    cmp_start = jnp.arange(n1) * CMP_STRIDE
    sel_start = jnp.arange(n_pad) * SEL_BLOCK
    overlap_t = ((cmp_start[None, :] < sel_start[:, None] + SEL_BLOCK) &
                 (cmp_start[None, :] + CMP_LEN > sel_start[:, None]) &
                 (jnp.arange(n1)[None, :] < n_cmp) & (jnp.arange(n_pad)[:, None] < n_sel)).astype(MXU_DTYPE)
    key_blk = jnp.arange(min(s, SEL_GROUP * SEL_BLOCK)) // SEL_BLOCK
    key_onehot = (key_blk[:, None] == jnp.arange(SEL_GROUP)[None, :]).astype(MXU_DTYPE)

    cos, sin = _rope_tables(positions)

    h = x
    for l in range(depth):
        cast = lambda w: w[l].astype(MXU_DTYPE)
        ffn1 = (ffn1_pre_g[l], ffn1_post_g[l], cast(ffn1_w_gate), cast(ffn1_w_up), cast(ffn1_w_down))
        ffn2 = (ffn2_pre_g[l], ffn2_post_g[l], cast(ffn2_w_gate), cast(ffn2_w_up), cast(ffn2_w_down))

        w_all = jnp.pad(w_in[l], ((0, 0), (0, off - w_in.shape[2]))).astype(MXU_DTYPE)
        h = _ffn(h, ffn1)
        xpl, qt, kc, vc, ks, vs_t, kw, vw_t, gates_t = _inproj(h, mix_pre_g[l], w_all, cols, cos, sin)

        w_ri = jnp.concatenate([_block_diag(lru_w_r[l]), _block_diag(lru_w_i[l])], axis=1).astype(MXU_DTYPE)
        ypl = _poollru(xpl, _block_diag(pool_w[l]).astype(MXU_DTYPE), pool_scale[l],
                       conv_w[l], conv_b[l], w_ri, jnp.concatenate([lru_b_r[l], lru_b_i[l]]), lru_lambda[l])

        (wk_a, pe_a), (wk_b, pe_b) = _compress_weights(cmp_w_k[l], cmp_pe[l])
        (wv_a, _), (wv_b, _) = _compress_weights(cmp_w_v[l], cmp_pe[l])
        k_cmp = _compress(kc.reshape(b, n1, -1), pe_a, pe_b, wk_a, wk_b, False)
        v_cmp_t = _compress(vc.reshape(b, n1, -1), pe_a, pe_b, wv_a, wv_b, True)

        oc_t, mb_t = _cmp_attention(qt, k_cmp, v_cmp_t, overlap_t, n_sel)
        y_t = _sel_win_attention(qt, mb_t, ks, key_onehot, vs_t, kw, vw_t, oc_t, gates_t)

        w_o = w_out[l].astype(MXU_DTYPE)
        split = pool_width + lru_width
        h = _outproj_ffn(h, ypl, y_t, w_o[:split], w_o[split:], mix_post_g[l], ffn2)
    return h
```

```python
import functools

import jax
import jax.numpy as jnp
from jax import lax
from jax.experimental import pallas as pl
from jax.experimental.pallas import tpu as pltpu

F32 = jnp.float32
MXU_DTYPE = jnp.bfloat16

POOL_WINDOWS = (2, 4, 8, 16)
POOL_GROUP = 64
LRU_HEAD_DIM = 64
LRU_C = 8.0
CONV_WIDTH = 4
HEAD_DIM = 64
N_KV_HEADS = 2
GQA_GROUP = 4
N_HEADS = N_KV_HEADS * GQA_GROUP
KV_WIDTH = N_KV_HEADS * HEAD_DIM
N_BRANCH = 3
CMP_LEN = 32
CMP_STRIDE = 16
SEL_BLOCK = 64
SEL_TOPK = 16
WINDOW = 512
ROPE_THETA = 10000.0
NORM_EPS = 1e-6
NEG_INF = -1e30
BIG_SCORE = 1e9

V7X_LANES = 128
V7X_VMEM_LIMIT_BYTES = 56 * 1024 * 1024

ROW_TILE = 512
SEQ_TILE = 512
CMP_Q_TILE = 512
CMP_CHUNK = 256
SEL_TILE = 512
SEL_GROUP = 128
ONES_ROWS = 16
M_INIT = -1e20
STREAM_TILES = 2
EXP_HEADROOM = 64.0
Q_SCALE = HEAD_DIM ** -0.5 * 1.4426950408889634


def _params(semantics):
    return pltpu.CompilerParams(dimension_semantics=semantics, vmem_limit_bytes=V7X_VMEM_LIMIT_BYTES)


def _rms(x, g):
    return x * lax.rsqrt(jnp.mean(x * x, axis=-1, keepdims=True) + NORM_EPS) * g


def _sigmoid(x):
    return 1.0 / (1.0 + jnp.exp(-x))


def _mm(a, b):
    return jnp.dot(a, b, preferred_element_type=F32)


def _mm_tn(a_t, b):
    return lax.dot_general(a_t, b, (((0,), (0,)), ((), ())), preferred_element_type=F32)


def _log2(n):
    assert n & (n - 1) == 0
    return n.bit_length() - 1


def _const_spec(shape):
    zeros = (0,) * len(shape)
    return pl.BlockSpec(shape, lambda *_: zeros, pipeline_mode=pl.Buffered(1))


def _rope_body(pos_ref, inv_ref, cos_ref, sin_ref):
    ang = pos_ref[0].astype(F32) * inv_ref[...]
    lane = lax.broadcasted_iota(jnp.int32, ang.shape, 1)
    cos_ref[0] = jnp.cos(ang)
    sin_ref[0] = jnp.where((lane & (HEAD_DIM - 1)) < HEAD_DIM // 2, -jnp.sin(ang), jnp.sin(ang))


def _rope_tables(positions):
    b, s = positions.shape
    inv = ROPE_THETA ** (-jnp.arange(0, HEAD_DIM, 2, dtype=F32) / HEAD_DIM)
    inv_row = jnp.tile(inv, V7X_LANES // (HEAD_DIM // 2))[None, :]
    ts = min(SEQ_TILE, s)
    out = jax.ShapeDtypeStruct((b, s, V7X_LANES), F32)
    return pl.pallas_call(
        _rope_body,
        grid=(b, s // ts),
        in_specs=[pl.BlockSpec((1, ts, 1), lambda i, j: (i, j, 0)),
                  pl.BlockSpec((1, V7X_LANES), lambda i, j: (0, 0))],
        out_specs=[pl.BlockSpec((1, ts, V7X_LANES), lambda i, j: (i, j, 0))] * 2,
        out_shape=[out, out],
        compiler_params=_params(("parallel", "parallel")),
        name="rope_tables",
    )(positions[:, :, None], inv_row)


def _ffn_math(h, ffn_refs):
    gpre_ref, gpost_ref, wg_ref, wu_ref, wd_ref = ffn_refs
    xn = _rms(h, gpre_ref[...]).astype(MXU_DTYPE)
    gate = _mm(xn, wg_ref[...])
    up = _mm(xn, wu_ref[...])
    act = (gate * _sigmoid(gate) * up).astype(MXU_DTYPE)
    f = _mm(act, wd_ref[...])
    return h + 0.5 * _rms(f, gpost_ref[...])


def _ffn_operands(g_pre, g_post, w_gate, w_up, w_down):
    d, dff = w_gate.shape
    specs = [_const_spec((1, d)), _const_spec((1, d)),
             _const_spec((d, dff)), _const_spec((d, dff)), _const_spec((dff, d))]
    return specs, (g_pre[None, :], g_post[None, :], w_gate, w_up, w_down)


N_FFN_OPERANDS = 5


def _ffn_body(h_ref, *refs):
    refs[N_FFN_OPERANDS][0] = _ffn_math(h_ref[0], refs[:N_FFN_OPERANDS])


def _ffn(h3, ffn_weights):
    b, s, d = h3.shape
    tm = min(ROW_TILE, s)
    row = pl.BlockSpec((1, tm, d), lambda i, j: (i, j, 0))
    ffn_specs, ffn_args = _ffn_operands(*ffn_weights)
    return pl.pallas_call(
        _ffn_body,
        grid=(b, s // tm),
        in_specs=[row] + ffn_specs,
        out_specs=row,
        out_shape=jax.ShapeDtypeStruct((b, s, d), F32),
        compiler_params=_params(("parallel", "parallel")),
        name="ffn",
    )(h3, *ffn_args)


def _swap_halves(x):
    n = x.shape[1]
    lane = lax.broadcasted_iota(jnp.int32, x.shape, 1)
    first_half = (lane & (HEAD_DIM - 1)) < HEAD_DIM // 2
    return jnp.where(first_half, pltpu.roll(x, n - HEAD_DIM // 2, 1), pltpu.roll(x, HEAD_DIM // 2, 1))


def _inproj_body(cols, h_ref, g_ref, w_ref, cos_ref, sin_ref,
                 xpl_ref, qt_ref, kc_ref, vc_ref, ks_ref, vst_ref, kw_ref, vwt_ref, gatet_ref):
    xn = _rms(h_ref[0], g_ref[...]).astype(MXU_DTYPE)
    proj = _mm(xn, w_ref[...])
    cos = cos_ref[0]
    sin = sin_ref[0]

    def rope(x):
        rep = x.shape[1] // V7X_LANES
        c = jnp.concatenate([cos] * rep, axis=1) if rep > 1 else cos
        s = jnp.concatenate([sin] * rep, axis=1) if rep > 1 else sin
        return x * c + _swap_halves(x) * s

    def seg(name):
        lo, hi = cols[name]
        return proj[:, lo:hi]

    xpl_ref[0] = seg("xpl")
    qt_ref[0] = (rope(seg("q")) * Q_SCALE).T.astype(qt_ref.dtype)
    kc_ref[0] = rope(seg("kc"))
    vc_ref[0] = seg("vc")
    ks_ref[0] = rope(seg("ks")).astype(ks_ref.dtype)
    vst_ref[0] = seg("vs").T.astype(vst_ref.dtype)
    kw_ref[0] = rope(seg("kw")).astype(kw_ref.dtype)
    vwt_ref[0] = seg("vw").T.astype(vwt_ref.dtype)
    gatet_ref[0] = _sigmoid(seg("g")).T[:gatet_ref.shape[1]]


def _inproj(h3, g_pre, w_all, cols, cos, sin):
    b, s, d = h3.shape
    tm = min(ROW_TILE, s)
    ncol = w_all.shape[1]
    width = lambda n: cols[n][1] - cols[n][0]
    row = lambda w: pl.BlockSpec((1, tm, w), lambda i, j: (i, j, 0))
    col = lambda w: pl.BlockSpec((1, w, tm), lambda i, j: (i, 0, j))
    tok = lambda n, dt: (jax.ShapeDtypeStruct((b, s, width(n)), dt), row(width(n)))
    chan = lambda w, dt: (jax.ShapeDtypeStruct((b, w, s), dt), col(w))
    outs = [tok("xpl", F32), chan(width("q"), MXU_DTYPE), tok("kc", F32), tok("vc", F32),
            tok("ks", MXU_DTYPE), chan(width("vs"), MXU_DTYPE), tok("kw", MXU_DTYPE),
            chan(width("vw"), MXU_DTYPE), chan(N_BRANCH * N_HEADS, F32)]
    return pl.pallas_call(
        functools.partial(_inproj_body, cols),
        grid=(b, s // tm),
        in_specs=[row(d), _const_spec((1, d)), _const_spec((d, ncol)), row(V7X_LANES), row(V7X_LANES)],
        out_specs=[o[1] for o in outs],
        out_shape=[o[0] for o in outs],
        compiler_params=_params(("parallel", "parallel")),
        name="mixer_in_proj",
    )(h3, g_pre[None, :], w_all, cos, sin)


def _poollru_body(x_ref, pw_ref, pscale_ref, cw_ref, cb_ref, wri_ref, bri_ref, lam_ref,
                  y_ref, pool_carry, conv_carry, h_carry):
    si = pl.program_id(1)
    ts = x_ref.shape[1]
    width = pw_ref.shape[0]
    halo_p = pool_carry.shape[0]
    halo_c = conv_carry.shape[0]

    @pl.when(si == 0)
    def _():
        pool_carry[...] = jnp.zeros_like(pool_carry)
        conv_carry[...] = jnp.zeros_like(conv_carry)
        h_carry[...] = jnp.zeros_like(h_carry)

    x = x_ref[0]
    xp = x[:, :width]
    xl = x[:, width:2 * width]
    gl = x[:, 2 * width:]

    ext = jnp.concatenate([pool_carry[...], xp], axis=0)
    sums = [ext]
    shift = 1
    for _ in POOL_WINDOWS:
        sums.append(sums[-1] + pltpu.roll(sums[-1], shift, 0))
        shift *= 2
    lane = lax.broadcasted_iota(jnp.int32, (1, width), 1)
    grp = lax.shift_right_logical(lane, _log2(POOL_GROUP))
    win_sum = sums[len(POOL_WINDOWS)]
    win = jnp.full((1, width), float(POOL_WINDOWS[-1]), F32)
    for gi in range(len(POOL_WINDOWS) - 2, -1, -1):
        win_sum = jnp.where(grp == gi, sums[gi + 1], win_sum)
        win = jnp.where(grp == gi, float(POOL_WINDOWS[gi]), win)
    win_sum = win_sum[halo_p:]
    t_abs = si * ts + lax.broadcasted_iota(jnp.int32, (ts, 1), 0)
    cnt = jnp.minimum((t_abs + 1).astype(F32), win)
    pooled = win_sum / cnt
    y_pool = _mm((pooled - xp).astype(MXU_DTYPE), pw_ref[...]) * pscale_ref[...]
    pool_carry[...] = xp[ts - halo_p:]

    extc = jnp.concatenate([conv_carry[...], xl], axis=0)
    xc = extc * cw_ref[CONV_WIDTH - 1:CONV_WIDTH, :]
    for k in range(1, CONV_WIDTH):
        xc = xc + pltpu.roll(extc, k, 0) * cw_ref[CONV_WIDTH - 1 - k:CONV_WIDTH - k, :]
    xc = xc[halo_c:] + cb_ref[...]
    conv_carry[...] = xl[ts - halo_c:]

    ri = _mm(xc.astype(MXU_DTYPE), wri_ref[...]) + bri_ref[...]
    r = _sigmoid(ri[:, :width])
    i_gate = _sigmoid(ri[:, width:])
    neg_lam = -lam_ref[...]
    softplus = jnp.maximum(neg_lam, 0.0) + jnp.log1p(jnp.exp(-jnp.abs(neg_lam)))
    log_a = -LRU_C * r * softplus
    a = jnp.exp(log_a)
    b = jnp.sqrt(-jnp.tanh(log_a) * (a * a + 1.0)) * (i_gate * xc)

    row = lax.broadcasted_iota(jnp.int32, (ts, 1), 0)
    k = 1
    while k < ts:
        keep = row >= k
        a_prev = jnp.where(keep, pltpu.roll(a, k, 0), 1.0)
        b_prev = jnp.where(keep, pltpu.roll(b, k, 0), 0.0)
        b = a * b_prev + b
        a = a * a_prev
        k *= 2
    h = a * h_carry[0:1, :] + b
    h_carry[...] = jnp.broadcast_to(h[ts - 1:ts, :], h_carry.shape)

    gelu = 0.5 * gl * (1.0 + jnp.tanh(0.7978845608028654 * (gl + 0.044715 * gl * gl * gl)))
    y_ref[0] = jnp.concatenate([y_pool, h * gelu], axis=1).astype(y_ref.dtype)


def _poollru(xpl, pool_w_bd, pool_scale, conv_w, conv_b, w_ri_bd, b_ri, lam):
    b, s, w3 = xpl.shape
    width = w3 // 3
    ts = min(SEQ_TILE, s)
    return pl.pallas_call(
        _poollru_body,
        grid=(b, s // ts),
        in_specs=[pl.BlockSpec((1, ts, w3), lambda i, j: (i, j, 0)),
                  _const_spec((width, width)), _const_spec((1, width)),
                  _const_spec((CONV_WIDTH, width)), _const_spec((1, width)),
                  _const_spec((width, 2 * width)), _const_spec((1, 2 * width)), _const_spec((1, width))],
        out_specs=pl.BlockSpec((1, ts, 2 * width), lambda i, j: (i, j, 0)),
        out_shape=jax.ShapeDtypeStruct((b, s, 2 * width), MXU_DTYPE),
        scratch_shapes=[pltpu.VMEM((POOL_WINDOWS[-1], width), F32),
                        pltpu.VMEM((8, width), F32),
                        pltpu.VMEM((8, width), F32)],
        compiler_params=_params(("parallel", "arbitrary")),
        name="pool_rglru",
    )(xpl, pool_w_bd, pool_scale[None, :], conv_w, conv_b[None, :], w_ri_bd, b_ri[None, :], lam[None, :])


def _compress_body(transpose_out, r_ref, pea_ref, peb_ref, wa_ref, wb_ref, o_ref):
    r = r_ref[0]
    first = _mm((r + pea_ref[...]).astype(MXU_DTYPE), wa_ref[...])
    second = _mm((r + peb_ref[...]).astype(MXU_DTYPE), wb_ref[...])
    n1 = r.shape[0]
    out = first + pltpu.roll(second, n1 - 1, 0)
    o_ref[0] = (out.T if transpose_out else out).astype(o_ref.dtype)


def _compress(rows, pe_a, pe_b, w_a, w_b, transpose_out):
    b, n1, k = rows.shape
    n_out = w_a.shape[1]
    out_dims = (n_out, n1) if transpose_out else (n1, n_out)
    return pl.pallas_call(
        functools.partial(_compress_body, transpose_out),
        grid=(b,),
        in_specs=[pl.BlockSpec((1, n1, k), lambda i: (i, 0, 0)),
                  _const_spec((1, k)), _const_spec((1, k)), _const_spec((k, n_out)), _const_spec((k, n_out))],
        out_specs=pl.BlockSpec((1,) + out_dims, lambda i: (i, 0, 0)),
        out_shape=jax.ShapeDtypeStruct((b,) + out_dims, MXU_DTYPE),
        compiler_params=_params(("parallel",)),
        name="compress_kv",
    )(rows, pe_a, pe_b, w_a, w_b)


def _stack_heads(q_ref):
    return jnp.concatenate([q_ref[0, g * HEAD_DIM:(g + 1) * HEAD_DIM, :] for g in range(GQA_GROUP)], axis=1)


def _place_in_kv_half(qt, kvh):
    zeros = jnp.zeros_like(qt)
    parts = [jnp.where(kvh == h, qt, zeros) for h in range(N_KV_HEADS)]
    return jnp.concatenate(parts, axis=0)


def _query_positions(start, tq):
    lane = lax.broadcasted_iota(jnp.int32, (1, GQA_GROUP * tq), 1)
    return start + (lane & (tq - 1))


def _with_ones(v_t):
    return jnp.concatenate([v_t, jnp.ones((ONES_ROWS, v_t.shape[1]), v_t.dtype)], axis=0)


def _normalize(acc):
    return acc[:HEAD_DIM] * (1.0 / acc[HEAD_DIM:HEAD_DIM + 1])


def _cmp_body(n_sel, q_ref, kc_ref, vct_ref, ovt_ref, oc_ref, mb_ref):
    tq = q_ref.shape[2]
    kvh = pl.program_id(1)
    start = pl.program_id(2) * tq
    qt = _stack_heads(q_ref)
    qp = _place_in_kv_half(qt, kvh)
    n1 = kc_ref.shape[1]
    t_row = _query_positions(start, tq)
    assert CMP_STRIDE * CMP_CHUNK >= tq + CMP_LEN

    def attend(nrows):
        s = _mm(kc_ref[0, :nrows, :], qp)
        lo = max(0, nrows - 2 * CMP_CHUNK)
        n_idx = lo + lax.broadcasted_iota(jnp.int32, (nrows - lo, 1), 0)
        tail = jnp.where(n_idx * CMP_STRIDE + (CMP_LEN - 1) <= t_row, s[lo:], NEG_INF)
        s = jnp.concatenate([s[:lo], tail], axis=0) if lo else tail
        m = jnp.max(s, axis=0, keepdims=True)
        p = jnp.exp2(s - jnp.where(m > 0.5 * NEG_INF, m, 0.0))
        den = jnp.sum(p, axis=0, keepdims=True)
        pn = p * (1.0 / jnp.where(den > 0.0, den, 1.0))
        oc = _mm(vct_ref[0, :, :nrows], pn.astype(MXU_DTYPE))
        psum = pn[:, 0:tq]
        for g in range(GQA_GROUP):
            oc_ref[0, g * HEAD_DIM:(g + 1) * HEAD_DIM, :] = oc[:, g * tq:(g + 1) * tq]
            if g:
                psum = psum + pn[:, g * tq:(g + 1) * tq]
        n_blk = min(n_sel, nrows * CMP_STRIDE // SEL_BLOCK)
        ovt = ovt_ref[:n_blk, :nrows]
        hi = psum.astype(MXU_DTYPE)
        low = (psum - hi.astype(F32)).astype(MXU_DTYPE)
        select(_mm(ovt, hi) + _mm(ovt, low), n_blk)

    def select(imp, n_blk):
        j = lax.broadcasted_iota(jnp.int32, (n_blk, 1), 0).astype(F32)
        t_q = start + lax.broadcasted_iota(jnp.int32, (1, tq), 1)
        cur = lax.shift_right_logical(t_q, _log2(SEL_BLOCK)).astype(F32)
        forced = (j == 0.0) | (j == cur) | (j == cur - 1.0)
        valid = j <= cur
        base = jnp.where(valid, jnp.where(forced, BIG_SCORE, imp), -BIG_SCORE)
        n_pad = mb_ref.shape[2]
        if n_pad > n_blk:
            mb_ref[0, 0, n_blk:, :] = jnp.full((n_pad - n_blk, tq), NEG_INF, mb_ref.dtype)

        score = jnp.where(forced, -jnp.inf, base)
        for _ in range(SEL_TOPK - 3):
            score = jnp.where(score == jnp.max(score, axis=0, keepdims=True), -jnp.inf, score)
        bias = jnp.where(score == -jnp.inf, 0.0, NEG_INF)
        mb_ref[0, 0, :n_blk, :] = bias.astype(mb_ref.dtype)
        taken = jnp.sum(jnp.where(valid, jnp.where(bias == 0.0, 1.0, 0.0), 0.0), axis=0, keepdims=True)
        tie = jnp.max(jnp.abs(taken - jnp.minimum(cur + 1.0, float(SEL_TOPK)))) > 0.0

        @pl.when(tie)
        def _():
            score = base
            bias = jnp.full(score.shape, NEG_INF, F32)
            for _ in range(SEL_TOPK):
                best = jnp.max(score, axis=0, keepdims=True)
                first = jnp.min(jnp.where(score == best, j, float(n_blk)), axis=0, keepdims=True)
                pick = j == first
                bias = jnp.where(pick, 0.0, bias)
                score = jnp.where(pick, -jnp.inf, score)
            mb_ref[0, 0, :n_blk, :] = bias.astype(mb_ref.dtype)

    assert n_sel >= SEL_TOPK and tq % SEL_BLOCK == 0
    n_need = lax.shift_right_logical(start + tq, _log2(CMP_STRIDE)) - 1
    chunks = lax.shift_right_logical(n_need + CMP_CHUNK - 1, _log2(CMP_CHUNK))
    n_variants = -(-n1 // CMP_CHUNK)
    for c in range(1, n_variants + 1):
        pl.when(chunks == c)(functools.partial(attend, min(c * CMP_CHUNK, n1)))


def _cmp_attention(qt, k_cmp, v_cmp_t, ov_t, n_sel):
    b, _, s = qt.shape
    n1 = k_cmp.shape[1]
    n_pad = ov_t.shape[0]
    tq = min(CMP_Q_TILE, s)
    group_rows = GQA_GROUP * HEAD_DIM
    head_blk = pl.BlockSpec((1, group_rows, tq), lambda i, h, j: (i, h, j))
    return pl.pallas_call(
        functools.partial(_cmp_body, n_sel),
        grid=(b, N_KV_HEADS, s // tq),
        in_specs=[head_blk,
                  pl.BlockSpec((1, n1, KV_WIDTH), lambda i, h, j: (i, 0, 0)),
                  pl.BlockSpec((1, HEAD_DIM, n1), lambda i, h, j: (i, h, 0)),
                  _const_spec((n_pad, n1))],
        out_specs=[head_blk, pl.BlockSpec((1, 1, n_pad, tq), lambda i, h, j: (i, h, 0, j))],
        out_shape=[jax.ShapeDtypeStruct((b, N_HEADS * HEAD_DIM, s), F32),
                   jax.ShapeDtypeStruct((b, N_KV_HEADS, n_pad, s), MXU_DTYPE)],
        compiler_params=_params(("parallel", "parallel", "parallel")),
        name="cmp_attention_select",
    )(qt, k_cmp, v_cmp_t, ov_t)


def _selwin_body(q_ref, mb_ref, ks_ref, oh_ref, vst_ref, kw_ref, vwt_ref, oc_ref, gate_ref, y_ref,
                 qa_ref, acc_ref, win_ref, seen_ref):
    tq = q_ref.shape[2]
    tk = tq
    kvh = pl.program_id(1)
    qi = pl.program_id(2)
    start = qi * tq
    rows = GQA_GROUP * tq
    qt = _stack_heads(q_ref)
    qp = _place_in_kv_half(qt, kvh)
    t_row = _query_positions(start, tq)
    group_shift = _log2(SEL_GROUP * SEL_BLOCK // tk)
    n_groups = mb_ref.shape[2] // SEL_GROUP

    for grp in range(n_groups):
        bias_rows = mb_ref[0, 0, grp * SEL_GROUP:(grp + 1) * SEL_GROUP, :]
        qa_ref[grp] = jnp.concatenate([jnp.concatenate([bias_rows] * GQA_GROUP, axis=1), qp], axis=0)

    def scores(kt, n_sub, causal):
        key_rows = pl.ds(pl.multiple_of(kt * tk, tk), n_sub * tk)
        oh_rows = pl.ds(pl.multiple_of(lax.rem(kt * tk, oh_ref.shape[0]), tk), n_sub * tk)
        k = jnp.concatenate([oh_ref[oh_rows, :], ks_ref[0, key_rows, :]], axis=1)
        s = _mm(k, qa_ref[lax.shift_right_logical(kt, group_shift)])
        if causal:
            kp = kt * tk + lax.broadcasted_iota(jnp.int32, (n_sub * tk, 1), 0)
            s = jnp.where(kp <= t_row, s, NEG_INF)
        return s

    def values(kt, n_sub):
        return _with_ones(vst_ref[0, :, pl.ds(pl.multiple_of(kt * tk, tk), n_sub * tk)])

    def col_max(s):
        return jnp.max(s, axis=0, keepdims=True)

    n_past = qi

    assert tq == WINDOW
    own_rows = pl.ds(pl.multiple_of(start, tq), tq)
    first_tile = qi == 0
    win_rows = pl.ds(pl.multiple_of(jnp.maximum(start - WINDOW, 0), tq), WINDOW + tq)

    def window_scores():
        s = _mm(kw_ref[0, win_rows, :], qp)
        kp = jnp.maximum(start - WINDOW, 0) + lax.broadcasted_iota(jnp.int32, (WINDOW + tq, 1), 0)
        causal_lhs, causal_rhs = -kp, -(t_row + 1)
        lhs = jnp.where(first_tile, causal_lhs[:WINDOW], kp[:WINDOW])
        rhs = jnp.where(first_tile, causal_rhs, t_row - WINDOW)
        return jnp.concatenate([jnp.where(lhs > rhs, s[:WINDOW], NEG_INF),
                                jnp.where(causal_lhs[WINDOW:] > causal_rhs, s[WINDOW:], NEG_INF)], axis=0)

    def pv(v_t, s, ref):
        return _mm(_with_ones(v_t), jnp.exp2(s - ref).astype(MXU_DTYPE))

    def self_scores(k_ref):
        k_t = k_ref[0, own_rows, :].astype(F32).T
        k_own = k_t[:HEAD_DIM]
        for h in range(1, N_KV_HEADS):
            k_own = jnp.where(kvh == h, k_t[h * HEAD_DIM:(h + 1) * HEAD_DIM], k_own)
        parts = [jnp.sum(qt[:, g * tq:(g + 1) * tq].astype(F32) * k_own, axis=0, keepdims=True)
                 for g in range(GQA_GROUP)]
        return jnp.concatenate(parts, axis=1)

    ref = self_scores(ks_ref)
    ref_w = self_scores(kw_ref)
    s_win = window_scores()
    win_ref[...] = pv(vwt_ref[0, :, win_rows], s_win, ref_w)
    excess_w = jnp.max(col_max(s_win) - ref_w)

    def consume(s, kt, n_sub):
        acc_ref[...] += _mm(values(kt, n_sub), jnp.exp2(s - ref).astype(MXU_DTYPE))
        seen_ref[...] = jnp.maximum(seen_ref[...], col_max(s))

    def stream(kt, n_sub, causal=False):
        consume(scores(kt, n_sub, causal), kt, n_sub)

    acc_ref[...] = jnp.zeros_like(acc_ref)
    seen_ref[...] = ref
    n_big = lax.shift_right_logical(n_past, _log2(STREAM_TILES))
    lax.fori_loop(0, n_big, lambda i, c: (stream(i * STREAM_TILES, STREAM_TILES), c)[1], 0)
    lax.fori_loop(n_big * STREAM_TILES, n_past, lambda i, c: (stream(i, 1), c)[1], 0)
    stream(qi, 1, causal=True)
    overflow = jnp.maximum(jnp.max(seen_ref[...] - ref), excess_w) > EXP_HEADROOM

    @pl.when(overflow)
    def _():
        def tile(kt, causal):
            s = scores(kt, 1, causal)
            m_t = col_max(s)
            return m_t, _mm(values(kt, 1), jnp.exp2(s - m_t).astype(MXU_DTYPE))

        def merge(carry, part):
            m, acc = carry
            m_t, o_t = part
            m_new = jnp.maximum(m, m_t)
            return m_new, acc * jnp.exp2(m - m_new) + o_t * jnp.exp2(m_t - m_new)

        carry = (jnp.full((1, rows), M_INIT, F32), jnp.zeros(acc_ref.shape, F32))
        carry = lax.fori_loop(0, n_past, lambda i, c: merge(c, tile(i, False)), carry)
        acc_ref[...] = merge(carry, tile(qi, True))[1]

        x_win = window_scores()
        win_ref[...] = pv(vwt_ref[0, :, win_rows], x_win, col_max(x_win))

    o_sel = _normalize(acc_ref[...])
    o_win = _normalize(win_ref[...])

    gate_row = kvh * (GQA_GROUP * N_BRANCH)
    for g in range(GQA_GROUP):
        cols = slice(g * tq, (g + 1) * tq)
        head = slice(g * HEAD_DIM, (g + 1) * HEAD_DIM)
        gates = [gate_ref[0, pl.ds(gate_row + N_BRANCH * g + br, 1), :] for br in range(N_BRANCH)]
        y = gates[0] * oc_ref[0, head, :] + gates[1] * o_sel[:, cols] + gates[2] * o_win[:, cols]
        y_ref[0, head, :] = y.astype(y_ref.dtype)


def _sel_win_attention(qt, mb_t, ks, key_onehot, vs_t, kw, vw_t, oc_t, gates_t):
    b, _, s = qt.shape
    tq = min(SEL_TILE, s)
    n_pad = mb_t.shape[2]
    n_gate = gates_t.shape[1]
    group_rows = GQA_GROUP * HEAD_DIM
    head_blk = pl.BlockSpec((1, group_rows, tq), lambda i, h, j: (i, h, j))
    once = pl.Buffered(1)
    keys = pl.BlockSpec((1, s, KV_WIDTH), lambda i, h, j: (i, 0, 0), pipeline_mode=once)
    vals = pl.BlockSpec((1, HEAD_DIM, s), lambda i, h, j: (i, h, 0), pipeline_mode=once)
    acc = pltpu.VMEM((HEAD_DIM + ONES_ROWS, GQA_GROUP * tq), F32)
    return pl.pallas_call(
        _selwin_body,
        grid=(b, N_KV_HEADS, s // tq),
        in_specs=[head_blk,
                  pl.BlockSpec((1, 1, n_pad, tq), lambda i, h, j: (i, h, 0, j)),
                  keys, _const_spec(key_onehot.shape), vals, keys, vals,
                  head_blk,
                  pl.BlockSpec((1, n_gate, tq), lambda i, h, j: (i, 0, j))],
        out_specs=head_blk,
        out_shape=jax.ShapeDtypeStruct((b, N_HEADS * HEAD_DIM, s), MXU_DTYPE),
        scratch_shapes=[pltpu.VMEM((n_pad // SEL_GROUP, SEL_GROUP + KV_WIDTH, GQA_GROUP * tq), MXU_DTYPE),
                        acc, acc, pltpu.VMEM((1, GQA_GROUP * tq), F32)],
        compiler_params=_params(("parallel", "parallel", "arbitrary")),
        name="sel_win_attention",
    )(qt, mb_t, ks, key_onehot, vs_t, kw, vw_t, oc_t, gates_t)


def _outproj_ffn_body(h_ref, ypl_ref, yat_ref, w1_ref, w2_ref, g_ref, *refs):
    ffn_refs, o_ref = refs[:N_FFN_OPERANDS], refs[N_FFN_OPERANDS]
    m = _mm(ypl_ref[0], w1_ref[...]) + _mm_tn(yat_ref[0], w2_ref[...])
    o_ref[0] = _ffn_math(h_ref[0] + _rms(m, g_ref[...]), ffn_refs)


def _outproj_ffn(h3, ypl, yat_t, w1, w2, g_post, ffn_weights):
    b, s, d = h3.shape
    tm = min(ROW_TILE, s)
    row = lambda w: pl.BlockSpec((1, tm, w), lambda i, j: (i, j, 0))
    ffn_specs, ffn_args = _ffn_operands(*ffn_weights)
    return pl.pallas_call(
        _outproj_ffn_body,
        grid=(b, s // tm),
        in_specs=[row(d), row(ypl.shape[2]),
                  pl.BlockSpec((1, yat_t.shape[1], tm), lambda i, j: (i, 0, j)),
                  _const_spec(w1.shape), _const_spec(w2.shape), _const_spec((1, d))] + ffn_specs,
        out_specs=row(d),
        out_shape=jax.ShapeDtypeStruct((b, s, d), F32),
        compiler_params=_params(("parallel", "parallel")),
        name="mixer_out_proj_ffn",
    )(h3, ypl, yat_t, w1, w2, g_post[None, :], *ffn_args)


def _block_diag(blocks):
    n, a, b = blocks.shape
    eye = jnp.eye(n, dtype=blocks.dtype)
    return jnp.einsum("nab,nm->namb", blocks, eye).reshape(n * a, n * b)


def _compress_weights(w, pe):
    eye = jnp.eye(N_KV_HEADS, dtype=w.dtype)
    halves = []
    for part in range(CMP_LEN // CMP_STRIDE):
        wl = w[part * CMP_STRIDE:(part + 1) * CMP_STRIDE]
        pel = pe[part * CMP_STRIDE:(part + 1) * CMP_STRIDE]
        wm = jnp.einsum("lde,hg->lhdge", wl, eye).reshape(CMP_STRIDE * KV_WIDTH, KV_WIDTH)
        pm = jnp.tile(pel[:, None, :], (1, N_KV_HEADS, 1)).reshape(1, -1)
        halves.append((wm.astype(MXU_DTYPE), pm))
    return halves


def kernel(x, positions, ffn1_pre_g, ffn1_post_g, ffn1_w_gate, ffn1_w_up, ffn1_w_down, mix_pre_g, mix_post_g,
           w_in, w_out, pool_w, pool_scale, conv_w, conv_b, lru_w_r, lru_b_r, lru_w_i, lru_b_i, lru_lambda,
           cmp_w_k, cmp_w_v, cmp_pe, ffn2_pre_g, ffn2_post_g, ffn2_w_gate, ffn2_w_up, ffn2_w_down):
    b, s, d = x.shape
    depth = w_in.shape[0]
    t = b * s
    pool_width = pool_w.shape[1] * pool_w.shape[2]
    lru_width = lru_w_r.shape[1] * lru_w_r.shape[2]
    attn_width = N_HEADS * HEAD_DIM
    assert pool_width == lru_width and s % SEL_TILE == 0 and s % (CMP_STRIDE * 8) == 0
    assert WINDOW % SEL_TILE == 0 and (SEL_GROUP * SEL_BLOCK) % (STREAM_TILES * SEL_TILE) == 0
    assert s >= WINDOW + SEL_TILE

    sizes = [("xpl", pool_width + 2 * lru_width), ("q", attn_width), ("kc", KV_WIDTH), ("vc", KV_WIDTH),
             ("ks", KV_WIDTH), ("vs", KV_WIDTH), ("kw", KV_WIDTH), ("vw", KV_WIDTH), ("g", V7X_LANES)]
    cols, off = {}, 0
    for name, width in sizes:
        cols[name] = (off, off + width)
        off += width

    n_sel = s // SEL_BLOCK
    n_pad = -(-n_sel // SEL_GROUP) * SEL_GROUP
    n1 = s // CMP_STRIDE
    n_cmp = (s - CMP_LEN) // CMP_STRIDE + 1
    cmp_start = jnp.arange(n1) * CMP_STRIDE
    sel_start = jnp.arange(n_pad) * SEL_BLOCK
    overlap_t = ((cmp_start[None, :] < sel_start[:, None] + SEL_BLOCK) &
                 (cmp_start[None, :] + CMP_LEN > sel_start[:, None]) &
                 (jnp.arange(n1)[None, :] < n_cmp) & (jnp.arange(n_pad)[:, None] < n_sel)).astype(MXU_DTYPE)
    key_blk = jnp.arange(min(s, SEL_GROUP * SEL_BLOCK)) // SEL_BLOCK
    key_onehot = (key_blk[:, None] == jnp.arange(SEL_GROUP)[None, :]).astype(MXU_DTYPE)

    cos, sin = _rope_tables(positions)

    h = x
    for l in range(depth):
        cast = lambda w: w[l].astype(MXU_DTYPE)
        ffn1 = (ffn1_pre_g[l], ffn1_post_g[l], cast(ffn1_w_gate), cast(ffn1_w_up), cast(ffn1_w_down))
        ffn2 = (ffn2_pre_g[l], ffn2_post_g[l], cast(ffn2_w_gate), cast(ffn2_w_up), cast(ffn2_w_down))

        w_all = jnp.pad(w_in[l], ((0, 0), (0, off - w_in.shape[2]))).astype(MXU_DTYPE)
        h = _ffn(h, ffn1)
        xpl, qt, kc, vc, ks, vs_t, kw, vw_t, gates_t = _inproj(h, mix_pre_g[l], w_all, cols, cos, sin)

        w_ri = jnp.concatenate([_block_diag(lru_w_r[l]), _block_diag(lru_w_i[l])], axis=1).astype(MXU_DTYPE)
        ypl = _poollru(xpl, _block_diag(pool_w[l]).astype(MXU_DTYPE), pool_scale[l],
                       conv_w[l], conv_b[l], w_ri, jnp.concatenate([lru_b_r[l], lru_b_i[l]]), lru_lambda[l])

        (wk_a, pe_a), (wk_b, pe_b) = _compress_weights(cmp_w_k[l], cmp_pe[l])
        (wv_a, _), (wv_b, _) = _compress_weights(cmp_w_v[l], cmp_pe[l])
        k_cmp = _compress(kc.reshape(b, n1, -1), pe_a, pe_b, wk_a, wk_b, False)
        v_cmp_t = _compress(vc.reshape(b, n1, -1), pe_a, pe_b, wv_a, wv_b, True)

        oc_t, mb_t = _cmp_attention(qt, k_cmp, v_cmp_t, overlap_t, n_sel)
        y_t = _sel_win_attention(qt, mb_t, ks, key_onehot, vs_t, kw, vw_t, oc_t, gates_t)

        w_o = w_out[l].astype(MXU_DTYPE)
        split = pool_width + lru_width
        h = _outproj_ffn(h, ypl, y_t, w_o[:split], w_o[split:], mix_post_g[l], ffn2)
    return h
```

```python
import functools

import jax
import jax.numpy as jnp
from jax import lax
from jax.experimental import pallas as pl
from jax.experimental.pallas import tpu as pltpu

F32 = jnp.float32
MXU_DTYPE = jnp.bfloat16

POOL_WINDOWS = (2, 4, 8, 16)
POOL_GROUP = 64
LRU_HEAD_DIM = 64
LRU_C = 8.0
CONV_WIDTH = 4
HEAD_DIM = 64
N_KV_HEADS = 2
GQA_GROUP = 4
N_HEADS = N_KV_HEADS * GQA_GROUP
KV_WIDTH = N_KV_HEADS * HEAD_DIM
N_BRANCH = 3
CMP_LEN = 32
CMP_STRIDE = 16
SEL_BLOCK = 64
SEL_TOPK = 16
WINDOW = 512
ROPE_THETA = 10000.0
NORM_EPS = 1e-6
NEG_INF = -1e30
BIG_SCORE = 1e9

V7X_LANES = 128
V7X_VMEM_LIMIT_BYTES = 56 * 1024 * 1024

ROW_TILE = 512
SEQ_TILE = 512
CMP_Q_TILE = 512
CMP_CHUNK = 256
SEL_TILE = 512
SEL_GROUP = 128
ONES_ROWS = 16
M_INIT = -1e20
STREAM_TILES = 2
EXP_HEADROOM = 64.0
Q_SCALE = HEAD_DIM ** -0.5 * 1.4426950408889634


def _params(semantics):
    return pltpu.CompilerParams(dimension_semantics=semantics, vmem_limit_bytes=V7X_VMEM_LIMIT_BYTES)


def _rms(x, g):
    return x * lax.rsqrt(jnp.mean(x * x, axis=-1, keepdims=True) + NORM_EPS) * g


def _sigmoid(x):
    return 1.0 / (1.0 + jnp.exp(-x))


def _mm(a, b):
    return jnp.dot(a, b, preferred_element_type=F32)


def _mm_tn(a_t, b):
    return lax.dot_general(a_t, b, (((0,), (0,)), ((), ())), preferred_element_type=F32)


def _log2(n):
    assert n & (n - 1) == 0
    return n.bit_length() - 1


def _const_spec(shape):
    zeros = (0,) * len(shape)
    return pl.BlockSpec(shape, lambda *_: zeros, pipeline_mode=pl.Buffered(1))


def _rope_body(pos_ref, inv_ref, cos_ref, sin_ref):
    ang = pos_ref[0].astype(F32) * inv_ref[...]
    lane = lax.broadcasted_iota(jnp.int32, ang.shape, 1)
    cos_ref[0] = jnp.cos(ang)
    sin_ref[0] = jnp.where((lane & (HEAD_DIM - 1)) < HEAD_DIM // 2, -jnp.sin(ang), jnp.sin(ang))


def _rope_tables(positions):
    b, s = positions.shape
    inv = ROPE_THETA ** (-jnp.arange(0, HEAD_DIM, 2, dtype=F32) / HEAD_DIM)
    inv_row = jnp.tile(inv, V7X_LANES // (HEAD_DIM // 2))[None, :]
    ts = min(SEQ_TILE, s)
    out = jax.ShapeDtypeStruct((b, s, V7X_LANES), F32)
    return pl.pallas_call(
        _rope_body,
        grid=(b, s // ts),
        in_specs=[pl.BlockSpec((1, ts, 1), lambda i, j: (i, j, 0)),
                  pl.BlockSpec((1, V7X_LANES), lambda i, j: (0, 0))],
        out_specs=[pl.BlockSpec((1, ts, V7X_LANES), lambda i, j: (i, j, 0))] * 2,
        out_shape=[out, out],
        compiler_params=_params(("parallel", "parallel")),
        name="rope_tables",
    )(positions[:, :, None], inv_row)


def _ffn_math(h, ffn_refs):
    gpre_ref, gpost_ref, wg_ref, wu_ref, wd_ref = ffn_refs
    xn = _rms(h, gpre_ref[...]).astype(MXU_DTYPE)
    gate = _mm(xn, wg_ref[...])
    up = _mm(xn, wu_ref[...])
    act = (gate * _sigmoid(gate) * up).astype(MXU_DTYPE)
    f = _mm(act, wd_ref[...])
    return h + 0.5 * _rms(f, gpost_ref[...])


def _ffn_operands(g_pre, g_post, w_gate, w_up, w_down):
    d, dff = w_gate.shape
    specs = [_const_spec((1, d)), _const_spec((1, d)),
             _const_spec((d, dff)), _const_spec((d, dff)), _const_spec((dff, d))]
    return specs, (g_pre[None, :], g_post[None, :], w_gate, w_up, w_down)


N_FFN_OPERANDS = 5


def _ffn_body(h_ref, *refs):
    refs[N_FFN_OPERANDS][0] = _ffn_math(h_ref[0], refs[:N_FFN_OPERANDS])


def _ffn(h3, ffn_weights):
    b, s, d = h3.shape
    tm = min(ROW_TILE, s)
    row = pl.BlockSpec((1, tm, d), lambda i, j: (i, j, 0))
    ffn_specs, ffn_args = _ffn_operands(*ffn_weights)
    return pl.pallas_call(
        _ffn_body,
        grid=(b, s // tm),
        in_specs=[row] + ffn_specs,
        out_specs=row,
        out_shape=jax.ShapeDtypeStruct((b, s, d), F32),
        compiler_params=_params(("parallel", "parallel")),
        name="ffn",
    )(h3, *ffn_args)


def _swap_halves(x):
    n = x.shape[1]
    lane = lax.broadcasted_iota(jnp.int32, x.shape, 1)
    first_half = (lane & (HEAD_DIM - 1)) < HEAD_DIM // 2
    return jnp.where(first_half, pltpu.roll(x, n - HEAD_DIM // 2, 1), pltpu.roll(x, HEAD_DIM // 2, 1))


def _inproj_body(cols, h_ref, g_ref, w_ref, cos_ref, sin_ref,
                 xpl_ref, qt_ref, kc_ref, vc_ref, ks_ref, vst_ref, kw_ref, vwt_ref, gatet_ref):
    xn = _rms(h_ref[0], g_ref[...]).astype(MXU_DTYPE)
    proj = _mm(xn, w_ref[...])
    cos = cos_ref[0]
    sin = sin_ref[0]

    def rope(x):
        rep = x.shape[1] // V7X_LANES
        c = jnp.concatenate([cos] * rep, axis=1) if rep > 1 else cos
        s = jnp.concatenate([sin] * rep, axis=1) if rep > 1 else sin
        return x * c + _swap_halves(x) * s

    def seg(name):
        lo, hi = cols[name]
        return proj[:, lo:hi]

    xpl_ref[0] = seg("xpl")
    qt_ref[0] = (rope(seg("q")) * Q_SCALE).T.astype(qt_ref.dtype)
    kc_ref[0] = rope(seg("kc"))
    vc_ref[0] = seg("vc")
    ks_ref[0] = rope(seg("ks")).astype(ks_ref.dtype)
    vst_ref[0] = seg("vs").T.astype(vst_ref.dtype)
    kw_ref[0] = rope(seg("kw")).astype(kw_ref.dtype)
    vwt_ref[0] = seg("vw").T.astype(vwt_ref.dtype)
    gatet_ref[0] = _sigmoid(seg("g")).T[:gatet_ref.shape[1]]


def _inproj(h3, g_pre, w_all, cols, cos, sin):
    b, s, d = h3.shape
    tm = min(ROW_TILE, s)
    ncol = w_all.shape[1]
    width = lambda n: cols[n][1] - cols[n][0]
    row = lambda w: pl.BlockSpec((1, tm, w), lambda i, j: (i, j, 0))
    col = lambda w: pl.BlockSpec((1, w, tm), lambda i, j: (i, 0, j))
    tok = lambda n, dt: (jax.ShapeDtypeStruct((b, s, width(n)), dt), row(width(n)))
    chan = lambda w, dt: (jax.ShapeDtypeStruct((b, w, s), dt), col(w))
    outs = [tok("xpl", F32), chan(width("q"), MXU_DTYPE), tok("kc", F32), tok("vc", F32),
            tok("ks", MXU_DTYPE), chan(width("vs"), MXU_DTYPE), tok("kw", MXU_DTYPE),
            chan(width("vw"), MXU_DTYPE), chan(N_BRANCH * N_HEADS, F32)]
    return pl.pallas_call(
        functools.partial(_inproj_body, cols),
        grid=(b, s // tm),
        in_specs=[row(d), _const_spec((1, d)), _const_spec((d, ncol)), row(V7X_LANES), row(V7X_LANES)],
        out_specs=[o[1] for o in outs],
        out_shape=[o[0] for o in outs],
        compiler_params=_params(("parallel", "parallel")),
        name="mixer_in_proj",
    )(h3, g_pre[None, :], w_all, cos, sin)


def _poollru_body(x_ref, pw_ref, pscale_ref, cw_ref, cb_ref, wri_ref, bri_ref, lam_ref,
                  y_ref, pool_carry, conv_carry, h_carry):
    si = pl.program_id(1)
    ts = x_ref.shape[1]
    width = pw_ref.shape[0]
    halo_p = pool_carry.shape[0]
    halo_c = conv_carry.shape[0]

    @pl.when(si == 0)
    def _():
        pool_carry[...] = jnp.zeros_like(pool_carry)
        conv_carry[...] = jnp.zeros_like(conv_carry)
        h_carry[...] = jnp.zeros_like(h_carry)

    x = x_ref[0]
    xp = x[:, :width]
    xl = x[:, width:2 * width]
    gl = x[:, 2 * width:]

    ext = jnp.concatenate([pool_carry[...], xp], axis=0)
    sums = [ext]
    shift = 1
    for _ in POOL_WINDOWS:
        sums.append(sums[-1] + pltpu.roll(sums[-1], shift, 0))
        shift *= 2
    lane = lax.broadcasted_iota(jnp.int32, (1, width), 1)
    grp = lax.shift_right_logical(lane, _log2(POOL_GROUP))
    win_sum = sums[len(POOL_WINDOWS)]
    win = jnp.full((1, width), float(POOL_WINDOWS[-1]), F32)
    for gi in range(len(POOL_WINDOWS) - 2, -1, -1):
        win_sum = jnp.where(grp == gi, sums[gi + 1], win_sum)
        win = jnp.where(grp == gi, float(POOL_WINDOWS[gi]), win)
    win_sum = win_sum[halo_p:]
    t_abs = si * ts + lax.broadcasted_iota(jnp.int32, (ts, 1), 0)
    cnt = jnp.minimum((t_abs + 1).astype(F32), win)
    pooled = win_sum / cnt
    y_pool = _mm((pooled - xp).astype(MXU_DTYPE), pw_ref[...]) * pscale_ref[...]
    pool_carry[...] = xp[ts - halo_p:]

    extc = jnp.concatenate([conv_carry[...], xl], axis=0)
    xc = extc * cw_ref[CONV_WIDTH - 1:CONV_WIDTH, :]
    for k in range(1, CONV_WIDTH):
        xc = xc + pltpu.roll(extc, k, 0) * cw_ref[CONV_WIDTH - 1 - k:CONV_WIDTH - k, :]
    xc = xc[halo_c:] + cb_ref[...]
    conv_carry[...] = xl[ts - halo_c:]

    ri = _mm(xc.astype(MXU_DTYPE), wri_ref[...]) + bri_ref[...]
    r = _sigmoid(ri[:, :width])
    i_gate = _sigmoid(ri[:, width:])
    neg_lam = -lam_ref[...]
    softplus = jnp.maximum(neg_lam, 0.0) + jnp.log1p(jnp.exp(-jnp.abs(neg_lam)))
    log_a = -LRU_C * r * softplus
    a = jnp.exp(log_a)
    b = jnp.sqrt(-jnp.tanh(log_a) * (a * a + 1.0)) * (i_gate * xc)

    row = lax.broadcasted_iota(jnp.int32, (ts, 1), 0)
    k = 1
    while k < ts:
        keep = row >= k
        a_prev = jnp.where(keep, pltpu.roll(a, k, 0), 1.0)
        b_prev = jnp.where(keep, pltpu.roll(b, k, 0), 0.0)
        b = a * b_prev + b
        a = a * a_prev
        k *= 2
    h = a * h_carry[0:1, :] + b
    h_carry[...] = jnp.broadcast_to(h[ts - 1:ts, :], h_carry.shape)

    gelu = 0.5 * gl * (1.0 + jnp.tanh(0.7978845608028654 * (gl + 0.044715 * gl * gl * gl)))
    y_ref[0] = jnp.concatenate([y_pool, h * gelu], axis=1).astype(y_ref.dtype)


def _poollru(xpl, pool_w_bd, pool_scale, conv_w, conv_b, w_ri_bd, b_ri, lam):
    b, s, w3 = xpl.shape
    width = w3 // 3
    ts = min(SEQ_TILE, s)
    return pl.pallas_call(
        _poollru_body,
        grid=(b, s // ts),
        in_specs=[pl.BlockSpec((1, ts, w3), lambda i, j: (i, j, 0)),
                  _const_spec((width, width)), _const_spec((1, width)),
                  _const_spec((CONV_WIDTH, width)), _const_spec((1, width)),
                  _const_spec((width, 2 * width)), _const_spec((1, 2 * width)), _const_spec((1, width))],
        out_specs=pl.BlockSpec((1, ts, 2 * width), lambda i, j: (i, j, 0)),
        out_shape=jax.ShapeDtypeStruct((b, s, 2 * width), MXU_DTYPE),
        scratch_shapes=[pltpu.VMEM((POOL_WINDOWS[-1], width), F32),
                        pltpu.VMEM((8, width), F32),
                        pltpu.VMEM((8, width), F32)],
        compiler_params=_params(("parallel", "arbitrary")),
        name="pool_rglru",
    )(xpl, pool_w_bd, pool_scale[None, :], conv_w, conv_b[None, :], w_ri_bd, b_ri[None, :], lam[None, :])


def _compress_body(transpose_out, r_ref, pea_ref, peb_ref, wa_ref, wb_ref, o_ref):
    r = r_ref[0]
    first = _mm((r + pea_ref[...]).astype(MXU_DTYPE), wa_ref[...])
    second = _mm((r + peb_ref[...]).astype(MXU_DTYPE), wb_ref[...])
    n1 = r.shape[0]
    out = first + pltpu.roll(second, n1 - 1, 0)
    o_ref[0] = (out.T if transpose_out else out).astype(o_ref.dtype)


def _compress(rows, pe_a, pe_b, w_a, w_b, transpose_out):
    b, n1, k = rows.shape
    n_out = w_a.shape[1]
    out_dims = (n_out, n1) if transpose_out else (n1, n_out)
    return pl.pallas_call(
        functools.partial(_compress_body, transpose_out),
        grid=(b,),
        in_specs=[pl.BlockSpec((1, n1, k), lambda i: (i, 0, 0)),
                  _const_spec((1, k)), _const_spec((1, k)), _const_spec((k, n_out)), _const_spec((k, n_out))],
        out_specs=pl.BlockSpec((1,) + out_dims, lambda i: (i, 0, 0)),
        out_shape=jax.ShapeDtypeStruct((b,) + out_dims, MXU_DTYPE),
        compiler_params=_params(("parallel",)),
        name="compress_kv",
    )(rows, pe_a, pe_b, w_a, w_b)


def _stack_heads(q_ref):
    return jnp.concatenate([q_ref[0, g * HEAD_DIM:(g + 1) * HEAD_DIM, :] for g in range(GQA_GROUP)], axis=1)


def _place_in_kv_half(qt, kvh):
    zeros = jnp.zeros_like(qt)
    parts = [jnp.where(kvh == h, qt, zeros) for h in range(N_KV_HEADS)]
    return jnp.concatenate(parts, axis=0)


def _query_positions(start, tq):
    lane = lax.broadcasted_iota(jnp.int32, (1, GQA_GROUP * tq), 1)
    return start + (lane & (tq - 1))


def _with_ones(v_t):
    return jnp.concatenate([v_t, jnp.ones((ONES_ROWS, v_t.shape[1]), v_t.dtype)], axis=0)


def _normalize(acc):
    return acc[:HEAD_DIM] * (1.0 / acc[HEAD_DIM:HEAD_DIM + 1])


def _cmp_body(n_sel, q_ref, kc_ref, vct_ref, ovt_ref, oc_ref, mb_ref):
    tq = q_ref.shape[2]
    kvh = pl.program_id(1)
    start = pl.program_id(2) * tq
    qt = _stack_heads(q_ref)
    qp = _place_in_kv_half(qt, kvh)
    n1 = kc_ref.shape[1]
    t_row = _query_positions(start, tq)
    assert CMP_STRIDE * CMP_CHUNK >= tq + CMP_LEN

    def attend(nrows):
        s = _mm(kc_ref[0, :nrows, :], qp)
        lo = max(0, nrows - 2 * CMP_CHUNK)
        n_idx = lo + lax.broadcasted_iota(jnp.int32, (nrows - lo, 1), 0)
        tail = jnp.where(n_idx * CMP_STRIDE + (CMP_LEN - 1) <= t_row, s[lo:], NEG_INF)
        s = jnp.concatenate([s[:lo], tail], axis=0) if lo else tail
        m = jnp.max(s, axis=0, keepdims=True)
        p = jnp.exp2(s - jnp.where(m > 0.5 * NEG_INF, m, 0.0))
        den = jnp.sum(p, axis=0, keepdims=True)
        pn = p * (1.0 / jnp.where(den > 0.0, den, 1.0))
        oc = _mm(vct_ref[0, :, :nrows], pn.astype(MXU_DTYPE))
        psum = pn[:, 0:tq]
        for g in range(GQA_GROUP):
            oc_ref[0, g * HEAD_DIM:(g + 1) * HEAD_DIM, :] = oc[:, g * tq:(g + 1) * tq]
            if g:
                psum = psum + pn[:, g * tq:(g + 1) * tq]
        n_blk = min(n_sel, nrows * CMP_STRIDE // SEL_BLOCK)
        ovt = ovt_ref[:n_blk, :nrows]
        hi = psum.astype(MXU_DTYPE)
        low = (psum - hi.astype(F32)).astype(MXU_DTYPE)
        select(_mm(ovt, hi) + _mm(ovt, low), n_blk)

    def select(imp, n_blk):
        j = lax.broadcasted_iota(jnp.int32, (n_blk, 1), 0).astype(F32)
        t_q = start + lax.broadcasted_iota(jnp.int32, (1, tq), 1)
        cur = lax.shift_right_logical(t_q, _log2(SEL_BLOCK)).astype(F32)
        forced = (j == 0.0) | (j == cur) | (j == cur - 1.0)
        valid = j <= cur
        base = jnp.where(valid, jnp.where(forced, BIG_SCORE, imp), -BIG_SCORE)
        n_pad = mb_ref.shape[2]
        if n_pad > n_blk:
            mb_ref[0, 0, n_blk:, :] = jnp.full((n_pad - n_blk, tq), NEG_INF, mb_ref.dtype)

        score = jnp.where(forced, -jnp.inf, base)
        for _ in range(SEL_TOPK - 3):
            score = jnp.where(score == jnp.max(score, axis=0, keepdims=True), -jnp.inf, score)
        bias = jnp.where(score == -jnp.inf, 0.0, NEG_INF)
        mb_ref[0, 0, :n_blk, :] = bias.astype(mb_ref.dtype)
        taken = jnp.sum(jnp.where(valid, jnp.where(bias == 0.0, 1.0, 0.0), 0.0), axis=0, keepdims=True)
        tie = jnp.max(jnp.abs(taken - jnp.minimum(cur + 1.0, float(SEL_TOPK)))) > 0.0

        @pl.when(tie)
        def _():
            score = base
            bias = jnp.full(score.shape, NEG_INF, F32)
            for _ in range(SEL_TOPK):
                best = jnp.max(score, axis=0, keepdims=True)
                first = jnp.min(jnp.where(score == best, j, float(n_blk)), axis=0, keepdims=True)
                pick = j == first
                bias = jnp.where(pick, 0.0, bias)
                score = jnp.where(pick, -jnp.inf, score)
            mb_ref[0, 0, :n_blk, :] = bias.astype(mb_ref.dtype)

    assert n_sel >= SEL_TOPK and tq % SEL_BLOCK == 0
    n_need = lax.shift_right_logical(start + tq, _log2(CMP_STRIDE)) - 1
    chunks = lax.shift_right_logical(n_need + CMP_CHUNK - 1, _log2(CMP_CHUNK))
    n_variants = -(-n1 // CMP_CHUNK)
    for c in range(1, n_variants + 1):
        pl.when(chunks == c)(functools.partial(attend, min(c * CMP_CHUNK, n1)))


def _cmp_attention(qt, k_cmp, v_cmp_t, ov_t, n_sel):
    b, _, s = qt.shape
    n1 = k_cmp.shape[1]
    n_pad = ov_t.shape[0]
    tq = min(CMP_Q_TILE, s)
    group_rows = GQA_GROUP * HEAD_DIM
    head_blk = pl.BlockSpec((1, group_rows, tq), lambda i, h, j: (i, h, j))
    return pl.pallas_call(
        functools.partial(_cmp_body, n_sel),
        grid=(b, N_KV_HEADS, s // tq),
        in_specs=[head_blk,
                  pl.BlockSpec((1, n1, KV_WIDTH), lambda i, h, j: (i, 0, 0)),
                  pl.BlockSpec((1, HEAD_DIM, n1), lambda i, h, j: (i, h, 0)),
                  _const_spec((n_pad, n1))],
        out_specs=[head_blk, pl.BlockSpec((1, 1, n_pad, tq), lambda i, h, j: (i, h, 0, j))],
        out_shape=[jax.ShapeDtypeStruct((b, N_HEADS * HEAD_DIM, s), F32),
                   jax.ShapeDtypeStruct((b, N_KV_HEADS, n_pad, s), MXU_DTYPE)],
        compiler_params=_params(("parallel", "parallel", "parallel")),
        name="cmp_attention_select",
    )(qt, k_cmp, v_cmp_t, ov_t)


def _selwin_body(q_ref, mb_ref, ks_ref, oh_ref, vst_ref, kw_ref, vwt_ref, oc_ref, gate_ref, y_ref,
                 qa_ref, acc_ref, win_ref):
    tq = q_ref.shape[2]
    tk = tq
    kvh = pl.program_id(1)
    qi = pl.program_id(2)
    start = qi * tq
    rows = GQA_GROUP * tq
    qt = _stack_heads(q_ref)
    qp = _place_in_kv_half(qt, kvh)
    t_row = _query_positions(start, tq)
    group_shift = _log2(SEL_GROUP * SEL_BLOCK // tk)
    n_groups = mb_ref.shape[2] // SEL_GROUP

    for grp in range(n_groups):
        bias_rows = mb_ref[0, 0, grp * SEL_GROUP:(grp + 1) * SEL_GROUP, :]
        qa_ref[grp] = jnp.concatenate([jnp.concatenate([bias_rows] * GQA_GROUP, axis=1), qp], axis=0)

    def scores(kt, n_sub, causal):
        key_rows = pl.ds(pl.multiple_of(kt * tk, tk), n_sub * tk)
        oh_rows = pl.ds(pl.multiple_of(lax.rem(kt * tk, oh_ref.shape[0]), tk), n_sub * tk)
        k = jnp.concatenate([oh_ref[oh_rows, :], ks_ref[0, key_rows, :]], axis=1)
        s = _mm(k, qa_ref[lax.shift_right_logical(kt, group_shift)])
        if causal:
            kp = kt * tk + lax.broadcasted_iota(jnp.int32, (n_sub * tk, 1), 0)
            s = jnp.where(kp <= t_row, s, NEG_INF)
        return s

    def values(kt, n_sub):
        return _with_ones(vst_ref[0, :, pl.ds(pl.multiple_of(kt * tk, tk), n_sub * tk)])

    def col_max(s):
        return jnp.max(s, axis=0, keepdims=True)

    n_past = qi

    assert tq == WINDOW
    own_rows = pl.ds(pl.multiple_of(start, tq), tq)
    first_tile = qi == 0
    win_rows = pl.ds(pl.multiple_of(jnp.maximum(start - WINDOW, 0), tq), WINDOW + tq)

    def window_scores():
        s = _mm(kw_ref[0, win_rows, :], qp)
        kp = jnp.maximum(start - WINDOW, 0) + lax.broadcasted_iota(jnp.int32, (WINDOW + tq, 1), 0)
        causal_lhs, causal_rhs = -kp, -(t_row + 1)
        lhs = jnp.where(first_tile, causal_lhs[:WINDOW], kp[:WINDOW])
        rhs = jnp.where(first_tile, causal_rhs, t_row - WINDOW)
        return jnp.concatenate([jnp.where(lhs > rhs, s[:WINDOW], NEG_INF),
                                jnp.where(causal_lhs[WINDOW:] > causal_rhs, s[WINDOW:], NEG_INF)], axis=0)

    def pv(v_t, s, ref):
        return _mm(_with_ones(v_t), jnp.exp2(s - ref).astype(MXU_DTYPE))

    def self_scores(k_ref):
        k_t = k_ref[0, own_rows, :].astype(F32).T
        k_own = k_t[:HEAD_DIM]
        for h in range(1, N_KV_HEADS):
            k_own = jnp.where(kvh == h, k_t[h * HEAD_DIM:(h + 1) * HEAD_DIM], k_own)
        parts = [jnp.sum(qt[:, g * tq:(g + 1) * tq].astype(F32) * k_own, axis=0, keepdims=True)
                 for g in range(GQA_GROUP)]
        return jnp.concatenate(parts, axis=1)

    ref = self_scores(ks_ref)
    ref_w = self_scores(kw_ref)
    win_ref[...] = pv(vwt_ref[0, :, win_rows], window_scores(), ref_w)

    def stream(kt, n_sub, causal=False):
        acc_ref[...] += pv(vst_ref[0, :, pl.ds(pl.multiple_of(kt * tk, tk), n_sub * tk)],
                           scores(kt, n_sub, causal), ref)

    acc_ref[...] = jnp.zeros_like(acc_ref)
    n_big = lax.shift_right_logical(n_past, _log2(STREAM_TILES))
    lax.fori_loop(0, n_big, lambda i, c: (stream(i * STREAM_TILES, STREAM_TILES), c)[1], 0)
    lax.fori_loop(n_big * STREAM_TILES, n_past, lambda i, c: (stream(i, 1), c)[1], 0)
    stream(qi, 1, causal=True)
    limit = 2.0 ** EXP_HEADROOM
    in_range = jnp.where(acc_ref[HEAD_DIM:HEAD_DIM + 1, :] < limit,
                         jnp.where(win_ref[HEAD_DIM:HEAD_DIM + 1, :] < limit, 1.0, 0.0), 0.0)
    overflow = jnp.min(in_range) < 1.0

    @pl.when(overflow)
    def _():
        def tile(kt, causal):
            s = scores(kt, 1, causal)
            m_t = col_max(s)
            return m_t, _mm(values(kt, 1), jnp.exp2(s - m_t).astype(MXU_DTYPE))

        def merge(carry, part):
            m, acc = carry
            m_t, o_t = part
            m_new = jnp.maximum(m, m_t)
            return m_new, acc * jnp.exp2(m - m_new) + o_t * jnp.exp2(m_t - m_new)

        carry = (jnp.full((1, rows), M_INIT, F32), jnp.zeros(acc_ref.shape, F32))
        carry = lax.fori_loop(0, n_past, lambda i, c: merge(c, tile(i, False)), carry)
        acc_ref[...] = merge(carry, tile(qi, True))[1]

        x_win = window_scores()
        win_ref[...] = pv(vwt_ref[0, :, win_rows], x_win, col_max(x_win))

    o_sel = _normalize(acc_ref[...])
    o_win = _normalize(win_ref[...])

    gate_row = kvh * (GQA_GROUP * N_BRANCH)
    for g in range(GQA_GROUP):
        cols = slice(g * tq, (g + 1) * tq)
        head = slice(g * HEAD_DIM, (g + 1) * HEAD_DIM)
        gates = [gate_ref[0, pl.ds(gate_row + N_BRANCH * g + br, 1), :] for br in range(N_BRANCH)]
        y = gates[0] * oc_ref[0, head, :] + gates[1] * o_sel[:, cols] + gates[2] * o_win[:, cols]
        y_ref[0, head, :] = y.astype(y_ref.dtype)


def _sel_win_attention(qt, mb_t, ks, key_onehot, vs_t, kw, vw_t, oc_t, gates_t):
    b, _, s = qt.shape
    tq = min(SEL_TILE, s)
    n_pad = mb_t.shape[2]
    n_gate = gates_t.shape[1]
    group_rows = GQA_GROUP * HEAD_DIM
    head_blk = pl.BlockSpec((1, group_rows, tq), lambda i, h, j: (i, h, j))
    once = pl.Buffered(1)
    keys = pl.BlockSpec((1, s, KV_WIDTH), lambda i, h, j: (i, 0, 0), pipeline_mode=once)
    vals = pl.BlockSpec((1, HEAD_DIM, s), lambda i, h, j: (i, h, 0), pipeline_mode=once)
    acc = pltpu.VMEM((HEAD_DIM + ONES_ROWS, GQA_GROUP * tq), F32)
    return pl.pallas_call(
        _selwin_body,
        grid=(b, N_KV_HEADS, s // tq),
        in_specs=[head_blk,
                  pl.BlockSpec((1, 1, n_pad, tq), lambda i, h, j: (i, h, 0, j)),
                  keys, _const_spec(key_onehot.shape), vals, keys, vals,
                  head_blk,
                  pl.BlockSpec((1, n_gate, tq), lambda i, h, j: (i, 0, j))],
        out_specs=head_blk,
        out_shape=jax.ShapeDtypeStruct((b, N_HEADS * HEAD_DIM, s), MXU_DTYPE),
        scratch_shapes=[pltpu.VMEM((n_pad // SEL_GROUP, SEL_GROUP + KV_WIDTH, GQA_GROUP * tq), MXU_DTYPE),
                        acc, acc],
        compiler_params=_params(("parallel", "parallel", "arbitrary")),
        name="sel_win_attention",
    )(qt, mb_t, ks, key_onehot, vs_t, kw, vw_t, oc_t, gates_t)


def _outproj_ffn_body(h_ref, ypl_ref, yat_ref, w1_ref, w2_ref, g_ref, *refs):
    ffn_refs, o_ref = refs[:N_FFN_OPERANDS], refs[N_FFN_OPERANDS]
    m = _mm(ypl_ref[0], w1_ref[...]) + _mm_tn(yat_ref[0], w2_ref[...])
    o_ref[0] = _ffn_math(h_ref[0] + _rms(m, g_ref[...]), ffn_refs)


def _outproj_ffn(h3, ypl, yat_t, w1, w2, g_post, ffn_weights):
    b, s, d = h3.shape
    tm = min(ROW_TILE, s)
    row = lambda w: pl.BlockSpec((1, tm, w), lambda i, j: (i, j, 0))
    ffn_specs, ffn_args = _ffn_operands(*ffn_weights)
    return pl.pallas_call(
        _outproj_ffn_body,
        grid=(b, s // tm),
        in_specs=[row(d), row(ypl.shape[2]),
                  pl.BlockSpec((1, yat_t.shape[1], tm), lambda i, j: (i, 0, j)),
                  _const_spec(w1.shape), _const_spec(w2.shape), _const_spec((1, d))] + ffn_specs,
        out_specs=row(d),
        out_shape=jax.ShapeDtypeStruct((b, s, d), F32),
        compiler_params=_params(("parallel", "parallel")),
        name="mixer_out_proj_ffn",
    )(h3, ypl, yat_t, w1, w2, g_post[None, :], *ffn_args)


def _block_diag(blocks):
    n, a, b = blocks.shape
    eye = jnp.eye(n, dtype=blocks.dtype)
    return jnp.einsum("nab,nm->namb", blocks, eye).reshape(n * a, n * b)


def _compress_weights(w, pe):
    eye = jnp.eye(N_KV_HEADS, dtype=w.dtype)
    halves = []
    for part in range(CMP_LEN // CMP_STRIDE):
        wl = w[part * CMP_STRIDE:(part + 1) * CMP_STRIDE]
        pel = pe[part * CMP_STRIDE:(part + 1) * CMP_STRIDE]
        wm = jnp.einsum("lde,hg->lhdge", wl, eye).reshape(CMP_STRIDE * KV_WIDTH, KV_WIDTH)
        pm = jnp.tile(pel[:, None, :], (1, N_KV_HEADS, 1)).reshape(1, -1)
        halves.append((wm.astype(MXU_DTYPE), pm))
    return halves


def kernel(x, positions, ffn1_pre_g, ffn1_post_g, ffn1_w_gate, ffn1_w_up, ffn1_w_down, mix_pre_g, mix_post_g,
           w_in, w_out, pool_w, pool_scale, conv_w, conv_b, lru_w_r, lru_b_r, lru_w_i, lru_b_i, lru_lambda,
           cmp_w_k, cmp_w_v, cmp_pe, ffn2_pre_g, ffn2_post_g, ffn2_w_gate, ffn2_w_up, ffn2_w_down):
    b, s, d = x.shape
    depth = w_in.shape[0]
    t = b * s
    pool_width = pool_w.shape[1] * pool_w.shape[2]
    lru_width = lru_w_r.shape[1] * lru_w_r.shape[2]
    attn_width = N_HEADS * HEAD_DIM
    assert pool_width == lru_width and s % SEL_TILE == 0 and s % (CMP_STRIDE * 8) == 0
    assert WINDOW % SEL_TILE == 0 and (SEL_GROUP * SEL_BLOCK) % (STREAM_TILES * SEL_TILE) == 0
    assert s >= WINDOW + SEL_TILE

    sizes = [("xpl", pool_width + 2 * lru_width), ("q", attn_width), ("kc", KV_WIDTH), ("vc", KV_WIDTH),
             ("ks", KV_WIDTH), ("vs", KV_WIDTH), ("kw", KV_WIDTH), ("vw", KV_WIDTH), ("g", V7X_LANES)]
    cols, off = {}, 0
    for name, width in sizes:
        cols[name] = (off, off + width)
        off += width

    n_sel = s // SEL_BLOCK
    n_pad = -(-n_sel // SEL_GROUP) * SEL_GROUP
    n1 = s // CMP_STRIDE
    n_cmp = (s - CMP_LEN) // CMP_STRIDE + 1
    cmp_start = jnp.arange(n1) * CMP_STRIDE
    sel_start = jnp.arange(n_pad) * SEL_BLOCK
    overlap_t = ((cmp_start[None, :] < sel_start[:, None] + SEL_BLOCK) &
                 (cmp_start[None, :] + CMP_LEN > sel_start[:, None]) &
                 (jnp.arange(n1)[None, :] < n_cmp) & (jnp.arange(n_pad)[:, None] < n_sel)).astype(MXU_DTYPE)
    key_blk = jnp.arange(min(s, SEL_GROUP * SEL_BLOCK)) // SEL_BLOCK
    key_onehot = (key_blk[:, None] == jnp.arange(SEL_GROUP)[None, :]).astype(MXU_DTYPE)

    cos, sin = _rope_tables(positions)

    h = x
    for l in range(depth):
        cast = lambda w: w[l].astype(MXU_DTYPE)
        ffn1 = (ffn1_pre_g[l], ffn1_post_g[l], cast(ffn1_w_gate), cast(ffn1_w_up), cast(ffn1_w_down))
        ffn2 = (ffn2_pre_g[l], ffn2_post_g[l], cast(ffn2_w_gate), cast(ffn2_w_up), cast(ffn2_w_down))

        w_all = jnp.pad(w_in[l], ((0, 0), (0, off - w_in.shape[2]))).astype(MXU_DTYPE)
        h = _ffn(h, ffn1)
        xpl, qt, kc, vc, ks, vs_t, kw, vw_t, gates_t = _inproj(h, mix_pre_g[l], w_all, cols, cos, sin)

        w_ri = jnp.concatenate([_block_diag(lru_w_r[l]), _block_diag(lru_w_i[l])], axis=1).astype(MXU_DTYPE)
        ypl = _poollru(xpl, _block_diag(pool_w[l]).astype(MXU_DTYPE), pool_scale[l],
                       conv_w[l], conv_b[l], w_ri, jnp.concatenate([lru_b_r[l], lru_b_i[l]]), lru_lambda[l])

        (wk_a, pe_a), (wk_b, pe_b) = _compress_weights(cmp_w_k[l], cmp_pe[l])
        (wv_a, _), (wv_b, _) = _compress_weights(cmp_w_v[l], cmp_pe[l])
        k_cmp = _compress(kc.reshape(b, n1, -1), pe_a, pe_b, wk_a, wk_b, False)
        v_cmp_t = _compress(vc.reshape(b, n1, -1), pe_a, pe_b, wv_a, wv_b, True)

        oc_t, mb_t = _cmp_attention(qt, k_cmp, v_cmp_t, overlap_t, n_sel)
        y_t = _sel_win_attention(qt, mb_t, ks, key_onehot, vs_t, kw, vw_t, oc_t, gates_t)

        w_o = w_out[l].astype(MXU_DTYPE)
        split = pool_width + lru_width
        h = _outproj_ffn(h, ypl, y_t, w_o[:split], w_o[split:], mix_post_g[l], ffn2)
    return h
```

```python
import functools

import jax
import jax.numpy as jnp
from jax import lax
from jax.experimental import pallas as pl
from jax.experimental.pallas import tpu as pltpu

F32 = jnp.float32
MXU_DTYPE = jnp.bfloat16

POOL_WINDOWS = (2, 4, 8, 16)
POOL_GROUP = 64
LRU_HEAD_DIM = 64
LRU_C = 8.0
CONV_WIDTH = 4
HEAD_DIM = 64
N_KV_HEADS = 2
GQA_GROUP = 4
N_HEADS = N_KV_HEADS * GQA_GROUP
KV_WIDTH = N_KV_HEADS * HEAD_DIM
N_BRANCH = 3
CMP_LEN = 32
CMP_STRIDE = 16
SEL_BLOCK = 64
SEL_TOPK = 16
WINDOW = 512
ROPE_THETA = 10000.0
NORM_EPS = 1e-6
NEG_INF = -1e30
BIG_SCORE = 1e9

V7X_LANES = 128
V7X_VMEM_LIMIT_BYTES = 56 * 1024 * 1024

ROW_TILE = 512
SEQ_TILE = 512
CMP_Q_TILE = 512
CMP_CHUNK = 256
SEL_TILE = 512
SEL_GROUP = 128
ONES_ROWS = 16
M_INIT = -1e20
STREAM_TILES = 4
EXP_HEADROOM = 64.0
Q_SCALE = HEAD_DIM ** -0.5 * 1.4426950408889634


def _params(semantics):
    return pltpu.CompilerParams(dimension_semantics=semantics, vmem_limit_bytes=V7X_VMEM_LIMIT_BYTES)


def _rms(x, g):
    return x * lax.rsqrt(jnp.mean(x * x, axis=-1, keepdims=True) + NORM_EPS) * g


def _sigmoid(x):
    return 1.0 / (1.0 + jnp.exp(-x))


def _mm(a, b):
    return jnp.dot(a, b, preferred_element_type=F32)


def _mm_tn(a_t, b):
    return lax.dot_general(a_t, b, (((0,), (0,)), ((), ())), preferred_element_type=F32)


def _log2(n):
    assert n & (n - 1) == 0
    return n.bit_length() - 1


def _const_spec(shape):
    zeros = (0,) * len(shape)
    return pl.BlockSpec(shape, lambda *_: zeros, pipeline_mode=pl.Buffered(1))


def _rope_body(pos_ref, inv_ref, cos_ref, sin_ref):
    ang = pos_ref[0].astype(F32) * inv_ref[...]
    lane = lax.broadcasted_iota(jnp.int32, ang.shape, 1)
    cos_ref[0] = jnp.cos(ang)
    sin_ref[0] = jnp.where((lane & (HEAD_DIM - 1)) < HEAD_DIM // 2, -jnp.sin(ang), jnp.sin(ang))


def _rope_tables(positions):
    b, s = positions.shape
    inv = ROPE_THETA ** (-jnp.arange(0, HEAD_DIM, 2, dtype=F32) / HEAD_DIM)
    inv_row = jnp.tile(inv, V7X_LANES // (HEAD_DIM // 2))[None, :]
    ts = min(SEQ_TILE, s)
    out = jax.ShapeDtypeStruct((b, s, V7X_LANES), F32)
    return pl.pallas_call(
        _rope_body,
        grid=(b, s // ts),
        in_specs=[pl.BlockSpec((1, ts, 1), lambda i, j: (i, j, 0)),
                  pl.BlockSpec((1, V7X_LANES), lambda i, j: (0, 0))],
        out_specs=[pl.BlockSpec((1, ts, V7X_LANES), lambda i, j: (i, j, 0))] * 2,
        out_shape=[out, out],
        compiler_params=_params(("parallel", "parallel")),
        name="rope_tables",
    )(positions[:, :, None], inv_row)


def _ffn_math(h, ffn_refs):
    gpre_ref, gpost_ref, wg_ref, wu_ref, wd_ref = ffn_refs
    xn = _rms(h, gpre_ref[...]).astype(MXU_DTYPE)
    gate = _mm(xn, wg_ref[...])
    up = _mm(xn, wu_ref[...])
    act = (gate * _sigmoid(gate) * up).astype(MXU_DTYPE)
    f = _mm(act, wd_ref[...])
    return h + 0.5 * _rms(f, gpost_ref[...])


def _ffn_operands(g_pre, g_post, w_gate, w_up, w_down):
    d, dff = w_gate.shape
    specs = [_const_spec((1, d)), _const_spec((1, d)),
             _const_spec((d, dff)), _const_spec((d, dff)), _const_spec((dff, d))]
    return specs, (g_pre[None, :], g_post[None, :], w_gate, w_up, w_down)


N_FFN_OPERANDS = 5


def _ffn_body(h_ref, *refs):
    refs[N_FFN_OPERANDS][0] = _ffn_math(h_ref[0], refs[:N_FFN_OPERANDS])


def _ffn(h3, ffn_weights):
    b, s, d = h3.shape
    tm = min(ROW_TILE, s)
    row = pl.BlockSpec((1, tm, d), lambda i, j: (i, j, 0))
    ffn_specs, ffn_args = _ffn_operands(*ffn_weights)
    return pl.pallas_call(
        _ffn_body,
        grid=(b, s // tm),
        in_specs=[row] + ffn_specs,
        out_specs=row,
        out_shape=jax.ShapeDtypeStruct((b, s, d), F32),
        compiler_params=_params(("parallel", "parallel")),
        name="ffn",
    )(h3, *ffn_args)


def _swap_halves(x):
    n = x.shape[1]
    lane = lax.broadcasted_iota(jnp.int32, x.shape, 1)
    first_half = (lane & (HEAD_DIM - 1)) < HEAD_DIM // 2
    return jnp.where(first_half, pltpu.roll(x, n - HEAD_DIM // 2, 1), pltpu.roll(x, HEAD_DIM // 2, 1))


def _inproj_body(cols, h_ref, g_ref, w_ref, cos_ref, sin_ref,
                 xpl_ref, qt_ref, kc_ref, vc_ref, ks_ref, vst_ref, kw_ref, vwt_ref, gatet_ref):
    xn = _rms(h_ref[0], g_ref[...]).astype(MXU_DTYPE)
    proj = _mm(xn, w_ref[...])
    cos = cos_ref[0]
    sin = sin_ref[0]

    def rope(x):
        rep = x.shape[1] // V7X_LANES
        c = jnp.concatenate([cos] * rep, axis=1) if rep > 1 else cos
        s = jnp.concatenate([sin] * rep, axis=1) if rep > 1 else sin
        return x * c + _swap_halves(x) * s

    def seg(name):
        lo, hi = cols[name]
        return proj[:, lo:hi]

    xpl_ref[0] = seg("xpl")
    qt_ref[0] = (rope(seg("q")) * Q_SCALE).T.astype(qt_ref.dtype)
    kc_ref[0] = rope(seg("kc"))
    vc_ref[0] = seg("vc")
    ks_ref[0] = rope(seg("ks")).astype(ks_ref.dtype)
    vst_ref[0] = seg("vs").T.astype(vst_ref.dtype)
    kw_ref[0] = rope(seg("kw")).astype(kw_ref.dtype)
    vwt_ref[0] = seg("vw").T.astype(vwt_ref.dtype)
    gatet_ref[0] = _sigmoid(seg("g")).T[:gatet_ref.shape[1]]


def _inproj(h3, g_pre, w_all, cols, cos, sin):
    b, s, d = h3.shape
    tm = min(ROW_TILE, s)
    ncol = w_all.shape[1]
    width = lambda n: cols[n][1] - cols[n][0]
    row = lambda w: pl.BlockSpec((1, tm, w), lambda i, j: (i, j, 0))
    col = lambda w: pl.BlockSpec((1, w, tm), lambda i, j: (i, 0, j))
    tok = lambda n, dt: (jax.ShapeDtypeStruct((b, s, width(n)), dt), row(width(n)))
    chan = lambda w, dt: (jax.ShapeDtypeStruct((b, w, s), dt), col(w))
    outs = [tok("xpl", F32), chan(width("q"), MXU_DTYPE), tok("kc", F32), tok("vc", F32),
            tok("ks", MXU_DTYPE), chan(width("vs"), MXU_DTYPE), tok("kw", MXU_DTYPE),
            chan(width("vw"), MXU_DTYPE), chan(N_BRANCH * N_HEADS, F32)]
    return pl.pallas_call(
        functools.partial(_inproj_body, cols),
        grid=(b, s // tm),
        in_specs=[row(d), _const_spec((1, d)), _const_spec((d, ncol)), row(V7X_LANES), row(V7X_LANES)],
        out_specs=[o[1] for o in outs],
        out_shape=[o[0] for o in outs],
        compiler_params=_params(("parallel", "parallel")),
        name="mixer_in_proj",
    )(h3, g_pre[None, :], w_all, cos, sin)


def _poollru_body(x_ref, pw_ref, pscale_ref, cw_ref, cb_ref, wri_ref, bri_ref, lam_ref,
                  y_ref, pool_carry, conv_carry, h_carry):
    si = pl.program_id(1)
    ts = x_ref.shape[1]
    width = pw_ref.shape[0]
    halo_p = pool_carry.shape[0]
    halo_c = conv_carry.shape[0]

    @pl.when(si == 0)
    def _():
        pool_carry[...] = jnp.zeros_like(pool_carry)
        conv_carry[...] = jnp.zeros_like(conv_carry)
        h_carry[...] = jnp.zeros_like(h_carry)

    x = x_ref[0]
    xp = x[:, :width]
    xl = x[:, width:2 * width]
    gl = x[:, 2 * width:]

    ext = jnp.concatenate([pool_carry[...], xp], axis=0)
    sums = [ext]
    shift = 1
    for _ in POOL_WINDOWS:
        sums.append(sums[-1] + pltpu.roll(sums[-1], shift, 0))
        shift *= 2
    lane = lax.broadcasted_iota(jnp.int32, (1, width), 1)
    grp = lax.shift_right_logical(lane, _log2(POOL_GROUP))
    win_sum = sums[len(POOL_WINDOWS)]
    win = jnp.full((1, width), float(POOL_WINDOWS[-1]), F32)
    for gi in range(len(POOL_WINDOWS) - 2, -1, -1):
        win_sum = jnp.where(grp == gi, sums[gi + 1], win_sum)
        win = jnp.where(grp == gi, float(POOL_WINDOWS[gi]), win)
    win_sum = win_sum[halo_p:]
    t_abs = si * ts + lax.broadcasted_iota(jnp.int32, (ts, 1), 0)
    cnt = jnp.minimum((t_abs + 1).astype(F32), win)
    pooled = win_sum / cnt
    y_pool = _mm((pooled - xp).astype(MXU_DTYPE), pw_ref[...]) * pscale_ref[...]
    pool_carry[...] = xp[ts - halo_p:]

    extc = jnp.concatenate([conv_carry[...], xl], axis=0)
    xc = extc * cw_ref[CONV_WIDTH - 1:CONV_WIDTH, :]
    for k in range(1, CONV_WIDTH):
        xc = xc + pltpu.roll(extc, k, 0) * cw_ref[CONV_WIDTH - 1 - k:CONV_WIDTH - k, :]
    xc = xc[halo_c:] + cb_ref[...]
    conv_carry[...] = xl[ts - halo_c:]

    ri = _mm(xc.astype(MXU_DTYPE), wri_ref[...]) + bri_ref[...]
    r = _sigmoid(ri[:, :width])
    i_gate = _sigmoid(ri[:, width:])
    neg_lam = -lam_ref[...]
    softplus = jnp.maximum(neg_lam, 0.0) + jnp.log1p(jnp.exp(-jnp.abs(neg_lam)))
    log_a = -LRU_C * r * softplus
    a = jnp.exp(log_a)
    b = jnp.sqrt(-jnp.tanh(log_a) * (a * a + 1.0)) * (i_gate * xc)

    row = lax.broadcasted_iota(jnp.int32, (ts, 1), 0)
    k = 1
    while k < ts:
        keep = row >= k
        a_prev = jnp.where(keep, pltpu.roll(a, k, 0), 1.0)
        b_prev = jnp.where(keep, pltpu.roll(b, k, 0), 0.0)
        b = a * b_prev + b
        a = a * a_prev
        k *= 2
    h = a * h_carry[0:1, :] + b
    h_carry[...] = jnp.broadcast_to(h[ts - 1:ts, :], h_carry.shape)

    gelu = 0.5 * gl * (1.0 + jnp.tanh(0.7978845608028654 * (gl + 0.044715 * gl * gl * gl)))
    y_ref[0] = jnp.concatenate([y_pool, h * gelu], axis=1).astype(y_ref.dtype)


def _poollru(xpl, pool_w_bd, pool_scale, conv_w, conv_b, w_ri_bd, b_ri, lam):
    b, s, w3 = xpl.shape
    width = w3 // 3
    ts = min(SEQ_TILE, s)
    return pl.pallas_call(
        _poollru_body,
        grid=(b, s // ts),
        in_specs=[pl.BlockSpec((1, ts, w3), lambda i, j: (i, j, 0)),
                  _const_spec((width, width)), _const_spec((1, width)),
                  _const_spec((CONV_WIDTH, width)), _const_spec((1, width)),
                  _const_spec((width, 2 * width)), _const_spec((1, 2 * width)), _const_spec((1, width))],
        out_specs=pl.BlockSpec((1, ts, 2 * width), lambda i, j: (i, j, 0)),
        out_shape=jax.ShapeDtypeStruct((b, s, 2 * width), MXU_DTYPE),
        scratch_shapes=[pltpu.VMEM((POOL_WINDOWS[-1], width), F32),
                        pltpu.VMEM((8, width), F32),
                        pltpu.VMEM((8, width), F32)],
        compiler_params=_params(("parallel", "arbitrary")),
        name="pool_rglru",
    )(xpl, pool_w_bd, pool_scale[None, :], conv_w, conv_b[None, :], w_ri_bd, b_ri[None, :], lam[None, :])


def _compress_body(transpose_out, r_ref, pea_ref, peb_ref, wa_ref, wb_ref, o_ref):
    r = r_ref[0]
    first = _mm((r + pea_ref[...]).astype(MXU_DTYPE), wa_ref[...])
    second = _mm((r + peb_ref[...]).astype(MXU_DTYPE), wb_ref[...])
    n1 = r.shape[0]
    out = first + pltpu.roll(second, n1 - 1, 0)
    o_ref[0] = (out.T if transpose_out else out).astype(o_ref.dtype)


def _compress(rows, pe_a, pe_b, w_a, w_b, transpose_out):
    b, n1, k = rows.shape
    n_out = w_a.shape[1]
    out_dims = (n_out, n1) if transpose_out else (n1, n_out)
    return pl.pallas_call(
        functools.partial(_compress_body, transpose_out),
        grid=(b,),
        in_specs=[pl.BlockSpec((1, n1, k), lambda i: (i, 0, 0)),
                  _const_spec((1, k)), _const_spec((1, k)), _const_spec((k, n_out)), _const_spec((k, n_out))],
        out_specs=pl.BlockSpec((1,) + out_dims, lambda i: (i, 0, 0)),
        out_shape=jax.ShapeDtypeStruct((b,) + out_dims, MXU_DTYPE),
        compiler_params=_params(("parallel",)),
        name="compress_kv",
    )(rows, pe_a, pe_b, w_a, w_b)


def _stack_heads(q_ref):
    return jnp.concatenate([q_ref[0, g * HEAD_DIM:(g + 1) * HEAD_DIM, :] for g in range(GQA_GROUP)], axis=1)


def _place_in_kv_half(qt, kvh):
    zeros = jnp.zeros_like(qt)
    parts = [jnp.where(kvh == h, qt, zeros) for h in range(N_KV_HEADS)]
    return jnp.concatenate(parts, axis=0)


def _query_positions(start, tq):
    lane = lax.broadcasted_iota(jnp.int32, (1, GQA_GROUP * tq), 1)
    return start + (lane & (tq - 1))


def _with_ones(v_t):
    return jnp.concatenate([v_t, jnp.ones((ONES_ROWS, v_t.shape[1]), v_t.dtype)], axis=0)


def _denominator(acc):
    return acc[HEAD_DIM:HEAD_DIM + 1]


def _normalize(acc):
    return acc[:HEAD_DIM] * (1.0 / _denominator(acc))


def _cmp_body(n_sel, q_ref, kc_ref, vct_ref, ovt_ref, oc_ref, mb_ref):
    tq = q_ref.shape[2]
    kvh = pl.program_id(1)
    start = pl.program_id(2) * tq
    qt = _stack_heads(q_ref)
    qp = _place_in_kv_half(qt, kvh)
    n1 = kc_ref.shape[1]
    t_row = _query_positions(start, tq)
    assert CMP_STRIDE * CMP_CHUNK >= tq + CMP_LEN

    def attend(nrows):
        s = _mm(kc_ref[0, :nrows, :], qp)
        lo = max(0, nrows - 2 * CMP_CHUNK)
        n_idx = lo + lax.broadcasted_iota(jnp.int32, (nrows - lo, 1), 0)
        tail = jnp.where(n_idx * CMP_STRIDE + (CMP_LEN - 1) <= t_row, s[lo:], NEG_INF)
        s = jnp.concatenate([s[:lo], tail], axis=0) if lo else tail
        m = jnp.max(s, axis=0, keepdims=True)
        p = jnp.exp2(s - jnp.where(m > 0.5 * NEG_INF, m, 0.0))
        den = jnp.sum(p, axis=0, keepdims=True)
        pn = p * (1.0 / jnp.where(den > 0.0, den, 1.0))
        oc = _mm(vct_ref[0, :, :nrows], pn.astype(MXU_DTYPE))
        psum = pn[:, 0:tq]
        for g in range(GQA_GROUP):
            oc_ref[0, g * HEAD_DIM:(g + 1) * HEAD_DIM, :] = oc[:, g * tq:(g + 1) * tq]
            if g:
                psum = psum + pn[:, g * tq:(g + 1) * tq]
        n_blk = min(n_sel, nrows * CMP_STRIDE // SEL_BLOCK)
        ovt = ovt_ref[:n_blk, :nrows]
        hi = psum.astype(MXU_DTYPE)
        low = (psum - hi.astype(F32)).astype(MXU_DTYPE)
        select(_mm(ovt, hi) + _mm(ovt, low), n_blk)

    def select(imp, n_blk):
        j = lax.broadcasted_iota(jnp.int32, (n_blk, 1), 0).astype(F32)
        t_q = start + lax.broadcasted_iota(jnp.int32, (1, tq), 1)
        cur = lax.shift_right_logical(t_q, _log2(SEL_BLOCK)).astype(F32)
        forced = (j == 0.0) | (j == cur) | (j == cur - 1.0)
        valid = j <= cur
        base = jnp.where(valid, jnp.where(forced, BIG_SCORE, imp), -BIG_SCORE)
        n_pad = mb_ref.shape[2]
        if n_pad > n_blk:
            mb_ref[0, 0, n_blk:, :] = jnp.full((n_pad - n_blk, tq), NEG_INF, mb_ref.dtype)

        score = jnp.where(forced, -jnp.inf, base)
        for _ in range(SEL_TOPK - 3):
            score = jnp.where(score == jnp.max(score, axis=0, keepdims=True), -jnp.inf, score)
        bias = jnp.where(score == -jnp.inf, 0.0, NEG_INF)
        mb_ref[0, 0, :n_blk, :] = bias.astype(mb_ref.dtype)
        taken = jnp.sum(jnp.where(valid, jnp.where(bias == 0.0, 1.0, 0.0), 0.0), axis=0, keepdims=True)
        tie = jnp.max(jnp.abs(taken - jnp.minimum(cur + 1.0, float(SEL_TOPK)))) > 0.0

        @pl.when(tie)
        def _():
            score = base
            bias = jnp.full(score.shape, NEG_INF, F32)
            for _ in range(SEL_TOPK):
                best = jnp.max(score, axis=0, keepdims=True)
                first = jnp.min(jnp.where(score == best, j, float(n_blk)), axis=0, keepdims=True)
                pick = j == first
                bias = jnp.where(pick, 0.0, bias)
                score = jnp.where(pick, -jnp.inf, score)
            mb_ref[0, 0, :n_blk, :] = bias.astype(mb_ref.dtype)

    assert n_sel >= SEL_TOPK and tq % SEL_BLOCK == 0
    n_need = lax.shift_right_logical(start + tq, _log2(CMP_STRIDE)) - 1
    chunks = lax.shift_right_logical(n_need + CMP_CHUNK - 1, _log2(CMP_CHUNK))
    n_variants = -(-n1 // CMP_CHUNK)
    for c in range(1, n_variants + 1):
        pl.when(chunks == c)(functools.partial(attend, min(c * CMP_CHUNK, n1)))


def _cmp_attention(qt, k_cmp, v_cmp_t, ov_t, n_sel):
    b, _, s = qt.shape
    n1 = k_cmp.shape[1]
    n_pad = ov_t.shape[0]
    tq = min(CMP_Q_TILE, s)
    group_rows = GQA_GROUP * HEAD_DIM
    head_blk = pl.BlockSpec((1, group_rows, tq), lambda i, h, j: (i, h, j))
    return pl.pallas_call(
        functools.partial(_cmp_body, n_sel),
        grid=(b, N_KV_HEADS, s // tq),
        in_specs=[head_blk,
                  pl.BlockSpec((1, n1, KV_WIDTH), lambda i, h, j: (i, 0, 0)),
                  pl.BlockSpec((1, HEAD_DIM, n1), lambda i, h, j: (i, h, 0)),
                  _const_spec((n_pad, n1))],
        out_specs=[head_blk, pl.BlockSpec((1, 1, n_pad, tq), lambda i, h, j: (i, h, 0, j))],
        out_shape=[jax.ShapeDtypeStruct((b, N_HEADS * HEAD_DIM, s), F32),
                   jax.ShapeDtypeStruct((b, N_KV_HEADS, n_pad, s), MXU_DTYPE)],
        compiler_params=_params(("parallel", "parallel", "parallel")),
        name="cmp_attention_select",
    )(qt, k_cmp, v_cmp_t, ov_t)


def _selwin_body(q_ref, mb_ref, ks_ref, oh_ref, vst_ref, kw_ref, vwt_ref, oc_ref, gate_ref, y_ref,
                 qa_ref, acc_ref, win_ref):
    tq = q_ref.shape[2]
    tk = tq
    kvh = pl.program_id(1)
    qi = pl.program_id(2)
    start = qi * tq
    rows = GQA_GROUP * tq
    qt = _stack_heads(q_ref)
    qp = _place_in_kv_half(qt, kvh)
    t_row = _query_positions(start, tq)
    group_shift = _log2(SEL_GROUP * SEL_BLOCK // tk)
    n_groups = mb_ref.shape[2] // SEL_GROUP

    for grp in range(n_groups):
        bias_rows = mb_ref[0, 0, grp * SEL_GROUP:(grp + 1) * SEL_GROUP, :]
        qa_ref[grp] = jnp.concatenate([jnp.concatenate([bias_rows] * GQA_GROUP, axis=1), qp], axis=0)

    def scores(kt, n_sub, causal):
        key_rows = pl.ds(pl.multiple_of(kt * tk, tk), n_sub * tk)
        oh_rows = pl.ds(pl.multiple_of(lax.rem(kt * tk, oh_ref.shape[0]), tk), n_sub * tk)
        k = jnp.concatenate([oh_ref[oh_rows, :], ks_ref[0, key_rows, :]], axis=1)
        s = _mm(k, qa_ref[lax.shift_right_logical(kt, group_shift)])
        if causal:
            kp = kt * tk + lax.broadcasted_iota(jnp.int32, (n_sub * tk, 1), 0)
            s = jnp.where(kp <= t_row, s, NEG_INF)
        return s

    def values(kt, n_sub):
        return vst_ref[0, :, pl.ds(pl.multiple_of(kt * tk, tk), n_sub * tk)]

    def col_max(s):
        return jnp.max(s, axis=0, keepdims=True)

    n_past = qi

    assert tq == WINDOW
    own_rows = pl.ds(pl.multiple_of(start, tq), tq)
    first_tile = qi == 0
    win_rows = pl.ds(pl.multiple_of(jnp.maximum(start - WINDOW, 0), tq), WINDOW + tq)

    def window_scores():
        s = _mm(kw_ref[0, win_rows, :], qp)
        kp = jnp.maximum(start - WINDOW, 0) + lax.broadcasted_iota(jnp.int32, (WINDOW + tq, 1), 0)
        causal_lhs, causal_rhs = -kp, -(t_row + 1)
        lhs = jnp.where(first_tile, causal_lhs[:WINDOW], kp[:WINDOW])
        rhs = jnp.where(first_tile, causal_rhs, t_row - WINDOW)
        return jnp.concatenate([jnp.where(lhs > rhs, s[:WINDOW], NEG_INF),
                                jnp.where(causal_lhs[WINDOW:] > causal_rhs, s[WINDOW:], NEG_INF)], axis=0)

    def pv(v_t, s, ref):
        return _mm(_with_ones(v_t), jnp.exp2(s - ref).astype(MXU_DTYPE))

    def self_scores(k_ref):
        k_t = k_ref[0, own_rows, :].astype(F32).T
        k_own = k_t[:HEAD_DIM]
        for h in range(1, N_KV_HEADS):
            k_own = jnp.where(kvh == h, k_t[h * HEAD_DIM:(h + 1) * HEAD_DIM], k_own)
        parts = [jnp.sum(qt[:, g * tq:(g + 1) * tq].astype(F32) * k_own, axis=0, keepdims=True)
                 for g in range(GQA_GROUP)]
        return jnp.concatenate(parts, axis=1)

    ref = self_scores(ks_ref)
    ref_w = self_scores(kw_ref)
    win_ref[...] = pv(vwt_ref[0, :, win_rows], window_scores(), ref_w)

    def stream(kt, n_sub, causal=False):
        acc_ref[...] += pv(values(kt, n_sub), scores(kt, n_sub, causal), ref)

    acc_ref[...] = jnp.zeros_like(acc_ref)
    n_big = lax.shift_right_logical(n_past, _log2(STREAM_TILES))
    lax.fori_loop(0, n_big, lambda i, c: (stream(i * STREAM_TILES, STREAM_TILES), c)[1], 0)
    done = n_big * STREAM_TILES
    part = STREAM_TILES // 2
    while part >= 1:
        pl.when((n_past & part) != 0)(functools.partial(stream, done, part))
        done = done + (n_past & part)
        part //= 2
    stream(qi, 1, causal=True)
    limit = 2.0 ** EXP_HEADROOM
    in_range = jnp.where(_denominator(acc_ref) < limit, jnp.where(_denominator(win_ref) < limit, 1.0, 0.0), 0.0)
    overflow = jnp.min(in_range) < 1.0

    @pl.when(overflow)
    def _():
        def tile(kt, causal):
            s = scores(kt, 1, causal)
            m_t = col_max(s)
            return m_t, pv(values(kt, 1), s, m_t)

        def merge(carry, part):
            m, acc = carry
            m_t, o_t = part
            m_new = jnp.maximum(m, m_t)
            return m_new, acc * jnp.exp2(m - m_new) + o_t * jnp.exp2(m_t - m_new)

        carry = (jnp.full((1, rows), M_INIT, F32), jnp.zeros(acc_ref.shape, F32))
        carry = lax.fori_loop(0, n_past, lambda i, c: merge(c, tile(i, False)), carry)
        acc_ref[...] = merge(carry, tile(qi, True))[1]

        x_win = window_scores()
        win_ref[...] = pv(vwt_ref[0, :, win_rows], x_win, col_max(x_win))

    o_sel = _normalize(acc_ref[...])
    o_win = _normalize(win_ref[...])

    gate_row = kvh * (GQA_GROUP * N_BRANCH)
    for g in range(GQA_GROUP):
        cols = slice(g * tq, (g + 1) * tq)
        head = slice(g * HEAD_DIM, (g + 1) * HEAD_DIM)
        gates = [gate_ref[0, pl.ds(gate_row + N_BRANCH * g + br, 1), :] for br in range(N_BRANCH)]
        y = gates[0] * oc_ref[0, head, :] + gates[1] * o_sel[:, cols] + gates[2] * o_win[:, cols]
        y_ref[0, head, :] = y.astype(y_ref.dtype)


def _sel_win_attention(qt, mb_t, ks, key_onehot, vs_t, kw, vw_t, oc_t, gates_t):
    b, _, s = qt.shape
    tq = min(SEL_TILE, s)
    n_pad = mb_t.shape[2]
    n_gate = gates_t.shape[1]
    group_rows = GQA_GROUP * HEAD_DIM
    head_blk = pl.BlockSpec((1, group_rows, tq), lambda i, h, j: (i, h, j))
    once = pl.Buffered(1)
    keys = pl.BlockSpec((1, s, KV_WIDTH), lambda i, h, j: (i, 0, 0), pipeline_mode=once)
    vals = pl.BlockSpec((1, HEAD_DIM, s), lambda i, h, j: (i, h, 0), pipeline_mode=once)
    acc = pltpu.VMEM((HEAD_DIM + ONES_ROWS, GQA_GROUP * tq), F32)
    return pl.pallas_call(
        _selwin_body,
        grid=(b, N_KV_HEADS, s // tq),
        in_specs=[head_blk,
                  pl.BlockSpec((1, 1, n_pad, tq), lambda i, h, j: (i, h, 0, j)),
                  keys, _const_spec(key_onehot.shape), vals, keys, vals,
                  head_blk,
                  pl.BlockSpec((1, n_gate, tq), lambda i, h, j: (i, 0, j))],
        out_specs=head_blk,
        out_shape=jax.ShapeDtypeStruct((b, N_HEADS * HEAD_DIM, s), MXU_DTYPE),
        scratch_shapes=[pltpu.VMEM((n_pad // SEL_GROUP, SEL_GROUP + KV_WIDTH, GQA_GROUP * tq), MXU_DTYPE),
                        acc, acc],
        compiler_params=_params(("parallel", "parallel", "arbitrary")),
        name="sel_win_attention",
    )(qt, mb_t, ks, key_onehot, vs_t, kw, vw_t, oc_t, gates_t)


def _outproj_ffn_body(h_ref, ypl_ref, yat_ref, w1_ref, w2_ref, g_ref, *refs):
    ffn_refs, o_ref = refs[:N_FFN_OPERANDS], refs[N_FFN_OPERANDS]
    m = _mm(ypl_ref[0], w1_ref[...]) + _mm_tn(yat_ref[0], w2_ref[...])
    o_ref[0] = _ffn_math(h_ref[0] + _rms(m, g_ref[...]), ffn_refs)


def _outproj_ffn(h3, ypl, yat_t, w1, w2, g_post, ffn_weights):
    b, s, d = h3.shape
    tm = min(ROW_TILE, s)
    row = lambda w: pl.BlockSpec((1, tm, w), lambda i, j: (i, j, 0))
    ffn_specs, ffn_args = _ffn_operands(*ffn_weights)
    return pl.pallas_call(
        _outproj_ffn_body,
        grid=(b, s // tm),
        in_specs=[row(d), row(ypl.shape[2]),
                  pl.BlockSpec((1, yat_t.shape[1], tm), lambda i, j: (i, 0, j)),
                  _const_spec(w1.shape), _const_spec(w2.shape), _const_spec((1, d))] + ffn_specs,
        out_specs=row(d),
        out_shape=jax.ShapeDtypeStruct((b, s, d), F32),
        compiler_params=_params(("parallel", "parallel")),
        name="mixer_out_proj_ffn",
    )(h3, ypl, yat_t, w1, w2, g_post[None, :], *ffn_args)


def _block_diag(blocks):
    n, a, b = blocks.shape
    eye = jnp.eye(n, dtype=blocks.dtype)
    return jnp.einsum("nab,nm->namb", blocks, eye).reshape(n * a, n * b)


def _compress_weights(w, pe):
    eye = jnp.eye(N_KV_HEADS, dtype=w.dtype)
    halves = []
    for part in range(CMP_LEN // CMP_STRIDE):
        wl = w[part * CMP_STRIDE:(part + 1) * CMP_STRIDE]
        pel = pe[part * CMP_STRIDE:(part + 1) * CMP_STRIDE]
        wm = jnp.einsum("lde,hg->lhdge", wl, eye).reshape(CMP_STRIDE * KV_WIDTH, KV_WIDTH)
        pm = jnp.tile(pel[:, None, :], (1, N_KV_HEADS, 1)).reshape(1, -1)
        halves.append((wm.astype(MXU_DTYPE), pm))
    return halves


def kernel(x, positions, ffn1_pre_g, ffn1_post_g, ffn1_w_gate, ffn1_w_up, ffn1_w_down, mix_pre_g, mix_post_g,
           w_in, w_out, pool_w, pool_scale, conv_w, conv_b, lru_w_r, lru_b_r, lru_w_i, lru_b_i, lru_lambda,
           cmp_w_k, cmp_w_v, cmp_pe, ffn2_pre_g, ffn2_post_g, ffn2_w_gate, ffn2_w_up, ffn2_w_down):
    b, s, d = x.shape
    depth = w_in.shape[0]
    t = b * s
    pool_width = pool_w.shape[1] * pool_w.shape[2]
    lru_width = lru_w_r.shape[1] * lru_w_r.shape[2]
    attn_width = N_HEADS * HEAD_DIM
    assert pool_width == lru_width and s % SEL_TILE == 0 and s % (CMP_STRIDE * 8) == 0
    assert WINDOW % SEL_TILE == 0 and (SEL_GROUP * SEL_BLOCK) % (STREAM_TILES * SEL_TILE) == 0
    assert s >= WINDOW + SEL_TILE

    sizes = [("xpl", pool_width + 2 * lru_width), ("q", attn_width), ("kc", KV_WIDTH), ("vc", KV_WIDTH),
             ("ks", KV_WIDTH), ("vs", KV_WIDTH), ("kw", KV_WIDTH), ("vw", KV_WIDTH), ("g", V7X_LANES)]
    cols, off = {}, 0
    for name, width in sizes:
        cols[name] = (off, off + width)
        off += width

    n_sel = s // SEL_BLOCK
    n_pad = -(-n_sel // SEL_GROUP) * SEL_GROUP
    n1 = s // CMP_STRIDE
    n_cmp = (s - CMP_LEN) // CMP_STRIDE + 1
    cmp_start = jnp.arange(n1) * CMP_STRIDE
    sel_start = jnp.arange(n_pad) * SEL_BLOCK
    overlap_t = ((cmp_start[None, :] < sel_start[:, None] + SEL_BLOCK) &
                 (cmp_start[None, :] + CMP_LEN > sel_start[:, None]) &
                 (jnp.arange(n1)[None, :] < n_cmp) & (jnp.arange(n_pad)[:, None] < n_sel)).astype(MXU_DTYPE)
    key_blk = jnp.arange(min(s, SEL_GROUP * SEL_BLOCK)) // SEL_BLOCK
    key_onehot = (key_blk[:, None] == jnp.arange(SEL_GROUP)[None, :]).astype(MXU_DTYPE)

    cos, sin = _rope_tables(positions)

    h = x
    for l in range(depth):
        cast = lambda w: w[l].astype(MXU_DTYPE)
        ffn1 = (ffn1_pre_g[l], ffn1_post_g[l], cast(ffn1_w_gate), cast(ffn1_w_up), cast(ffn1_w_down))
        ffn2 = (ffn2_pre_g[l], ffn2_post_g[l], cast(ffn2_w_gate), cast(ffn2_w_up), cast(ffn2_w_down))

        w_all = jnp.pad(w_in[l], ((0, 0), (0, off - w_in.shape[2]))).astype(MXU_DTYPE)
        h = _ffn(h, ffn1)
        xpl, qt, kc, vc, ks, vs_t, kw, vw_t, gates_t = _inproj(h, mix_pre_g[l], w_all, cols, cos, sin)

        w_ri = jnp.concatenate([_block_diag(lru_w_r[l]), _block_diag(lru_w_i[l])], axis=1).astype(MXU_DTYPE)
        ypl = _poollru(xpl, _block_diag(pool_w[l]).astype(MXU_DTYPE), pool_scale[l],
                       conv_w[l], conv_b[l], w_ri, jnp.concatenate([lru_b_r[l], lru_b_i[l]]), lru_lambda[l])

        (wk_a, pe_a), (wk_b, pe_b) = _compress_weights(cmp_w_k[l], cmp_pe[l])
        (wv_a, _), (wv_b, _) = _compress_weights(cmp_w_v[l], cmp_pe[l])
        k_cmp = _compress(kc.reshape(b, n1, -1), pe_a, pe_b, wk_a, wk_b, False)
        v_cmp_t = _compress(vc.reshape(b, n1, -1), pe_a, pe_b, wv_a, wv_b, True)

        oc_t, mb_t = _cmp_attention(qt, k_cmp, v_cmp_t, overlap_t, n_sel)
        y_t = _sel_win_attention(qt, mb_t, ks, key_onehot, vs_t, kw, vw_t, oc_t, gates_t)

        w_o = w_out[l].astype(MXU_DTYPE)
        split = pool_width + lru_width
        h = _outproj_ffn(h, ypl, y_t, w_o[:split], w_o[split:], mix_post_g[l], ffn2)
    return h
```

```python
import functools

import jax
import jax.numpy as jnp
from jax import lax
from jax.experimental import pallas as pl
from jax.experimental.pallas import tpu as pltpu

F32 = jnp.float32
MXU_DTYPE = jnp.bfloat16

POOL_WINDOWS = (2, 4, 8, 16)
POOL_GROUP = 64
LRU_HEAD_DIM = 64
LRU_C = 8.0
CONV_WIDTH = 4
HEAD_DIM = 64
N_KV_HEADS = 2
GQA_GROUP = 4
N_HEADS = N_KV_HEADS * GQA_GROUP
KV_WIDTH = N_KV_HEADS * HEAD_DIM
N_BRANCH = 3
CMP_LEN = 32
CMP_STRIDE = 16
SEL_BLOCK = 64
SEL_TOPK = 16
WINDOW = 512
ROPE_THETA = 10000.0
NORM_EPS = 1e-6
NEG_INF = -1e30
BIG_SCORE = 1e9

V7X_LANES = 128
V7X_VMEM_LIMIT_BYTES = 56 * 1024 * 1024

ROW_TILE = 512
SEQ_TILE = 512
CMP_Q_TILE = 512
CMP_CHUNK = 256
SEL_TILE = 512
SEL_GROUP = 128
ONES_ROWS = 16
M_INIT = -1e20
STREAM_TILES = 4
EXP_HEADROOM = 64.0
Q_SCALE = HEAD_DIM ** -0.5 * 1.4426950408889634


def _params(semantics):
    return pltpu.CompilerParams(dimension_semantics=semantics, vmem_limit_bytes=V7X_VMEM_LIMIT_BYTES)


def _rms(x, g):
    return x * lax.rsqrt(jnp.mean(x * x, axis=-1, keepdims=True) + NORM_EPS) * g


def _sigmoid(x):
    return 1.0 / (1.0 + jnp.exp(-x))


def _mm(a, b):
    return jnp.dot(a, b, preferred_element_type=F32)


def _mm_tn(a_t, b):
    return lax.dot_general(a_t, b, (((0,), (0,)), ((), ())), preferred_element_type=F32)


def _log2(n):
    assert n & (n - 1) == 0
    return n.bit_length() - 1


def _const_spec(shape):
    zeros = (0,) * len(shape)
    return pl.BlockSpec(shape, lambda *_: zeros, pipeline_mode=pl.Buffered(1))


def _rope_body(pos_ref, inv_ref, cos_ref, sin_ref):
    ang = pos_ref[0].astype(F32) * inv_ref[...]
    lane = lax.broadcasted_iota(jnp.int32, ang.shape, 1)
    cos_ref[0] = jnp.cos(ang)
    sin_ref[0] = jnp.where((lane & (HEAD_DIM - 1)) < HEAD_DIM // 2, -jnp.sin(ang), jnp.sin(ang))


def _rope_tables(positions):
    b, s = positions.shape
    inv = ROPE_THETA ** (-jnp.arange(0, HEAD_DIM, 2, dtype=F32) / HEAD_DIM)
    inv_row = jnp.tile(inv, V7X_LANES // (HEAD_DIM // 2))[None, :]
    ts = min(SEQ_TILE, s)
    out = jax.ShapeDtypeStruct((b, s, V7X_LANES), F32)
    return pl.pallas_call(
        _rope_body,
        grid=(b, s // ts),
        in_specs=[pl.BlockSpec((1, ts, 1), lambda i, j: (i, j, 0)),
                  pl.BlockSpec((1, V7X_LANES), lambda i, j: (0, 0))],
        out_specs=[pl.BlockSpec((1, ts, V7X_LANES), lambda i, j: (i, j, 0))] * 2,
        out_shape=[out, out],
        compiler_params=_params(("parallel", "parallel")),
        name="rope_tables",
    )(positions[:, :, None], inv_row)


def _ffn_math(h, ffn_refs):
    gpre_ref, gpost_ref, wg_ref, wu_ref, wd_ref = ffn_refs
    xn = _rms(h, gpre_ref[...]).astype(MXU_DTYPE)
    gate = _mm(xn, wg_ref[...])
    up = _mm(xn, wu_ref[...])
    act = (gate * _sigmoid(gate) * up).astype(MXU_DTYPE)
    f = _mm(act, wd_ref[...])
    return h + 0.5 * _rms(f, gpost_ref[...])


def _ffn_operands(g_pre, g_post, w_gate, w_up, w_down):
    d, dff = w_gate.shape
    specs = [_const_spec((1, d)), _const_spec((1, d)),
             _const_spec((d, dff)), _const_spec((d, dff)), _const_spec((dff, d))]
    return specs, (g_pre[None, :], g_post[None, :], w_gate, w_up, w_down)


N_FFN_OPERANDS = 5


def _ffn_body(h_ref, *refs):
    refs[N_FFN_OPERANDS][0] = _ffn_math(h_ref[0], refs[:N_FFN_OPERANDS])


def _ffn(h3, ffn_weights):
    b, s, d = h3.shape
    tm = min(ROW_TILE, s)
    row = pl.BlockSpec((1, tm, d), lambda i, j: (i, j, 0))
    ffn_specs, ffn_args = _ffn_operands(*ffn_weights)
    return pl.pallas_call(
        _ffn_body,
        grid=(b, s // tm),
        in_specs=[row] + ffn_specs,
        out_specs=row,
        out_shape=jax.ShapeDtypeStruct((b, s, d), F32),
        compiler_params=_params(("parallel", "parallel")),
        name="ffn",
    )(h3, *ffn_args)


def _swap_halves(x):
    n = x.shape[1]
    lane = lax.broadcasted_iota(jnp.int32, x.shape, 1)
    first_half = (lane & (HEAD_DIM - 1)) < HEAD_DIM // 2
    return jnp.where(first_half, pltpu.roll(x, n - HEAD_DIM // 2, 1), pltpu.roll(x, HEAD_DIM // 2, 1))


def _inproj_body(cols, h_ref, g_ref, w_ref, cos_ref, sin_ref,
                 xpl_ref, qt_ref, kc_ref, vc_ref, ks_ref, vst_ref, kw_ref, vwt_ref, gatet_ref):
    xn = _rms(h_ref[0], g_ref[...]).astype(MXU_DTYPE)
    proj = _mm(xn, w_ref[...])
    cos = cos_ref[0]
    sin = sin_ref[0]

    def rope(x):
        rep = x.shape[1] // V7X_LANES
        c = jnp.concatenate([cos] * rep, axis=1) if rep > 1 else cos
        s = jnp.concatenate([sin] * rep, axis=1) if rep > 1 else sin
        return x * c + _swap_halves(x) * s

    def seg(name):
        lo, hi = cols[name]
        return proj[:, lo:hi]

    xpl_ref[0] = seg("xpl")
    qt_ref[0] = (rope(seg("q")) * Q_SCALE).T.astype(qt_ref.dtype)
    kc_ref[0] = rope(seg("kc"))
    vc_ref[0] = seg("vc")
    ks_ref[0] = rope(seg("ks")).astype(ks_ref.dtype)
    vst_ref[0] = seg("vs").T.astype(vst_ref.dtype)
    kw_ref[0] = rope(seg("kw")).astype(kw_ref.dtype)
    vwt_ref[0] = seg("vw").T.astype(vwt_ref.dtype)
    gatet_ref[0] = _sigmoid(seg("g")).T[:gatet_ref.shape[1]]


def _inproj(h3, g_pre, w_all, cols, cos, sin):
    b, s, d = h3.shape
    tm = min(ROW_TILE, s)
    ncol = w_all.shape[1]
    width = lambda n: cols[n][1] - cols[n][0]
    row = lambda w: pl.BlockSpec((1, tm, w), lambda i, j: (i, j, 0))
    col = lambda w: pl.BlockSpec((1, w, tm), lambda i, j: (i, 0, j))
    tok = lambda n, dt: (jax.ShapeDtypeStruct((b, s, width(n)), dt), row(width(n)))
    chan = lambda w, dt: (jax.ShapeDtypeStruct((b, w, s), dt), col(w))
    outs = [tok("xpl", F32), chan(width("q"), MXU_DTYPE), tok("kc", F32), tok("vc", F32),
            tok("ks", MXU_DTYPE), chan(width("vs"), MXU_DTYPE), tok("kw", MXU_DTYPE),
            chan(width("vw"), MXU_DTYPE), chan(N_BRANCH * N_HEADS, F32)]
    return pl.pallas_call(
        functools.partial(_inproj_body, cols),
        grid=(b, s // tm),
        in_specs=[row(d), _const_spec((1, d)), _const_spec((d, ncol)), row(V7X_LANES), row(V7X_LANES)],
        out_specs=[o[1] for o in outs],
        out_shape=[o[0] for o in outs],
        compiler_params=_params(("parallel", "parallel")),
        name="mixer_in_proj",
    )(h3, g_pre[None, :], w_all, cos, sin)


def _poollru_body(x_ref, pw_ref, pscale_ref, cw_ref, cb_ref, wri_ref, bri_ref, lam_ref,
                  y_ref, pool_carry, conv_carry, h_carry):
    si = pl.program_id(1)
    ts = x_ref.shape[1]
    width = pw_ref.shape[0]
    halo_p = pool_carry.shape[0]
    halo_c = conv_carry.shape[0]

    @pl.when(si == 0)
    def _():
        pool_carry[...] = jnp.zeros_like(pool_carry)
        conv_carry[...] = jnp.zeros_like(conv_carry)
        h_carry[...] = jnp.zeros_like(h_carry)

    x = x_ref[0]
    xp = x[:, :width]
    xl = x[:, width:2 * width]
    gl = x[:, 2 * width:]

    ext = jnp.concatenate([pool_carry[...], xp], axis=0)
    sums = [ext]
    shift = 1
    for _ in POOL_WINDOWS:
        sums.append(sums[-1] + pltpu.roll(sums[-1], shift, 0))
        shift *= 2
    lane = lax.broadcasted_iota(jnp.int32, (1, width), 1)
    grp = lax.shift_right_logical(lane, _log2(POOL_GROUP))
    win_sum = sums[len(POOL_WINDOWS)]
    win = jnp.full((1, width), float(POOL_WINDOWS[-1]), F32)
    for gi in range(len(POOL_WINDOWS) - 2, -1, -1):
        win_sum = jnp.where(grp == gi, sums[gi + 1], win_sum)
        win = jnp.where(grp == gi, float(POOL_WINDOWS[gi]), win)
    win_sum = win_sum[halo_p:]
    t_abs = si * ts + lax.broadcasted_iota(jnp.int32, (ts, 1), 0)
    cnt = jnp.minimum((t_abs + 1).astype(F32), win)
    pooled = win_sum / cnt
    y_pool = _mm((pooled - xp).astype(MXU_DTYPE), pw_ref[...]) * pscale_ref[...]
    pool_carry[...] = xp[ts - halo_p:]

    extc = jnp.concatenate([conv_carry[...], xl], axis=0)
    xc = extc * cw_ref[CONV_WIDTH - 1:CONV_WIDTH, :]
    for k in range(1, CONV_WIDTH):
        xc = xc + pltpu.roll(extc, k, 0) * cw_ref[CONV_WIDTH - 1 - k:CONV_WIDTH - k, :]
    xc = xc[halo_c:] + cb_ref[...]
    conv_carry[...] = xl[ts - halo_c:]

    ri = _mm(xc.astype(MXU_DTYPE), wri_ref[...]) + bri_ref[...]
    r = _sigmoid(ri[:, :width])
    i_gate = _sigmoid(ri[:, width:])
    neg_lam = -lam_ref[...]
    softplus = jnp.maximum(neg_lam, 0.0) + jnp.log1p(jnp.exp(-jnp.abs(neg_lam)))
    log_a = -LRU_C * r * softplus
    a = jnp.exp(log_a)
    b = jnp.sqrt(-jnp.tanh(log_a) * (a * a + 1.0)) * (i_gate * xc)

    row = lax.broadcasted_iota(jnp.int32, (ts, 1), 0)
    k = 1
    while k < ts:
        keep = row >= k
        a_prev = jnp.where(keep, pltpu.roll(a, k, 0), 1.0)
        b_prev = jnp.where(keep, pltpu.roll(b, k, 0), 0.0)
        b = a * b_prev + b
        a = a * a_prev
        k *= 2
    h = a * h_carry[0:1, :] + b
    h_carry[...] = jnp.broadcast_to(h[ts - 1:ts, :], h_carry.shape)

    gelu = 0.5 * gl * (1.0 + jnp.tanh(0.7978845608028654 * (gl + 0.044715 * gl * gl * gl)))
    y_ref[0] = jnp.concatenate([y_pool, h * gelu], axis=1).astype(y_ref.dtype)


def _poollru(xpl, pool_w_bd, pool_scale, conv_w, conv_b, w_ri_bd, b_ri, lam):
    b, s, w3 = xpl.shape
    width = w3 // 3
    ts = min(SEQ_TILE, s)
    return pl.pallas_call(
        _poollru_body,
        grid=(b, s // ts),
        in_specs=[pl.BlockSpec((1, ts, w3), lambda i, j: (i, j, 0)),
                  _const_spec((width, width)), _const_spec((1, width)),
                  _const_spec((CONV_WIDTH, width)), _const_spec((1, width)),
                  _const_spec((width, 2 * width)), _const_spec((1, 2 * width)), _const_spec((1, width))],
        out_specs=pl.BlockSpec((1, ts, 2 * width), lambda i, j: (i, j, 0)),
        out_shape=jax.ShapeDtypeStruct((b, s, 2 * width), MXU_DTYPE),
        scratch_shapes=[pltpu.VMEM((POOL_WINDOWS[-1], width), F32),
                        pltpu.VMEM((8, width), F32),
                        pltpu.VMEM((8, width), F32)],
        compiler_params=_params(("parallel", "arbitrary")),
        name="pool_rglru",
    )(xpl, pool_w_bd, pool_scale[None, :], conv_w, conv_b[None, :], w_ri_bd, b_ri[None, :], lam[None, :])


def _compress_body(transpose_out, r_ref, pea_ref, peb_ref, wa_ref, wb_ref, o_ref):
    r = r_ref[0]
    first = _mm((r + pea_ref[...]).astype(MXU_DTYPE), wa_ref[...])
    second = _mm((r + peb_ref[...]).astype(MXU_DTYPE), wb_ref[...])
    n1 = r.shape[0]
    out = first + pltpu.roll(second, n1 - 1, 0)
    o_ref[0] = (out.T if transpose_out else out).astype(o_ref.dtype)


def _compress(rows, pe_a, pe_b, w_a, w_b, transpose_out):
    b, n1, k = rows.shape
    n_out = w_a.shape[1]
    out_dims = (n_out, n1) if transpose_out else (n1, n_out)
    return pl.pallas_call(
        functools.partial(_compress_body, transpose_out),
        grid=(b,),
        in_specs=[pl.BlockSpec((1, n1, k), lambda i: (i, 0, 0)),
                  _const_spec((1, k)), _const_spec((1, k)), _const_spec((k, n_out)), _const_spec((k, n_out))],
        out_specs=pl.BlockSpec((1,) + out_dims, lambda i: (i, 0, 0)),
        out_shape=jax.ShapeDtypeStruct((b,) + out_dims, MXU_DTYPE),
        compiler_params=_params(("parallel",)),
        name="compress_kv",
    )(rows, pe_a, pe_b, w_a, w_b)


def _stack_heads(q_ref):
    return jnp.concatenate([q_ref[0, g * HEAD_DIM:(g + 1) * HEAD_DIM, :] for g in range(GQA_GROUP)], axis=1)


def _place_in_kv_half(qt, kvh):
    zeros = jnp.zeros_like(qt)
    parts = [jnp.where(kvh == h, qt, zeros) for h in range(N_KV_HEADS)]
    return jnp.concatenate(parts, axis=0)


def _query_positions(start, tq):
    lane = lax.broadcasted_iota(jnp.int32, (1, GQA_GROUP * tq), 1)
    return start + (lane & (tq - 1))


def _with_ones(v_t):
    return jnp.concatenate([v_t, jnp.ones((ONES_ROWS, v_t.shape[1]), v_t.dtype)], axis=0)


def _denominator(acc):
    return acc[HEAD_DIM:HEAD_DIM + 1]


def _normalize(acc):
    return acc[:HEAD_DIM] * (1.0 / _denominator(acc))


def _cmp_body(n_sel, q_ref, kc_ref, vct_ref, oc_ref, mb_ref, band_ref):
    tq = q_ref.shape[2]
    kvh = pl.program_id(1)
    start = pl.program_id(2) * tq
    qt = _stack_heads(q_ref)
    qp = _place_in_kv_half(qt, kvh)
    n1 = kc_ref.shape[1]
    t_row = _query_positions(start, tq)
    assert CMP_STRIDE * CMP_CHUNK >= tq + CMP_LEN

    def attend(nrows):
        s = _mm(kc_ref[0, :nrows, :], qp)
        lo = max(0, nrows - 2 * CMP_CHUNK)
        n_idx = lo + lax.broadcasted_iota(jnp.int32, (nrows - lo, 1), 0)
        tail = jnp.where(n_idx * CMP_STRIDE + (CMP_LEN - 1) <= t_row, s[lo:], NEG_INF)
        s = jnp.concatenate([s[:lo], tail], axis=0) if lo else tail
        m = jnp.max(s, axis=0, keepdims=True)
        p = jnp.exp2(s - jnp.where(m > 0.5 * NEG_INF, m, 0.0))
        den = jnp.sum(p, axis=0, keepdims=True)
        pn = p * (1.0 / jnp.where(den > 0.0, den, 1.0))
        oc = _mm(vct_ref[0, :, :nrows], pn.astype(MXU_DTYPE))
        psum = pn[:, 0:tq]
        for g in range(GQA_GROUP):
            oc_ref[0, g * HEAD_DIM:(g + 1) * HEAD_DIM, :] = oc[:, g * tq:(g + 1) * tq]
            if g:
                psum = psum + pn[:, g * tq:(g + 1) * tq]
        n_blk = min(n_sel, nrows * CMP_STRIDE // SEL_BLOCK)
        step, lead, pad = SEL_BLOCK // CMP_STRIDE, CMP_LEN // CMP_STRIDE - 1, 8
        cols = []
        for c in range(tq // V7X_LANES):
            band_ref[c, 0:pad, :] = jnp.zeros((pad, V7X_LANES), F32)
            band_ref[c, pad:pad + nrows, :] = psum[:, c * V7X_LANES:(c + 1) * V7X_LANES]
            taps = [band_ref[c, pl.ds(pad - lead + k, n_blk, stride=step), :] for k in range(step + lead)]
            cols.append(functools.reduce(lambda a, b: a + b, taps))
        select(jnp.concatenate(cols, axis=1), n_blk)

    def select(imp, n_blk):
        j = lax.broadcasted_iota(jnp.int32, (n_blk, 1), 0).astype(F32)
        t_q = start + lax.broadcasted_iota(jnp.int32, (1, tq), 1)
        cur = lax.shift_right_logical(t_q, _log2(SEL_BLOCK)).astype(F32)
        forced = (j == 0.0) | (j == cur) | (j == cur - 1.0)
        valid = j <= cur
        base = jnp.where(valid, jnp.where(forced, BIG_SCORE, imp), -BIG_SCORE)
        n_pad = mb_ref.shape[2]
        if n_pad > n_blk:
            mb_ref[0, 0, n_blk:, :] = jnp.full((n_pad - n_blk, tq), NEG_INF, mb_ref.dtype)

        score = jnp.where(forced, -jnp.inf, base)
        for _ in range(SEL_TOPK - 3):
            score = jnp.where(score == jnp.max(score, axis=0, keepdims=True), -jnp.inf, score)
        bias = jnp.where(score == -jnp.inf, 0.0, NEG_INF)
        mb_ref[0, 0, :n_blk, :] = bias.astype(mb_ref.dtype)
        taken = jnp.sum(jnp.where(valid, jnp.where(bias == 0.0, 1.0, 0.0), 0.0), axis=0, keepdims=True)
        tie = jnp.max(jnp.abs(taken - jnp.minimum(cur + 1.0, float(SEL_TOPK)))) > 0.0

        @pl.when(tie)
        def _():
            score = base
            bias = jnp.full(score.shape, NEG_INF, F32)
            for _ in range(SEL_TOPK):
                best = jnp.max(score, axis=0, keepdims=True)
                first = jnp.min(jnp.where(score == best, j, float(n_blk)), axis=0, keepdims=True)
                pick = j == first
                bias = jnp.where(pick, 0.0, bias)
                score = jnp.where(pick, -jnp.inf, score)
            mb_ref[0, 0, :n_blk, :] = bias.astype(mb_ref.dtype)

    assert n_sel >= SEL_TOPK and tq % SEL_BLOCK == 0
    n_need = lax.shift_right_logical(start + tq, _log2(CMP_STRIDE)) - 1
    chunks = lax.shift_right_logical(n_need + CMP_CHUNK - 1, _log2(CMP_CHUNK))
    n_variants = -(-n1 // CMP_CHUNK)
    for c in range(1, n_variants + 1):
        pl.when(chunks == c)(functools.partial(attend, min(c * CMP_CHUNK, n1)))


def _cmp_attention(qt, k_cmp, v_cmp_t, n_sel, n_pad):
    b, _, s = qt.shape
    n1 = k_cmp.shape[1]
    tq = min(CMP_Q_TILE, s)
    group_rows = GQA_GROUP * HEAD_DIM
    head_blk = pl.BlockSpec((1, group_rows, tq), lambda i, h, j: (i, h, j))
    return pl.pallas_call(
        functools.partial(_cmp_body, n_sel),
        grid=(b, N_KV_HEADS, s // tq),
        in_specs=[head_blk,
                  pl.BlockSpec((1, n1, KV_WIDTH), lambda i, h, j: (i, 0, 0)),
                  pl.BlockSpec((1, HEAD_DIM, n1), lambda i, h, j: (i, h, 0))],
        out_specs=[head_blk, pl.BlockSpec((1, 1, n_pad, tq), lambda i, h, j: (i, h, 0, j))],
        out_shape=[jax.ShapeDtypeStruct((b, N_HEADS * HEAD_DIM, s), F32),
                   jax.ShapeDtypeStruct((b, N_KV_HEADS, n_pad, s), MXU_DTYPE)],
        scratch_shapes=[pltpu.VMEM((tq // V7X_LANES, n1 + 8, V7X_LANES), F32)],
        compiler_params=_params(("parallel", "parallel", "parallel")),
        name="cmp_attention_select",
    )(qt, k_cmp, v_cmp_t)


def _selwin_body(q_ref, mb_ref, ks_ref, oh_ref, vst_ref, kw_ref, vwt_ref, oc_ref, gate_ref, y_ref,
                 qa_ref, acc_ref, win_ref):
    tq = q_ref.shape[2]
    tk = tq
    kvh = pl.program_id(1)
    qi = pl.program_id(2)
    start = qi * tq
    rows = GQA_GROUP * tq
    qt = _stack_heads(q_ref)
    qp = _place_in_kv_half(qt, kvh)
    t_row = _query_positions(start, tq)
    group_shift = _log2(SEL_GROUP * SEL_BLOCK // tk)
    n_groups = mb_ref.shape[2] // SEL_GROUP

    for grp in range(n_groups):
        bias_rows = mb_ref[0, 0, grp * SEL_GROUP:(grp + 1) * SEL_GROUP, :]
        qa_ref[grp] = jnp.concatenate([jnp.concatenate([bias_rows] * GQA_GROUP, axis=1), qp], axis=0)

    def scores(kt, n_sub, causal):
        key_rows = pl.ds(pl.multiple_of(kt * tk, tk), n_sub * tk)
        oh_rows = pl.ds(pl.multiple_of(lax.rem(kt * tk, oh_ref.shape[0]), tk), n_sub * tk)
        k = jnp.concatenate([oh_ref[oh_rows, :], ks_ref[0, key_rows, :]], axis=1)
        s = _mm(k, qa_ref[lax.shift_right_logical(kt, group_shift)])
        if causal:
            kp = kt * tk + lax.broadcasted_iota(jnp.int32, (n_sub * tk, 1), 0)
            s = jnp.where(kp <= t_row, s, NEG_INF)
        return s

    def values(kt, n_sub):
        return vst_ref[0, :, pl.ds(pl.multiple_of(kt * tk, tk), n_sub * tk)]

    def col_max(s):
        return jnp.max(s, axis=0, keepdims=True)

    n_past = qi

    assert tq == WINDOW
    own_rows = pl.ds(pl.multiple_of(start, tq), tq)
    first_tile = qi == 0
    win_rows = pl.ds(pl.multiple_of(jnp.maximum(start - WINDOW, 0), tq), WINDOW + tq)

    def window_scores():
        s = _mm(kw_ref[0, win_rows, :], qp)
        kp = jnp.maximum(start - WINDOW, 0) + lax.broadcasted_iota(jnp.int32, (WINDOW + tq, 1), 0)
        causal_lhs, causal_rhs = -kp, -(t_row + 1)
        lhs = jnp.where(first_tile, causal_lhs[:WINDOW], kp[:WINDOW])
        rhs = jnp.where(first_tile, causal_rhs, t_row - WINDOW)
        return jnp.concatenate([jnp.where(lhs > rhs, s[:WINDOW], NEG_INF),
                                jnp.where(causal_lhs[WINDOW:] > causal_rhs, s[WINDOW:], NEG_INF)], axis=0)

    def pv(v_t, s, ref):
        return _mm(_with_ones(v_t), jnp.exp2(s - ref).astype(MXU_DTYPE))

    def self_scores(k_ref):
        k_t = k_ref[0, own_rows, :].astype(F32).T
        k_own = k_t[:HEAD_DIM]
        for h in range(1, N_KV_HEADS):
            k_own = jnp.where(kvh == h, k_t[h * HEAD_DIM:(h + 1) * HEAD_DIM], k_own)
        parts = [jnp.sum(qt[:, g * tq:(g + 1) * tq].astype(F32) * k_own, axis=0, keepdims=True)
                 for g in range(GQA_GROUP)]
        return jnp.concatenate(parts, axis=1)

    ref = self_scores(ks_ref)
    ref_w = self_scores(kw_ref)
    win_ref[...] = pv(vwt_ref[0, :, win_rows], window_scores(), ref_w)

    def stream(kt, n_sub, causal=False):
        acc_ref[...] += pv(values(kt, n_sub), scores(kt, n_sub, causal), ref)

    acc_ref[...] = jnp.zeros_like(acc_ref)
    n_big = lax.shift_right_logical(n_past, _log2(STREAM_TILES))
    lax.fori_loop(0, n_big, lambda i, c: (stream(i * STREAM_TILES, STREAM_TILES), c)[1], 0)
    done = n_big * STREAM_TILES
    part = STREAM_TILES // 2
    while part >= 1:
        pl.when((n_past & part) != 0)(functools.partial(stream, done, part))
        done = done + (n_past & part)
        part //= 2
    stream(qi, 1, causal=True)
    limit = 2.0 ** EXP_HEADROOM
    in_range = jnp.where(_denominator(acc_ref) < limit, jnp.where(_denominator(win_ref) < limit, 1.0, 0.0), 0.0)
    overflow = jnp.min(in_range) < 1.0

    @pl.when(overflow)
    def _():
        def tile(kt, causal):
            s = scores(kt, 1, causal)
            m_t = col_max(s)
            return m_t, pv(values(kt, 1), s, m_t)

        def merge(carry, part):
            m, acc = carry
            m_t, o_t = part
            m_new = jnp.maximum(m, m_t)
            return m_new, acc * jnp.exp2(m - m_new) + o_t * jnp.exp2(m_t - m_new)

        carry = (jnp.full((1, rows), M_INIT, F32), jnp.zeros(acc_ref.shape, F32))
        carry = lax.fori_loop(0, n_past, lambda i, c: merge(c, tile(i, False)), carry)
        acc_ref[...] = merge(carry, tile(qi, True))[1]

        x_win = window_scores()
        win_ref[...] = pv(vwt_ref[0, :, win_rows], x_win, col_max(x_win))

    o_sel = _normalize(acc_ref[...])
    o_win = _normalize(win_ref[...])

    gate_row = kvh * (GQA_GROUP * N_BRANCH)
    for g in range(GQA_GROUP):
        cols = slice(g * tq, (g + 1) * tq)
        head = slice(g * HEAD_DIM, (g + 1) * HEAD_DIM)
        gates = [gate_ref[0, pl.ds(gate_row + N_BRANCH * g + br, 1), :] for br in range(N_BRANCH)]
        y = gates[0] * oc_ref[0, head, :] + gates[1] * o_sel[:, cols] + gates[2] * o_win[:, cols]
        y_ref[0, head, :] = y.astype(y_ref.dtype)


def _sel_win_attention(qt, mb_t, ks, key_onehot, vs_t, kw, vw_t, oc_t, gates_t):
    b, _, s = qt.shape
    tq = min(SEL_TILE, s)
    n_pad = mb_t.shape[2]
    n_gate = gates_t.shape[1]
    group_rows = GQA_GROUP * HEAD_DIM
    head_blk = pl.BlockSpec((1, group_rows, tq), lambda i, h, j: (i, h, j))
    once = pl.Buffered(1)
    keys = pl.BlockSpec((1, s, KV_WIDTH), lambda i, h, j: (i, 0, 0), pipeline_mode=once)
    vals = pl.BlockSpec((1, HEAD_DIM, s), lambda i, h, j: (i, h, 0), pipeline_mode=once)
    acc = pltpu.VMEM((HEAD_DIM + ONES_ROWS, GQA_GROUP * tq), F32)
    return pl.pallas_call(
        _selwin_body,
        grid=(b, N_KV_HEADS, s // tq),
        in_specs=[head_blk,
                  pl.BlockSpec((1, 1, n_pad, tq), lambda i, h, j: (i, h, 0, j)),
                  keys, _const_spec(key_onehot.shape), vals, keys, vals,
                  head_blk,
                  pl.BlockSpec((1, n_gate, tq), lambda i, h, j: (i, 0, j))],
        out_specs=head_blk,
        out_shape=jax.ShapeDtypeStruct((b, N_HEADS * HEAD_DIM, s), MXU_DTYPE),
        scratch_shapes=[pltpu.VMEM((n_pad // SEL_GROUP, SEL_GROUP + KV_WIDTH, GQA_GROUP * tq), MXU_DTYPE),
                        acc, acc],
        compiler_params=_params(("parallel", "parallel", "arbitrary")),
        name="sel_win_attention",
    )(qt, mb_t, ks, key_onehot, vs_t, kw, vw_t, oc_t, gates_t)


def _outproj_ffn_body(h_ref, ypl_ref, yat_ref, w1_ref, w2_ref, g_ref, *refs):
    ffn_refs, o_ref = refs[:N_FFN_OPERANDS], refs[N_FFN_OPERANDS]
    m = _mm(ypl_ref[0], w1_ref[...]) + _mm_tn(yat_ref[0], w2_ref[...])
    o_ref[0] = _ffn_math(h_ref[0] + _rms(m, g_ref[...]), ffn_refs)


def _outproj_ffn(h3, ypl, yat_t, w1, w2, g_post, ffn_weights):
    b, s, d = h3.shape
    tm = min(ROW_TILE, s)
    row = lambda w: pl.BlockSpec((1, tm, w), lambda i, j: (i, j, 0))
    ffn_specs, ffn_args = _ffn_operands(*ffn_weights)
    return pl.pallas_call(
        _outproj_ffn_body,
        grid=(b, s // tm),
        in_specs=[row(d), row(ypl.shape[2]),
                  pl.BlockSpec((1, yat_t.shape[1], tm), lambda i, j: (i, 0, j)),
                  _const_spec(w1.shape), _const_spec(w2.shape), _const_spec((1, d))] + ffn_specs,
        out_specs=row(d),
        out_shape=jax.ShapeDtypeStruct((b, s, d), F32),
        compiler_params=_params(("parallel", "parallel")),
        name="mixer_out_proj_ffn",
    )(h3, ypl, yat_t, w1, w2, g_post[None, :], *ffn_args)


def _block_diag(blocks):
    n, a, b = blocks.shape
    eye = jnp.eye(n, dtype=blocks.dtype)
    return jnp.einsum("nab,nm->namb", blocks, eye).reshape(n * a, n * b)


def _compress_weights(w, pe):
    eye = jnp.eye(N_KV_HEADS, dtype=w.dtype)
    halves = []
    for part in range(CMP_LEN // CMP_STRIDE):
        wl = w[part * CMP_STRIDE:(part + 1) * CMP_STRIDE]
        pel = pe[part * CMP_STRIDE:(part + 1) * CMP_STRIDE]
        wm = jnp.einsum("lde,hg->lhdge", wl, eye).reshape(CMP_STRIDE * KV_WIDTH, KV_WIDTH)
        pm = jnp.tile(pel[:, None, :], (1, N_KV_HEADS, 1)).reshape(1, -1)
        halves.append((wm.astype(MXU_DTYPE), pm))
    return halves


def kernel(x, positions, ffn1_pre_g, ffn1_post_g, ffn1_w_gate, ffn1_w_up, ffn1_w_down, mix_pre_g, mix_post_g,
           w_in, w_out, pool_w, pool_scale, conv_w, conv_b, lru_w_r, lru_b_r, lru_w_i, lru_b_i, lru_lambda,
           cmp_w_k, cmp_w_v, cmp_pe, ffn2_pre_g, ffn2_post_g, ffn2_w_gate, ffn2_w_up, ffn2_w_down):
    b, s, d = x.shape
    depth = w_in.shape[0]
    t = b * s
    pool_width = pool_w.shape[1] * pool_w.shape[2]
    lru_width = lru_w_r.shape[1] * lru_w_r.shape[2]
    attn_width = N_HEADS * HEAD_DIM
    assert pool_width == lru_width and s % SEL_TILE == 0 and s % (CMP_STRIDE * 8) == 0
    assert WINDOW % SEL_TILE == 0 and (SEL_GROUP * SEL_BLOCK) % (STREAM_TILES * SEL_TILE) == 0
    assert s >= WINDOW + SEL_TILE

    sizes = [("xpl", pool_width + 2 * lru_width), ("q", attn_width), ("kc", KV_WIDTH), ("vc", KV_WIDTH),
             ("ks", KV_WIDTH), ("vs", KV_WIDTH), ("kw", KV_WIDTH), ("vw", KV_WIDTH), ("g", V7X_LANES)]
    cols, off = {}, 0
    for name, width in sizes:
        cols[name] = (off, off + width)
        off += width

    n_sel = s // SEL_BLOCK
    n_pad = -(-n_sel // SEL_GROUP) * SEL_GROUP
    n1 = s // CMP_STRIDE
    key_blk = jnp.arange(min(s, SEL_GROUP * SEL_BLOCK)) // SEL_BLOCK
    key_onehot = (key_blk[:, None] == jnp.arange(SEL_GROUP)[None, :]).astype(MXU_DTYPE)

    cos, sin = _rope_tables(positions)

    h = x
    for l in range(depth):
        cast = lambda w: w[l].astype(MXU_DTYPE)
        ffn1 = (ffn1_pre_g[l], ffn1_post_g[l], cast(ffn1_w_gate), cast(ffn1_w_up), cast(ffn1_w_down))
        ffn2 = (ffn2_pre_g[l], ffn2_post_g[l], cast(ffn2_w_gate), cast(ffn2_w_up), cast(ffn2_w_down))

        w_all = jnp.pad(w_in[l], ((0, 0), (0, off - w_in.shape[2]))).astype(MXU_DTYPE)
        h = _ffn(h, ffn1)
        xpl, qt, kc, vc, ks, vs_t, kw, vw_t, gates_t = _inproj(h, mix_pre_g[l], w_all, cols, cos, sin)

        w_ri = jnp.concatenate([_block_diag(lru_w_r[l]), _block_diag(lru_w_i[l])], axis=1).astype(MXU_DTYPE)
        ypl = _poollru(xpl, _block_diag(pool_w[l]).astype(MXU_DTYPE), pool_scale[l],
                       conv_w[l], conv_b[l], w_ri, jnp.concatenate([lru_b_r[l], lru_b_i[l]]), lru_lambda[l])

        (wk_a, pe_a), (wk_b, pe_b) = _compress_weights(cmp_w_k[l], cmp_pe[l])
        (wv_a, _), (wv_b, _) = _compress_weights(cmp_w_v[l], cmp_pe[l])
        k_cmp = _compress(kc.reshape(b, n1, -1), pe_a, pe_b, wk_a, wk_b, False)
        v_cmp_t = _compress(vc.reshape(b, n1, -1), pe_a, pe_b, wv_a, wv_b, True)

        oc_t, mb_t = _cmp_attention(qt, k_cmp, v_cmp_t, n_sel, n_pad)
        y_t = _sel_win_attention(qt, mb_t, ks, key_onehot, vs_t, kw, vw_t, oc_t, gates_t)

        w_o = w_out[l].astype(MXU_DTYPE)
        split = pool_width + lru_width
        h = _outproj_ffn(h, ypl, y_t, w_o[:split], w_o[split:], mix_post_g[l], ffn2)
    return h
```

```python
import functools

import jax
import jax.numpy as jnp
from jax import lax
from jax.experimental import pallas as pl
from jax.experimental.pallas import tpu as pltpu

F32 = jnp.float32
MXU_DTYPE = jnp.bfloat16

POOL_WINDOWS = (2, 4, 8, 16)
POOL_GROUP = 64
LRU_C = 8.0
CONV_WIDTH = 4
HEAD_DIM = 64
N_KV_HEADS = 2
GQA_GROUP = 4
N_HEADS = N_KV_HEADS * GQA_GROUP
KV_WIDTH = N_KV_HEADS * HEAD_DIM
N_BRANCH = 3
CMP_LEN = 32
CMP_STRIDE = 16
SEL_BLOCK = 64
SEL_TOPK = 16
WINDOW = 512
ROPE_THETA = 10000.0
NORM_EPS = 1e-6
NEG_INF = -1e30
BIG_SCORE = 1e9
GELU_C0 = 0.7978845608028654
GELU_C1 = 0.044715
LOG2_E = 1.4426950408889634

V7X_LANES = 128
V7X_SUBLANES = 8
V7X_VMEM_LIMIT_BYTES = 56 * 1024 * 1024

ROW_TILE = 512
SEQ_TILE = 512
CMP_Q_TILE = 512
CMP_CHUNK = 256
SEL_TILE = 512
SEL_GROUP = 128
ONES_ROWS = 16
M_INIT = -1e20
STREAM_TILES = 4
EXP_HEADROOM = 64.0
Q_SCALE = HEAD_DIM ** -0.5 * LOG2_E


def _params(semantics):
    return pltpu.CompilerParams(dimension_semantics=semantics, vmem_limit_bytes=V7X_VMEM_LIMIT_BYTES)


def _rms(x, g):
    return x * lax.rsqrt(jnp.mean(x * x, axis=-1, keepdims=True) + NORM_EPS) * g


def _sigmoid(x):
    return 1.0 / (1.0 + jnp.exp(-x))


def _mm(a, b):
    return jnp.dot(a, b, preferred_element_type=F32)


def _mm_tn(a_t, b):
    return lax.dot_general(a_t, b, (((0,), (0,)), ((), ())), preferred_element_type=F32)


def _log2(n):
    assert n & (n - 1) == 0
    return n.bit_length() - 1


def _const_spec(shape):
    zeros = (0,) * len(shape)
    return pl.BlockSpec(shape, lambda *_: zeros, pipeline_mode=pl.Buffered(1))


def _rope_body(pos_ref, inv_ref, cos_ref, sin_ref):
    ang = pos_ref[0].astype(F32) * inv_ref[...]
    lane = lax.broadcasted_iota(jnp.int32, ang.shape, 1)
    cos_ref[0] = jnp.cos(ang)
    sin_ref[0] = jnp.where((lane & (HEAD_DIM - 1)) < HEAD_DIM // 2, -jnp.sin(ang), jnp.sin(ang))


def _rope_tables(positions):
    b, s = positions.shape
    inv = ROPE_THETA ** (-jnp.arange(0, HEAD_DIM, 2, dtype=F32) / HEAD_DIM)
    inv_row = jnp.tile(inv, V7X_LANES // (HEAD_DIM // 2))[None, :]
    ts = min(SEQ_TILE, s)
    out = jax.ShapeDtypeStruct((b, s, V7X_LANES), F32)
    return pl.pallas_call(
        _rope_body,
        grid=(b, s // ts),
        in_specs=[pl.BlockSpec((1, ts, 1), lambda i, j: (i, j, 0)),
                  pl.BlockSpec((1, V7X_LANES), lambda i, j: (0, 0))],
        out_specs=[pl.BlockSpec((1, ts, V7X_LANES), lambda i, j: (i, j, 0))] * 2,
        out_shape=[out, out],
        compiler_params=_params(("parallel", "parallel")),
        name="rope_tables",
    )(positions[:, :, None], inv_row)


def _ffn_math(h, ffn_refs):
    gpre_ref, gpost_ref, wg_ref, wu_ref, wd_ref = ffn_refs
    xn = _rms(h, gpre_ref[...]).astype(MXU_DTYPE)
    gate = _mm(xn, wg_ref[...])
    up = _mm(xn, wu_ref[...])
    act = (gate * _sigmoid(gate) * up).astype(MXU_DTYPE)
    f = _mm(act, wd_ref[...])
    return h + 0.5 * _rms(f, gpost_ref[...])


def _ffn_operands(g_pre, g_post, w_gate, w_up, w_down):
    d, dff = w_gate.shape
    specs = [_const_spec((1, d)), _const_spec((1, d)),
             _const_spec((d, dff)), _const_spec((d, dff)), _const_spec((dff, d))]
    return specs, (g_pre[None, :], g_post[None, :], w_gate, w_up, w_down)


N_FFN_OPERANDS = 5


def _ffn_body(h_ref, *refs):
    refs[N_FFN_OPERANDS][0] = _ffn_math(h_ref[0], refs[:N_FFN_OPERANDS])


def _ffn(h3, ffn_weights):
    b, s, d = h3.shape
    tm = min(ROW_TILE, s)
    row = pl.BlockSpec((1, tm, d), lambda i, j: (i, j, 0))
    ffn_specs, ffn_args = _ffn_operands(*ffn_weights)
    return pl.pallas_call(
        _ffn_body,
        grid=(b, s // tm),
        in_specs=[row] + ffn_specs,
        out_specs=row,
        out_shape=jax.ShapeDtypeStruct((b, s, d), F32),
        compiler_params=_params(("parallel", "parallel")),
        name="ffn",
    )(h3, *ffn_args)


def _swap_halves(x):
    n = x.shape[1]
    lane = lax.broadcasted_iota(jnp.int32, x.shape, 1)
    first_half = (lane & (HEAD_DIM - 1)) < HEAD_DIM // 2
    return jnp.where(first_half, pltpu.roll(x, n - HEAD_DIM // 2, 1), pltpu.roll(x, HEAD_DIM // 2, 1))


def _inproj_body(cols, h_ref, g_ref, w_ref, cos_ref, sin_ref,
                 xpl_ref, qt_ref, kc_ref, vc_ref, ks_ref, vst_ref, kw_ref, vwt_ref, gatet_ref):
    xn = _rms(h_ref[0], g_ref[...]).astype(MXU_DTYPE)
    proj = _mm(xn, w_ref[...])
    cos = cos_ref[0]
    sin = sin_ref[0]

    def rope(x):
        rep = x.shape[1] // V7X_LANES
        c = jnp.concatenate([cos] * rep, axis=1) if rep > 1 else cos
        s = jnp.concatenate([sin] * rep, axis=1) if rep > 1 else sin
        return x * c + _swap_halves(x) * s

    def seg(name):
        lo, hi = cols[name]
        return proj[:, lo:hi]

    xpl_ref[0] = seg("xpl")
    qt_ref[0] = (rope(seg("q")) * Q_SCALE).T.astype(qt_ref.dtype)
    kc_ref[0] = rope(seg("kc"))
    vc_ref[0] = seg("vc")
    ks_ref[0] = rope(seg("ks")).astype(ks_ref.dtype)
    vst_ref[0] = seg("vs").T.astype(vst_ref.dtype)
    kw_ref[0] = rope(seg("kw")).astype(kw_ref.dtype)
    vwt_ref[0] = seg("vw").T.astype(vwt_ref.dtype)
    gatet_ref[0] = _sigmoid(seg("g")).T[:gatet_ref.shape[1]]


def _inproj(h3, g_pre, w_all, cols, cos, sin):
    b, s, d = h3.shape
    tm = min(ROW_TILE, s)
    ncol = w_all.shape[1]
    width = lambda n: cols[n][1] - cols[n][0]
    row = lambda w: pl.BlockSpec((1, tm, w), lambda i, j: (i, j, 0))
    col = lambda w: pl.BlockSpec((1, w, tm), lambda i, j: (i, 0, j))
    tok = lambda n, dt: (jax.ShapeDtypeStruct((b, s, width(n)), dt), row(width(n)))
    chan = lambda w, dt: (jax.ShapeDtypeStruct((b, w, s), dt), col(w))
    outs = [tok("xpl", F32), chan(width("q"), MXU_DTYPE), tok("kc", F32), tok("vc", F32),
            tok("ks", MXU_DTYPE), chan(width("vs"), MXU_DTYPE), tok("kw", MXU_DTYPE),
            chan(width("vw"), MXU_DTYPE), chan(N_BRANCH * N_HEADS, F32)]
    return pl.pallas_call(
        functools.partial(_inproj_body, cols),
        grid=(b, s // tm),
        in_specs=[row(d), _const_spec((1, d)), _const_spec((d, ncol)), row(V7X_LANES), row(V7X_LANES)],
        out_specs=[o[1] for o in outs],
        out_shape=[o[0] for o in outs],
        compiler_params=_params(("parallel", "parallel")),
        name="mixer_in_proj",
    )(h3, g_pre[None, :], w_all, cos, sin)


def _poollru_body(x_ref, pw_ref, pscale_ref, cw_ref, cb_ref, wri_ref, bri_ref, lam_ref,
                  y_ref, pool_carry, conv_carry, h_carry):
    si = pl.program_id(1)
    ts = x_ref.shape[1]
    width = pw_ref.shape[0]
    halo_p = pool_carry.shape[0]
    halo_c = conv_carry.shape[0]

    @pl.when(si == 0)
    def _():
        pool_carry[...] = jnp.zeros_like(pool_carry)
        conv_carry[...] = jnp.zeros_like(conv_carry)
        h_carry[...] = jnp.zeros_like(h_carry)

    x = x_ref[0]
    xp = x[:, :width]
    xl = x[:, width:2 * width]
    gl = x[:, 2 * width:]

    ext = jnp.concatenate([pool_carry[...], xp], axis=0)
    sums = [ext]
    shift = 1
    for _ in POOL_WINDOWS:
        sums.append(sums[-1] + pltpu.roll(sums[-1], shift, 0))
        shift *= 2
    lane = lax.broadcasted_iota(jnp.int32, (1, width), 1)
    grp = lax.shift_right_logical(lane, _log2(POOL_GROUP))
    win_sum = sums[len(POOL_WINDOWS)]
    win = jnp.full((1, width), float(POOL_WINDOWS[-1]), F32)
    for gi in range(len(POOL_WINDOWS) - 2, -1, -1):
        win_sum = jnp.where(grp == gi, sums[gi + 1], win_sum)
        win = jnp.where(grp == gi, float(POOL_WINDOWS[gi]), win)
    win_sum = win_sum[halo_p:]
    t_abs = si * ts + lax.broadcasted_iota(jnp.int32, (ts, 1), 0)
    cnt = jnp.minimum((t_abs + 1).astype(F32), win)
    pooled = win_sum / cnt
    y_pool = _mm((pooled - xp).astype(MXU_DTYPE), pw_ref[...]) * pscale_ref[...]
    pool_carry[...] = xp[ts - halo_p:]

    extc = jnp.concatenate([conv_carry[...], xl], axis=0)
    xc = extc * cw_ref[CONV_WIDTH - 1:CONV_WIDTH, :]
    for k in range(1, CONV_WIDTH):
        xc = xc + pltpu.roll(extc, k, 0) * cw_ref[CONV_WIDTH - 1 - k:CONV_WIDTH - k, :]
    xc = xc[halo_c:] + cb_ref[...]
    conv_carry[...] = xl[ts - halo_c:]

    ri = _mm(xc.astype(MXU_DTYPE), wri_ref[...]) + bri_ref[...]
    r = _sigmoid(ri[:, :width])
    i_gate = _sigmoid(ri[:, width:])
    neg_lam = -lam_ref[...]
    softplus = jnp.maximum(neg_lam, 0.0) + jnp.log1p(jnp.exp(-jnp.abs(neg_lam)))
    log_a = -LRU_C * r * softplus
    a = jnp.exp(log_a)
    one_minus_a2 = -jnp.tanh(log_a) * (a * a + 1.0)
    mult = jnp.where(one_minus_a2 > 0.0, one_minus_a2 * lax.rsqrt(one_minus_a2), 0.0)
    b = mult * (i_gate * xc)

    row_in_group = lax.broadcasted_iota(jnp.int32, (ts, 1), 0) & (V7X_SUBLANES - 1)
    k = 1
    while k < V7X_SUBLANES:
        keep = row_in_group >= k
        a_prev = jnp.where(keep, pltpu.roll(a, k, 0), 1.0)
        b_prev = jnp.where(keep, pltpu.roll(b, k, 0), 0.0)
        b = a * b_prev + b
        a = a * a_prev
        k *= 2
    h_prev = h_carry[0:1, :]
    groups = []
    for g in range(ts // V7X_SUBLANES):
        rows_g = slice(g * V7X_SUBLANES, (g + 1) * V7X_SUBLANES)
        h_g = a[rows_g] * h_prev + b[rows_g]
        groups.append(h_g)
        h_prev = h_g[V7X_SUBLANES - 1:V7X_SUBLANES]
    h = jnp.concatenate(groups, axis=0)
    h_carry[...] = jnp.broadcast_to(h_prev, h_carry.shape)

    gelu = 0.5 * gl * (1.0 + jnp.tanh(GELU_C0 * (gl + GELU_C1 * gl * gl * gl)))
    y_ref[0] = jnp.concatenate([y_pool, h * gelu], axis=1).astype(y_ref.dtype)


def _poollru(xpl, pool_w_bd, pool_scale, conv_w, conv_b, w_ri_bd, b_ri, lam):
    b, s, w3 = xpl.shape
    width = w3 // 3
    ts = min(SEQ_TILE, s)
    return pl.pallas_call(
        _poollru_body,
        grid=(b, s // ts),
        in_specs=[pl.BlockSpec((1, ts, w3), lambda i, j: (i, j, 0)),
                  _const_spec((width, width)), _const_spec((1, width)),
                  _const_spec((CONV_WIDTH, width)), _const_spec((1, width)),
                  _const_spec((width, 2 * width)), _const_spec((1, 2 * width)), _const_spec((1, width))],
        out_specs=pl.BlockSpec((1, ts, 2 * width), lambda i, j: (i, j, 0)),
        out_shape=jax.ShapeDtypeStruct((b, s, 2 * width), MXU_DTYPE),
        scratch_shapes=[pltpu.VMEM((POOL_WINDOWS[-1], width), F32),
                        pltpu.VMEM((V7X_SUBLANES, width), F32),
                        pltpu.VMEM((V7X_SUBLANES, width), F32)],
        compiler_params=_params(("parallel", "arbitrary")),
        name="pool_rglru",
    )(xpl, pool_w_bd, pool_scale[None, :], conv_w, conv_b[None, :], w_ri_bd, b_ri[None, :], lam[None, :])


def _compress_body(transpose_out, r_ref, pea_ref, peb_ref, wa_ref, wb_ref, o_ref):
    r = r_ref[0]
    first = _mm((r + pea_ref[...]).astype(MXU_DTYPE), wa_ref[...])
    second = _mm((r + peb_ref[...]).astype(MXU_DTYPE), wb_ref[...])
    n1 = r.shape[0]
    out = first + pltpu.roll(second, n1 - 1, 0)
    o_ref[0] = (out.T if transpose_out else out).astype(o_ref.dtype)


def _compress(rows, pe_a, pe_b, w_a, w_b, transpose_out):
    b, n1, k = rows.shape
    n_out = w_a.shape[1]
    out_dims = (n_out, n1) if transpose_out else (n1, n_out)
    return pl.pallas_call(
        functools.partial(_compress_body, transpose_out),
        grid=(b,),
        in_specs=[pl.BlockSpec((1, n1, k), lambda i: (i, 0, 0)),
                  _const_spec((1, k)), _const_spec((1, k)), _const_spec((k, n_out)), _const_spec((k, n_out))],
        out_specs=pl.BlockSpec((1,) + out_dims, lambda i: (i, 0, 0)),
        out_shape=jax.ShapeDtypeStruct((b,) + out_dims, MXU_DTYPE),
        compiler_params=_params(("parallel",)),
        name="compress_kv",
    )(rows, pe_a, pe_b, w_a, w_b)


def _stack_heads(q_ref):
    return jnp.concatenate([q_ref[0, g * HEAD_DIM:(g + 1) * HEAD_DIM, :] for g in range(GQA_GROUP)], axis=1)


def _place_in_kv_half(qt, kvh):
    zeros = jnp.zeros_like(qt)
    parts = [jnp.where(kvh == h, qt, zeros) for h in range(N_KV_HEADS)]
    return jnp.concatenate(parts, axis=0)


def _query_positions(start, tq):
    lane = lax.broadcasted_iota(jnp.int32, (1, GQA_GROUP * tq), 1)
    return start + (lane & (tq - 1))


def _with_ones(v_t):
    return jnp.concatenate([v_t, jnp.ones((ONES_ROWS, v_t.shape[1]), v_t.dtype)], axis=0)


def _denominator(acc):
    return acc[HEAD_DIM:HEAD_DIM + 1]


def _normalize(acc):
    return acc[:HEAD_DIM] * (1.0 / _denominator(acc))


def _cmp_body(n_sel, q_ref, kc_ref, vct_ref, oc_ref, mb_ref, band_ref):
    tq = q_ref.shape[2]
    kvh = pl.program_id(1)
    start = pl.program_id(2) * tq
    qt = _stack_heads(q_ref)
    qp = _place_in_kv_half(qt, kvh)
    n1 = kc_ref.shape[1]
    t_row = _query_positions(start, tq)
    assert CMP_STRIDE * CMP_CHUNK >= tq + CMP_LEN

    def attend(nrows):
        s = _mm(kc_ref[0, :nrows, :], qp)
        lo = max(0, nrows - 2 * CMP_CHUNK)
        n_idx = lo + lax.broadcasted_iota(jnp.int32, (nrows - lo, 1), 0)
        tail = jnp.where(n_idx * CMP_STRIDE + (CMP_LEN - 1) <= t_row, s[lo:], NEG_INF)
        s = jnp.concatenate([s[:lo], tail], axis=0) if lo else tail
        m = jnp.max(s, axis=0, keepdims=True)
        p = jnp.exp2(s - jnp.where(m > 0.5 * NEG_INF, m, 0.0))
        den = jnp.sum(p, axis=0, keepdims=True)
        pn = p * (1.0 / jnp.where(den > 0.0, den, 1.0))
        oc = _mm(vct_ref[0, :, :nrows], pn.astype(MXU_DTYPE))
        psum = pn[:, 0:tq]
        for g in range(GQA_GROUP):
            oc_ref[0, g * HEAD_DIM:(g + 1) * HEAD_DIM, :] = oc[:, g * tq:(g + 1) * tq]
            if g:
                psum = psum + pn[:, g * tq:(g + 1) * tq]
        n_blk = min(n_sel, nrows * CMP_STRIDE // SEL_BLOCK)
        step, lead, pad = SEL_BLOCK // CMP_STRIDE, CMP_LEN // CMP_STRIDE - 1, V7X_SUBLANES
        cols = []
        for c in range(tq // V7X_LANES):
            band_ref[c, 0:pad, :] = jnp.zeros((pad, V7X_LANES), F32)
            band_ref[c, pad:pad + nrows, :] = psum[:, c * V7X_LANES:(c + 1) * V7X_LANES]
            taps = [band_ref[c, pl.ds(pad - lead + k, n_blk, stride=step), :] for k in range(step + lead)]
            cols.append(functools.reduce(lambda a, b: a + b, taps))
        select(jnp.concatenate(cols, axis=1), n_blk)

    def select(imp, n_blk):
        j = lax.broadcasted_iota(jnp.int32, (n_blk, 1), 0).astype(F32)
        t_q = start + lax.broadcasted_iota(jnp.int32, (1, tq), 1)
        cur = lax.shift_right_logical(t_q, _log2(SEL_BLOCK)).astype(F32)
        forced = (j == 0.0) | (j == cur) | (j == cur - 1.0)
        valid = j <= cur
        base = jnp.where(valid, jnp.where(forced, BIG_SCORE, imp), -BIG_SCORE)
        n_pad = mb_ref.shape[2]
        if n_pad > n_blk:
            mb_ref[0, 0, n_blk:, :] = jnp.full((n_pad - n_blk, tq), NEG_INF, mb_ref.dtype)

        score = jnp.where(forced, -jnp.inf, base)
        for _ in range(SEL_TOPK - 3):
            score = jnp.where(score == jnp.max(score, axis=0, keepdims=True), -jnp.inf, score)
        bias = jnp.where(score == -jnp.inf, 0.0, NEG_INF)
        mb_ref[0, 0, :n_blk, :] = bias.astype(mb_ref.dtype)
        taken = jnp.sum(jnp.where(valid, jnp.where(bias == 0.0, 1.0, 0.0), 0.0), axis=0, keepdims=True)
        tie = jnp.max(jnp.abs(taken - jnp.minimum(cur + 1.0, float(SEL_TOPK)))) > 0.0

        @pl.when(tie)
        def _():
            score = base
            bias = jnp.full(score.shape, NEG_INF, F32)
            for _ in range(SEL_TOPK):
                best = jnp.max(score, axis=0, keepdims=True)
                first = jnp.min(jnp.where(score == best, j, float(n_blk)), axis=0, keepdims=True)
                pick = j == first
                bias = jnp.where(pick, 0.0, bias)
                score = jnp.where(pick, -jnp.inf, score)
            mb_ref[0, 0, :n_blk, :] = bias.astype(mb_ref.dtype)

    assert n_sel >= SEL_TOPK and tq % SEL_BLOCK == 0
    n_need = lax.shift_right_logical(start + tq, _log2(CMP_STRIDE)) - 1
    chunks = lax.shift_right_logical(n_need + CMP_CHUNK - 1, _log2(CMP_CHUNK))
    n_variants = -(-n1 // CMP_CHUNK)
    for c in range(1, n_variants + 1):
        pl.when(chunks == c)(functools.partial(attend, min(c * CMP_CHUNK, n1)))


def _cmp_attention(qt, k_cmp, v_cmp_t, n_sel, n_pad):
    b, _, s = qt.shape
    n1 = k_cmp.shape[1]
    tq = min(CMP_Q_TILE, s)
    group_rows = GQA_GROUP * HEAD_DIM
    head_blk = pl.BlockSpec((1, group_rows, tq), lambda i, h, j: (i, h, j))
    return pl.pallas_call(
        functools.partial(_cmp_body, n_sel),
        grid=(b, N_KV_HEADS, s // tq),
        in_specs=[head_blk,
                  pl.BlockSpec((1, n1, KV_WIDTH), lambda i, h, j: (i, 0, 0)),
                  pl.BlockSpec((1, HEAD_DIM, n1), lambda i, h, j: (i, h, 0))],
        out_specs=[head_blk, pl.BlockSpec((1, 1, n_pad, tq), lambda i, h, j: (i, h, 0, j))],
        out_shape=[jax.ShapeDtypeStruct((b, N_HEADS * HEAD_DIM, s), F32),
                   jax.ShapeDtypeStruct((b, N_KV_HEADS, n_pad, s), MXU_DTYPE)],
        scratch_shapes=[pltpu.VMEM((tq // V7X_LANES, n1 + V7X_SUBLANES, V7X_LANES), F32)],
        compiler_params=_params(("parallel", "parallel", "parallel")),
        name="cmp_attention_select",
    )(qt, k_cmp, v_cmp_t)


def _selwin_body(q_ref, mb_ref, ks_ref, oh_ref, vst_ref, kw_ref, vwt_ref, oc_ref, gate_ref, y_ref,
                 qa_ref, acc_ref, win_ref):
    tq = q_ref.shape[2]
    tk = tq
    kvh = pl.program_id(1)
    qi = pl.program_id(2)
    start = qi * tq
    rows = GQA_GROUP * tq
    qt = _stack_heads(q_ref)
    qp = _place_in_kv_half(qt, kvh)
    t_row = _query_positions(start, tq)
    group_shift = _log2(SEL_GROUP * SEL_BLOCK // tk)
    n_groups = mb_ref.shape[2] // SEL_GROUP

    for grp in range(n_groups):
        bias_rows = mb_ref[0, 0, grp * SEL_GROUP:(grp + 1) * SEL_GROUP, :]
        qa_ref[grp] = jnp.concatenate([jnp.concatenate([bias_rows] * GQA_GROUP, axis=1), qp], axis=0)

    def scores(kt, n_sub, causal):
        key_rows = pl.ds(pl.multiple_of(kt * tk, tk), n_sub * tk)
        oh_rows = pl.ds(pl.multiple_of(lax.rem(kt * tk, oh_ref.shape[0]), tk), n_sub * tk)
        k = jnp.concatenate([oh_ref[oh_rows, :], ks_ref[0, key_rows, :]], axis=1)
        s = _mm(k, qa_ref[lax.shift_right_logical(kt, group_shift)])
        if causal:
            kp = kt * tk + lax.broadcasted_iota(jnp.int32, (n_sub * tk, 1), 0)
            s = jnp.where(kp <= t_row, s, NEG_INF)
        return s

    def values(kt, n_sub):
        return vst_ref[0, :, pl.ds(pl.multiple_of(kt * tk, tk), n_sub * tk)]

    def col_max(s):
        return jnp.max(s, axis=0, keepdims=True)

    n_past = qi

    assert tq == WINDOW
    own_rows = pl.ds(pl.multiple_of(start, tq), tq)
    first_tile = qi == 0
    win_rows = pl.ds(pl.multiple_of(jnp.maximum(start - WINDOW, 0), tq), WINDOW + tq)

    def window_scores():
        s = _mm(kw_ref[0, win_rows, :], qp)
        kp = jnp.maximum(start - WINDOW, 0) + lax.broadcasted_iota(jnp.int32, (WINDOW + tq, 1), 0)
        causal_lhs, causal_rhs = -kp, -(t_row + 1)
        lhs = jnp.where(first_tile, causal_lhs[:WINDOW], kp[:WINDOW])
        rhs = jnp.where(first_tile, causal_rhs, t_row - WINDOW)
        return jnp.concatenate([jnp.where(lhs > rhs, s[:WINDOW], NEG_INF),
                                jnp.where(causal_lhs[WINDOW:] > causal_rhs, s[WINDOW:], NEG_INF)], axis=0)

    def pv(v_t, s, ref):
        return _mm(_with_ones(v_t), jnp.exp2(s - ref).astype(MXU_DTYPE))

    def self_scores(k_ref):
        k_t = k_ref[0, own_rows, :].astype(F32).T
        k_own = k_t[:HEAD_DIM]
        for h in range(1, N_KV_HEADS):
            k_own = jnp.where(kvh == h, k_t[h * HEAD_DIM:(h + 1) * HEAD_DIM], k_own)
        parts = [jnp.sum(qt[:, g * tq:(g + 1) * tq].astype(F32) * k_own, axis=0, keepdims=True)
                 for g in range(GQA_GROUP)]
        return jnp.concatenate(parts, axis=1)

    ref = self_scores(ks_ref)
    ref_w = self_scores(kw_ref)
    win_ref[...] = pv(vwt_ref[0, :, win_rows], window_scores(), ref_w)

    def stream(kt, n_sub, causal=False):
        acc_ref[...] += pv(values(kt, n_sub), scores(kt, n_sub, causal), ref)

    acc_ref[...] = jnp.zeros_like(acc_ref)
    n_big = lax.shift_right_logical(n_past, _log2(STREAM_TILES))
    lax.fori_loop(0, n_big, lambda i, c: (stream(i * STREAM_TILES, STREAM_TILES), c)[1], 0)
    done = n_big * STREAM_TILES
    part = STREAM_TILES // 2
    while part >= 1:
        pl.when((n_past & part) != 0)(functools.partial(stream, done, part))
        done = done + (n_past & part)
        part //= 2
    stream(qi, 1, causal=True)
    limit = 2.0 ** EXP_HEADROOM
    in_range = jnp.where(_denominator(acc_ref) < limit, jnp.where(_denominator(win_ref) < limit, 1.0, 0.0), 0.0)
    overflow = jnp.min(in_range) < 1.0

    @pl.when(overflow)
    def _():
        def tile(kt, causal):
            s = scores(kt, 1, causal)
            m_t = col_max(s)
            return m_t, pv(values(kt, 1), s, m_t)

        def merge(carry, part):
            m, acc = carry
            m_t, o_t = part
            m_new = jnp.maximum(m, m_t)
            return m_new, acc * jnp.exp2(m - m_new) + o_t * jnp.exp2(m_t - m_new)

        carry = (jnp.full((1, rows), M_INIT, F32), jnp.zeros(acc_ref.shape, F32))
        carry = lax.fori_loop(0, n_past, lambda i, c: merge(c, tile(i, False)), carry)
        acc_ref[...] = merge(carry, tile(qi, True))[1]

        x_win = window_scores()
        win_ref[...] = pv(vwt_ref[0, :, win_rows], x_win, col_max(x_win))

    o_sel = _normalize(acc_ref[...])
    o_win = _normalize(win_ref[...])

    gate_row = kvh * (GQA_GROUP * N_BRANCH)
    for g in range(GQA_GROUP):
        cols = slice(g * tq, (g + 1) * tq)
        head = slice(g * HEAD_DIM, (g + 1) * HEAD_DIM)
        gates = [gate_ref[0, pl.ds(gate_row + N_BRANCH * g + br, 1), :] for br in range(N_BRANCH)]
        y = gates[0] * oc_ref[0, head, :] + gates[1] * o_sel[:, cols] + gates[2] * o_win[:, cols]
        y_ref[0, head, :] = y.astype(y_ref.dtype)


def _sel_win_attention(qt, mb_t, ks, key_onehot, vs_t, kw, vw_t, oc_t, gates_t):
    b, _, s = qt.shape
    tq = min(SEL_TILE, s)
    n_pad = mb_t.shape[2]
    n_gate = gates_t.shape[1]
    group_rows = GQA_GROUP * HEAD_DIM
    head_blk = pl.BlockSpec((1, group_rows, tq), lambda i, h, j: (i, h, j))
    once = pl.Buffered(1)
    keys = pl.BlockSpec((1, s, KV_WIDTH), lambda i, h, j: (i, 0, 0), pipeline_mode=once)
    vals = pl.BlockSpec((1, HEAD_DIM, s), lambda i, h, j: (i, h, 0), pipeline_mode=once)
    acc = pltpu.VMEM((HEAD_DIM + ONES_ROWS, GQA_GROUP * tq), F32)
    return pl.pallas_call(
        _selwin_body,
        grid=(b, N_KV_HEADS, s // tq),
        in_specs=[head_blk,
                  pl.BlockSpec((1, 1, n_pad, tq), lambda i, h, j: (i, h, 0, j)),
                  keys, _const_spec(key_onehot.shape), vals, keys, vals,
                  head_blk,
                  pl.BlockSpec((1, n_gate, tq), lambda i, h, j: (i, 0, j))],
        out_specs=head_blk,
        out_shape=jax.ShapeDtypeStruct((b, N_HEADS * HEAD_DIM, s), MXU_DTYPE),
        scratch_shapes=[pltpu.VMEM((n_pad // SEL_GROUP, SEL_GROUP + KV_WIDTH, GQA_GROUP * tq), MXU_DTYPE),
                        acc, acc],
        compiler_params=_params(("parallel", "parallel", "arbitrary")),
        name="sel_win_attention",
    )(qt, mb_t, ks, key_onehot, vs_t, kw, vw_t, oc_t, gates_t)


def _outproj_ffn_body(h_ref, ypl_ref, yat_ref, w1_ref, w2_ref, g_ref, *refs):
    ffn_refs, o_ref = refs[:N_FFN_OPERANDS], refs[N_FFN_OPERANDS]
    m = _mm(ypl_ref[0], w1_ref[...]) + _mm_tn(yat_ref[0], w2_ref[...])
    o_ref[0] = _ffn_math(h_ref[0] + _rms(m, g_ref[...]), ffn_refs)


def _outproj_ffn(h3, ypl, yat_t, w1, w2, g_post, ffn_weights):
    b, s, d = h3.shape
    tm = min(ROW_TILE, s)
    row = lambda w: pl.BlockSpec((1, tm, w), lambda i, j: (i, j, 0))
    ffn_specs, ffn_args = _ffn_operands(*ffn_weights)
    return pl.pallas_call(
        _outproj_ffn_body,
        grid=(b, s // tm),
        in_specs=[row(d), row(ypl.shape[2]),
                  pl.BlockSpec((1, yat_t.shape[1], tm), lambda i, j: (i, 0, j)),
                  _const_spec(w1.shape), _const_spec(w2.shape), _const_spec((1, d))] + ffn_specs,
        out_specs=row(d),
        out_shape=jax.ShapeDtypeStruct((b, s, d), F32),
        compiler_params=_params(("parallel", "parallel")),
        name="mixer_out_proj_ffn",
    )(h3, ypl, yat_t, w1, w2, g_post[None, :], *ffn_args)


def _block_diag(blocks):
    n, a, b = blocks.shape
    eye = jnp.eye(n, dtype=blocks.dtype)
    return jnp.einsum("nab,nm->namb", blocks, eye).reshape(n * a, n * b)


def _compress_weights(w, pe):
    eye = jnp.eye(N_KV_HEADS, dtype=w.dtype)
    halves = []
    for part in range(CMP_LEN // CMP_STRIDE):
        wl = w[part * CMP_STRIDE:(part + 1) * CMP_STRIDE]
        pel = pe[part * CMP_STRIDE:(part + 1) * CMP_STRIDE]
        wm = jnp.einsum("lde,hg->lhdge", wl, eye).reshape(CMP_STRIDE * KV_WIDTH, KV_WIDTH)
        pm = jnp.tile(pel[:, None, :], (1, N_KV_HEADS, 1)).reshape(1, -1)
        halves.append((wm.astype(MXU_DTYPE), pm))
    return halves


def kernel(x, positions, ffn1_pre_g, ffn1_post_g, ffn1_w_gate, ffn1_w_up, ffn1_w_down, mix_pre_g, mix_post_g,
           w_in, w_out, pool_w, pool_scale, conv_w, conv_b, lru_w_r, lru_b_r, lru_w_i, lru_b_i, lru_lambda,
           cmp_w_k, cmp_w_v, cmp_pe, ffn2_pre_g, ffn2_post_g, ffn2_w_gate, ffn2_w_up, ffn2_w_down):
    b, s, d = x.shape
    depth = w_in.shape[0]
    t = b * s
    pool_width = pool_w.shape[1] * pool_w.shape[2]
    lru_width = lru_w_r.shape[1] * lru_w_r.shape[2]
    attn_width = N_HEADS * HEAD_DIM
    assert pool_width == lru_width and s % SEL_TILE == 0 and s % (CMP_STRIDE * 8) == 0
    assert WINDOW % SEL_TILE == 0 and (SEL_GROUP * SEL_BLOCK) % (STREAM_TILES * SEL_TILE) == 0
    assert s >= WINDOW + SEL_TILE

    sizes = [("xpl", pool_width + 2 * lru_width), ("q", attn_width), ("kc", KV_WIDTH), ("vc", KV_WIDTH),
             ("ks", KV_WIDTH), ("vs", KV_WIDTH), ("kw", KV_WIDTH), ("vw", KV_WIDTH), ("g", V7X_LANES)]
    cols, off = {}, 0
    for name, width in sizes:
        cols[name] = (off, off + width)
        off += width

    n_sel = s // SEL_BLOCK
    n_pad = -(-n_sel // SEL_GROUP) * SEL_GROUP
    n1 = s // CMP_STRIDE
    key_blk = jnp.arange(min(s, SEL_GROUP * SEL_BLOCK)) // SEL_BLOCK
    key_onehot = (key_blk[:, None] == jnp.arange(SEL_GROUP)[None, :]).astype(MXU_DTYPE)

    cos, sin = _rope_tables(positions)

    h = x
    for l in range(depth):
        cast = lambda w: w[l].astype(MXU_DTYPE)
        ffn1 = (ffn1_pre_g[l], ffn1_post_g[l], cast(ffn1_w_gate), cast(ffn1_w_up), cast(ffn1_w_down))
        ffn2 = (ffn2_pre_g[l], ffn2_post_g[l], cast(ffn2_w_gate), cast(ffn2_w_up), cast(ffn2_w_down))

        w_all = jnp.pad(w_in[l], ((0, 0), (0, off - w_in.shape[2]))).astype(MXU_DTYPE)
        h = _ffn(h, ffn1)
        xpl, qt, kc, vc, ks, vs_t, kw, vw_t, gates_t = _inproj(h, mix_pre_g[l], w_all, cols, cos, sin)

        w_ri = jnp.concatenate([_block_diag(lru_w_r[l]), _block_diag(lru_w_i[l])], axis=1).astype(MXU_DTYPE)
        ypl = _poollru(xpl, _block_diag(pool_w[l]).astype(MXU_DTYPE), pool_scale[l],
                       conv_w[l], conv_b[l], w_ri, jnp.concatenate([lru_b_r[l], lru_b_i[l]]), lru_lambda[l])

        (wk_a, pe_a), (wk_b, pe_b) = _compress_weights(cmp_w_k[l], cmp_pe[l])
        (wv_a, _), (wv_b, _) = _compress_weights(cmp_w_v[l], cmp_pe[l])
        k_cmp = _compress(kc.reshape(b, n1, -1), pe_a, pe_b, wk_a, wk_b, False)
        v_cmp_t = _compress(vc.reshape(b, n1, -1), pe_a, pe_b, wv_a, wv_b, True)

        oc_t, mb_t = _cmp_attention(qt, k_cmp, v_cmp_t, n_sel, n_pad)
        y_t = _sel_win_attention(qt, mb_t, ks, key_onehot, vs_t, kw, vw_t, oc_t, gates_t)

        w_o = w_out[l].astype(MXU_DTYPE)
        split = pool_width + lru_width
        h = _outproj_ffn(h, ypl, y_t, w_o[:split], w_o[split:], mix_post_g[l], ffn2)
    return h
```

```python
import functools

import jax
import jax.numpy as jnp
from jax import lax
from jax.experimental import pallas as pl
from jax.experimental.pallas import tpu as pltpu

F32 = jnp.float32
MXU_DTYPE = jnp.bfloat16

POOL_WINDOWS = (2, 4, 8, 16)
POOL_GROUP = 64
LRU_C = 8.0
CONV_WIDTH = 4
HEAD_DIM = 64
N_KV_HEADS = 2
GQA_GROUP = 4
N_HEADS = N_KV_HEADS * GQA_GROUP
KV_WIDTH = N_KV_HEADS * HEAD_DIM
N_BRANCH = 3
CMP_LEN = 32
CMP_STRIDE = 16
SEL_BLOCK = 64
SEL_TOPK = 16
WINDOW = 512
ROPE_THETA = 10000.0
NORM_EPS = 1e-6
NEG_INF = -1e30
BIG_SCORE = 1e9
GELU_C0 = 0.7978845608028654
GELU_C1 = 0.044715
LOG2_E = 1.4426950408889634

V7X_LANES = 128
V7X_SUBLANES = 8
V7X_VMEM_LIMIT_BYTES = 56 * 1024 * 1024

ROW_TILE = 512
SEQ_TILE = 512
CMP_Q_TILE = 512
CMP_CHUNK = 128
SEL_TILE = 512
SEL_GROUP = 128
ONES_ROWS = 16
M_INIT = -1e20
STREAM_TILES = 4
EXP_HEADROOM = 64.0
Q_SCALE = HEAD_DIM ** -0.5 * LOG2_E


def _params(semantics):
    return pltpu.CompilerParams(dimension_semantics=semantics, vmem_limit_bytes=V7X_VMEM_LIMIT_BYTES)


def _rms(x, g):
    return x * lax.rsqrt(jnp.mean(x * x, axis=-1, keepdims=True) + NORM_EPS) * g


def _sigmoid(x):
    return 1.0 / (1.0 + jnp.exp(-x))


def _mm(a, b):
    return jnp.dot(a, b, preferred_element_type=F32)


def _mm_tn(a_t, b):
    return lax.dot_general(a_t, b, (((0,), (0,)), ((), ())), preferred_element_type=F32)


def _log2(n):
    assert n & (n - 1) == 0
    return n.bit_length() - 1


def _const_spec(shape):
    zeros = (0,) * len(shape)
    return pl.BlockSpec(shape, lambda *_: zeros, pipeline_mode=pl.Buffered(1))


def _rope_body(pos_ref, inv_ref, cos_ref, sin_ref):
    ang = pos_ref[0].astype(F32) * inv_ref[...]
    lane = lax.broadcasted_iota(jnp.int32, ang.shape, 1)
    cos_ref[0] = jnp.cos(ang)
    sin_ref[0] = jnp.where((lane & (HEAD_DIM - 1)) < HEAD_DIM // 2, -jnp.sin(ang), jnp.sin(ang))


def _rope_tables(positions):
    b, s = positions.shape
    inv = ROPE_THETA ** (-jnp.arange(0, HEAD_DIM, 2, dtype=F32) / HEAD_DIM)
    inv_row = jnp.tile(inv, V7X_LANES // (HEAD_DIM // 2))[None, :]
    ts = min(SEQ_TILE, s)
    out = jax.ShapeDtypeStruct((b, s, V7X_LANES), F32)
    return pl.pallas_call(
        _rope_body,
        grid=(b, s // ts),
        in_specs=[pl.BlockSpec((1, ts, 1), lambda i, j: (i, j, 0)),
                  pl.BlockSpec((1, V7X_LANES), lambda i, j: (0, 0))],
        out_specs=[pl.BlockSpec((1, ts, V7X_LANES), lambda i, j: (i, j, 0))] * 2,
        out_shape=[out, out],
        compiler_params=_params(("parallel", "parallel")),
        name="rope_tables",
    )(positions[:, :, None], inv_row)


def _ffn_math(h, ffn_refs):
    gpre_ref, gpost_ref, wg_ref, wu_ref, wd_ref = ffn_refs
    xn = _rms(h, gpre_ref[...]).astype(MXU_DTYPE)
    gate = _mm(xn, wg_ref[...])
    up = _mm(xn, wu_ref[...])
    act = (gate * _sigmoid(gate) * up).astype(MXU_DTYPE)
    f = _mm(act, wd_ref[...])
    return h + 0.5 * _rms(f, gpost_ref[...])


def _ffn_operands(g_pre, g_post, w_gate, w_up, w_down):
    d, dff = w_gate.shape
    specs = [_const_spec((1, d)), _const_spec((1, d)),
             _const_spec((d, dff)), _const_spec((d, dff)), _const_spec((dff, d))]
    return specs, (g_pre[None, :], g_post[None, :], w_gate, w_up, w_down)


N_FFN_OPERANDS = 5


def _ffn_body(h_ref, *refs):
    refs[N_FFN_OPERANDS][0] = _ffn_math(h_ref[0], refs[:N_FFN_OPERANDS])


def _ffn(h3, ffn_weights):
    b, s, d = h3.shape
    tm = min(ROW_TILE, s)
    row = pl.BlockSpec((1, tm, d), lambda i, j: (i, j, 0))
    ffn_specs, ffn_args = _ffn_operands(*ffn_weights)
    return pl.pallas_call(
        _ffn_body,
        grid=(b, s // tm),
        in_specs=[row] + ffn_specs,
        out_specs=row,
        out_shape=jax.ShapeDtypeStruct((b, s, d), F32),
        compiler_params=_params(("parallel", "parallel")),
        name="ffn",
    )(h3, *ffn_args)


def _swap_halves(x):
    n = x.shape[1]
    lane = lax.broadcasted_iota(jnp.int32, x.shape, 1)
    first_half = (lane & (HEAD_DIM - 1)) < HEAD_DIM // 2
    return jnp.where(first_half, pltpu.roll(x, n - HEAD_DIM // 2, 1), pltpu.roll(x, HEAD_DIM // 2, 1))


def _inproj_body(cols, h_ref, g_ref, w_ref, cos_ref, sin_ref,
                 xpl_ref, qt_ref, kc_ref, vc_ref, ks_ref, vst_ref, kw_ref, vwt_ref, gatet_ref):
    xn = _rms(h_ref[0], g_ref[...]).astype(MXU_DTYPE)
    proj = _mm(xn, w_ref[...])
    cos = cos_ref[0]
    sin = sin_ref[0]

    def rope(x):
        rep = x.shape[1] // V7X_LANES
        c = jnp.concatenate([cos] * rep, axis=1) if rep > 1 else cos
        s = jnp.concatenate([sin] * rep, axis=1) if rep > 1 else sin
        return x * c + _swap_halves(x) * s

    def seg(name):
        lo, hi = cols[name]
        return proj[:, lo:hi]

    xpl_ref[0] = seg("xpl")
    qt_ref[0] = (rope(seg("q")) * Q_SCALE).T.astype(qt_ref.dtype)
    kc_ref[0] = rope(seg("kc"))
    vc_ref[0] = seg("vc")
    ks_ref[0] = rope(seg("ks")).astype(ks_ref.dtype)
    vst_ref[0] = seg("vs").T.astype(vst_ref.dtype)
    kw_ref[0] = rope(seg("kw")).astype(kw_ref.dtype)
    vwt_ref[0] = seg("vw").T.astype(vwt_ref.dtype)
    gatet_ref[0] = _sigmoid(seg("g")).T[:gatet_ref.shape[1]]


def _inproj(h3, g_pre, w_all, cols, cos, sin):
    b, s, d = h3.shape
    tm = min(ROW_TILE, s)
    ncol = w_all.shape[1]
    width = lambda n: cols[n][1] - cols[n][0]
    row = lambda w: pl.BlockSpec((1, tm, w), lambda i, j: (i, j, 0))
    col = lambda w: pl.BlockSpec((1, w, tm), lambda i, j: (i, 0, j))
    tok = lambda n, dt: (jax.ShapeDtypeStruct((b, s, width(n)), dt), row(width(n)))
    chan = lambda w, dt: (jax.ShapeDtypeStruct((b, w, s), dt), col(w))
    outs = [tok("xpl", F32), chan(width("q"), MXU_DTYPE), tok("kc", F32), tok("vc", F32),
            tok("ks", MXU_DTYPE), chan(width("vs"), MXU_DTYPE), tok("kw", MXU_DTYPE),
            chan(width("vw"), MXU_DTYPE), chan(N_BRANCH * N_HEADS, F32)]
    return pl.pallas_call(
        functools.partial(_inproj_body, cols),
        grid=(b, s // tm),
        in_specs=[row(d), _const_spec((1, d)), _const_spec((d, ncol)), row(V7X_LANES), row(V7X_LANES)],
        out_specs=[o[1] for o in outs],
        out_shape=[o[0] for o in outs],
        compiler_params=_params(("parallel", "parallel")),
        name="mixer_in_proj",
    )(h3, g_pre[None, :], w_all, cos, sin)


def _poollru_body(x_ref, pw_ref, pscale_ref, cw_ref, cb_ref, wri_ref, bri_ref, lam_ref,
                  y_ref, pool_carry, conv_carry, h_carry):
    si = pl.program_id(1)
    ts = x_ref.shape[1]
    width = pw_ref.shape[0]
    halo_p = pool_carry.shape[0]
    halo_c = conv_carry.shape[0]

    @pl.when(si == 0)
    def _():
        pool_carry[...] = jnp.zeros_like(pool_carry)
        conv_carry[...] = jnp.zeros_like(conv_carry)
        h_carry[...] = jnp.zeros_like(h_carry)

    x = x_ref[0]
    xp = x[:, :width]
    xl = x[:, width:2 * width]
    gl = x[:, 2 * width:]

    ext = jnp.concatenate([pool_carry[...], xp], axis=0)
    sums = [ext]
    shift = 1
    for _ in POOL_WINDOWS:
        sums.append(sums[-1] + pltpu.roll(sums[-1], shift, 0))
        shift *= 2
    lane = lax.broadcasted_iota(jnp.int32, (1, width), 1)
    grp = lax.shift_right_logical(lane, _log2(POOL_GROUP))
    win_sum = sums[len(POOL_WINDOWS)]
    win = jnp.full((1, width), float(POOL_WINDOWS[-1]), F32)
    for gi in range(len(POOL_WINDOWS) - 2, -1, -1):
        win_sum = jnp.where(grp == gi, sums[gi + 1], win_sum)
        win = jnp.where(grp == gi, float(POOL_WINDOWS[gi]), win)
    win_sum = win_sum[halo_p:]
    t_abs = si * ts + lax.broadcasted_iota(jnp.int32, (ts, 1), 0)
    cnt = jnp.minimum((t_abs + 1).astype(F32), win)
    pooled = win_sum / cnt
    y_pool = _mm((pooled - xp).astype(MXU_DTYPE), pw_ref[...]) * pscale_ref[...]
    pool_carry[...] = xp[ts - halo_p:]

    extc = jnp.concatenate([conv_carry[...], xl], axis=0)
    xc = extc * cw_ref[CONV_WIDTH - 1:CONV_WIDTH, :]
    for k in range(1, CONV_WIDTH):
        xc = xc + pltpu.roll(extc, k, 0) * cw_ref[CONV_WIDTH - 1 - k:CONV_WIDTH - k, :]
    xc = xc[halo_c:] + cb_ref[...]
    conv_carry[...] = xl[ts - halo_c:]

    ri = _mm(xc.astype(MXU_DTYPE), wri_ref[...]) + bri_ref[...]
    r = _sigmoid(ri[:, :width])
    i_gate = _sigmoid(ri[:, width:])
    neg_lam = -lam_ref[...]
    softplus = jnp.maximum(neg_lam, 0.0) + jnp.log1p(jnp.exp(-jnp.abs(neg_lam)))
    log_a = -LRU_C * r * softplus
    a = jnp.exp(log_a)
    one_minus_a2 = -jnp.tanh(log_a) * (a * a + 1.0)
    mult = jnp.where(one_minus_a2 > 0.0, one_minus_a2 * lax.rsqrt(one_minus_a2), 0.0)
    b = mult * (i_gate * xc)

    row_in_group = lax.broadcasted_iota(jnp.int32, (ts, 1), 0) & (V7X_SUBLANES - 1)
    k = 1
    while k < V7X_SUBLANES:
        keep = row_in_group >= k
        a_prev = jnp.where(keep, pltpu.roll(a, k, 0), 1.0)
        b_prev = jnp.where(keep, pltpu.roll(b, k, 0), 0.0)
        b = a * b_prev + b
        a = a * a_prev
        k *= 2
    h_prev = h_carry[0:1, :]
    groups = []
    for g in range(ts // V7X_SUBLANES):
        rows_g = slice(g * V7X_SUBLANES, (g + 1) * V7X_SUBLANES)
        h_g = a[rows_g] * h_prev + b[rows_g]
        groups.append(h_g)
        h_prev = h_g[V7X_SUBLANES - 1:V7X_SUBLANES]
    h = jnp.concatenate(groups, axis=0)
    h_carry[...] = jnp.broadcast_to(h_prev, h_carry.shape)

    gelu = 0.5 * gl * (1.0 + jnp.tanh(GELU_C0 * (gl + GELU_C1 * gl * gl * gl)))
    y_ref[0] = jnp.concatenate([y_pool, h * gelu], axis=1).astype(y_ref.dtype)


def _poollru(xpl, pool_w_bd, pool_scale, conv_w, conv_b, w_ri_bd, b_ri, lam):
    b, s, w3 = xpl.shape
    width = w3 // 3
    ts = min(SEQ_TILE, s)
    return pl.pallas_call(
        _poollru_body,
        grid=(b, s // ts),
        in_specs=[pl.BlockSpec((1, ts, w3), lambda i, j: (i, j, 0)),
                  _const_spec((width, width)), _const_spec((1, width)),
                  _const_spec((CONV_WIDTH, width)), _const_spec((1, width)),
                  _const_spec((width, 2 * width)), _const_spec((1, 2 * width)), _const_spec((1, width))],
        out_specs=pl.BlockSpec((1, ts, 2 * width), lambda i, j: (i, j, 0)),
        out_shape=jax.ShapeDtypeStruct((b, s, 2 * width), MXU_DTYPE),
        scratch_shapes=[pltpu.VMEM((POOL_WINDOWS[-1], width), F32),
                        pltpu.VMEM((V7X_SUBLANES, width), F32),
                        pltpu.VMEM((V7X_SUBLANES, width), F32)],
        compiler_params=_params(("parallel", "arbitrary")),
        name="pool_rglru",
    )(xpl, pool_w_bd, pool_scale[None, :], conv_w, conv_b[None, :], w_ri_bd, b_ri[None, :], lam[None, :])


def _compress_body(transpose_out, r_ref, pea_ref, peb_ref, wa_ref, wb_ref, o_ref):
    r = r_ref[0]
    first = _mm((r + pea_ref[...]).astype(MXU_DTYPE), wa_ref[...])
    second = _mm((r + peb_ref[...]).astype(MXU_DTYPE), wb_ref[...])
    n1 = r.shape[0]
    out = first + pltpu.roll(second, n1 - 1, 0)
    o_ref[0] = (out.T if transpose_out else out).astype(o_ref.dtype)


def _compress(rows, pe_a, pe_b, w_a, w_b, transpose_out):
    b, n1, k = rows.shape
    n_out = w_a.shape[1]
    out_dims = (n_out, n1) if transpose_out else (n1, n_out)
    return pl.pallas_call(
        functools.partial(_compress_body, transpose_out),
        grid=(b,),
        in_specs=[pl.BlockSpec((1, n1, k), lambda i: (i, 0, 0)),
                  _const_spec((1, k)), _const_spec((1, k)), _const_spec((k, n_out)), _const_spec((k, n_out))],
        out_specs=pl.BlockSpec((1,) + out_dims, lambda i: (i, 0, 0)),
        out_shape=jax.ShapeDtypeStruct((b,) + out_dims, MXU_DTYPE),
        compiler_params=_params(("parallel",)),
        name="compress_kv",
    )(rows, pe_a, pe_b, w_a, w_b)


def _stack_heads(q_ref):
    return jnp.concatenate([q_ref[0, g * HEAD_DIM:(g + 1) * HEAD_DIM, :] for g in range(GQA_GROUP)], axis=1)


def _place_in_kv_half(qt, kvh):
    zeros = jnp.zeros_like(qt)
    parts = [jnp.where(kvh == h, qt, zeros) for h in range(N_KV_HEADS)]
    return jnp.concatenate(parts, axis=0)


def _query_positions(start, tq):
    lane = lax.broadcasted_iota(jnp.int32, (1, GQA_GROUP * tq), 1)
    return start + (lane & (tq - 1))


def _with_ones(v_t):
    return jnp.concatenate([v_t, jnp.ones((ONES_ROWS, v_t.shape[1]), v_t.dtype)], axis=0)


def _denominator(acc):
    return acc[HEAD_DIM:HEAD_DIM + 1]


def _normalize(acc):
    return acc[:HEAD_DIM] * (1.0 / _denominator(acc))


def _cmp_body(n_sel, q_ref, kc_ref, vct_ref, oc_ref, mb_ref, band_ref):
    tq = q_ref.shape[2]
    kvh = pl.program_id(1)
    start = pl.program_id(2) * tq
    qt = _stack_heads(q_ref)
    qp = _place_in_kv_half(qt, kvh)
    n1 = kc_ref.shape[1]
    t_row = _query_positions(start, tq)
    assert CMP_STRIDE * CMP_CHUNK >= tq + CMP_LEN

    def attend(nrows):
        s = _mm(kc_ref[0, :nrows, :], qp)
        lo = max(0, nrows - 2 * CMP_CHUNK)
        n_idx = lo + lax.broadcasted_iota(jnp.int32, (nrows - lo, 1), 0)
        tail = jnp.where(n_idx * CMP_STRIDE + (CMP_LEN - 1) <= t_row, s[lo:], NEG_INF)
        s = jnp.concatenate([s[:lo], tail], axis=0) if lo else tail
        m = jnp.max(s, axis=0, keepdims=True)
        p = jnp.exp2(s - jnp.where(m > 0.5 * NEG_INF, m, 0.0))
        den = jnp.sum(p, axis=0, keepdims=True)
        pn = p * (1.0 / jnp.where(den > 0.0, den, 1.0))
        oc = _mm(vct_ref[0, :, :nrows], pn.astype(MXU_DTYPE))
        psum = pn[:, 0:tq]
        for g in range(GQA_GROUP):
            oc_ref[0, g * HEAD_DIM:(g + 1) * HEAD_DIM, :] = oc[:, g * tq:(g + 1) * tq]
            if g:
                psum = psum + pn[:, g * tq:(g + 1) * tq]
        n_blk = min(n_sel, nrows * CMP_STRIDE // SEL_BLOCK)
        step, lead, pad = SEL_BLOCK // CMP_STRIDE, CMP_LEN // CMP_STRIDE - 1, V7X_SUBLANES
        cols = []
        for c in range(tq // V7X_LANES):
            band_ref[c, 0:pad, :] = jnp.zeros((pad, V7X_LANES), F32)
            band_ref[c, pad:pad + nrows, :] = psum[:, c * V7X_LANES:(c + 1) * V7X_LANES]
            taps = [band_ref[c, pl.ds(pad - lead + k, n_blk, stride=step), :] for k in range(step + lead)]
            cols.append(functools.reduce(lambda a, b: a + b, taps))
        select(jnp.concatenate(cols, axis=1), n_blk)

    def select(imp, n_blk):
        j = lax.broadcasted_iota(jnp.int32, (n_blk, 1), 0).astype(F32)
        t_q = start + lax.broadcasted_iota(jnp.int32, (1, tq), 1)
        cur = lax.shift_right_logical(t_q, _log2(SEL_BLOCK)).astype(F32)
        forced = (j == 0.0) | (j == cur) | (j == cur - 1.0)
        valid = j <= cur
        base = jnp.where(valid, jnp.where(forced, BIG_SCORE, imp), -BIG_SCORE)
        n_pad = mb_ref.shape[2]
        if n_pad > n_blk:
            mb_ref[0, 0, n_blk:, :] = jnp.full((n_pad - n_blk, tq), NEG_INF, mb_ref.dtype)

        score = jnp.where(forced, -jnp.inf, base)
        for _ in range(SEL_TOPK - 3):
            score = jnp.where(score == jnp.max(score, axis=0, keepdims=True), -jnp.inf, score)
        bias = jnp.where(score == -jnp.inf, 0.0, NEG_INF)
        mb_ref[0, 0, :n_blk, :] = bias.astype(mb_ref.dtype)
        taken = jnp.sum(jnp.where(valid, jnp.where(bias == 0.0, 1.0, 0.0), 0.0), axis=0, keepdims=True)
        tie = jnp.max(jnp.abs(taken - jnp.minimum(cur + 1.0, float(SEL_TOPK)))) > 0.0

        @pl.when(tie)
        def _():
            score = base
            bias = jnp.full(score.shape, NEG_INF, F32)
            for _ in range(SEL_TOPK):
                best = jnp.max(score, axis=0, keepdims=True)
                first = jnp.min(jnp.where(score == best, j, float(n_blk)), axis=0, keepdims=True)
                pick = j == first
                bias = jnp.where(pick, 0.0, bias)
                score = jnp.where(pick, -jnp.inf, score)
            mb_ref[0, 0, :n_blk, :] = bias.astype(mb_ref.dtype)

    assert n_sel >= SEL_TOPK and tq % SEL_BLOCK == 0
    n_need = lax.shift_right_logical(start + tq, _log2(CMP_STRIDE)) - 1
    chunks = lax.shift_right_logical(n_need + CMP_CHUNK - 1, _log2(CMP_CHUNK))
    n_variants = -(-n1 // CMP_CHUNK)
    for c in range(1, n_variants + 1):
        pl.when(chunks == c)(functools.partial(attend, min(c * CMP_CHUNK, n1)))


def _cmp_attention(qt, k_cmp, v_cmp_t, n_sel, n_pad):
    b, _, s = qt.shape
    n1 = k_cmp.shape[1]
    tq = min(CMP_Q_TILE, s)
    group_rows = GQA_GROUP * HEAD_DIM
    head_blk = pl.BlockSpec((1, group_rows, tq), lambda i, h, j: (i, h, j))
    return pl.pallas_call(
        functools.partial(_cmp_body, n_sel),
        grid=(b, N_KV_HEADS, s // tq),
        in_specs=[head_blk,
                  pl.BlockSpec((1, n1, KV_WIDTH), lambda i, h, j: (i, 0, 0)),
                  pl.BlockSpec((1, HEAD_DIM, n1), lambda i, h, j: (i, h, 0))],
        out_specs=[head_blk, pl.BlockSpec((1, 1, n_pad, tq), lambda i, h, j: (i, h, 0, j))],
        out_shape=[jax.ShapeDtypeStruct((b, N_HEADS * HEAD_DIM, s), F32),
                   jax.ShapeDtypeStruct((b, N_KV_HEADS, n_pad, s), MXU_DTYPE)],
        scratch_shapes=[pltpu.VMEM((tq // V7X_LANES, n1 + V7X_SUBLANES, V7X_LANES), F32)],
        compiler_params=_params(("parallel", "parallel", "parallel")),
        name="cmp_attention_select",
    )(qt, k_cmp, v_cmp_t)


def _selwin_body(q_ref, mb_ref, ks_ref, oh_ref, vst_ref, kw_ref, vwt_ref, oc_ref, gate_ref, y_ref,
                 qa_ref, acc_ref, win_ref):
    tq = q_ref.shape[2]
    tk = tq
    kvh = pl.program_id(1)
    qi = pl.program_id(2)
    start = qi * tq
    rows = GQA_GROUP * tq
    qt = _stack_heads(q_ref)
    qp = _place_in_kv_half(qt, kvh)
    t_row = _query_positions(start, tq)
    group_shift = _log2(SEL_GROUP * SEL_BLOCK // tk)
    n_groups = mb_ref.shape[2] // SEL_GROUP

    for grp in range(n_groups):
        bias_rows = mb_ref[0, 0, grp * SEL_GROUP:(grp + 1) * SEL_GROUP, :]
        qa_ref[grp] = jnp.concatenate([jnp.concatenate([bias_rows] * GQA_GROUP, axis=1), qp], axis=0)

    def scores(kt, n_sub, causal):
        key_rows = pl.ds(pl.multiple_of(kt * tk, tk), n_sub * tk)
        oh_rows = pl.ds(pl.multiple_of(lax.rem(kt * tk, oh_ref.shape[0]), tk), n_sub * tk)
        k = jnp.concatenate([oh_ref[oh_rows, :], ks_ref[0, key_rows, :]], axis=1)
        s = _mm(k, qa_ref[lax.shift_right_logical(kt, group_shift)])
        if causal:
            kp = kt * tk + lax.broadcasted_iota(jnp.int32, (n_sub * tk, 1), 0)
            s = jnp.where(kp <= t_row, s, NEG_INF)
        return s

    def values(kt, n_sub):
        return vst_ref[0, :, pl.ds(pl.multiple_of(kt * tk, tk), n_sub * tk)]

    def col_max(s):
        return jnp.max(s, axis=0, keepdims=True)

    n_past = qi

    assert tq == WINDOW
    own_rows = pl.ds(pl.multiple_of(start, tq), tq)
    first_tile = qi == 0
    win_rows = pl.ds(pl.multiple_of(jnp.maximum(start - WINDOW, 0), tq), WINDOW + tq)

    def window_scores():
        s = _mm(kw_ref[0, win_rows, :], qp)
        kp = jnp.maximum(start - WINDOW, 0) + lax.broadcasted_iota(jnp.int32, (WINDOW + tq, 1), 0)
        causal_lhs, causal_rhs = -kp, -(t_row + 1)
        lhs = jnp.where(first_tile, causal_lhs[:WINDOW], kp[:WINDOW])
        rhs = jnp.where(first_tile, causal_rhs, t_row - WINDOW)
        return jnp.concatenate([jnp.where(lhs > rhs, s[:WINDOW], NEG_INF),
                                jnp.where(causal_lhs[WINDOW:] > causal_rhs, s[WINDOW:], NEG_INF)], axis=0)

    def pv(v_t, s, ref):
        return _mm(_with_ones(v_t), jnp.exp2(s - ref).astype(MXU_DTYPE))

    def self_scores(k_ref):
        k_t = k_ref[0, own_rows, :].astype(F32).T
        k_own = k_t[:HEAD_DIM]
        for h in range(1, N_KV_HEADS):
            k_own = jnp.where(kvh == h, k_t[h * HEAD_DIM:(h + 1) * HEAD_DIM], k_own)
        parts = [jnp.sum(qt[:, g * tq:(g + 1) * tq].astype(F32) * k_own, axis=0, keepdims=True)
                 for g in range(GQA_GROUP)]
        return jnp.concatenate(parts, axis=1)

    ref = self_scores(ks_ref)
    ref_w = self_scores(kw_ref)
    win_ref[...] = pv(vwt_ref[0, :, win_rows], window_scores(), ref_w)

    def stream(kt, n_sub, causal=False):
        acc_ref[...] += pv(values(kt, n_sub), scores(kt, n_sub, causal), ref)

    acc_ref[...] = jnp.zeros_like(acc_ref)
    n_big = lax.shift_right_logical(n_past, _log2(STREAM_TILES))
    lax.fori_loop(0, n_big, lambda i, c: (stream(i * STREAM_TILES, STREAM_TILES), c)[1], 0)
    done = n_big * STREAM_TILES
    part = STREAM_TILES // 2
    while part >= 1:
        pl.when((n_past & part) != 0)(functools.partial(stream, done, part))
        done = done + (n_past & part)
        part //= 2
    stream(qi, 1, causal=True)
    limit = 2.0 ** EXP_HEADROOM
    in_range = jnp.where(_denominator(acc_ref) < limit, jnp.where(_denominator(win_ref) < limit, 1.0, 0.0), 0.0)
    overflow = jnp.min(in_range) < 1.0

    @pl.when(overflow)
    def _():
        def tile(kt, causal):
            s = scores(kt, 1, causal)
            m_t = col_max(s)
            return m_t, pv(values(kt, 1), s, m_t)

        def merge(carry, part):
            m, acc = carry
            m_t, o_t = part
            m_new = jnp.maximum(m, m_t)
            return m_new, acc * jnp.exp2(m - m_new) + o_t * jnp.exp2(m_t - m_new)

        carry = (jnp.full((1, rows), M_INIT, F32), jnp.zeros(acc_ref.shape, F32))
        carry = lax.fori_loop(0, n_past, lambda i, c: merge(c, tile(i, False)), carry)
        acc_ref[...] = merge(carry, tile(qi, True))[1]

        x_win = window_scores()
        win_ref[...] = pv(vwt_ref[0, :, win_rows], x_win, col_max(x_win))

    o_sel = _normalize(acc_ref[...])
    o_win = _normalize(win_ref[...])

    gate_row = kvh * (GQA_GROUP * N_BRANCH)
    for g in range(GQA_GROUP):
        cols = slice(g * tq, (g + 1) * tq)
        head = slice(g * HEAD_DIM, (g + 1) * HEAD_DIM)
        gates = [gate_ref[0, pl.ds(gate_row + N_BRANCH * g + br, 1), :] for br in range(N_BRANCH)]
        y = gates[0] * oc_ref[0, head, :] + gates[1] * o_sel[:, cols] + gates[2] * o_win[:, cols]
        y_ref[0, head, :] = y.astype(y_ref.dtype)


def _sel_win_attention(qt, mb_t, ks, key_onehot, vs_t, kw, vw_t, oc_t, gates_t):
    b, _, s = qt.shape
    tq = min(SEL_TILE, s)
    n_pad = mb_t.shape[2]
    n_gate = gates_t.shape[1]
    group_rows = GQA_GROUP * HEAD_DIM
    head_blk = pl.BlockSpec((1, group_rows, tq), lambda i, h, j: (i, h, j))
    once = pl.Buffered(1)
    keys = pl.BlockSpec((1, s, KV_WIDTH), lambda i, h, j: (i, 0, 0), pipeline_mode=once)
    vals = pl.BlockSpec((1, HEAD_DIM, s), lambda i, h, j: (i, h, 0), pipeline_mode=once)
    acc = pltpu.VMEM((HEAD_DIM + ONES_ROWS, GQA_GROUP * tq), F32)
    return pl.pallas_call(
        _selwin_body,
        grid=(b, N_KV_HEADS, s // tq),
        in_specs=[head_blk,
                  pl.BlockSpec((1, 1, n_pad, tq), lambda i, h, j: (i, h, 0, j)),
                  keys, _const_spec(key_onehot.shape), vals, keys, vals,
                  head_blk,
                  pl.BlockSpec((1, n_gate, tq), lambda i, h, j: (i, 0, j))],
        out_specs=head_blk,
        out_shape=jax.ShapeDtypeStruct((b, N_HEADS * HEAD_DIM, s), MXU_DTYPE),
        scratch_shapes=[pltpu.VMEM((n_pad // SEL_GROUP, SEL_GROUP + KV_WIDTH, GQA_GROUP * tq), MXU_DTYPE),
                        acc, acc],
        compiler_params=_params(("parallel", "parallel", "arbitrary")),
        name="sel_win_attention",
    )(qt, mb_t, ks, key_onehot, vs_t, kw, vw_t, oc_t, gates_t)


def _outproj_ffn_body(h_ref, ypl_ref, yat_ref, w1_ref, w2_ref, g_ref, *refs):
    ffn_refs, o_ref = refs[:N_FFN_OPERANDS], refs[N_FFN_OPERANDS]
    m = _mm(ypl_ref[0], w1_ref[...]) + _mm_tn(yat_ref[0], w2_ref[...])
    o_ref[0] = _ffn_math(h_ref[0] + _rms(m, g_ref[...]), ffn_refs)


def _outproj_ffn(h3, ypl, yat_t, w1, w2, g_post, ffn_weights):
    b, s, d = h3.shape
    tm = min(ROW_TILE, s)
    row = lambda w: pl.BlockSpec((1, tm, w), lambda i, j: (i, j, 0))
    ffn_specs, ffn_args = _ffn_operands(*ffn_weights)
    return pl.pallas_call(
        _outproj_ffn_body,
        grid=(b, s // tm),
        in_specs=[row(d), row(ypl.shape[2]),
                  pl.BlockSpec((1, yat_t.shape[1], tm), lambda i, j: (i, 0, j)),
                  _const_spec(w1.shape), _const_spec(w2.shape), _const_spec((1, d))] + ffn_specs,
        out_specs=row(d),
        out_shape=jax.ShapeDtypeStruct((b, s, d), F32),
        compiler_params=_params(("parallel", "parallel")),
        name="mixer_out_proj_ffn",
    )(h3, ypl, yat_t, w1, w2, g_post[None, :], *ffn_args)


def _block_diag(blocks):
    n, a, b = blocks.shape
    eye = jnp.eye(n, dtype=blocks.dtype)
    return jnp.einsum("nab,nm->namb", blocks, eye).reshape(n * a, n * b)


def _compress_weights(w, pe):
    eye = jnp.eye(N_KV_HEADS, dtype=w.dtype)
    halves = []
    for part in range(CMP_LEN // CMP_STRIDE):
        wl = w[part * CMP_STRIDE:(part + 1) * CMP_STRIDE]
        pel = pe[part * CMP_STRIDE:(part + 1) * CMP_STRIDE]
        wm = jnp.einsum("lde,hg->lhdge", wl, eye).reshape(CMP_STRIDE * KV_WIDTH, KV_WIDTH)
        pm = jnp.tile(pel[:, None, :], (1, N_KV_HEADS, 1)).reshape(1, -1)
        halves.append((wm.astype(MXU_DTYPE), pm))
    return halves


def kernel(x, positions, ffn1_pre_g, ffn1_post_g, ffn1_w_gate, ffn1_w_up, ffn1_w_down, mix_pre_g, mix_post_g,
           w_in, w_out, pool_w, pool_scale, conv_w, conv_b, lru_w_r, lru_b_r, lru_w_i, lru_b_i, lru_lambda,
           cmp_w_k, cmp_w_v, cmp_pe, ffn2_pre_g, ffn2_post_g, ffn2_w_gate, ffn2_w_up, ffn2_w_down):
    b, s, d = x.shape
    depth = w_in.shape[0]
    t = b * s
    pool_width = pool_w.shape[1] * pool_w.shape[2]
    lru_width = lru_w_r.shape[1] * lru_w_r.shape[2]
    attn_width = N_HEADS * HEAD_DIM
    assert pool_width == lru_width and s % SEL_TILE == 0 and s % (CMP_STRIDE * 8) == 0
    assert WINDOW % SEL_TILE == 0 and (SEL_GROUP * SEL_BLOCK) % (STREAM_TILES * SEL_TILE) == 0
    assert s >= WINDOW + SEL_TILE

    sizes = [("xpl", pool_width + 2 * lru_width), ("q", attn_width), ("kc", KV_WIDTH), ("vc", KV_WIDTH),
             ("ks", KV_WIDTH), ("vs", KV_WIDTH), ("kw", KV_WIDTH), ("vw", KV_WIDTH), ("g", V7X_LANES)]
    cols, off = {}, 0
    for name, width in sizes:
        cols[name] = (off, off + width)
        off += width

    n_sel = s // SEL_BLOCK
    n_pad = -(-n_sel // SEL_GROUP) * SEL_GROUP
    n1 = s // CMP_STRIDE
    key_blk = jnp.arange(min(s, SEL_GROUP * SEL_BLOCK)) // SEL_BLOCK
    key_onehot = (key_blk[:, None] == jnp.arange(SEL_GROUP)[None, :]).astype(MXU_DTYPE)

    cos, sin = _rope_tables(positions)

    h = x
    for l in range(depth):
        cast = lambda w: w[l].astype(MXU_DTYPE)
        ffn1 = (ffn1_pre_g[l], ffn1_post_g[l], cast(ffn1_w_gate), cast(ffn1_w_up), cast(ffn1_w_down))
        ffn2 = (ffn2_pre_g[l], ffn2_post_g[l], cast(ffn2_w_gate), cast(ffn2_w_up), cast(ffn2_w_down))

        w_all = jnp.pad(w_in[l], ((0, 0), (0, off - w_in.shape[2]))).astype(MXU_DTYPE)
        h = _ffn(h, ffn1)
        xpl, qt, kc, vc, ks, vs_t, kw, vw_t, gates_t = _inproj(h, mix_pre_g[l], w_all, cols, cos, sin)

        w_ri = jnp.concatenate([_block_diag(lru_w_r[l]), _block_diag(lru_w_i[l])], axis=1).astype(MXU_DTYPE)
        ypl = _poollru(xpl, _block_diag(pool_w[l]).astype(MXU_DTYPE), pool_scale[l],
                       conv_w[l], conv_b[l], w_ri, jnp.concatenate([lru_b_r[l], lru_b_i[l]]), lru_lambda[l])

        (wk_a, pe_a), (wk_b, pe_b) = _compress_weights(cmp_w_k[l], cmp_pe[l])
        (wv_a, _), (wv_b, _) = _compress_weights(cmp_w_v[l], cmp_pe[l])
        k_cmp = _compress(kc.reshape(b, n1, -1), pe_a, pe_b, wk_a, wk_b, False)
        v_cmp_t = _compress(vc.reshape(b, n1, -1), pe_a, pe_b, wv_a, wv_b, True)

        oc_t, mb_t = _cmp_attention(qt, k_cmp, v_cmp_t, n_sel, n_pad)
        y_t = _sel_win_attention(qt, mb_t, ks, key_onehot, vs_t, kw, vw_t, oc_t, gates_t)

        w_o = w_out[l].astype(MXU_DTYPE)
        split = pool_width + lru_width
        h = _outproj_ffn(h, ypl, y_t, w_o[:split], w_o[split:], mix_post_g[l], ffn2)
    return h
```

```python
import functools

import jax
import jax.numpy as jnp
from jax import lax
from jax.experimental import pallas as pl
from jax.experimental.pallas import tpu as pltpu

F32 = jnp.float32
MXU_DTYPE = jnp.bfloat16

POOL_WINDOWS = (2, 4, 8, 16)
POOL_GROUP = 64
LRU_C = 8.0
CONV_WIDTH = 4
HEAD_DIM = 64
N_KV_HEADS = 2
GQA_GROUP = 4
N_HEADS = N_KV_HEADS * GQA_GROUP
KV_WIDTH = N_KV_HEADS * HEAD_DIM
N_BRANCH = 3
CMP_LEN = 32
CMP_STRIDE = 16
SEL_BLOCK = 64
SEL_TOPK = 16
WINDOW = 512
ROPE_THETA = 10000.0
NORM_EPS = 1e-6
NEG_INF = -1e30
BIG_SCORE = 1e9
GELU_C0 = 0.7978845608028654
GELU_C1 = 0.044715
LOG2_E = 1.4426950408889634

V7X_LANES = 128
V7X_SUBLANES = 8
V7X_VMEM_LIMIT_BYTES = 56 * 1024 * 1024

ROW_TILE = 512
SEQ_TILE = 512
CMP_Q_TILE = 512
CMP_CHUNK = 128
SEL_TILE = 512
SEL_GROUP = 128
ONES_ROWS = 16
M_INIT = -1e20
STREAM_TILES = 4
EXP_HEADROOM = 64.0
Q_SCALE = HEAD_DIM ** -0.5 * LOG2_E


def _params(semantics):
    return pltpu.CompilerParams(dimension_semantics=semantics, vmem_limit_bytes=V7X_VMEM_LIMIT_BYTES)


def _rms(x, g):
    return x * lax.rsqrt(jnp.mean(x * x, axis=-1, keepdims=True) + NORM_EPS) * g


def _sigmoid(x):
    return 1.0 / (1.0 + jnp.exp(-x))


def _mm(a, b):
    return jnp.dot(a, b, preferred_element_type=F32)


def _mm_tn(a_t, b):
    return lax.dot_general(a_t, b, (((0,), (0,)), ((), ())), preferred_element_type=F32)


def _log2(n):
    assert n & (n - 1) == 0
    return n.bit_length() - 1


def _const_spec(shape):
    zeros = (0,) * len(shape)
    return pl.BlockSpec(shape, lambda *_: zeros, pipeline_mode=pl.Buffered(1))


def _rope_body(pos_ref, inv_ref, cos_ref, sin_ref):
    ang = pos_ref[0].astype(F32) * inv_ref[...]
    lane = lax.broadcasted_iota(jnp.int32, ang.shape, 1)
    cos_ref[0] = jnp.cos(ang)
    sin_ref[0] = jnp.where((lane & (HEAD_DIM - 1)) < HEAD_DIM // 2, -jnp.sin(ang), jnp.sin(ang))


def _rope_tables(positions):
    b, s = positions.shape
    inv = ROPE_THETA ** (-jnp.arange(0, HEAD_DIM, 2, dtype=F32) / HEAD_DIM)
    inv_row = jnp.tile(inv, V7X_LANES // (HEAD_DIM // 2))[None, :]
    ts = min(SEQ_TILE, s)
    out = jax.ShapeDtypeStruct((b, s, V7X_LANES), F32)
    return pl.pallas_call(
        _rope_body,
        grid=(b, s // ts),
        in_specs=[pl.BlockSpec((1, ts, 1), lambda i, j: (i, j, 0)),
                  pl.BlockSpec((1, V7X_LANES), lambda i, j: (0, 0))],
        out_specs=[pl.BlockSpec((1, ts, V7X_LANES), lambda i, j: (i, j, 0))] * 2,
        out_shape=[out, out],
        compiler_params=_params(("parallel", "parallel")),
        name="rope_tables",
    )(positions[:, :, None], inv_row)


def _ffn_math(h, ffn_refs):
    gpre_ref, gpost_ref, wg_ref, wu_ref, wd_ref = ffn_refs
    xn = _rms(h, gpre_ref[...]).astype(MXU_DTYPE)
    gate = _mm(xn, wg_ref[...])
    up = _mm(xn, wu_ref[...])
    act = (gate * _sigmoid(gate) * up).astype(MXU_DTYPE)
    f = _mm(act, wd_ref[...])
    return h + 0.5 * _rms(f, gpost_ref[...])


def _ffn_operands(g_pre, g_post, w_gate, w_up, w_down):
    d, dff = w_gate.shape
    specs = [_const_spec((1, d)), _const_spec((1, d)),
             _const_spec((d, dff)), _const_spec((d, dff)), _const_spec((dff, d))]
    return specs, (g_pre[None, :], g_post[None, :], w_gate, w_up, w_down)


N_FFN_OPERANDS = 5


def _ffn_body(h_ref, *refs):
    refs[N_FFN_OPERANDS][0] = _ffn_math(h_ref[0], refs[:N_FFN_OPERANDS])


def _ffn(h3, ffn_weights):
    b, s, d = h3.shape
    tm = min(ROW_TILE, s)
    row = pl.BlockSpec((1, tm, d), lambda i, j: (i, j, 0))
    ffn_specs, ffn_args = _ffn_operands(*ffn_weights)
    return pl.pallas_call(
        _ffn_body,
        grid=(b, s // tm),
        in_specs=[row] + ffn_specs,
        out_specs=row,
        out_shape=jax.ShapeDtypeStruct((b, s, d), F32),
        compiler_params=_params(("parallel", "parallel")),
        name="ffn",
    )(h3, *ffn_args)


def _swap_halves(x):
    n = x.shape[1]
    lane = lax.broadcasted_iota(jnp.int32, x.shape, 1)
    first_half = (lane & (HEAD_DIM - 1)) < HEAD_DIM // 2
    return jnp.where(first_half, pltpu.roll(x, n - HEAD_DIM // 2, 1), pltpu.roll(x, HEAD_DIM // 2, 1))


def _inproj_body(cols, h_ref, g_ref, w_ref, cos_ref, sin_ref,
                 xpl_ref, qt_ref, kc_ref, vc_ref, ks_ref, vst_ref, kw_ref, vwt_ref, gatet_ref):
    xn = _rms(h_ref[0], g_ref[...]).astype(MXU_DTYPE)
    proj = _mm(xn, w_ref[...])
    cos = cos_ref[0]
    sin = sin_ref[0]

    def rope(x):
        rep = x.shape[1] // V7X_LANES
        c = jnp.concatenate([cos] * rep, axis=1) if rep > 1 else cos
        s = jnp.concatenate([sin] * rep, axis=1) if rep > 1 else sin
        return x * c + _swap_halves(x) * s

    def seg(name):
        lo, hi = cols[name]
        return proj[:, lo:hi]

    xpl_ref[0] = seg("xpl")
    qt_ref[0] = (rope(seg("q")) * Q_SCALE).T.astype(qt_ref.dtype)
    kc_ref[0] = rope(seg("kc"))
    vc_ref[0] = seg("vc")
    ks_ref[0] = rope(seg("ks")).astype(ks_ref.dtype)
    vst_ref[0] = seg("vs").T.astype(vst_ref.dtype)
    kw_ref[0] = rope(seg("kw")).astype(kw_ref.dtype)
    vwt_ref[0] = seg("vw").T.astype(vwt_ref.dtype)
    gatet_ref[0] = _sigmoid(seg("g")).T[:gatet_ref.shape[1]]


def _inproj(h3, g_pre, w_all, cols, cos, sin):
    b, s, d = h3.shape
    tm = min(ROW_TILE, s)
    ncol = w_all.shape[1]
    width = lambda n: cols[n][1] - cols[n][0]
    row = lambda w: pl.BlockSpec((1, tm, w), lambda i, j: (i, j, 0))
    col = lambda w: pl.BlockSpec((1, w, tm), lambda i, j: (i, 0, j))
    tok = lambda n, dt: (jax.ShapeDtypeStruct((b, s, width(n)), dt), row(width(n)))
    chan = lambda w, dt: (jax.ShapeDtypeStruct((b, w, s), dt), col(w))
    outs = [tok("xpl", F32), chan(width("q"), MXU_DTYPE), tok("kc", F32), tok("vc", F32),
            tok("ks", MXU_DTYPE), chan(width("vs"), MXU_DTYPE), tok("kw", MXU_DTYPE),
            chan(width("vw"), MXU_DTYPE), chan(N_BRANCH * N_HEADS, F32)]
    return pl.pallas_call(
        functools.partial(_inproj_body, cols),
        grid=(b, s // tm),
        in_specs=[row(d), _const_spec((1, d)), _const_spec((d, ncol)), row(V7X_LANES), row(V7X_LANES)],
        out_specs=[o[1] for o in outs],
        out_shape=[o[0] for o in outs],
        compiler_params=_params(("parallel", "parallel")),
        name="mixer_in_proj",
    )(h3, g_pre[None, :], w_all, cos, sin)


def _poollru_body(x_ref, pw_ref, pscale_ref, cw_ref, cb_ref, wri_ref, bri_ref, lam_ref,
                  y_ref, pool_carry, conv_carry, h_carry):
    si = pl.program_id(1)
    ts = x_ref.shape[1]
    width = pw_ref.shape[0]
    halo_p = pool_carry.shape[0]
    halo_c = conv_carry.shape[0]

    @pl.when(si == 0)
    def _():
        pool_carry[...] = jnp.zeros_like(pool_carry)
        conv_carry[...] = jnp.zeros_like(conv_carry)
        h_carry[...] = jnp.zeros_like(h_carry)

    x = x_ref[0]
    xp = x[:, :width]
    xl = x[:, width:2 * width]
    gl = x[:, 2 * width:]

    ext = jnp.concatenate([pool_carry[...], xp], axis=0)
    sums = [ext]
    shift = 1
    for _ in POOL_WINDOWS:
        sums.append(sums[-1] + pltpu.roll(sums[-1], shift, 0))
        shift *= 2
    lane = lax.broadcasted_iota(jnp.int32, (1, width), 1)
    grp = lax.shift_right_logical(lane, _log2(POOL_GROUP))
    win_sum = sums[len(POOL_WINDOWS)]
    win = jnp.full((1, width), float(POOL_WINDOWS[-1]), F32)
    for gi in range(len(POOL_WINDOWS) - 2, -1, -1):
        win_sum = jnp.where(grp == gi, sums[gi + 1], win_sum)
        win = jnp.where(grp == gi, float(POOL_WINDOWS[gi]), win)
    win_sum = win_sum[halo_p:]
    t_abs = si * ts + lax.broadcasted_iota(jnp.int32, (ts, 1), 0)
    cnt = jnp.minimum((t_abs + 1).astype(F32), win)
    pooled = win_sum / cnt
    y_pool = _mm((pooled - xp).astype(MXU_DTYPE), pw_ref[...]) * pscale_ref[...]
    pool_carry[...] = xp[ts - halo_p:]

    extc = jnp.concatenate([conv_carry[...], xl], axis=0)
    xc = extc * cw_ref[CONV_WIDTH - 1:CONV_WIDTH, :]
    for k in range(1, CONV_WIDTH):
        xc = xc + pltpu.roll(extc, k, 0) * cw_ref[CONV_WIDTH - 1 - k:CONV_WIDTH - k, :]
    xc = xc[halo_c:] + cb_ref[...]
    conv_carry[...] = xl[ts - halo_c:]

    ri = _mm(xc.astype(MXU_DTYPE), wri_ref[...]) + bri_ref[...]
    r = _sigmoid(ri[:, :width])
    i_gate = _sigmoid(ri[:, width:])
    neg_lam = -lam_ref[...]
    softplus = jnp.maximum(neg_lam, 0.0) + jnp.log1p(jnp.exp(-jnp.abs(neg_lam)))
    log_a = -LRU_C * r * softplus
    a = jnp.exp(log_a)
    one_minus_a2 = -jnp.tanh(log_a) * (a * a + 1.0)
    mult = jnp.where(one_minus_a2 > 0.0, one_minus_a2 * lax.rsqrt(one_minus_a2), 0.0)
    b = mult * (i_gate * xc)

    row_in_group = lax.broadcasted_iota(jnp.int32, (ts, 1), 0) & (V7X_SUBLANES - 1)
    k = 1
    while k < V7X_SUBLANES:
        keep = row_in_group >= k
        a_prev = jnp.where(keep, pltpu.roll(a, k, 0), 1.0)
        b_prev = jnp.where(keep, pltpu.roll(b, k, 0), 0.0)
        b = a * b_prev + b
        a = a * a_prev
        k *= 2
    h_prev = h_carry[0:1, :]
    groups = []
    for g in range(ts // V7X_SUBLANES):
        rows_g = slice(g * V7X_SUBLANES, (g + 1) * V7X_SUBLANES)
        h_g = a[rows_g] * h_prev + b[rows_g]
        groups.append(h_g)
        h_prev = h_g[V7X_SUBLANES - 1:V7X_SUBLANES]
    h = jnp.concatenate(groups, axis=0)
    h_carry[...] = jnp.broadcast_to(h_prev, h_carry.shape)

    gelu = 0.5 * gl * (1.0 + jnp.tanh(GELU_C0 * (gl + GELU_C1 * gl * gl * gl)))
    y_ref[0] = jnp.concatenate([y_pool, h * gelu], axis=1).astype(y_ref.dtype)


def _poollru(xpl, pool_w_bd, pool_scale, conv_w, conv_b, w_ri_bd, b_ri, lam):
    b, s, w3 = xpl.shape
    width = w3 // 3
    ts = min(SEQ_TILE, s)
    return pl.pallas_call(
        _poollru_body,
        grid=(b, s // ts),
        in_specs=[pl.BlockSpec((1, ts, w3), lambda i, j: (i, j, 0)),
                  _const_spec((width, width)), _const_spec((1, width)),
                  _const_spec((CONV_WIDTH, width)), _const_spec((1, width)),
                  _const_spec((width, 2 * width)), _const_spec((1, 2 * width)), _const_spec((1, width))],
        out_specs=pl.BlockSpec((1, ts, 2 * width), lambda i, j: (i, j, 0)),
        out_shape=jax.ShapeDtypeStruct((b, s, 2 * width), MXU_DTYPE),
        scratch_shapes=[pltpu.VMEM((POOL_WINDOWS[-1], width), F32),
                        pltpu.VMEM((V7X_SUBLANES, width), F32),
                        pltpu.VMEM((V7X_SUBLANES, width), F32)],
        compiler_params=_params(("parallel", "arbitrary")),
        name="pool_rglru",
    )(xpl, pool_w_bd, pool_scale[None, :], conv_w, conv_b[None, :], w_ri_bd, b_ri[None, :], lam[None, :])


def _compress_body(transpose_out, x_ref, pe_ref, w_ref, o_ref):
    n1 = x_ref.shape[1] // CMP_STRIDE
    first = jnp.zeros((n1, w_ref.shape[2]), F32)
    second = first
    for l in range(CMP_STRIDE):
        x = x_ref[0, pl.ds(l, n1, stride=CMP_STRIDE), :]
        first = first + _mm((x + pe_ref[l]).astype(MXU_DTYPE), w_ref[l])
        second = second + _mm((x + pe_ref[l + CMP_STRIDE]).astype(MXU_DTYPE), w_ref[l + CMP_STRIDE])
    out = first + pltpu.roll(second, n1 - 1, 0)
    o_ref[0] = (out.T if transpose_out else out).astype(o_ref.dtype)


def _compress(x, pe_rows, w_bd, transpose_out):
    b, s, width = x.shape
    n1 = s // CMP_STRIDE
    out_dims = (width, n1) if transpose_out else (n1, width)
    return pl.pallas_call(
        functools.partial(_compress_body, transpose_out),
        grid=(b,),
        in_specs=[pl.BlockSpec((1, s, width), lambda i: (i, 0, 0)),
                  _const_spec(pe_rows.shape), _const_spec(w_bd.shape)],
        out_specs=pl.BlockSpec((1,) + out_dims, lambda i: (i, 0, 0)),
        out_shape=jax.ShapeDtypeStruct((b,) + out_dims, MXU_DTYPE),
        compiler_params=_params(("parallel",)),
        name="compress_kv",
    )(x, pe_rows, w_bd)


def _stack_heads(q_ref):
    return jnp.concatenate([q_ref[0, g * HEAD_DIM:(g + 1) * HEAD_DIM, :] for g in range(GQA_GROUP)], axis=1)


def _place_in_kv_half(qt, kvh):
    zeros = jnp.zeros_like(qt)
    parts = [jnp.where(kvh == h, qt, zeros) for h in range(N_KV_HEADS)]
    return jnp.concatenate(parts, axis=0)


def _query_positions(start, tq):
    lane = lax.broadcasted_iota(jnp.int32, (1, GQA_GROUP * tq), 1)
    return start + (lane & (tq - 1))


def _with_ones(v_t):
    return jnp.concatenate([v_t, jnp.ones((ONES_ROWS, v_t.shape[1]), v_t.dtype)], axis=0)


def _denominator(acc):
    return acc[HEAD_DIM:HEAD_DIM + 1]


def _normalize(acc):
    return acc[:HEAD_DIM] * (1.0 / _denominator(acc))


def _cmp_body(n_sel, q_ref, kc_ref, vct_ref, oc_ref, mb_ref, band_ref):
    tq = q_ref.shape[2]
    kvh = pl.program_id(1)
    start = pl.program_id(2) * tq
    qt = _stack_heads(q_ref)
    qp = _place_in_kv_half(qt, kvh)
    n1 = kc_ref.shape[1]
    t_row = _query_positions(start, tq)
    assert CMP_STRIDE * CMP_CHUNK >= tq + CMP_LEN

    def attend(nrows):
        s = _mm(kc_ref[0, :nrows, :], qp)
        lo = max(0, nrows - 2 * CMP_CHUNK)
        n_idx = lo + lax.broadcasted_iota(jnp.int32, (nrows - lo, 1), 0)
        tail = jnp.where(n_idx * CMP_STRIDE + (CMP_LEN - 1) <= t_row, s[lo:], NEG_INF)
        s = jnp.concatenate([s[:lo], tail], axis=0) if lo else tail
        m = jnp.max(s, axis=0, keepdims=True)
        p = jnp.exp2(s - jnp.where(m > 0.5 * NEG_INF, m, 0.0))
        den = jnp.sum(p, axis=0, keepdims=True)
        pn = p * (1.0 / jnp.where(den > 0.0, den, 1.0))
        oc = _mm(vct_ref[0, :, :nrows], pn.astype(MXU_DTYPE))
        psum = pn[:, 0:tq]
        for g in range(GQA_GROUP):
            oc_ref[0, g * HEAD_DIM:(g + 1) * HEAD_DIM, :] = oc[:, g * tq:(g + 1) * tq]
            if g:
                psum = psum + pn[:, g * tq:(g + 1) * tq]
        n_blk = min(n_sel, nrows * CMP_STRIDE // SEL_BLOCK)
        step, lead, pad = SEL_BLOCK // CMP_STRIDE, CMP_LEN // CMP_STRIDE - 1, V7X_SUBLANES
        cols = []
        for c in range(tq // V7X_LANES):
            band_ref[c, 0:pad, :] = jnp.zeros((pad, V7X_LANES), F32)
            band_ref[c, pad:pad + nrows, :] = psum[:, c * V7X_LANES:(c + 1) * V7X_LANES]
            taps = [band_ref[c, pl.ds(pad - lead + k, n_blk, stride=step), :] for k in range(step + lead)]
            cols.append(functools.reduce(lambda a, b: a + b, taps))
        select(jnp.concatenate(cols, axis=1), n_blk)

    def select(imp, n_blk):
        j = lax.broadcasted_iota(jnp.int32, (n_blk, 1), 0).astype(F32)
        t_q = start + lax.broadcasted_iota(jnp.int32, (1, tq), 1)
        cur = lax.shift_right_logical(t_q, _log2(SEL_BLOCK)).astype(F32)
        forced = (j == 0.0) | (j == cur) | (j == cur - 1.0)
        valid = j <= cur
        base = jnp.where(valid, jnp.where(forced, BIG_SCORE, imp), -BIG_SCORE)
        n_pad = mb_ref.shape[2]
        if n_pad > n_blk:
            mb_ref[0, 0, n_blk:, :] = jnp.full((n_pad - n_blk, tq), NEG_INF, mb_ref.dtype)

        score = jnp.where(forced, -jnp.inf, base)
        for _ in range(SEL_TOPK - 3):
            score = jnp.where(score == jnp.max(score, axis=0, keepdims=True), -jnp.inf, score)
        bias = jnp.where(score == -jnp.inf, 0.0, NEG_INF)
        mb_ref[0, 0, :n_blk, :] = bias.astype(mb_ref.dtype)
        taken = jnp.sum(jnp.where(valid, jnp.where(bias == 0.0, 1.0, 0.0), 0.0), axis=0, keepdims=True)
        tie = jnp.max(jnp.abs(taken - jnp.minimum(cur + 1.0, float(SEL_TOPK)))) > 0.0

        @pl.when(tie)
        def _():
            score = base
            bias = jnp.full(score.shape, NEG_INF, F32)
            for _ in range(SEL_TOPK):
                best = jnp.max(score, axis=0, keepdims=True)
                first = jnp.min(jnp.where(score == best, j, float(n_blk)), axis=0, keepdims=True)
                pick = j == first
                bias = jnp.where(pick, 0.0, bias)
                score = jnp.where(pick, -jnp.inf, score)
            mb_ref[0, 0, :n_blk, :] = bias.astype(mb_ref.dtype)

    assert n_sel >= SEL_TOPK and tq % SEL_BLOCK == 0
    n_need = lax.shift_right_logical(start + tq, _log2(CMP_STRIDE)) - 1
    chunks = lax.shift_right_logical(n_need + CMP_CHUNK - 1, _log2(CMP_CHUNK))
    n_variants = -(-n1 // CMP_CHUNK)
    for c in range(1, n_variants + 1):
        pl.when(chunks == c)(functools.partial(attend, min(c * CMP_CHUNK, n1)))


def _cmp_attention(qt, k_cmp, v_cmp_t, n_sel, n_pad):
    b, _, s = qt.shape
    n1 = k_cmp.shape[1]
    tq = min(CMP_Q_TILE, s)
    group_rows = GQA_GROUP * HEAD_DIM
    head_blk = pl.BlockSpec((1, group_rows, tq), lambda i, h, j: (i, h, j))
    return pl.pallas_call(
        functools.partial(_cmp_body, n_sel),
        grid=(b, N_KV_HEADS, s // tq),
        in_specs=[head_blk,
                  pl.BlockSpec((1, n1, KV_WIDTH), lambda i, h, j: (i, 0, 0)),
                  pl.BlockSpec((1, HEAD_DIM, n1), lambda i, h, j: (i, h, 0))],
        out_specs=[head_blk, pl.BlockSpec((1, 1, n_pad, tq), lambda i, h, j: (i, h, 0, j))],
        out_shape=[jax.ShapeDtypeStruct((b, N_HEADS * HEAD_DIM, s), F32),
                   jax.ShapeDtypeStruct((b, N_KV_HEADS, n_pad, s), MXU_DTYPE)],
        scratch_shapes=[pltpu.VMEM((tq // V7X_LANES, n1 + V7X_SUBLANES, V7X_LANES), F32)],
        compiler_params=_params(("parallel", "parallel", "parallel")),
        name="cmp_attention_select",
    )(qt, k_cmp, v_cmp_t)


def _selwin_body(q_ref, mb_ref, ks_ref, oh_ref, vst_ref, kw_ref, vwt_ref, oc_ref, gate_ref, y_ref,
                 qa_ref, acc_ref, win_ref):
    tq = q_ref.shape[2]
    tk = tq
    kvh = pl.program_id(1)
    qi = pl.program_id(2)
    start = qi * tq
    rows = GQA_GROUP * tq
    qt = _stack_heads(q_ref)
    qp = _place_in_kv_half(qt, kvh)
    t_row = _query_positions(start, tq)
    group_shift = _log2(SEL_GROUP * SEL_BLOCK // tk)
    n_groups = mb_ref.shape[2] // SEL_GROUP

    for grp in range(n_groups):
        bias_rows = mb_ref[0, 0, grp * SEL_GROUP:(grp + 1) * SEL_GROUP, :]
        qa_ref[grp] = jnp.concatenate([jnp.concatenate([bias_rows] * GQA_GROUP, axis=1), qp], axis=0)

    def scores(kt, n_sub, causal):
        key_rows = pl.ds(pl.multiple_of(kt * tk, tk), n_sub * tk)
        oh_rows = pl.ds(pl.multiple_of(lax.rem(kt * tk, oh_ref.shape[0]), tk), n_sub * tk)
        k = jnp.concatenate([oh_ref[oh_rows, :], ks_ref[0, key_rows, :]], axis=1)
        s = _mm(k, qa_ref[lax.shift_right_logical(kt, group_shift)])
        if causal:
            kp = kt * tk + lax.broadcasted_iota(jnp.int32, (n_sub * tk, 1), 0)
            s = jnp.where(kp <= t_row, s, NEG_INF)
        return s

    def values(kt, n_sub):
        return vst_ref[0, :, pl.ds(pl.multiple_of(kt * tk, tk), n_sub * tk)]

    def col_max(s):
        return jnp.max(s, axis=0, keepdims=True)

    n_past = qi

    assert tq == WINDOW
    own_rows = pl.ds(pl.multiple_of(start, tq), tq)
    first_tile = qi == 0
    win_rows = pl.ds(pl.multiple_of(jnp.maximum(start - WINDOW, 0), tq), WINDOW + tq)

    def window_scores():
        s = _mm(kw_ref[0, win_rows, :], qp)
        kp = jnp.maximum(start - WINDOW, 0) + lax.broadcasted_iota(jnp.int32, (WINDOW + tq, 1), 0)
        causal_lhs, causal_rhs = -kp, -(t_row + 1)
        lhs = jnp.where(first_tile, causal_lhs[:WINDOW], kp[:WINDOW])
        rhs = jnp.where(first_tile, causal_rhs, t_row - WINDOW)
        return jnp.concatenate([jnp.where(lhs > rhs, s[:WINDOW], NEG_INF),
                                jnp.where(causal_lhs[WINDOW:] > causal_rhs, s[WINDOW:], NEG_INF)], axis=0)

    def pv(v_t, s, ref):
        return _mm(_with_ones(v_t), jnp.exp2(s - ref).astype(MXU_DTYPE))

    def self_scores(k_ref):
        k_t = k_ref[0, own_rows, :].astype(F32).T
        k_own = k_t[:HEAD_DIM]
        for h in range(1, N_KV_HEADS):
            k_own = jnp.where(kvh == h, k_t[h * HEAD_DIM:(h + 1) * HEAD_DIM], k_own)
        parts = [jnp.sum(qt[:, g * tq:(g + 1) * tq].astype(F32) * k_own, axis=0, keepdims=True)
                 for g in range(GQA_GROUP)]
        return jnp.concatenate(parts, axis=1)

    ref = self_scores(ks_ref)
    ref_w = self_scores(kw_ref)
    win_ref[...] = pv(vwt_ref[0, :, win_rows], window_scores(), ref_w)

    def stream(kt, n_sub, causal=False):
        acc_ref[...] += pv(values(kt, n_sub), scores(kt, n_sub, causal), ref)

    acc_ref[...] = jnp.zeros_like(acc_ref)
    n_big = lax.shift_right_logical(n_past, _log2(STREAM_TILES))
    lax.fori_loop(0, n_big, lambda i, c: (stream(i * STREAM_TILES, STREAM_TILES), c)[1], 0)
    done = n_big * STREAM_TILES
    part = STREAM_TILES // 2
    while part >= 1:
        pl.when((n_past & part) != 0)(functools.partial(stream, done, part))
        done = done + (n_past & part)
        part //= 2
    stream(qi, 1, causal=True)
    limit = 2.0 ** EXP_HEADROOM
    in_range = jnp.where(_denominator(acc_ref) < limit, jnp.where(_denominator(win_ref) < limit, 1.0, 0.0), 0.0)
    overflow = jnp.min(in_range) < 1.0

    @pl.when(overflow)
    def _():
        def tile(kt, causal):
            s = scores(kt, 1, causal)
            m_t = col_max(s)
            return m_t, pv(values(kt, 1), s, m_t)

        def merge(carry, part):
            m, acc = carry
            m_t, o_t = part
            m_new = jnp.maximum(m, m_t)
            return m_new, acc * jnp.exp2(m - m_new) + o_t * jnp.exp2(m_t - m_new)

        carry = (jnp.full((1, rows), M_INIT, F32), jnp.zeros(acc_ref.shape, F32))
        carry = lax.fori_loop(0, n_past, lambda i, c: merge(c, tile(i, False)), carry)
        acc_ref[...] = merge(carry, tile(qi, True))[1]

        x_win = window_scores()
        win_ref[...] = pv(vwt_ref[0, :, win_rows], x_win, col_max(x_win))

    o_sel = _normalize(acc_ref[...])
    o_win = _normalize(win_ref[...])

    gate_row = kvh * (GQA_GROUP * N_BRANCH)
    for g in range(GQA_GROUP):
        cols = slice(g * tq, (g + 1) * tq)
        head = slice(g * HEAD_DIM, (g + 1) * HEAD_DIM)
        gates = [gate_ref[0, pl.ds(gate_row + N_BRANCH * g + br, 1), :] for br in range(N_BRANCH)]
        y = gates[0] * oc_ref[0, head, :] + gates[1] * o_sel[:, cols] + gates[2] * o_win[:, cols]
        y_ref[0, head, :] = y.astype(y_ref.dtype)


def _sel_win_attention(qt, mb_t, ks, key_onehot, vs_t, kw, vw_t, oc_t, gates_t):
    b, _, s = qt.shape
    tq = min(SEL_TILE, s)
    n_pad = mb_t.shape[2]
    n_gate = gates_t.shape[1]
    group_rows = GQA_GROUP * HEAD_DIM
    head_blk = pl.BlockSpec((1, group_rows, tq), lambda i, h, j: (i, h, j))
    once = pl.Buffered(1)
    keys = pl.BlockSpec((1, s, KV_WIDTH), lambda i, h, j: (i, 0, 0), pipeline_mode=once)
    vals = pl.BlockSpec((1, HEAD_DIM, s), lambda i, h, j: (i, h, 0), pipeline_mode=once)
    acc = pltpu.VMEM((HEAD_DIM + ONES_ROWS, GQA_GROUP * tq), F32)
    return pl.pallas_call(
        _selwin_body,
        grid=(b, N_KV_HEADS, s // tq),
        in_specs=[head_blk,
                  pl.BlockSpec((1, 1, n_pad, tq), lambda i, h, j: (i, h, 0, j)),
                  keys, _const_spec(key_onehot.shape), vals, keys, vals,
                  head_blk,
                  pl.BlockSpec((1, n_gate, tq), lambda i, h, j: (i, 0, j))],
        out_specs=head_blk,
        out_shape=jax.ShapeDtypeStruct((b, N_HEADS * HEAD_DIM, s), MXU_DTYPE),
        scratch_shapes=[pltpu.VMEM((n_pad // SEL_GROUP, SEL_GROUP + KV_WIDTH, GQA_GROUP * tq), MXU_DTYPE),
                        acc, acc],
        compiler_params=_params(("parallel", "parallel", "arbitrary")),
        name="sel_win_attention",
    )(qt, mb_t, ks, key_onehot, vs_t, kw, vw_t, oc_t, gates_t)


def _outproj_ffn_body(h_ref, ypl_ref, yat_ref, w1_ref, w2_ref, g_ref, *refs):
    ffn_refs, o_ref = refs[:N_FFN_OPERANDS], refs[N_FFN_OPERANDS]
    m = _mm(ypl_ref[0], w1_ref[...]) + _mm_tn(yat_ref[0], w2_ref[...])
    o_ref[0] = _ffn_math(h_ref[0] + _rms(m, g_ref[...]), ffn_refs)


def _outproj_ffn(h3, ypl, yat_t, w1, w2, g_post, ffn_weights):
    b, s, d = h3.shape
    tm = min(ROW_TILE, s)
    row = lambda w: pl.BlockSpec((1, tm, w), lambda i, j: (i, j, 0))
    ffn_specs, ffn_args = _ffn_operands(*ffn_weights)
    return pl.pallas_call(
        _outproj_ffn_body,
        grid=(b, s // tm),
        in_specs=[row(d), row(ypl.shape[2]),
                  pl.BlockSpec((1, yat_t.shape[1], tm), lambda i, j: (i, 0, j)),
                  _const_spec(w1.shape), _const_spec(w2.shape), _const_spec((1, d))] + ffn_specs,
        out_specs=row(d),
        out_shape=jax.ShapeDtypeStruct((b, s, d), F32),
        compiler_params=_params(("parallel", "parallel")),
        name="mixer_out_proj_ffn",
    )(h3, ypl, yat_t, w1, w2, g_post[None, :], *ffn_args)


def _block_diag(blocks):
    n, a, b = blocks.shape
    eye = jnp.eye(n, dtype=blocks.dtype)
    return jnp.einsum("nab,nm->namb", blocks, eye).reshape(n * a, n * b)


def _compress_weights(w):
    eye = jnp.eye(N_KV_HEADS, dtype=w.dtype)
    return jnp.einsum("lde,hg->lhdge", w, eye).reshape(CMP_LEN, KV_WIDTH, KV_WIDTH).astype(MXU_DTYPE)


def kernel(x, positions, ffn1_pre_g, ffn1_post_g, ffn1_w_gate, ffn1_w_up, ffn1_w_down, mix_pre_g, mix_post_g,
           w_in, w_out, pool_w, pool_scale, conv_w, conv_b, lru_w_r, lru_b_r, lru_w_i, lru_b_i, lru_lambda,
           cmp_w_k, cmp_w_v, cmp_pe, ffn2_pre_g, ffn2_post_g, ffn2_w_gate, ffn2_w_up, ffn2_w_down):
    b, s, d = x.shape
    depth = w_in.shape[0]
    t = b * s
    pool_width = pool_w.shape[1] * pool_w.shape[2]
    lru_width = lru_w_r.shape[1] * lru_w_r.shape[2]
    attn_width = N_HEADS * HEAD_DIM
    assert pool_width == lru_width and s % SEL_TILE == 0 and s % (CMP_STRIDE * 8) == 0
    assert WINDOW % SEL_TILE == 0 and (SEL_GROUP * SEL_BLOCK) % (STREAM_TILES * SEL_TILE) == 0
    assert s >= WINDOW + SEL_TILE

    sizes = [("xpl", pool_width + 2 * lru_width), ("q", attn_width), ("kc", KV_WIDTH), ("vc", KV_WIDTH),
             ("ks", KV_WIDTH), ("vs", KV_WIDTH), ("kw", KV_WIDTH), ("vw", KV_WIDTH), ("g", V7X_LANES)]
    cols, off = {}, 0
    for name, width in sizes:
        cols[name] = (off, off + width)
        off += width

    n_sel = s // SEL_BLOCK
    n_pad = -(-n_sel // SEL_GROUP) * SEL_GROUP
    n1 = s // CMP_STRIDE
    key_blk = jnp.arange(min(s, SEL_GROUP * SEL_BLOCK)) // SEL_BLOCK
    key_onehot = (key_blk[:, None] == jnp.arange(SEL_GROUP)[None, :]).astype(MXU_DTYPE)

    cos, sin = _rope_tables(positions)

    h = x
    for l in range(depth):
        cast = lambda w: w[l].astype(MXU_DTYPE)
        ffn1 = (ffn1_pre_g[l], ffn1_post_g[l], cast(ffn1_w_gate), cast(ffn1_w_up), cast(ffn1_w_down))
        ffn2 = (ffn2_pre_g[l], ffn2_post_g[l], cast(ffn2_w_gate), cast(ffn2_w_up), cast(ffn2_w_down))

        w_all = jnp.pad(w_in[l], ((0, 0), (0, off - w_in.shape[2]))).astype(MXU_DTYPE)
        h = _ffn(h, ffn1)
        xpl, qt, kc, vc, ks, vs_t, kw, vw_t, gates_t = _inproj(h, mix_pre_g[l], w_all, cols, cos, sin)

        w_ri = jnp.concatenate([_block_diag(lru_w_r[l]), _block_diag(lru_w_i[l])], axis=1).astype(MXU_DTYPE)
        ypl = _poollru(xpl, _block_diag(pool_w[l]).astype(MXU_DTYPE), pool_scale[l],
                       conv_w[l], conv_b[l], w_ri, jnp.concatenate([lru_b_r[l], lru_b_i[l]]), lru_lambda[l])

        pe_rows = jnp.tile(cmp_pe[l], (1, N_KV_HEADS))[:, None, :]
        k_cmp = _compress(kc, pe_rows, _compress_weights(cmp_w_k[l]), False)
        v_cmp_t = _compress(vc, pe_rows, _compress_weights(cmp_w_v[l]), True)

        oc_t, mb_t = _cmp_attention(qt, k_cmp, v_cmp_t, n_sel, n_pad)
        y_t = _sel_win_attention(qt, mb_t, ks, key_onehot, vs_t, kw, vw_t, oc_t, gates_t)

        w_o = w_out[l].astype(MXU_DTYPE)
        split = pool_width + lru_width
        h = _outproj_ffn(h, ypl, y_t, w_o[:split], w_o[split:], mix_post_g[l], ffn2)
    return h
```

```python
import functools

import jax
import jax.numpy as jnp
from jax import lax
from jax.experimental import pallas as pl
from jax.experimental.pallas import tpu as pltpu

F32 = jnp.float32
MXU_DTYPE = jnp.bfloat16

POOL_WINDOWS = (2, 4, 8, 16)
POOL_GROUP = 64
LRU_C = 8.0
CONV_WIDTH = 4
HEAD_DIM = 64
N_KV_HEADS = 2
GQA_GROUP = 4
N_HEADS = N_KV_HEADS * GQA_GROUP
KV_WIDTH = N_KV_HEADS * HEAD_DIM
N_BRANCH = 3
CMP_LEN = 32
CMP_STRIDE = 16
SEL_BLOCK = 64
SEL_TOPK = 16
WINDOW = 512
ROPE_THETA = 10000.0
NORM_EPS = 1e-6
NEG_INF = -1e30
BIG_SCORE = 1e9
GELU_C0 = 0.7978845608028654
GELU_C1 = 0.044715
LOG2_E = 1.4426950408889634

V7X_LANES = 128
V7X_SUBLANES = 8
V7X_VMEM_LIMIT_BYTES = 56 * 1024 * 1024

ROW_TILE = 512
SEQ_TILE = 512
CMP_Q_TILE = 512
CMP_CHUNK = 128
SEL_TILE = 512
SEL_GROUP = 128
ONES_ROWS = 16
M_INIT = -1e20
STREAM_TILES = 4
EXP_HEADROOM = 64.0
Q_SCALE = HEAD_DIM ** -0.5 * LOG2_E


def _params(semantics):
    return pltpu.CompilerParams(dimension_semantics=semantics, vmem_limit_bytes=V7X_VMEM_LIMIT_BYTES)


def _rms(x, g):
    return x * lax.rsqrt(jnp.mean(x * x, axis=-1, keepdims=True) + NORM_EPS) * g


def _sigmoid(x):
    return 1.0 / (1.0 + jnp.exp(-x))


def _mm(a, b):
    return jnp.dot(a, b, preferred_element_type=F32)


def _mm_tn(a_t, b):
    return lax.dot_general(a_t, b, (((0,), (0,)), ((), ())), preferred_element_type=F32)


def _log2(n):
    assert n & (n - 1) == 0
    return n.bit_length() - 1


def _const_spec(shape):
    zeros = (0,) * len(shape)
    return pl.BlockSpec(shape, lambda *_: zeros, pipeline_mode=pl.Buffered(1))


def _layer_spec(stacked, layer):
    index = (layer,) + (0,) * (stacked.ndim - 1)
    return pl.BlockSpec((None,) + stacked.shape[1:], lambda *_: index, pipeline_mode=pl.Buffered(1))


def _rope_body(pos_ref, inv_ref, cos_ref, sin_ref):
    ang = pos_ref[0].astype(F32) * inv_ref[...]
    lane = lax.broadcasted_iota(jnp.int32, ang.shape, 1)
    cos_ref[0] = jnp.cos(ang)
    sin_ref[0] = jnp.where((lane & (HEAD_DIM - 1)) < HEAD_DIM // 2, -jnp.sin(ang), jnp.sin(ang))


def _rope_tables(positions):
    b, s = positions.shape
    inv = ROPE_THETA ** (-jnp.arange(0, HEAD_DIM, 2, dtype=F32) / HEAD_DIM)
    inv_row = jnp.tile(inv, V7X_LANES // (HEAD_DIM // 2))[None, :]
    ts = min(SEQ_TILE, s)
    out = jax.ShapeDtypeStruct((b, s, V7X_LANES), F32)
    return pl.pallas_call(
        _rope_body,
        grid=(b, s // ts),
        in_specs=[pl.BlockSpec((1, ts, 1), lambda i, j: (i, j, 0)),
                  pl.BlockSpec((1, V7X_LANES), lambda i, j: (0, 0))],
        out_specs=[pl.BlockSpec((1, ts, V7X_LANES), lambda i, j: (i, j, 0))] * 2,
        out_shape=[out, out],
        compiler_params=_params(("parallel", "parallel")),
        name="rope_tables",
    )(positions[:, :, None], inv_row)


def _ffn_math(h, ffn_refs):
    gpre_ref, gpost_ref, wg_ref, wu_ref, wd_ref = ffn_refs
    xn = _rms(h, gpre_ref[...]).astype(MXU_DTYPE)
    gate = _mm(xn, wg_ref[...])
    up = _mm(xn, wu_ref[...])
    act = (gate * _sigmoid(gate) * up).astype(MXU_DTYPE)
    f = _mm(act, wd_ref[...])
    return h + 0.5 * _rms(f, gpost_ref[...])


def _ffn_operands(layer, g_pre, g_post, w_gate, w_up, w_down):
    args = (g_pre, g_post, w_gate, w_up, w_down)
    return [_layer_spec(a, layer) for a in args], args


N_FFN_OPERANDS = 5


def _ffn_body(h_ref, *refs):
    refs[N_FFN_OPERANDS][0] = _ffn_math(h_ref[0], refs[:N_FFN_OPERANDS])


def _ffn(layer, h3, ffn_weights):
    b, s, d = h3.shape
    tm = min(ROW_TILE, s)
    row = pl.BlockSpec((1, tm, d), lambda i, j: (i, j, 0))
    ffn_specs, ffn_args = _ffn_operands(layer, *ffn_weights)
    return pl.pallas_call(
        _ffn_body,
        grid=(b, s // tm),
        in_specs=[row] + ffn_specs,
        out_specs=row,
        out_shape=jax.ShapeDtypeStruct((b, s, d), F32),
        compiler_params=_params(("parallel", "parallel")),
        name="ffn",
    )(h3, *ffn_args)


def _swap_halves(x):
    n = x.shape[1]
    lane = lax.broadcasted_iota(jnp.int32, x.shape, 1)
    first_half = (lane & (HEAD_DIM - 1)) < HEAD_DIM // 2
    return jnp.where(first_half, pltpu.roll(x, n - HEAD_DIM // 2, 1), pltpu.roll(x, HEAD_DIM // 2, 1))


def _inproj_body(cols, h_ref, g_ref, w_ref, cos_ref, sin_ref,
                 xpl_ref, qt_ref, kc_ref, vc_ref, ks_ref, vst_ref, kw_ref, vwt_ref, gatet_ref):
    xn = _rms(h_ref[0], g_ref[...]).astype(MXU_DTYPE)
    proj = _mm(xn, w_ref[...])
    cos = cos_ref[0]
    sin = sin_ref[0]

    def rope(x):
        rep = x.shape[1] // V7X_LANES
        c = jnp.concatenate([cos] * rep, axis=1) if rep > 1 else cos
        s = jnp.concatenate([sin] * rep, axis=1) if rep > 1 else sin
        return x * c + _swap_halves(x) * s

    def seg(name):
        lo, hi = cols[name]
        return proj[:, lo:hi]

    xpl_ref[0] = seg("xpl")
    qt_ref[0] = (rope(seg("q")) * Q_SCALE).T.astype(qt_ref.dtype)
    kc_ref[0] = rope(seg("kc"))
    vc_ref[0] = seg("vc")
    ks_ref[0] = rope(seg("ks")).astype(ks_ref.dtype)
    vst_ref[0] = seg("vs").T.astype(vst_ref.dtype)
    kw_ref[0] = rope(seg("kw")).astype(kw_ref.dtype)
    vwt_ref[0] = seg("vw").T.astype(vwt_ref.dtype)
    gatet_ref[0] = _sigmoid(seg("g")).T[:gatet_ref.shape[1]]


def _inproj(layer, h3, g_pre, w_all, cols, cos, sin):
    b, s, d = h3.shape
    tm = min(ROW_TILE, s)
    width = lambda n: cols[n][1] - cols[n][0]
    row = lambda w: pl.BlockSpec((1, tm, w), lambda i, j: (i, j, 0))
    col = lambda w: pl.BlockSpec((1, w, tm), lambda i, j: (i, 0, j))
    tok = lambda n, dt: (jax.ShapeDtypeStruct((b, s, width(n)), dt), row(width(n)))
    chan = lambda w, dt: (jax.ShapeDtypeStruct((b, w, s), dt), col(w))
    outs = [tok("xpl", F32), chan(width("q"), MXU_DTYPE), tok("kc", F32), tok("vc", F32),
            tok("ks", MXU_DTYPE), chan(width("vs"), MXU_DTYPE), tok("kw", MXU_DTYPE),
            chan(width("vw"), MXU_DTYPE), chan(N_BRANCH * N_HEADS, F32)]
    return pl.pallas_call(
        functools.partial(_inproj_body, cols),
        grid=(b, s // tm),
        in_specs=[row(d), _layer_spec(g_pre, layer), _layer_spec(w_all, layer), row(V7X_LANES), row(V7X_LANES)],
        out_specs=[o[1] for o in outs],
        out_shape=[o[0] for o in outs],
        compiler_params=_params(("parallel", "parallel")),
        name="mixer_in_proj",
    )(h3, g_pre, w_all, cos, sin)


def _poollru_body(x_ref, pw_ref, pscale_ref, cw_ref, cb_ref, wri_ref, bri_ref, lam_ref,
                  y_ref, pool_carry, conv_carry, h_carry):
    si = pl.program_id(1)
    ts = x_ref.shape[1]
    width = pw_ref.shape[0]
    halo_p = pool_carry.shape[0]
    halo_c = conv_carry.shape[0]

    @pl.when(si == 0)
    def _():
        pool_carry[...] = jnp.zeros_like(pool_carry)
        conv_carry[...] = jnp.zeros_like(conv_carry)
        h_carry[...] = jnp.zeros_like(h_carry)

    x = x_ref[0]
    xp = x[:, :width]
    xl = x[:, width:2 * width]
    gl = x[:, 2 * width:]

    ext = jnp.concatenate([pool_carry[...], xp], axis=0)
    sums = [ext]
    shift = 1
    for _ in POOL_WINDOWS:
        sums.append(sums[-1] + pltpu.roll(sums[-1], shift, 0))
        shift *= 2
    lane = lax.broadcasted_iota(jnp.int32, (1, width), 1)
    grp = lax.shift_right_logical(lane, _log2(POOL_GROUP))
    win_sum = sums[len(POOL_WINDOWS)]
    win = jnp.full((1, width), float(POOL_WINDOWS[-1]), F32)
    for gi in range(len(POOL_WINDOWS) - 2, -1, -1):
        win_sum = jnp.where(grp == gi, sums[gi + 1], win_sum)
        win = jnp.where(grp == gi, float(POOL_WINDOWS[gi]), win)
    win_sum = win_sum[halo_p:]
    t_abs = si * ts + lax.broadcasted_iota(jnp.int32, (ts, 1), 0)
    cnt = jnp.minimum((t_abs + 1).astype(F32), win)
    pooled = win_sum / cnt
    y_pool = _mm((pooled - xp).astype(MXU_DTYPE), pw_ref[...]) * pscale_ref[...]
    pool_carry[...] = xp[ts - halo_p:]

    extc = jnp.concatenate([conv_carry[...], xl], axis=0)
    xc = extc * cw_ref[CONV_WIDTH - 1:CONV_WIDTH, :]
    for k in range(1, CONV_WIDTH):
        xc = xc + pltpu.roll(extc, k, 0) * cw_ref[CONV_WIDTH - 1 - k:CONV_WIDTH - k, :]
    xc = xc[halo_c:] + cb_ref[...]
    conv_carry[...] = xl[ts - halo_c:]

    ri = _mm(xc.astype(MXU_DTYPE), wri_ref[...]) + bri_ref[...]
    r = _sigmoid(ri[:, :width])
    i_gate = _sigmoid(ri[:, width:])
    neg_lam = -lam_ref[...]
    softplus = jnp.maximum(neg_lam, 0.0) + jnp.log1p(jnp.exp(-jnp.abs(neg_lam)))
    log_a = -LRU_C * r * softplus
    a = jnp.exp(log_a)
    one_minus_a2 = -jnp.tanh(log_a) * (a * a + 1.0)
    mult = jnp.where(one_minus_a2 > 0.0, one_minus_a2 * lax.rsqrt(one_minus_a2), 0.0)
    b = mult * (i_gate * xc)

    row_in_group = lax.broadcasted_iota(jnp.int32, (ts, 1), 0) & (V7X_SUBLANES - 1)
    k = 1
    while k < V7X_SUBLANES:
        keep = row_in_group >= k
        a_prev = jnp.where(keep, pltpu.roll(a, k, 0), 1.0)
        b_prev = jnp.where(keep, pltpu.roll(b, k, 0), 0.0)
        b = a * b_prev + b
        a = a * a_prev
        k *= 2
    h_prev = h_carry[0:1, :]
    groups = []
    for g in range(ts // V7X_SUBLANES):
        rows_g = slice(g * V7X_SUBLANES, (g + 1) * V7X_SUBLANES)
        h_g = a[rows_g] * h_prev + b[rows_g]
        groups.append(h_g)
        h_prev = h_g[V7X_SUBLANES - 1:V7X_SUBLANES]
    h = jnp.concatenate(groups, axis=0)
    h_carry[...] = jnp.broadcast_to(h_prev, h_carry.shape)

    gelu = 0.5 * gl * (1.0 + jnp.tanh(GELU_C0 * (gl + GELU_C1 * gl * gl * gl)))
    y_ref[0] = jnp.concatenate([y_pool, h * gelu], axis=1).astype(y_ref.dtype)


def _poollru(xpl, pool_w_bd, pool_scale, conv_w, conv_b, w_ri_bd, b_ri, lam):
    b, s, w3 = xpl.shape
    width = w3 // 3
    ts = min(SEQ_TILE, s)
    return pl.pallas_call(
        _poollru_body,
        grid=(b, s // ts),
        in_specs=[pl.BlockSpec((1, ts, w3), lambda i, j: (i, j, 0)),
                  _const_spec((width, width)), _const_spec((1, width)),
                  _const_spec((CONV_WIDTH, width)), _const_spec((1, width)),
                  _const_spec((width, 2 * width)), _const_spec((1, 2 * width)), _const_spec((1, width))],
        out_specs=pl.BlockSpec((1, ts, 2 * width), lambda i, j: (i, j, 0)),
        out_shape=jax.ShapeDtypeStruct((b, s, 2 * width), MXU_DTYPE),
        scratch_shapes=[pltpu.VMEM((POOL_WINDOWS[-1], width), F32),
                        pltpu.VMEM((V7X_SUBLANES, width), F32),
                        pltpu.VMEM((V7X_SUBLANES, width), F32)],
        compiler_params=_params(("parallel", "arbitrary")),
        name="pool_rglru",
    )(xpl, pool_w_bd, pool_scale[None, :], conv_w, conv_b[None, :], w_ri_bd, b_ri[None, :], lam[None, :])


def _compress_body(transpose_out, x_ref, pe_ref, w_ref, o_ref):
    n1 = x_ref.shape[1] // CMP_STRIDE
    first = jnp.zeros((n1, w_ref.shape[2]), F32)
    second = first
    for l in range(CMP_STRIDE):
        x = x_ref[0, pl.ds(l, n1, stride=CMP_STRIDE), :]
        first = first + _mm((x + pe_ref[l]).astype(MXU_DTYPE), w_ref[l])
        second = second + _mm((x + pe_ref[l + CMP_STRIDE]).astype(MXU_DTYPE), w_ref[l + CMP_STRIDE])
    out = first + pltpu.roll(second, n1 - 1, 0)
    o_ref[0] = (out.T if transpose_out else out).astype(o_ref.dtype)


def _compress(x, pe_rows, w_bd, transpose_out):
    b, s, width = x.shape
    n1 = s // CMP_STRIDE
    out_dims = (width, n1) if transpose_out else (n1, width)
    return pl.pallas_call(
        functools.partial(_compress_body, transpose_out),
        grid=(b,),
        in_specs=[pl.BlockSpec((1, s, width), lambda i: (i, 0, 0)),
                  _const_spec(pe_rows.shape), _const_spec(w_bd.shape)],
        out_specs=pl.BlockSpec((1,) + out_dims, lambda i: (i, 0, 0)),
        out_shape=jax.ShapeDtypeStruct((b,) + out_dims, MXU_DTYPE),
        compiler_params=_params(("parallel",)),
        name="compress_kv",
    )(x, pe_rows, w_bd)


def _stack_heads(q_ref):
    return jnp.concatenate([q_ref[0, g * HEAD_DIM:(g + 1) * HEAD_DIM, :] for g in range(GQA_GROUP)], axis=1)


def _place_in_kv_half(qt, kvh):
    zeros = jnp.zeros_like(qt)
    parts = [jnp.where(kvh == h, qt, zeros) for h in range(N_KV_HEADS)]
    return jnp.concatenate(parts, axis=0)


def _query_positions(start, tq):
    lane = lax.broadcasted_iota(jnp.int32, (1, GQA_GROUP * tq), 1)
    return start + (lane & (tq - 1))


def _with_ones(v_t):
    return jnp.concatenate([v_t, jnp.ones((ONES_ROWS, v_t.shape[1]), v_t.dtype)], axis=0)


def _denominator(acc):
    return acc[HEAD_DIM:HEAD_DIM + 1]


def _normalize(acc):
    return acc[:HEAD_DIM] * (1.0 / _denominator(acc))


def _cmp_body(n_sel, q_ref, kc_ref, vct_ref, oc_ref, mb_ref, band_ref):
    tq = q_ref.shape[2]
    kvh = pl.program_id(1)
    start = pl.program_id(2) * tq
    qt = _stack_heads(q_ref)
    qp = _place_in_kv_half(qt, kvh)
    n1 = kc_ref.shape[1]
    t_row = _query_positions(start, tq)
    assert CMP_STRIDE * CMP_CHUNK >= tq + CMP_LEN

    def attend(nrows):
        s = _mm(kc_ref[0, :nrows, :], qp)
        lo = max(0, nrows - 2 * CMP_CHUNK)
        n_idx = lo + lax.broadcasted_iota(jnp.int32, (nrows - lo, 1), 0)
        tail = jnp.where(n_idx * CMP_STRIDE + (CMP_LEN - 1) <= t_row, s[lo:], NEG_INF)
        s = jnp.concatenate([s[:lo], tail], axis=0) if lo else tail
        m = jnp.max(s, axis=0, keepdims=True)
        p = jnp.exp2(s - jnp.where(m > 0.5 * NEG_INF, m, 0.0))
        den = jnp.sum(p, axis=0, keepdims=True)
        pn = p * (1.0 / jnp.where(den > 0.0, den, 1.0))
        oc = _mm(vct_ref[0, :, :nrows], pn.astype(MXU_DTYPE))
        psum = pn[:, 0:tq]
        for g in range(GQA_GROUP):
            oc_ref[0, g * HEAD_DIM:(g + 1) * HEAD_DIM, :] = oc[:, g * tq:(g + 1) * tq]
            if g:
                psum = psum + pn[:, g * tq:(g + 1) * tq]
        n_blk = min(n_sel, nrows * CMP_STRIDE // SEL_BLOCK)
        step, lead, pad = SEL_BLOCK // CMP_STRIDE, CMP_LEN // CMP_STRIDE - 1, V7X_SUBLANES
        cols = []
        for c in range(tq // V7X_LANES):
            band_ref[c, 0:pad, :] = jnp.zeros((pad, V7X_LANES), F32)
            band_ref[c, pad:pad + nrows, :] = psum[:, c * V7X_LANES:(c + 1) * V7X_LANES]
            taps = [band_ref[c, pl.ds(pad - lead + k, n_blk, stride=step), :] for k in range(step + lead)]
            cols.append(functools.reduce(lambda a, b: a + b, taps))
        select(jnp.concatenate(cols, axis=1), n_blk)

    def select(imp, n_blk):
        j = lax.broadcasted_iota(jnp.int32, (n_blk, 1), 0).astype(F32)
        t_q = start + lax.broadcasted_iota(jnp.int32, (1, tq), 1)
        cur = lax.shift_right_logical(t_q, _log2(SEL_BLOCK)).astype(F32)
        forced = (j == 0.0) | (j == cur) | (j == cur - 1.0)
        valid = j <= cur
        base = jnp.where(valid, jnp.where(forced, BIG_SCORE, imp), -BIG_SCORE)
        n_pad = mb_ref.shape[2]
        if n_pad > n_blk:
            mb_ref[0, 0, n_blk:, :] = jnp.full((n_pad - n_blk, tq), NEG_INF, mb_ref.dtype)

        score = jnp.where(forced, -jnp.inf, base)
        for _ in range(SEL_TOPK - 3):
            score = jnp.where(score == jnp.max(score, axis=0, keepdims=True), -jnp.inf, score)
        bias = jnp.where(score == -jnp.inf, 0.0, NEG_INF)
        mb_ref[0, 0, :n_blk, :] = bias.astype(mb_ref.dtype)
        taken = jnp.sum(jnp.where(valid, jnp.where(bias == 0.0, 1.0, 0.0), 0.0), axis=0, keepdims=True)
        tie = jnp.max(jnp.abs(taken - jnp.minimum(cur + 1.0, float(SEL_TOPK)))) > 0.0

        @pl.when(tie)
        def _():
            score = base
            bias = jnp.full(score.shape, NEG_INF, F32)
            for _ in range(SEL_TOPK):
                best = jnp.max(score, axis=0, keepdims=True)
                first = jnp.min(jnp.where(score == best, j, float(n_blk)), axis=0, keepdims=True)
                pick = j == first
                bias = jnp.where(pick, 0.0, bias)
                score = jnp.where(pick, -jnp.inf, score)
            mb_ref[0, 0, :n_blk, :] = bias.astype(mb_ref.dtype)

    assert n_sel >= SEL_TOPK and tq % SEL_BLOCK == 0
    n_need = lax.shift_right_logical(start + tq, _log2(CMP_STRIDE)) - 1
    chunks = lax.shift_right_logical(n_need + CMP_CHUNK - 1, _log2(CMP_CHUNK))
    n_variants = -(-n1 // CMP_CHUNK)
    for c in range(1, n_variants + 1):
        pl.when(chunks == c)(functools.partial(attend, min(c * CMP_CHUNK, n1)))


def _cmp_attention(qt, k_cmp, v_cmp_t, n_sel, n_pad):
    b, _, s = qt.shape
    n1 = k_cmp.shape[1]
    tq = min(CMP_Q_TILE, s)
    group_rows = GQA_GROUP * HEAD_DIM
    head_blk = pl.BlockSpec((1, group_rows, tq), lambda i, h, j: (i, h, j))
    return pl.pallas_call(
        functools.partial(_cmp_body, n_sel),
        grid=(b, N_KV_HEADS, s // tq),
        in_specs=[head_blk,
                  pl.BlockSpec((1, n1, KV_WIDTH), lambda i, h, j: (i, 0, 0)),
                  pl.BlockSpec((1, HEAD_DIM, n1), lambda i, h, j: (i, h, 0))],
        out_specs=[head_blk, pl.BlockSpec((1, 1, n_pad, tq), lambda i, h, j: (i, h, 0, j))],
        out_shape=[jax.ShapeDtypeStruct((b, N_HEADS * HEAD_DIM, s), F32),
                   jax.ShapeDtypeStruct((b, N_KV_HEADS, n_pad, s), MXU_DTYPE)],
        scratch_shapes=[pltpu.VMEM((tq // V7X_LANES, n1 + V7X_SUBLANES, V7X_LANES), F32)],
        compiler_params=_params(("parallel", "parallel", "parallel")),
        name="cmp_attention_select",
    )(qt, k_cmp, v_cmp_t)


def _selwin_body(q_ref, mb_ref, ks_ref, oh_ref, vst_ref, kw_ref, vwt_ref, oc_ref, gate_ref, y_ref,
                 qa_ref, acc_ref, win_ref):
    tq = q_ref.shape[2]
    tk = tq
    kvh = pl.program_id(1)
    qi = pl.program_id(2)
    start = qi * tq
    rows = GQA_GROUP * tq
    qt = _stack_heads(q_ref)
    qp = _place_in_kv_half(qt, kvh)
    t_row = _query_positions(start, tq)
    group_shift = _log2(SEL_GROUP * SEL_BLOCK // tk)
    n_groups = mb_ref.shape[2] // SEL_GROUP

    for grp in range(n_groups):
        bias_rows = mb_ref[0, 0, grp * SEL_GROUP:(grp + 1) * SEL_GROUP, :]
        qa_ref[grp] = jnp.concatenate([jnp.concatenate([bias_rows] * GQA_GROUP, axis=1), qp], axis=0)

    def scores(kt, n_sub, causal):
        key_rows = pl.ds(pl.multiple_of(kt * tk, tk), n_sub * tk)
        oh_rows = pl.ds(pl.multiple_of(lax.rem(kt * tk, oh_ref.shape[0]), tk), n_sub * tk)
        k = jnp.concatenate([oh_ref[oh_rows, :], ks_ref[0, key_rows, :]], axis=1)
        s = _mm(k, qa_ref[lax.shift_right_logical(kt, group_shift)])
        if causal:
            kp = kt * tk + lax.broadcasted_iota(jnp.int32, (n_sub * tk, 1), 0)
            s = jnp.where(kp <= t_row, s, NEG_INF)
        return s

    def values(kt, n_sub):
        return vst_ref[0, :, pl.ds(pl.multiple_of(kt * tk, tk), n_sub * tk)]

    def col_max(s):
        return jnp.max(s, axis=0, keepdims=True)

    n_past = qi

    assert tq == WINDOW
    own_rows = pl.ds(pl.multiple_of(start, tq), tq)
    first_tile = qi == 0
    win_rows = pl.ds(pl.multiple_of(jnp.maximum(start - WINDOW, 0), tq), WINDOW + tq)

    def window_scores():
        s = _mm(kw_ref[0, win_rows, :], qp)
        kp = jnp.maximum(start - WINDOW, 0) + lax.broadcasted_iota(jnp.int32, (WINDOW + tq, 1), 0)
        causal_lhs, causal_rhs = -kp, -(t_row + 1)
        lhs = jnp.where(first_tile, causal_lhs[:WINDOW], kp[:WINDOW])
        rhs = jnp.where(first_tile, causal_rhs, t_row - WINDOW)
        return jnp.concatenate([jnp.where(lhs > rhs, s[:WINDOW], NEG_INF),
                                jnp.where(causal_lhs[WINDOW:] > causal_rhs, s[WINDOW:], NEG_INF)], axis=0)

    def pv(v_t, s, ref):
        return _mm(_with_ones(v_t), jnp.exp2(s - ref).astype(MXU_DTYPE))

    def self_scores(k_ref):
        k_t = k_ref[0, own_rows, :].astype(F32).T
        k_own = k_t[:HEAD_DIM]
        for h in range(1, N_KV_HEADS):
            k_own = jnp.where(kvh == h, k_t[h * HEAD_DIM:(h + 1) * HEAD_DIM], k_own)
        parts = [jnp.sum(qt[:, g * tq:(g + 1) * tq].astype(F32) * k_own, axis=0, keepdims=True)
                 for g in range(GQA_GROUP)]
        return jnp.concatenate(parts, axis=1)

    ref = self_scores(ks_ref)
    ref_w = self_scores(kw_ref)
    win_ref[...] = pv(vwt_ref[0, :, win_rows], window_scores(), ref_w)

    def stream(kt, n_sub, causal=False):
        acc_ref[...] += pv(values(kt, n_sub), scores(kt, n_sub, causal), ref)

    acc_ref[...] = jnp.zeros_like(acc_ref)
    n_big = lax.shift_right_logical(n_past, _log2(STREAM_TILES))
    lax.fori_loop(0, n_big, lambda i, c: (stream(i * STREAM_TILES, STREAM_TILES), c)[1], 0)
    done = n_big * STREAM_TILES
    part = STREAM_TILES // 2
    while part >= 1:
        pl.when((n_past & part) != 0)(functools.partial(stream, done, part))
        done = done + (n_past & part)
        part //= 2
    stream(qi, 1, causal=True)
    limit = 2.0 ** EXP_HEADROOM
    in_range = jnp.where(_denominator(acc_ref) < limit, jnp.where(_denominator(win_ref) < limit, 1.0, 0.0), 0.0)
    overflow = jnp.min(in_range) < 1.0

    @pl.when(overflow)
    def _():
        def tile(kt, causal):
            s = scores(kt, 1, causal)
            m_t = col_max(s)
            return m_t, pv(values(kt, 1), s, m_t)

        def merge(carry, part):
            m, acc = carry
            m_t, o_t = part
            m_new = jnp.maximum(m, m_t)
            return m_new, acc * jnp.exp2(m - m_new) + o_t * jnp.exp2(m_t - m_new)

        carry = (jnp.full((1, rows), M_INIT, F32), jnp.zeros(acc_ref.shape, F32))
        carry = lax.fori_loop(0, n_past, lambda i, c: merge(c, tile(i, False)), carry)
        acc_ref[...] = merge(carry, tile(qi, True))[1]

        x_win = window_scores()
        win_ref[...] = pv(vwt_ref[0, :, win_rows], x_win, col_max(x_win))

    o_sel = _normalize(acc_ref[...])
    o_win = _normalize(win_ref[...])

    gate_row = kvh * (GQA_GROUP * N_BRANCH)
    for g in range(GQA_GROUP):
        cols = slice(g * tq, (g + 1) * tq)
        head = slice(g * HEAD_DIM, (g + 1) * HEAD_DIM)
        gates = [gate_ref[0, pl.ds(gate_row + N_BRANCH * g + br, 1), :] for br in range(N_BRANCH)]
        y = gates[0] * oc_ref[0, head, :] + gates[1] * o_sel[:, cols] + gates[2] * o_win[:, cols]
        y_ref[0, head, :] = y.astype(y_ref.dtype)


def _sel_win_attention(qt, mb_t, ks, key_onehot, vs_t, kw, vw_t, oc_t, gates_t):
    b, _, s = qt.shape
    tq = min(SEL_TILE, s)
    n_pad = mb_t.shape[2]
    n_gate = gates_t.shape[1]
    group_rows = GQA_GROUP * HEAD_DIM
    head_blk = pl.BlockSpec((1, group_rows, tq), lambda i, h, j: (i, h, j))
    once = pl.Buffered(1)
    keys = pl.BlockSpec((1, s, KV_WIDTH), lambda i, h, j: (i, 0, 0), pipeline_mode=once)
    vals = pl.BlockSpec((1, HEAD_DIM, s), lambda i, h, j: (i, h, 0), pipeline_mode=once)
    acc = pltpu.VMEM((HEAD_DIM + ONES_ROWS, GQA_GROUP * tq), F32)
    return pl.pallas_call(
        _selwin_body,
        grid=(b, N_KV_HEADS, s // tq),
        in_specs=[head_blk,
                  pl.BlockSpec((1, 1, n_pad, tq), lambda i, h, j: (i, h, 0, j)),
                  keys, _const_spec(key_onehot.shape), vals, keys, vals,
                  head_blk,
                  pl.BlockSpec((1, n_gate, tq), lambda i, h, j: (i, 0, j))],
        out_specs=head_blk,
        out_shape=jax.ShapeDtypeStruct((b, N_HEADS * HEAD_DIM, s), MXU_DTYPE),
        scratch_shapes=[pltpu.VMEM((n_pad // SEL_GROUP, SEL_GROUP + KV_WIDTH, GQA_GROUP * tq), MXU_DTYPE),
                        acc, acc],
        compiler_params=_params(("parallel", "parallel", "arbitrary")),
        name="sel_win_attention",
    )(qt, mb_t, ks, key_onehot, vs_t, kw, vw_t, oc_t, gates_t)


def _outproj_ffn_body(h_ref, ypl_ref, yat_ref, w_ref, g_ref, *refs):
    ffn_refs, o_ref = refs[:N_FFN_OPERANDS], refs[N_FFN_OPERANDS]
    split = ypl_ref.shape[2]
    m = _mm(ypl_ref[0], w_ref[:split, :]) + _mm_tn(yat_ref[0], w_ref[split:, :])
    o_ref[0] = _ffn_math(h_ref[0] + _rms(m, g_ref[...]), ffn_refs)


def _outproj_ffn(layer, h3, ypl, yat_t, w_out, g_post, ffn_weights):
    b, s, d = h3.shape
    tm = min(ROW_TILE, s)
    row = lambda w: pl.BlockSpec((1, tm, w), lambda i, j: (i, j, 0))
    ffn_specs, ffn_args = _ffn_operands(layer, *ffn_weights)
    return pl.pallas_call(
        _outproj_ffn_body,
        grid=(b, s // tm),
        in_specs=[row(d), row(ypl.shape[2]),
                  pl.BlockSpec((1, yat_t.shape[1], tm), lambda i, j: (i, 0, j)),
                  _layer_spec(w_out, layer), _layer_spec(g_post, layer)] + ffn_specs,
        out_specs=row(d),
        out_shape=jax.ShapeDtypeStruct((b, s, d), F32),
        compiler_params=_params(("parallel", "parallel")),
        name="mixer_out_proj_ffn",
    )(h3, ypl, yat_t, w_out, g_post, *ffn_args)


def _block_diag(blocks):
    n, a, b = blocks.shape
    eye = jnp.eye(n, dtype=blocks.dtype)
    return jnp.einsum("nab,nm->namb", blocks, eye).reshape(n * a, n * b)


def _compress_weights(w):
    eye = jnp.eye(N_KV_HEADS, dtype=w.dtype)
    return jnp.einsum("lde,hg->lhdge", w, eye).reshape(CMP_LEN, KV_WIDTH, KV_WIDTH).astype(MXU_DTYPE)


def kernel(x, positions, ffn1_pre_g, ffn1_post_g, ffn1_w_gate, ffn1_w_up, ffn1_w_down, mix_pre_g, mix_post_g,
           w_in, w_out, pool_w, pool_scale, conv_w, conv_b, lru_w_r, lru_b_r, lru_w_i, lru_b_i, lru_lambda,
           cmp_w_k, cmp_w_v, cmp_pe, ffn2_pre_g, ffn2_post_g, ffn2_w_gate, ffn2_w_up, ffn2_w_down):
    b, s, d = x.shape
    depth = w_in.shape[0]
    pool_width = pool_w.shape[1] * pool_w.shape[2]
    lru_width = lru_w_r.shape[1] * lru_w_r.shape[2]
    attn_width = N_HEADS * HEAD_DIM
    assert pool_width == lru_width and s % SEL_TILE == 0 and s % (CMP_STRIDE * 8) == 0
    assert WINDOW % SEL_TILE == 0 and (SEL_GROUP * SEL_BLOCK) % (STREAM_TILES * SEL_TILE) == 0
    assert s >= WINDOW + SEL_TILE

    sizes = [("xpl", pool_width + 2 * lru_width), ("q", attn_width), ("kc", KV_WIDTH), ("vc", KV_WIDTH),
             ("ks", KV_WIDTH), ("vs", KV_WIDTH), ("kw", KV_WIDTH), ("vw", KV_WIDTH), ("g", V7X_LANES)]
    cols, off = {}, 0
    for name, width in sizes:
        cols[name] = (off, off + width)
        off += width

    n_sel = s // SEL_BLOCK
    n_pad = -(-n_sel // SEL_GROUP) * SEL_GROUP
    n1 = s // CMP_STRIDE
    key_blk = jnp.arange(min(s, SEL_GROUP * SEL_BLOCK)) // SEL_BLOCK
    key_onehot = (key_blk[:, None] == jnp.arange(SEL_GROUP)[None, :]).astype(MXU_DTYPE)

    cos, sin = _rope_tables(positions)

    cast = lambda w: w.astype(MXU_DTYPE)
    gain = lambda g: g[:, None, :]
    ffn1 = (gain(ffn1_pre_g), gain(ffn1_post_g), cast(ffn1_w_gate), cast(ffn1_w_up), cast(ffn1_w_down))
    ffn2 = (gain(ffn2_pre_g), gain(ffn2_post_g), cast(ffn2_w_gate), cast(ffn2_w_up), cast(ffn2_w_down))
    w_all = cast(jnp.pad(w_in, ((0, 0), (0, 0), (0, off - w_in.shape[2]))))
    w_o = cast(w_out)

    h = x
    for l in range(depth):
        h = _ffn(l, h, ffn1)
        xpl, qt, kc, vc, ks, vs_t, kw, vw_t, gates_t = _inproj(l, h, gain(mix_pre_g), w_all, cols, cos, sin)

        w_ri = jnp.concatenate([_block_diag(lru_w_r[l]), _block_diag(lru_w_i[l])], axis=1).astype(MXU_DTYPE)
        ypl = _poollru(xpl, _block_diag(pool_w[l]).astype(MXU_DTYPE), pool_scale[l],
                       conv_w[l], conv_b[l], w_ri, jnp.concatenate([lru_b_r[l], lru_b_i[l]]), lru_lambda[l])

        pe_rows = jnp.tile(cmp_pe[l], (1, N_KV_HEADS))[:, None, :]
        k_cmp = _compress(kc, pe_rows, _compress_weights(cmp_w_k[l]), False)
        v_cmp_t = _compress(vc, pe_rows, _compress_weights(cmp_w_v[l]), True)

        oc_t, mb_t = _cmp_attention(qt, k_cmp, v_cmp_t, n_sel, n_pad)
        y_t = _sel_win_attention(qt, mb_t, ks, key_onehot, vs_t, kw, vw_t, oc_t, gates_t)

        h = _outproj_ffn(l, h, ypl, y_t, w_o, gain(mix_post_g), ffn2)
    return h
```

```python
import functools

import jax
import jax.numpy as jnp
from jax import lax
from jax.experimental import pallas as pl
from jax.experimental.pallas import tpu as pltpu

F32 = jnp.float32
MXU_DTYPE = jnp.bfloat16

POOL_WINDOWS = (2, 4, 8, 16)
POOL_GROUP = 64
LRU_C = 8.0
CONV_WIDTH = 4
HEAD_DIM = 64
N_KV_HEADS = 2
GQA_GROUP = 4
N_HEADS = N_KV_HEADS * GQA_GROUP
KV_WIDTH = N_KV_HEADS * HEAD_DIM
N_BRANCH = 3
CMP_LEN = 32
CMP_STRIDE = 16
SEL_BLOCK = 64
SEL_TOPK = 16
WINDOW = 512
ROPE_THETA = 10000.0
NORM_EPS = 1e-6
NEG_INF = -1e30
BIG_SCORE = 1e9
GELU_C0 = 0.7978845608028654
GELU_C1 = 0.044715
LOG2_E = 1.4426950408889634

V7X_LANES = 128
V7X_SUBLANES = 8
V7X_VMEM_LIMIT_BYTES = 56 * 1024 * 1024

ROW_TILE = 512
SEQ_TILE = 512
CMP_Q_TILE = 512
CMP_CHUNK = 128
SEL_TILE = 512
SEL_GROUP = 128
ONES_ROWS = 16
M_INIT = -1e20
STREAM_TILES = 4
EXP_HEADROOM = 64.0
Q_SCALE = HEAD_DIM ** -0.5 * LOG2_E


def _params(semantics):
    return pltpu.CompilerParams(dimension_semantics=semantics, vmem_limit_bytes=V7X_VMEM_LIMIT_BYTES)


def _rms(x, g):
    return x * lax.rsqrt(jnp.mean(x * x, axis=-1, keepdims=True) + NORM_EPS) * g


def _sigmoid(x):
    return 1.0 / (1.0 + jnp.exp(-x))


def _mm(a, b):
    return jnp.dot(a, b, preferred_element_type=F32)


def _mm_tn(a_t, b):
    return lax.dot_general(a_t, b, (((0,), (0,)), ((), ())), preferred_element_type=F32)


def _log2(n):
    assert n & (n - 1) == 0
    return n.bit_length() - 1


def _const_spec(shape):
    zeros = (0,) * len(shape)
    return pl.BlockSpec(shape, lambda *_: zeros, pipeline_mode=pl.Buffered(1))


def _layer_spec(stacked, layer):
    index = (layer,) + (0,) * (stacked.ndim - 1)
    return pl.BlockSpec((None,) + stacked.shape[1:], lambda *_: index, pipeline_mode=pl.Buffered(1))


def _rope_body(pos_ref, inv_ref, cos_ref, sin_ref):
    ang = pos_ref[0].astype(F32) * inv_ref[...]
    lane = lax.broadcasted_iota(jnp.int32, ang.shape, 1)
    cos_ref[0] = jnp.cos(ang)
    sin_ref[0] = jnp.where((lane & (HEAD_DIM - 1)) < HEAD_DIM // 2, -jnp.sin(ang), jnp.sin(ang))


def _rope_tables(positions):
    b, s = positions.shape
    inv = ROPE_THETA ** (-jnp.arange(0, HEAD_DIM, 2, dtype=F32) / HEAD_DIM)
    inv_row = jnp.tile(inv, V7X_LANES // (HEAD_DIM // 2))[None, :]
    ts = min(SEQ_TILE, s)
    out = jax.ShapeDtypeStruct((b, s, V7X_LANES), F32)
    return pl.pallas_call(
        _rope_body,
        grid=(b, s // ts),
        in_specs=[pl.BlockSpec((1, ts, 1), lambda i, j: (i, j, 0)),
                  pl.BlockSpec((1, V7X_LANES), lambda i, j: (0, 0))],
        out_specs=[pl.BlockSpec((1, ts, V7X_LANES), lambda i, j: (i, j, 0))] * 2,
        out_shape=[out, out],
        compiler_params=_params(("parallel", "parallel")),
        name="rope_tables",
    )(positions[:, :, None], inv_row)


def _ffn_math(h, ffn_refs):
    gpre_ref, gpost_ref, wg_ref, wu_ref, wd_ref = ffn_refs
    xn = _rms(h, gpre_ref[...]).astype(MXU_DTYPE)
    gate = _mm(xn, wg_ref[...])
    up = _mm(xn, wu_ref[...])
    act = (gate * _sigmoid(gate) * up).astype(MXU_DTYPE)
    f = _mm(act, wd_ref[...])
    return h + 0.5 * _rms(f, gpost_ref[...])


def _ffn_operands(layer, g_pre, g_post, w_gate, w_up, w_down):
    args = (g_pre, g_post, w_gate, w_up, w_down)
    return [_layer_spec(a, layer) for a in args], args


N_FFN_OPERANDS = 5


def _ffn_body(h_ref, *refs):
    refs[N_FFN_OPERANDS][0] = _ffn_math(h_ref[0], refs[:N_FFN_OPERANDS])


def _ffn(layer, h3, ffn_weights):
    b, s, d = h3.shape
    tm = min(ROW_TILE, s)
    row = pl.BlockSpec((1, tm, d), lambda i, j: (i, j, 0))
    ffn_specs, ffn_args = _ffn_operands(layer, *ffn_weights)
    return pl.pallas_call(
        _ffn_body,
        grid=(b, s // tm),
        in_specs=[row] + ffn_specs,
        out_specs=row,
        out_shape=jax.ShapeDtypeStruct((b, s, d), F32),
        compiler_params=_params(("parallel", "parallel")),
        name="ffn",
    )(h3, *ffn_args)


def _swap_halves(x):
    n = x.shape[1]
    lane = lax.broadcasted_iota(jnp.int32, x.shape, 1)
    first_half = (lane & (HEAD_DIM - 1)) < HEAD_DIM // 2
    return jnp.where(first_half, pltpu.roll(x, n - HEAD_DIM // 2, 1), pltpu.roll(x, HEAD_DIM // 2, 1))


def _inproj_body(cols, h_ref, g_ref, w_ref, cos_ref, sin_ref,
                 xpl_ref, qt_ref, kc_ref, vc_ref, ks_ref, vst_ref, kw_ref, vwt_ref, gatet_ref):
    xn = _rms(h_ref[0], g_ref[...]).astype(MXU_DTYPE)
    proj = _mm(xn, w_ref[...])
    cos = cos_ref[0]
    sin = sin_ref[0]

    def rope(x):
        rep = x.shape[1] // V7X_LANES
        c = jnp.concatenate([cos] * rep, axis=1) if rep > 1 else cos
        s = jnp.concatenate([sin] * rep, axis=1) if rep > 1 else sin
        return x * c + _swap_halves(x) * s

    def seg(name):
        lo, hi = cols[name]
        return proj[:, lo:hi]

    xpl_ref[0] = seg("xpl")
    qt_ref[0] = (rope(seg("q")) * Q_SCALE).T.astype(qt_ref.dtype)
    kc_ref[0] = rope(seg("kc"))
    vc_ref[0] = seg("vc")
    ks_ref[0] = rope(seg("ks")).astype(ks_ref.dtype)
    vst_ref[0] = seg("vs").T.astype(vst_ref.dtype)
    kw_ref[0] = rope(seg("kw")).astype(kw_ref.dtype)
    vwt_ref[0] = seg("vw").T.astype(vwt_ref.dtype)
    gatet_ref[0] = _sigmoid(seg("g")).T[:gatet_ref.shape[1]]


def _inproj(layer, h3, g_pre, w_all, cols, cos, sin):
    b, s, d = h3.shape
    tm = min(ROW_TILE, s)
    width = lambda n: cols[n][1] - cols[n][0]
    row = lambda w: pl.BlockSpec((1, tm, w), lambda i, j: (i, j, 0))
    col = lambda w: pl.BlockSpec((1, w, tm), lambda i, j: (i, 0, j))
    tok = lambda n, dt: (jax.ShapeDtypeStruct((b, s, width(n)), dt), row(width(n)))
    chan = lambda w, dt: (jax.ShapeDtypeStruct((b, w, s), dt), col(w))
    outs = [tok("xpl", F32), chan(width("q"), MXU_DTYPE), tok("kc", F32), tok("vc", F32),
            tok("ks", MXU_DTYPE), chan(width("vs"), MXU_DTYPE), tok("kw", MXU_DTYPE),
            chan(width("vw"), MXU_DTYPE), chan(N_BRANCH * N_HEADS, F32)]
    return pl.pallas_call(
        functools.partial(_inproj_body, cols),
        grid=(b, s // tm),
        in_specs=[row(d), _layer_spec(g_pre, layer), _layer_spec(w_all, layer), row(V7X_LANES), row(V7X_LANES)],
        out_specs=[o[1] for o in outs],
        out_shape=[o[0] for o in outs],
        compiler_params=_params(("parallel", "parallel")),
        name="mixer_in_proj",
    )(h3, g_pre, w_all, cos, sin)


def _poollru_body(x_ref, pw_ref, pscale_ref, cw_ref, cb_ref, wri_ref, bri_ref, lam_ref,
                  y_ref, pool_carry, conv_carry, h_carry):
    si = pl.program_id(1)
    ts = x_ref.shape[1]
    width = pw_ref.shape[0]
    halo_p = pool_carry.shape[0]
    halo_c = conv_carry.shape[0]

    @pl.when(si == 0)
    def _():
        pool_carry[...] = jnp.zeros_like(pool_carry)
        conv_carry[...] = jnp.zeros_like(conv_carry)
        h_carry[...] = jnp.zeros_like(h_carry)

    x = x_ref[0]
    xp = x[:, :width]
    xl = x[:, width:2 * width]
    gl = x[:, 2 * width:]

    ext = jnp.concatenate([pool_carry[...], xp], axis=0)
    sums = [ext]
    shift = 1
    for _ in POOL_WINDOWS:
        sums.append(sums[-1] + pltpu.roll(sums[-1], shift, 0))
        shift *= 2
    lane = lax.broadcasted_iota(jnp.int32, (1, width), 1)
    grp = lax.shift_right_logical(lane, _log2(POOL_GROUP))
    win_sum = sums[len(POOL_WINDOWS)]
    win = jnp.full((1, width), float(POOL_WINDOWS[-1]), F32)
    for gi in range(len(POOL_WINDOWS) - 2, -1, -1):
        win_sum = jnp.where(grp == gi, sums[gi + 1], win_sum)
        win = jnp.where(grp == gi, float(POOL_WINDOWS[gi]), win)
    win_sum = win_sum[halo_p:]
    t_abs = si * ts + lax.broadcasted_iota(jnp.int32, (ts, 1), 0)
    cnt = jnp.minimum((t_abs + 1).astype(F32), win)
    pooled = win_sum / cnt
    y_pool = _mm((pooled - xp).astype(MXU_DTYPE), pw_ref[...]) * pscale_ref[...]
    pool_carry[...] = xp[ts - halo_p:]

    extc = jnp.concatenate([conv_carry[...], xl], axis=0)
    xc = extc * cw_ref[CONV_WIDTH - 1:CONV_WIDTH, :]
    for k in range(1, CONV_WIDTH):
        xc = xc + pltpu.roll(extc, k, 0) * cw_ref[CONV_WIDTH - 1 - k:CONV_WIDTH - k, :]
    xc = xc[halo_c:] + cb_ref[...]
    conv_carry[...] = xl[ts - halo_c:]

    ri = _mm(xc.astype(MXU_DTYPE), wri_ref[...]) + bri_ref[...]
    r = _sigmoid(ri[:, :width])
    i_gate = _sigmoid(ri[:, width:])
    neg_lam = -lam_ref[...]
    softplus = jnp.maximum(neg_lam, 0.0) + jnp.log1p(jnp.exp(-jnp.abs(neg_lam)))
    log_a = -LRU_C * r * softplus
    a = jnp.exp(log_a)
    one_minus_a2 = -jnp.tanh(log_a) * (a * a + 1.0)
    mult = jnp.where(one_minus_a2 > 0.0, one_minus_a2 * lax.rsqrt(one_minus_a2), 0.0)
    b = mult * (i_gate * xc)

    row_in_group = lax.broadcasted_iota(jnp.int32, (ts, 1), 0) & (V7X_SUBLANES - 1)
    k = 1
    while k < V7X_SUBLANES:
        keep = row_in_group >= k
        a_prev = jnp.where(keep, pltpu.roll(a, k, 0), 1.0)
        b_prev = jnp.where(keep, pltpu.roll(b, k, 0), 0.0)
        b = a * b_prev + b
        a = a * a_prev
        k *= 2
    h_prev = h_carry[0:1, :]
    groups = []
    for g in range(ts // V7X_SUBLANES):
        rows_g = slice(g * V7X_SUBLANES, (g + 1) * V7X_SUBLANES)
        h_g = a[rows_g] * h_prev + b[rows_g]
        groups.append(h_g)
        h_prev = h_g[V7X_SUBLANES - 1:V7X_SUBLANES]
    h = jnp.concatenate(groups, axis=0)
    h_carry[...] = jnp.broadcast_to(h_prev, h_carry.shape)

    gelu = 0.5 * gl * (1.0 + jnp.tanh(GELU_C0 * (gl + GELU_C1 * gl * gl * gl)))
    y_ref[0] = jnp.concatenate([y_pool, h * gelu], axis=1).astype(y_ref.dtype)


def _poollru(xpl, pool_w_bd, pool_scale, conv_w, conv_b, w_ri_bd, b_ri, lam):
    b, s, w3 = xpl.shape
    width = w3 // 3
    ts = min(SEQ_TILE, s)
    return pl.pallas_call(
        _poollru_body,
        grid=(b, s // ts),
        in_specs=[pl.BlockSpec((1, ts, w3), lambda i, j: (i, j, 0)),
                  _const_spec((width, width)), _const_spec((1, width)),
                  _const_spec((CONV_WIDTH, width)), _const_spec((1, width)),
                  _const_spec((width, 2 * width)), _const_spec((1, 2 * width)), _const_spec((1, width))],
        out_specs=pl.BlockSpec((1, ts, 2 * width), lambda i, j: (i, j, 0)),
        out_shape=jax.ShapeDtypeStruct((b, s, 2 * width), MXU_DTYPE),
        scratch_shapes=[pltpu.VMEM((POOL_WINDOWS[-1], width), F32),
                        pltpu.VMEM((V7X_SUBLANES, width), F32),
                        pltpu.VMEM((V7X_SUBLANES, width), F32)],
        compiler_params=_params(("parallel", "arbitrary")),
        name="pool_rglru",
    )(xpl, pool_w_bd, pool_scale[None, :], conv_w, conv_b[None, :], w_ri_bd, b_ri[None, :], lam[None, :])


def _compress_body(transpose_out, x_ref, pe_ref, w_ref, o_ref):
    n1 = x_ref.shape[1] // CMP_STRIDE
    first = jnp.zeros((n1, w_ref.shape[2]), F32)
    second = first
    for l in range(CMP_STRIDE):
        x = x_ref[0, pl.ds(l, n1, stride=CMP_STRIDE), :]
        first = first + _mm((x + pe_ref[l]).astype(MXU_DTYPE), w_ref[l])
        second = second + _mm((x + pe_ref[l + CMP_STRIDE]).astype(MXU_DTYPE), w_ref[l + CMP_STRIDE])
    out = first + pltpu.roll(second, n1 - 1, 0)
    o_ref[0] = (out.T if transpose_out else out).astype(o_ref.dtype)


def _compress(x, pe_rows, w_bd, transpose_out):
    b, s, width = x.shape
    n1 = s // CMP_STRIDE
    out_dims = (width, n1) if transpose_out else (n1, width)
    return pl.pallas_call(
        functools.partial(_compress_body, transpose_out),
        grid=(b,),
        in_specs=[pl.BlockSpec((1, s, width), lambda i: (i, 0, 0)),
                  _const_spec(pe_rows.shape), _const_spec(w_bd.shape)],
        out_specs=pl.BlockSpec((1,) + out_dims, lambda i: (i, 0, 0)),
        out_shape=jax.ShapeDtypeStruct((b,) + out_dims, MXU_DTYPE),
        compiler_params=_params(("parallel",)),
        name="compress_kv",
    )(x, pe_rows, w_bd)


def _stack_heads(q_ref):
    return jnp.concatenate([q_ref[0, g * HEAD_DIM:(g + 1) * HEAD_DIM, :] for g in range(GQA_GROUP)], axis=1)


def _place_in_kv_half(qt, kvh):
    zeros = jnp.zeros_like(qt)
    parts = [jnp.where(kvh == h, qt, zeros) for h in range(N_KV_HEADS)]
    return jnp.concatenate(parts, axis=0)


def _query_positions(start, tq):
    lane = lax.broadcasted_iota(jnp.int32, (1, GQA_GROUP * tq), 1)
    return start + (lane & (tq - 1))


def _with_ones(v_t):
    return jnp.concatenate([v_t, jnp.ones((ONES_ROWS, v_t.shape[1]), v_t.dtype)], axis=0)


def _denominator(acc):
    return acc[HEAD_DIM:HEAD_DIM + 1]


def _normalize(acc):
    return acc[:HEAD_DIM] * (1.0 / _denominator(acc))


def _cmp_body(n_sel, q_ref, kc_ref, vct_ref, oc_ref, mb_ref, band_ref):
    tq = q_ref.shape[2]
    kvh = pl.program_id(1)
    start = pl.program_id(2) * tq
    qt = _stack_heads(q_ref)
    qp = _place_in_kv_half(qt, kvh)
    n1 = kc_ref.shape[1]
    t_row = _query_positions(start, tq)
    assert CMP_STRIDE * CMP_CHUNK >= tq + CMP_LEN

    def attend(nrows):
        s = _mm(kc_ref[0, :nrows, :], qp)
        lo = max(0, nrows - 2 * CMP_CHUNK)
        n_idx = lo + lax.broadcasted_iota(jnp.int32, (nrows - lo, 1), 0)
        tail = jnp.where(n_idx * CMP_STRIDE + (CMP_LEN - 1) <= t_row, s[lo:], NEG_INF)
        s = jnp.concatenate([s[:lo], tail], axis=0) if lo else tail
        m = jnp.max(s, axis=0, keepdims=True)
        p = jnp.exp2(s - jnp.where(m > 0.5 * NEG_INF, m, 0.0))
        den = jnp.sum(p, axis=0, keepdims=True)
        pn = p * (1.0 / jnp.where(den > 0.0, den, 1.0))
        oc = _mm(vct_ref[0, :, :nrows], pn.astype(MXU_DTYPE))
        psum = pn[:, 0:tq]
        for g in range(GQA_GROUP):
            oc_ref[0, g * HEAD_DIM:(g + 1) * HEAD_DIM, :] = oc[:, g * tq:(g + 1) * tq]
            if g:
                psum = psum + pn[:, g * tq:(g + 1) * tq]
        n_blk = min(n_sel, nrows * CMP_STRIDE // SEL_BLOCK)
        step, lead, pad = SEL_BLOCK // CMP_STRIDE, CMP_LEN // CMP_STRIDE - 1, V7X_SUBLANES
        cols = []
        for c in range(tq // V7X_LANES):
            band_ref[c, 0:pad, :] = jnp.zeros((pad, V7X_LANES), F32)
            band_ref[c, pad:pad + nrows, :] = psum[:, c * V7X_LANES:(c + 1) * V7X_LANES]
            taps = [band_ref[c, pl.ds(pad - lead + k, n_blk, stride=step), :] for k in range(step + lead)]
            cols.append(functools.reduce(lambda a, b: a + b, taps))
        select(jnp.concatenate(cols, axis=1), n_blk)

    def select(imp, n_blk):
        j = lax.broadcasted_iota(jnp.int32, (n_blk, 1), 0).astype(F32)
        t_q = start + lax.broadcasted_iota(jnp.int32, (1, tq), 1)
        cur = lax.shift_right_logical(t_q, _log2(SEL_BLOCK)).astype(F32)
        forced = (j == 0.0) | (j == cur) | (j == cur - 1.0)
        valid = j <= cur
        base = jnp.where(valid, jnp.where(forced, BIG_SCORE, imp), -BIG_SCORE)
        n_pad = mb_ref.shape[2]
        if n_pad > n_blk:
            mb_ref[0, 0, n_blk:, :] = jnp.full((n_pad - n_blk, tq), NEG_INF, mb_ref.dtype)

        score = jnp.where(forced, -jnp.inf, base)
        for _ in range(SEL_TOPK - 3):
            score = jnp.where(score == jnp.max(score, axis=0, keepdims=True), -jnp.inf, score)
        bias = jnp.where(score == -jnp.inf, 0.0, NEG_INF)
        mb_ref[0, 0, :n_blk, :] = bias.astype(mb_ref.dtype)
        taken = jnp.sum(jnp.where(valid, jnp.where(bias == 0.0, 1.0, 0.0), 0.0), axis=0, keepdims=True)
        tie = jnp.max(jnp.abs(taken - jnp.minimum(cur + 1.0, float(SEL_TOPK)))) > 0.0

        @pl.when(tie)
        def _():
            score = base
            bias = jnp.full(score.shape, NEG_INF, F32)
            for _ in range(SEL_TOPK):
                best = jnp.max(score, axis=0, keepdims=True)
                first = jnp.min(jnp.where(score == best, j, float(n_blk)), axis=0, keepdims=True)
                pick = j == first
                bias = jnp.where(pick, 0.0, bias)
                score = jnp.where(pick, -jnp.inf, score)
            mb_ref[0, 0, :n_blk, :] = bias.astype(mb_ref.dtype)

    assert n_sel >= SEL_TOPK and tq % SEL_BLOCK == 0
    n_need = lax.shift_right_logical(start + tq, _log2(CMP_STRIDE)) - 1
    chunks = lax.shift_right_logical(n_need + CMP_CHUNK - 1, _log2(CMP_CHUNK))
    n_variants = -(-n1 // CMP_CHUNK)
    for c in range(1, n_variants + 1):
        pl.when(chunks == c)(functools.partial(attend, min(c * CMP_CHUNK, n1)))


def _cmp_attention(qt, k_cmp, v_cmp_t, n_sel, n_pad):
    b, _, s = qt.shape
    n1 = k_cmp.shape[1]
    tq = min(CMP_Q_TILE, s)
    group_rows = GQA_GROUP * HEAD_DIM
    head_blk = pl.BlockSpec((1, group_rows, tq), lambda i, h, j: (i, h, j))
    return pl.pallas_call(
        functools.partial(_cmp_body, n_sel),
        grid=(b, N_KV_HEADS, s // tq),
        in_specs=[head_blk,
                  pl.BlockSpec((1, n1, KV_WIDTH), lambda i, h, j: (i, 0, 0)),
                  pl.BlockSpec((1, HEAD_DIM, n1), lambda i, h, j: (i, h, 0))],
        out_specs=[head_blk, pl.BlockSpec((1, 1, n_pad, tq), lambda i, h, j: (i, h, 0, j))],
        out_shape=[jax.ShapeDtypeStruct((b, N_HEADS * HEAD_DIM, s), F32),
                   jax.ShapeDtypeStruct((b, N_KV_HEADS, n_pad, s), MXU_DTYPE)],
        scratch_shapes=[pltpu.VMEM((tq // V7X_LANES, n1 + V7X_SUBLANES, V7X_LANES), F32)],
        compiler_params=_params(("parallel", "parallel", "parallel")),
        name="cmp_attention_select",
    )(qt, k_cmp, v_cmp_t)


def _selwin_body(q_ref, mb_ref, ks_ref, oh_ref, vst_ref, kw_ref, vwt_ref, oc_ref, gate_ref, y_ref,
                 qa_ref, acc_ref, win_ref):
    tq = q_ref.shape[2]
    tk = tq
    kvh = pl.program_id(1)
    qi = pl.program_id(2)
    start = qi * tq
    rows = GQA_GROUP * tq
    qt = _stack_heads(q_ref)
    qp = _place_in_kv_half(qt, kvh)
    t_row = _query_positions(start, tq)
    group_shift = _log2(SEL_GROUP * SEL_BLOCK // tk)
    n_groups = mb_ref.shape[2] // SEL_GROUP

    for grp in range(n_groups):
        bias_rows = mb_ref[0, 0, grp * SEL_GROUP:(grp + 1) * SEL_GROUP, :]
        qa_ref[grp] = jnp.concatenate([jnp.concatenate([bias_rows] * GQA_GROUP, axis=1), qp], axis=0)

    def scores(kt, n_sub, causal):
        key_rows = pl.ds(pl.multiple_of(kt * tk, tk), n_sub * tk)
        oh_rows = pl.ds(pl.multiple_of(lax.rem(kt * tk, oh_ref.shape[0]), tk), n_sub * tk)
        k = jnp.concatenate([oh_ref[oh_rows, :], ks_ref[0, key_rows, :]], axis=1)
        s = _mm(k, qa_ref[lax.shift_right_logical(kt, group_shift)])
        if causal:
            kp = kt * tk + lax.broadcasted_iota(jnp.int32, (n_sub * tk, 1), 0)
            s = jnp.where(kp <= t_row, s, NEG_INF)
        return s

    def values(kt, n_sub):
        return vst_ref[0, :, pl.ds(pl.multiple_of(kt * tk, tk), n_sub * tk)]

    def col_max(s):
        return jnp.max(s, axis=0, keepdims=True)

    n_past = qi

    assert tq == WINDOW
    own_rows = pl.ds(pl.multiple_of(start, tq), tq)
    first_tile = qi == 0
    win_rows = pl.ds(pl.multiple_of(jnp.maximum(start - WINDOW, 0), tq), WINDOW + tq)

    def window_scores():
        s = _mm(kw_ref[0, win_rows, :], qp)
        kp = jnp.maximum(start - WINDOW, 0) + lax.broadcasted_iota(jnp.int32, (WINDOW + tq, 1), 0)
        causal_lhs, causal_rhs = -kp, -(t_row + 1)
        lhs = jnp.where(first_tile, causal_lhs[:WINDOW], kp[:WINDOW])
        rhs = jnp.where(first_tile, causal_rhs, t_row - WINDOW)
        return jnp.concatenate([jnp.where(lhs > rhs, s[:WINDOW], NEG_INF),
                                jnp.where(causal_lhs[WINDOW:] > causal_rhs, s[WINDOW:], NEG_INF)], axis=0)

    def pv(v_t, s, ref):
        return _mm(_with_ones(v_t), jnp.exp2(s - ref).astype(MXU_DTYPE))

    def self_scores(k_ref):
        k_t = k_ref[0, own_rows, :].astype(F32).T
        k_own = k_t[:HEAD_DIM]
        for h in range(1, N_KV_HEADS):
            k_own = jnp.where(kvh == h, k_t[h * HEAD_DIM:(h + 1) * HEAD_DIM], k_own)
        parts = [jnp.sum(qt[:, g * tq:(g + 1) * tq].astype(F32) * k_own, axis=0, keepdims=True)
                 for g in range(GQA_GROUP)]
        return jnp.concatenate(parts, axis=1)

    ref = self_scores(ks_ref)
    ref_w = self_scores(kw_ref)
    win_ref[...] = pv(vwt_ref[0, :, win_rows], window_scores(), ref_w)

    def stream(kt, n_sub, causal=False):
        acc_ref[...] += pv(values(kt, n_sub), scores(kt, n_sub, causal), ref)

    acc_ref[...] = jnp.zeros_like(acc_ref)
    n_big = lax.shift_right_logical(n_past, _log2(STREAM_TILES))
    lax.fori_loop(0, n_big, lambda i, c: (stream(i * STREAM_TILES, STREAM_TILES), c)[1], 0)

    def tail(n_sub):
        kt = n_big * STREAM_TILES
        s = scores(kt, n_sub, False)
        kp = qi * tk + lax.broadcasted_iota(jnp.int32, (tk, 1), 0)
        diag = jnp.where(kp <= t_row, s[(n_sub - 1) * tk:], NEG_INF)
        s = jnp.concatenate([s[:(n_sub - 1) * tk], diag], axis=0) if n_sub > 1 else diag
        acc_ref[...] += pv(values(kt, n_sub), s, ref)

    for rem in range(STREAM_TILES):
        pl.when((n_past & (STREAM_TILES - 1)) == rem)(functools.partial(tail, rem + 1))
    limit = 2.0 ** EXP_HEADROOM
    in_range = jnp.where(_denominator(acc_ref) < limit, jnp.where(_denominator(win_ref) < limit, 1.0, 0.0), 0.0)
    overflow = jnp.min(in_range) < 1.0

    @pl.when(overflow)
    def _():
        def tile(kt, causal):
            s = scores(kt, 1, causal)
            m_t = col_max(s)
            return m_t, pv(values(kt, 1), s, m_t)

        def merge(carry, part):
            m, acc = carry
            m_t, o_t = part
            m_new = jnp.maximum(m, m_t)
            return m_new, acc * jnp.exp2(m - m_new) + o_t * jnp.exp2(m_t - m_new)

        carry = (jnp.full((1, rows), M_INIT, F32), jnp.zeros(acc_ref.shape, F32))
        carry = lax.fori_loop(0, n_past, lambda i, c: merge(c, tile(i, False)), carry)
        acc_ref[...] = merge(carry, tile(qi, True))[1]

        x_win = window_scores()
        win_ref[...] = pv(vwt_ref[0, :, win_rows], x_win, col_max(x_win))

    o_sel = _normalize(acc_ref[...])
    o_win = _normalize(win_ref[...])

    gate_row = kvh * (GQA_GROUP * N_BRANCH)
    for g in range(GQA_GROUP):
        cols = slice(g * tq, (g + 1) * tq)
        head = slice(g * HEAD_DIM, (g + 1) * HEAD_DIM)
        gates = [gate_ref[0, pl.ds(gate_row + N_BRANCH * g + br, 1), :] for br in range(N_BRANCH)]
        y = gates[0] * oc_ref[0, head, :] + gates[1] * o_sel[:, cols] + gates[2] * o_win[:, cols]
        y_ref[0, head, :] = y.astype(y_ref.dtype)


def _sel_win_attention(qt, mb_t, ks, key_onehot, vs_t, kw, vw_t, oc_t, gates_t):
    b, _, s = qt.shape
    tq = min(SEL_TILE, s)
    n_pad = mb_t.shape[2]
    n_gate = gates_t.shape[1]
    group_rows = GQA_GROUP * HEAD_DIM
    head_blk = pl.BlockSpec((1, group_rows, tq), lambda i, h, j: (i, h, j))
    once = pl.Buffered(1)
    keys = pl.BlockSpec((1, s, KV_WIDTH), lambda i, h, j: (i, 0, 0), pipeline_mode=once)
    vals = pl.BlockSpec((1, HEAD_DIM, s), lambda i, h, j: (i, h, 0), pipeline_mode=once)
    acc = pltpu.VMEM((HEAD_DIM + ONES_ROWS, GQA_GROUP * tq), F32)
    return pl.pallas_call(
        _selwin_body,
        grid=(b, N_KV_HEADS, s // tq),
        in_specs=[head_blk,
                  pl.BlockSpec((1, 1, n_pad, tq), lambda i, h, j: (i, h, 0, j)),
                  keys, _const_spec(key_onehot.shape), vals, keys, vals,
                  head_blk,
                  pl.BlockSpec((1, n_gate, tq), lambda i, h, j: (i, 0, j))],
        out_specs=head_blk,
        out_shape=jax.ShapeDtypeStruct((b, N_HEADS * HEAD_DIM, s), MXU_DTYPE),
        scratch_shapes=[pltpu.VMEM((n_pad // SEL_GROUP, SEL_GROUP + KV_WIDTH, GQA_GROUP * tq), MXU_DTYPE),
                        acc, acc],
        compiler_params=_params(("parallel", "parallel", "arbitrary")),
        name="sel_win_attention",
    )(qt, mb_t, ks, key_onehot, vs_t, kw, vw_t, oc_t, gates_t)


def _outproj_ffn_body(h_ref, ypl_ref, yat_ref, w_ref, g_ref, *refs):
    ffn_refs, o_ref = refs[:N_FFN_OPERANDS], refs[N_FFN_OPERANDS]
    split = ypl_ref.shape[2]
    m = _mm(ypl_ref[0], w_ref[:split, :]) + _mm_tn(yat_ref[0], w_ref[split:, :])
    o_ref[0] = _ffn_math(h_ref[0] + _rms(m, g_ref[...]), ffn_refs)


def _outproj_ffn(layer, h3, ypl, yat_t, w_out, g_post, ffn_weights):
    b, s, d = h3.shape
    tm = min(ROW_TILE, s)
    row = lambda w: pl.BlockSpec((1, tm, w), lambda i, j: (i, j, 0))
    ffn_specs, ffn_args = _ffn_operands(layer, *ffn_weights)
    return pl.pallas_call(
        _outproj_ffn_body,
        grid=(b, s // tm),
        in_specs=[row(d), row(ypl.shape[2]),
                  pl.BlockSpec((1, yat_t.shape[1], tm), lambda i, j: (i, 0, j)),
                  _layer_spec(w_out, layer), _layer_spec(g_post, layer)] + ffn_specs,
        out_specs=row(d),
        out_shape=jax.ShapeDtypeStruct((b, s, d), F32),
        compiler_params=_params(("parallel", "parallel")),
        name="mixer_out_proj_ffn",
    )(h3, ypl, yat_t, w_out, g_post, *ffn_args)


def _block_diag(blocks):
    n, a, b = blocks.shape
    eye = jnp.eye(n, dtype=blocks.dtype)
    return jnp.einsum("nab,nm->namb", blocks, eye).reshape(n * a, n * b)


def _compress_weights(w):
    eye = jnp.eye(N_KV_HEADS, dtype=w.dtype)
    return jnp.einsum("lde,hg->lhdge", w, eye).reshape(CMP_LEN, KV_WIDTH, KV_WIDTH).astype(MXU_DTYPE)


def kernel(x, positions, ffn1_pre_g, ffn1_post_g, ffn1_w_gate, ffn1_w_up, ffn1_w_down, mix_pre_g, mix_post_g,
           w_in, w_out, pool_w, pool_scale, conv_w, conv_b, lru_w_r, lru_b_r, lru_w_i, lru_b_i, lru_lambda,
           cmp_w_k, cmp_w_v, cmp_pe, ffn2_pre_g, ffn2_post_g, ffn2_w_gate, ffn2_w_up, ffn2_w_down):
    b, s, d = x.shape
    depth = w_in.shape[0]
    pool_width = pool_w.shape[1] * pool_w.shape[2]
    lru_width = lru_w_r.shape[1] * lru_w_r.shape[2]
    attn_width = N_HEADS * HEAD_DIM
    assert pool_width == lru_width and s % SEL_TILE == 0 and s % (CMP_STRIDE * 8) == 0
    assert WINDOW % SEL_TILE == 0 and (SEL_GROUP * SEL_BLOCK) % (STREAM_TILES * SEL_TILE) == 0
    assert s >= WINDOW + SEL_TILE

    sizes = [("xpl", pool_width + 2 * lru_width), ("q", attn_width), ("kc", KV_WIDTH), ("vc", KV_WIDTH),
             ("ks", KV_WIDTH), ("vs", KV_WIDTH), ("kw", KV_WIDTH), ("vw", KV_WIDTH), ("g", V7X_LANES)]
    cols, off = {}, 0
    for name, width in sizes:
        cols[name] = (off, off + width)
        off += width

    n_sel = s // SEL_BLOCK
    n_pad = -(-n_sel // SEL_GROUP) * SEL_GROUP
    n1 = s // CMP_STRIDE
    key_blk = jnp.arange(min(s, SEL_GROUP * SEL_BLOCK)) // SEL_BLOCK
    key_onehot = (key_blk[:, None] == jnp.arange(SEL_GROUP)[None, :]).astype(MXU_DTYPE)

    cos, sin = _rope_tables(positions)

    cast = lambda w: w.astype(MXU_DTYPE)
    gain = lambda g: g[:, None, :]
    ffn1 = (gain(ffn1_pre_g), gain(ffn1_post_g), cast(ffn1_w_gate), cast(ffn1_w_up), cast(ffn1_w_down))
    ffn2 = (gain(ffn2_pre_g), gain(ffn2_post_g), cast(ffn2_w_gate), cast(ffn2_w_up), cast(ffn2_w_down))
    w_all = cast(jnp.pad(w_in, ((0, 0), (0, 0), (0, off - w_in.shape[2]))))
    w_o = cast(w_out)

    h = x
    for l in range(depth):
        h = _ffn(l, h, ffn1)
        xpl, qt, kc, vc, ks, vs_t, kw, vw_t, gates_t = _inproj(l, h, gain(mix_pre_g), w_all, cols, cos, sin)

        w_ri = jnp.concatenate([_block_diag(lru_w_r[l]), _block_diag(lru_w_i[l])], axis=1).astype(MXU_DTYPE)
        ypl = _poollru(xpl, _block_diag(pool_w[l]).astype(MXU_DTYPE), pool_scale[l],
                       conv_w[l], conv_b[l], w_ri, jnp.concatenate([lru_b_r[l], lru_b_i[l]]), lru_lambda[l])

        pe_rows = jnp.tile(cmp_pe[l], (1, N_KV_HEADS))[:, None, :]
        k_cmp = _compress(kc, pe_rows, _compress_weights(cmp_w_k[l]), False)
        v_cmp_t = _compress(vc, pe_rows, _compress_weights(cmp_w_v[l]), True)

        oc_t, mb_t = _cmp_attention(qt, k_cmp, v_cmp_t, n_sel, n_pad)
        y_t = _sel_win_attention(qt, mb_t, ks, key_onehot, vs_t, kw, vw_t, oc_t, gates_t)

        h = _outproj_ffn(l, h, ypl, y_t, w_o, gain(mix_post_g), ffn2)
    return h
```

```python
import functools

import jax
import jax.numpy as jnp
from jax import lax
from jax.experimental import pallas as pl
from jax.experimental.pallas import tpu as pltpu

F32 = jnp.float32
MXU_DTYPE = jnp.bfloat16

POOL_WINDOWS = (2, 4, 8, 16)
POOL_GROUP = 64
LRU_C = 8.0
CONV_WIDTH = 4
HEAD_DIM = 64
N_KV_HEADS = 2
GQA_GROUP = 4
N_HEADS = N_KV_HEADS * GQA_GROUP
KV_WIDTH = N_KV_HEADS * HEAD_DIM
N_BRANCH = 3
CMP_LEN = 32
CMP_STRIDE = 16
SEL_BLOCK = 64
SEL_TOPK = 16
WINDOW = 512
ROPE_THETA = 10000.0
NORM_EPS = 1e-6
NEG_INF = -1e30
BIG_SCORE = 1e9
GELU_C0 = 0.7978845608028654
GELU_C1 = 0.044715
LOG2_E = 1.4426950408889634

V7X_LANES = 128
V7X_SUBLANES = 8
V7X_VMEM_LIMIT_BYTES = 56 * 1024 * 1024

ROW_TILE = 512
SEQ_TILE = 512
CMP_Q_TILE = 512
CMP_CHUNK = 128
SEL_TILE = 512
SEL_GROUP = 128
ONES_ROWS = 16
M_INIT = -1e20
STREAM_TILES = 4
EXP_HEADROOM = 64.0
Q_SCALE = HEAD_DIM ** -0.5 * LOG2_E


def _params(semantics):
    return pltpu.CompilerParams(dimension_semantics=semantics, vmem_limit_bytes=V7X_VMEM_LIMIT_BYTES)


def _rms(x, g):
    return x * lax.rsqrt(jnp.mean(x * x, axis=-1, keepdims=True) + NORM_EPS) * g


def _sigmoid(x):
    return 1.0 / (1.0 + jnp.exp(-x))


def _mm(a, b):
    return jnp.dot(a, b, preferred_element_type=F32)


def _mm_tn(a_t, b):
    return lax.dot_general(a_t, b, (((0,), (0,)), ((), ())), preferred_element_type=F32)


def _log2(n):
    assert n & (n - 1) == 0
    return n.bit_length() - 1


def _const_spec(shape):
    zeros = (0,) * len(shape)
    return pl.BlockSpec(shape, lambda *_: zeros, pipeline_mode=pl.Buffered(1))


def _layer_spec(stacked, layer):
    index = (layer,) + (0,) * (stacked.ndim - 1)
    return pl.BlockSpec((None,) + stacked.shape[1:], lambda *_: index, pipeline_mode=pl.Buffered(1))


def _rope_tables(pos, inv_row):
    ang = pos.astype(F32) * inv_row
    lane = lax.broadcasted_iota(jnp.int32, ang.shape, 1)
    sin = jnp.sin(ang)
    return jnp.cos(ang), jnp.where((lane & (HEAD_DIM - 1)) < HEAD_DIM // 2, -sin, sin)


def _ffn_math(h, ffn_refs):
    gpre_ref, gpost_ref, wg_ref, wu_ref, wd_ref = ffn_refs
    xn = _rms(h, gpre_ref[...]).astype(MXU_DTYPE)
    gate = _mm(xn, wg_ref[...])
    up = _mm(xn, wu_ref[...])
    act = (gate * _sigmoid(gate) * up).astype(MXU_DTYPE)
    f = _mm(act, wd_ref[...])
    return h + 0.5 * _rms(f, gpost_ref[...])


def _ffn_operands(layer, g_pre, g_post, w_gate, w_up, w_down):
    args = (g_pre, g_post, w_gate, w_up, w_down)
    return [_layer_spec(a, layer) for a in args], args


N_FFN_OPERANDS = 5


def _ffn_body(h_ref, *refs):
    refs[N_FFN_OPERANDS][0] = _ffn_math(h_ref[0], refs[:N_FFN_OPERANDS])


def _ffn(layer, h3, ffn_weights):
    b, s, d = h3.shape
    tm = min(ROW_TILE, s)
    row = pl.BlockSpec((1, tm, d), lambda i, j: (i, j, 0))
    ffn_specs, ffn_args = _ffn_operands(layer, *ffn_weights)
    return pl.pallas_call(
        _ffn_body,
        grid=(b, s // tm),
        in_specs=[row] + ffn_specs,
        out_specs=row,
        out_shape=jax.ShapeDtypeStruct((b, s, d), F32),
        compiler_params=_params(("parallel", "parallel")),
        name="ffn",
    )(h3, *ffn_args)


def _swap_halves(x):
    n = x.shape[1]
    lane = lax.broadcasted_iota(jnp.int32, x.shape, 1)
    first_half = (lane & (HEAD_DIM - 1)) < HEAD_DIM // 2
    return jnp.where(first_half, pltpu.roll(x, n - HEAD_DIM // 2, 1), pltpu.roll(x, HEAD_DIM // 2, 1))


def _inproj_body(cols, h_ref, g_ref, w_ref, pos_ref, inv_ref,
                 xpl_ref, qt_ref, kc_ref, vc_ref, ks_ref, vst_ref, kw_ref, vwt_ref, gatet_ref):
    xn = _rms(h_ref[0], g_ref[...]).astype(MXU_DTYPE)
    proj = _mm(xn, w_ref[...])
    cos, sin = _rope_tables(pos_ref[0], inv_ref[...])

    def rope(x):
        rep = x.shape[1] // V7X_LANES
        c = jnp.concatenate([cos] * rep, axis=1) if rep > 1 else cos
        s = jnp.concatenate([sin] * rep, axis=1) if rep > 1 else sin
        return x * c + _swap_halves(x) * s

    def seg(name):
        lo, hi = cols[name]
        return proj[:, lo:hi]

    xpl_ref[0] = seg("xpl")
    qt_ref[0] = (rope(seg("q")) * Q_SCALE).T.astype(qt_ref.dtype)
    kc_ref[0] = rope(seg("kc"))
    vc_ref[0] = seg("vc")
    ks_ref[0] = rope(seg("ks")).astype(ks_ref.dtype)
    vst_ref[0] = seg("vs").T.astype(vst_ref.dtype)
    kw_ref[0] = rope(seg("kw")).astype(kw_ref.dtype)
    vwt_ref[0] = seg("vw").T.astype(vwt_ref.dtype)
    gatet_ref[0] = _sigmoid(seg("g")).T[:gatet_ref.shape[1]]


def _inproj(layer, h3, g_pre, w_all, cols, positions, inv_row):
    b, s, d = h3.shape
    tm = min(ROW_TILE, s)
    width = lambda n: cols[n][1] - cols[n][0]
    row = lambda w: pl.BlockSpec((1, tm, w), lambda i, j: (i, j, 0))
    col = lambda w: pl.BlockSpec((1, w, tm), lambda i, j: (i, 0, j))
    tok = lambda n, dt: (jax.ShapeDtypeStruct((b, s, width(n)), dt), row(width(n)))
    chan = lambda w, dt: (jax.ShapeDtypeStruct((b, w, s), dt), col(w))
    outs = [tok("xpl", F32), chan(width("q"), MXU_DTYPE), tok("kc", F32), tok("vc", F32),
            tok("ks", MXU_DTYPE), chan(width("vs"), MXU_DTYPE), tok("kw", MXU_DTYPE),
            chan(width("vw"), MXU_DTYPE), chan(N_BRANCH * N_HEADS, F32)]
    return pl.pallas_call(
        functools.partial(_inproj_body, cols),
        grid=(b, s // tm),
        in_specs=[row(d), _layer_spec(g_pre, layer), _layer_spec(w_all, layer), row(1), _const_spec(inv_row.shape)],
        out_specs=[o[1] for o in outs],
        out_shape=[o[0] for o in outs],
        compiler_params=_params(("parallel", "parallel")),
        name="mixer_in_proj",
    )(h3, g_pre, w_all, positions[:, :, None], inv_row)


def _poollru_body(x_ref, pw_ref, pscale_ref, cw_ref, cb_ref, wri_ref, bri_ref, lam_ref,
                  y_ref, pool_carry, conv_carry, h_carry):
    si = pl.program_id(1)
    ts = x_ref.shape[1]
    width = pw_ref.shape[0]
    halo_p = pool_carry.shape[0]
    halo_c = conv_carry.shape[0]

    @pl.when(si == 0)
    def _():
        pool_carry[...] = jnp.zeros_like(pool_carry)
        conv_carry[...] = jnp.zeros_like(conv_carry)
        h_carry[...] = jnp.zeros_like(h_carry)

    x = x_ref[0]
    xp = x[:, :width]
    xl = x[:, width:2 * width]
    gl = x[:, 2 * width:]

    ext = jnp.concatenate([pool_carry[...], xp], axis=0)
    sums = [ext]
    shift = 1
    for _ in POOL_WINDOWS:
        sums.append(sums[-1] + pltpu.roll(sums[-1], shift, 0))
        shift *= 2
    lane = lax.broadcasted_iota(jnp.int32, (1, width), 1)
    grp = lax.shift_right_logical(lane, _log2(POOL_GROUP))
    win_sum = sums[len(POOL_WINDOWS)]
    win = jnp.full((1, width), float(POOL_WINDOWS[-1]), F32)
    for gi in range(len(POOL_WINDOWS) - 2, -1, -1):
        win_sum = jnp.where(grp == gi, sums[gi + 1], win_sum)
        win = jnp.where(grp == gi, float(POOL_WINDOWS[gi]), win)
    win_sum = win_sum[halo_p:]
    t_abs = si * ts + lax.broadcasted_iota(jnp.int32, (ts, 1), 0)
    cnt = jnp.minimum((t_abs + 1).astype(F32), win)
    pooled = win_sum / cnt
    y_pool = _mm((pooled - xp).astype(MXU_DTYPE), pw_ref[...]) * pscale_ref[...]
    pool_carry[...] = xp[ts - halo_p:]

    extc = jnp.concatenate([conv_carry[...], xl], axis=0)
    xc = extc * cw_ref[CONV_WIDTH - 1:CONV_WIDTH, :]
    for k in range(1, CONV_WIDTH):
        xc = xc + pltpu.roll(extc, k, 0) * cw_ref[CONV_WIDTH - 1 - k:CONV_WIDTH - k, :]
    xc = xc[halo_c:] + cb_ref[...]
    conv_carry[...] = xl[ts - halo_c:]

    ri = _mm(xc.astype(MXU_DTYPE), wri_ref[...]) + bri_ref[...]
    r = _sigmoid(ri[:, :width])
    i_gate = _sigmoid(ri[:, width:])
    neg_lam = -lam_ref[...]
    softplus = jnp.maximum(neg_lam, 0.0) + jnp.log1p(jnp.exp(-jnp.abs(neg_lam)))
    log_a = -LRU_C * r * softplus
    a = jnp.exp(log_a)
    one_minus_a2 = -jnp.tanh(log_a) * (a * a + 1.0)
    mult = jnp.where(one_minus_a2 > 0.0, one_minus_a2 * lax.rsqrt(one_minus_a2), 0.0)
    b = mult * (i_gate * xc)

    row_in_group = lax.broadcasted_iota(jnp.int32, (ts, 1), 0) & (V7X_SUBLANES - 1)
    k = 1
    while k < V7X_SUBLANES:
        keep = row_in_group >= k
        a_prev = jnp.where(keep, pltpu.roll(a, k, 0), 1.0)
        b_prev = jnp.where(keep, pltpu.roll(b, k, 0), 0.0)
        b = a * b_prev + b
        a = a * a_prev
        k *= 2
    h_prev = h_carry[0:1, :]
    groups = []
    for g in range(ts // V7X_SUBLANES):
        rows_g = slice(g * V7X_SUBLANES, (g + 1) * V7X_SUBLANES)
        h_g = a[rows_g] * h_prev + b[rows_g]
        groups.append(h_g)
        h_prev = h_g[V7X_SUBLANES - 1:V7X_SUBLANES]
    h = jnp.concatenate(groups, axis=0)
    h_carry[...] = jnp.broadcast_to(h_prev, h_carry.shape)

    gelu = 0.5 * gl * (1.0 + jnp.tanh(GELU_C0 * (gl + GELU_C1 * gl * gl * gl)))
    y_ref[0] = jnp.concatenate([y_pool, h * gelu], axis=1).astype(y_ref.dtype)


def _poollru(xpl, pool_w_bd, pool_scale, conv_w, conv_b, w_ri_bd, b_ri, lam):
    b, s, w3 = xpl.shape
    width = w3 // 3
    ts = min(SEQ_TILE, s)
    return pl.pallas_call(
        _poollru_body,
        grid=(b, s // ts),
        in_specs=[pl.BlockSpec((1, ts, w3), lambda i, j: (i, j, 0)),
                  _const_spec((width, width)), _const_spec((1, width)),
                  _const_spec((CONV_WIDTH, width)), _const_spec((1, width)),
                  _const_spec((width, 2 * width)), _const_spec((1, 2 * width)), _const_spec((1, width))],
        out_specs=pl.BlockSpec((1, ts, 2 * width), lambda i, j: (i, j, 0)),
        out_shape=jax.ShapeDtypeStruct((b, s, 2 * width), MXU_DTYPE),
        scratch_shapes=[pltpu.VMEM((POOL_WINDOWS[-1], width), F32),
                        pltpu.VMEM((V7X_SUBLANES, width), F32),
                        pltpu.VMEM((V7X_SUBLANES, width), F32)],
        compiler_params=_params(("parallel", "arbitrary")),
        name="pool_rglru",
    )(xpl, pool_w_bd, pool_scale[None, :], conv_w, conv_b[None, :], w_ri_bd, b_ri[None, :], lam[None, :])


def _compress_body(transpose_out, x_ref, pe_ref, w_ref, o_ref):
    n1 = x_ref.shape[1] // CMP_STRIDE
    first = jnp.zeros((n1, w_ref.shape[2]), F32)
    second = first
    for l in range(CMP_STRIDE):
        x = x_ref[0, pl.ds(l, n1, stride=CMP_STRIDE), :]
        first = first + _mm((x + pe_ref[l]).astype(MXU_DTYPE), w_ref[l])
        second = second + _mm((x + pe_ref[l + CMP_STRIDE]).astype(MXU_DTYPE), w_ref[l + CMP_STRIDE])
    out = first + pltpu.roll(second, n1 - 1, 0)
    o_ref[0] = (out.T if transpose_out else out).astype(o_ref.dtype)


def _compress(x, pe_rows, w_bd, transpose_out):
    b, s, width = x.shape
    n1 = s // CMP_STRIDE
    out_dims = (width, n1) if transpose_out else (n1, width)
    return pl.pallas_call(
        functools.partial(_compress_body, transpose_out),
        grid=(b,),
        in_specs=[pl.BlockSpec((1, s, width), lambda i: (i, 0, 0)),
                  _const_spec(pe_rows.shape), _const_spec(w_bd.shape)],
        out_specs=pl.BlockSpec((1,) + out_dims, lambda i: (i, 0, 0)),
        out_shape=jax.ShapeDtypeStruct((b,) + out_dims, MXU_DTYPE),
        compiler_params=_params(("parallel",)),
        name="compress_kv",
    )(x, pe_rows, w_bd)


def _stack_heads(q_ref):
    return jnp.concatenate([q_ref[0, g * HEAD_DIM:(g + 1) * HEAD_DIM, :] for g in range(GQA_GROUP)], axis=1)


def _place_in_kv_half(qt, kvh):
    zeros = jnp.zeros_like(qt)
    parts = [jnp.where(kvh == h, qt, zeros) for h in range(N_KV_HEADS)]
    return jnp.concatenate(parts, axis=0)


def _query_positions(start, tq):
    lane = lax.broadcasted_iota(jnp.int32, (1, GQA_GROUP * tq), 1)
    return start + (lane & (tq - 1))


def _with_ones(v_t):
    return jnp.concatenate([v_t, jnp.ones((ONES_ROWS, v_t.shape[1]), v_t.dtype)], axis=0)


def _denominator(acc):
    return acc[HEAD_DIM:HEAD_DIM + 1]


def _normalize(acc):
    return acc[:HEAD_DIM] * (1.0 / _denominator(acc))


def _cmp_body(n_sel, q_ref, kc_ref, vct_ref, oc_ref, mb_ref, band_ref):
    tq = q_ref.shape[2]
    kvh = pl.program_id(1)
    start = pl.program_id(2) * tq
    qt = _stack_heads(q_ref)
    qp = _place_in_kv_half(qt, kvh)
    n1 = kc_ref.shape[1]
    t_row = _query_positions(start, tq)
    assert CMP_STRIDE * CMP_CHUNK >= tq + CMP_LEN

    def attend(nrows):
        s = _mm(kc_ref[0, :nrows, :], qp)
        lo = max(0, nrows - 2 * CMP_CHUNK)
        n_idx = lo + lax.broadcasted_iota(jnp.int32, (nrows - lo, 1), 0)
        tail = jnp.where(n_idx * CMP_STRIDE + (CMP_LEN - 1) <= t_row, s[lo:], NEG_INF)
        s = jnp.concatenate([s[:lo], tail], axis=0) if lo else tail
        m = jnp.max(s, axis=0, keepdims=True)
        p = jnp.exp2(s - jnp.where(m > 0.5 * NEG_INF, m, 0.0))
        den = jnp.sum(p, axis=0, keepdims=True)
        pn = p * (1.0 / jnp.where(den > 0.0, den, 1.0))
        oc = _mm(vct_ref[0, :, :nrows], pn.astype(MXU_DTYPE))
        psum = pn[:, 0:tq]
        for g in range(GQA_GROUP):
            oc_ref[0, g * HEAD_DIM:(g + 1) * HEAD_DIM, :] = oc[:, g * tq:(g + 1) * tq]
            if g:
                psum = psum + pn[:, g * tq:(g + 1) * tq]
        n_blk = min(n_sel, nrows * CMP_STRIDE // SEL_BLOCK)
        step, lead, pad = SEL_BLOCK // CMP_STRIDE, CMP_LEN // CMP_STRIDE - 1, V7X_SUBLANES
        cols = []
        for c in range(tq // V7X_LANES):
            band_ref[c, 0:pad, :] = jnp.zeros((pad, V7X_LANES), F32)
            band_ref[c, pad:pad + nrows, :] = psum[:, c * V7X_LANES:(c + 1) * V7X_LANES]
            taps = [band_ref[c, pl.ds(pad - lead + k, n_blk, stride=step), :] for k in range(step + lead)]
            cols.append(functools.reduce(lambda a, b: a + b, taps))
        select(jnp.concatenate(cols, axis=1), n_blk)

    def select(imp, n_blk):
        j = lax.broadcasted_iota(jnp.int32, (n_blk, 1), 0).astype(F32)
        t_q = start + lax.broadcasted_iota(jnp.int32, (1, tq), 1)
        cur = lax.shift_right_logical(t_q, _log2(SEL_BLOCK)).astype(F32)
        forced = (j == 0.0) | (j == cur) | (j == cur - 1.0)
        valid = j <= cur
        base = jnp.where(valid, jnp.where(forced, BIG_SCORE, imp), -BIG_SCORE)
        n_pad = mb_ref.shape[2]
        if n_pad > n_blk:
            mb_ref[0, 0, n_blk:, :] = jnp.full((n_pad - n_blk, tq), NEG_INF, mb_ref.dtype)

        score = jnp.where(forced, -jnp.inf, base)
        for _ in range(SEL_TOPK - 3):
            score = jnp.where(score == jnp.max(score, axis=0, keepdims=True), -jnp.inf, score)
        bias = jnp.where(score == -jnp.inf, 0.0, NEG_INF)
        mb_ref[0, 0, :n_blk, :] = bias.astype(mb_ref.dtype)
        taken = jnp.sum(jnp.where(valid, jnp.where(bias == 0.0, 1.0, 0.0), 0.0), axis=0, keepdims=True)
        tie = jnp.max(jnp.abs(taken - jnp.minimum(cur + 1.0, float(SEL_TOPK)))) > 0.0

        @pl.when(tie)
        def _():
            score = base
            bias = jnp.full(score.shape, NEG_INF, F32)
            for _ in range(SEL_TOPK):
                best = jnp.max(score, axis=0, keepdims=True)
                first = jnp.min(jnp.where(score == best, j, float(n_blk)), axis=0, keepdims=True)
                pick = j == first
                bias = jnp.where(pick, 0.0, bias)
                score = jnp.where(pick, -jnp.inf, score)
            mb_ref[0, 0, :n_blk, :] = bias.astype(mb_ref.dtype)

    assert n_sel >= SEL_TOPK and tq % SEL_BLOCK == 0
    n_need = lax.shift_right_logical(start + tq, _log2(CMP_STRIDE)) - 1
    chunks = lax.shift_right_logical(n_need + CMP_CHUNK - 1, _log2(CMP_CHUNK))
    n_variants = -(-n1 // CMP_CHUNK)
    for c in range(1, n_variants + 1):
        pl.when(chunks == c)(functools.partial(attend, min(c * CMP_CHUNK, n1)))


def _cmp_attention(qt, k_cmp, v_cmp_t, n_sel, n_pad):
    b, _, s = qt.shape
    n1 = k_cmp.shape[1]
    tq = min(CMP_Q_TILE, s)
    group_rows = GQA_GROUP * HEAD_DIM
    head_blk = pl.BlockSpec((1, group_rows, tq), lambda i, h, j: (i, h, j))
    return pl.pallas_call(
        functools.partial(_cmp_body, n_sel),
        grid=(b, N_KV_HEADS, s // tq),
        in_specs=[head_blk,
                  pl.BlockSpec((1, n1, KV_WIDTH), lambda i, h, j: (i, 0, 0)),
                  pl.BlockSpec((1, HEAD_DIM, n1), lambda i, h, j: (i, h, 0))],
        out_specs=[head_blk, pl.BlockSpec((1, 1, n_pad, tq), lambda i, h, j: (i, h, 0, j))],
        out_shape=[jax.ShapeDtypeStruct((b, N_HEADS * HEAD_DIM, s), F32),
                   jax.ShapeDtypeStruct((b, N_KV_HEADS, n_pad, s), MXU_DTYPE)],
        scratch_shapes=[pltpu.VMEM((tq // V7X_LANES, n1 + V7X_SUBLANES, V7X_LANES), F32)],
        compiler_params=_params(("parallel", "parallel", "parallel")),
        name="cmp_attention_select",
    )(qt, k_cmp, v_cmp_t)


def _selwin_body(q_ref, mb_ref, ks_ref, oh_ref, vst_ref, kw_ref, vwt_ref, oc_ref, gate_ref, y_ref,
                 qa_ref, acc_ref, win_ref):
    tq = q_ref.shape[2]
    tk = tq
    kvh = pl.program_id(1)
    qi = pl.program_id(2)
    start = qi * tq
    rows = GQA_GROUP * tq
    qt = _stack_heads(q_ref)
    qp = _place_in_kv_half(qt, kvh)
    t_row = _query_positions(start, tq)
    group_shift = _log2(SEL_GROUP * SEL_BLOCK // tk)
    n_groups = mb_ref.shape[2] // SEL_GROUP

    for grp in range(n_groups):
        bias_rows = mb_ref[0, 0, grp * SEL_GROUP:(grp + 1) * SEL_GROUP, :]
        qa_ref[grp] = jnp.concatenate([jnp.concatenate([bias_rows] * GQA_GROUP, axis=1), qp], axis=0)

    def scores(kt, n_sub, causal):
        key_rows = pl.ds(pl.multiple_of(kt * tk, tk), n_sub * tk)
        oh_rows = pl.ds(pl.multiple_of(lax.rem(kt * tk, oh_ref.shape[0]), tk), n_sub * tk)
        k = jnp.concatenate([oh_ref[oh_rows, :], ks_ref[0, key_rows, :]], axis=1)
        s = _mm(k, qa_ref[lax.shift_right_logical(kt, group_shift)])
        if causal:
            kp = kt * tk + lax.broadcasted_iota(jnp.int32, (n_sub * tk, 1), 0)
            s = jnp.where(kp <= t_row, s, NEG_INF)
        return s

    def values(kt, n_sub):
        return vst_ref[0, :, pl.ds(pl.multiple_of(kt * tk, tk), n_sub * tk)]

    def col_max(s):
        return jnp.max(s, axis=0, keepdims=True)

    n_past = qi

    assert tq == WINDOW
    own_rows = pl.ds(pl.multiple_of(start, tq), tq)
    first_tile = qi == 0
    win_rows = pl.ds(pl.multiple_of(jnp.maximum(start - WINDOW, 0), tq), WINDOW + tq)

    def window_scores():
        s = _mm(kw_ref[0, win_rows, :], qp)
        kp = jnp.maximum(start - WINDOW, 0) + lax.broadcasted_iota(jnp.int32, (WINDOW + tq, 1), 0)
        causal_lhs, causal_rhs = -kp, -(t_row + 1)
        lhs = jnp.where(first_tile, causal_lhs[:WINDOW], kp[:WINDOW])
        rhs = jnp.where(first_tile, causal_rhs, t_row - WINDOW)
        return jnp.concatenate([jnp.where(lhs > rhs, s[:WINDOW], NEG_INF),
                                jnp.where(causal_lhs[WINDOW:] > causal_rhs, s[WINDOW:], NEG_INF)], axis=0)

    def pv(v_t, s, ref):
        return _mm(_with_ones(v_t), jnp.exp2(s - ref).astype(MXU_DTYPE))

    def self_scores(k_ref):
        k_t = k_ref[0, own_rows, :].astype(F32).T
        k_own = k_t[:HEAD_DIM]
        for h in range(1, N_KV_HEADS):
            k_own = jnp.where(kvh == h, k_t[h * HEAD_DIM:(h + 1) * HEAD_DIM], k_own)
        parts = [jnp.sum(qt[:, g * tq:(g + 1) * tq].astype(F32) * k_own, axis=0, keepdims=True)
                 for g in range(GQA_GROUP)]
        return jnp.concatenate(parts, axis=1)

    ref = self_scores(ks_ref)
    ref_w = self_scores(kw_ref)
    win_ref[...] = pv(vwt_ref[0, :, win_rows], window_scores(), ref_w)

    def stream(kt, n_sub, causal=False):
        acc_ref[...] += pv(values(kt, n_sub), scores(kt, n_sub, causal), ref)

    acc_ref[...] = jnp.zeros_like(acc_ref)
    n_big = lax.shift_right_logical(n_past, _log2(STREAM_TILES))
    lax.fori_loop(0, n_big, lambda i, c: (stream(i * STREAM_TILES, STREAM_TILES), c)[1], 0)

    def tail(n_sub):
        kt = n_big * STREAM_TILES
        s = scores(kt, n_sub, False)
        kp = qi * tk + lax.broadcasted_iota(jnp.int32, (tk, 1), 0)
        diag = jnp.where(kp <= t_row, s[(n_sub - 1) * tk:], NEG_INF)
        s = jnp.concatenate([s[:(n_sub - 1) * tk], diag], axis=0) if n_sub > 1 else diag
        acc_ref[...] += pv(values(kt, n_sub), s, ref)

    for rem in range(STREAM_TILES):
        pl.when((n_past & (STREAM_TILES - 1)) == rem)(functools.partial(tail, rem + 1))
    limit = 2.0 ** EXP_HEADROOM
    in_range = jnp.where(_denominator(acc_ref) < limit, jnp.where(_denominator(win_ref) < limit, 1.0, 0.0), 0.0)
    overflow = jnp.min(in_range) < 1.0

    @pl.when(overflow)
    def _():
        def tile(kt, causal):
            s = scores(kt, 1, causal)
            m_t = col_max(s)
            return m_t, pv(values(kt, 1), s, m_t)

        def merge(carry, part):
            m, acc = carry
            m_t, o_t = part
            m_new = jnp.maximum(m, m_t)
            return m_new, acc * jnp.exp2(m - m_new) + o_t * jnp.exp2(m_t - m_new)

        carry = (jnp.full((1, rows), M_INIT, F32), jnp.zeros(acc_ref.shape, F32))
        carry = lax.fori_loop(0, n_past, lambda i, c: merge(c, tile(i, False)), carry)
        acc_ref[...] = merge(carry, tile(qi, True))[1]

        x_win = window_scores()
        win_ref[...] = pv(vwt_ref[0, :, win_rows], x_win, col_max(x_win))

    o_sel = _normalize(acc_ref[...])
    o_win = _normalize(win_ref[...])

    gate_row = kvh * (GQA_GROUP * N_BRANCH)
    for g in range(GQA_GROUP):
        cols = slice(g * tq, (g + 1) * tq)
        head = slice(g * HEAD_DIM, (g + 1) * HEAD_DIM)
        gates = [gate_ref[0, pl.ds(gate_row + N_BRANCH * g + br, 1), :] for br in range(N_BRANCH)]
        y = gates[0] * oc_ref[0, head, :] + gates[1] * o_sel[:, cols] + gates[2] * o_win[:, cols]
        y_ref[0, head, :] = y.astype(y_ref.dtype)


def _sel_win_attention(qt, mb_t, ks, key_onehot, vs_t, kw, vw_t, oc_t, gates_t):
    b, _, s = qt.shape
    tq = min(SEL_TILE, s)
    n_pad = mb_t.shape[2]
    n_gate = gates_t.shape[1]
    group_rows = GQA_GROUP * HEAD_DIM
    head_blk = pl.BlockSpec((1, group_rows, tq), lambda i, h, j: (i, h, j))
    once = pl.Buffered(1)
    keys = pl.BlockSpec((1, s, KV_WIDTH), lambda i, h, j: (i, 0, 0), pipeline_mode=once)
    vals = pl.BlockSpec((1, HEAD_DIM, s), lambda i, h, j: (i, h, 0), pipeline_mode=once)
    acc = pltpu.VMEM((HEAD_DIM + ONES_ROWS, GQA_GROUP * tq), F32)
    return pl.pallas_call(
        _selwin_body,
        grid=(b, N_KV_HEADS, s // tq),
        in_specs=[head_blk,
                  pl.BlockSpec((1, 1, n_pad, tq), lambda i, h, j: (i, h, 0, j)),
                  keys, _const_spec(key_onehot.shape), vals, keys, vals,
                  head_blk,
                  pl.BlockSpec((1, n_gate, tq), lambda i, h, j: (i, 0, j))],
        out_specs=head_blk,
        out_shape=jax.ShapeDtypeStruct((b, N_HEADS * HEAD_DIM, s), MXU_DTYPE),
        scratch_shapes=[pltpu.VMEM((n_pad // SEL_GROUP, SEL_GROUP + KV_WIDTH, GQA_GROUP * tq), MXU_DTYPE),
                        acc, acc],
        compiler_params=_params(("parallel", "parallel", "arbitrary")),
        name="sel_win_attention",
    )(qt, mb_t, ks, key_onehot, vs_t, kw, vw_t, oc_t, gates_t)


def _outproj_ffn_body(h_ref, ypl_ref, yat_ref, w_ref, g_ref, *refs):
    ffn_refs, o_ref = refs[:N_FFN_OPERANDS], refs[N_FFN_OPERANDS]
    split = ypl_ref.shape[2]
    m = _mm(ypl_ref[0], w_ref[:split, :]) + _mm_tn(yat_ref[0], w_ref[split:, :])
    o_ref[0] = _ffn_math(h_ref[0] + _rms(m, g_ref[...]), ffn_refs)


def _outproj_ffn(layer, h3, ypl, yat_t, w_out, g_post, ffn_weights):
    b, s, d = h3.shape
    tm = min(ROW_TILE, s)
    row = lambda w: pl.BlockSpec((1, tm, w), lambda i, j: (i, j, 0))
    ffn_specs, ffn_args = _ffn_operands(layer, *ffn_weights)
    return pl.pallas_call(
        _outproj_ffn_body,
        grid=(b, s // tm),
        in_specs=[row(d), row(ypl.shape[2]),
                  pl.BlockSpec((1, yat_t.shape[1], tm), lambda i, j: (i, 0, j)),
                  _layer_spec(w_out, layer), _layer_spec(g_post, layer)] + ffn_specs,
        out_specs=row(d),
        out_shape=jax.ShapeDtypeStruct((b, s, d), F32),
        compiler_params=_params(("parallel", "parallel")),
        name="mixer_out_proj_ffn",
    )(h3, ypl, yat_t, w_out, g_post, *ffn_args)


def _block_diag(blocks):
    n, a, b = blocks.shape
    eye = jnp.eye(n, dtype=blocks.dtype)
    return jnp.einsum("nab,nm->namb", blocks, eye).reshape(n * a, n * b)


def _compress_weights(w):
    eye = jnp.eye(N_KV_HEADS, dtype=w.dtype)
    return jnp.einsum("lde,hg->lhdge", w, eye).reshape(CMP_LEN, KV_WIDTH, KV_WIDTH).astype(MXU_DTYPE)


def kernel(x, positions, ffn1_pre_g, ffn1_post_g, ffn1_w_gate, ffn1_w_up, ffn1_w_down, mix_pre_g, mix_post_g,
           w_in, w_out, pool_w, pool_scale, conv_w, conv_b, lru_w_r, lru_b_r, lru_w_i, lru_b_i, lru_lambda,
           cmp_w_k, cmp_w_v, cmp_pe, ffn2_pre_g, ffn2_post_g, ffn2_w_gate, ffn2_w_up, ffn2_w_down):
    b, s, d = x.shape
    depth = w_in.shape[0]
    pool_width = pool_w.shape[1] * pool_w.shape[2]
    lru_width = lru_w_r.shape[1] * lru_w_r.shape[2]
    attn_width = N_HEADS * HEAD_DIM
    assert pool_width == lru_width and s % SEL_TILE == 0 and s % (CMP_STRIDE * 8) == 0
    assert WINDOW % SEL_TILE == 0 and (SEL_GROUP * SEL_BLOCK) % (STREAM_TILES * SEL_TILE) == 0
    assert s >= WINDOW + SEL_TILE

    sizes = [("xpl", pool_width + 2 * lru_width), ("q", attn_width), ("kc", KV_WIDTH), ("vc", KV_WIDTH),
             ("ks", KV_WIDTH), ("vs", KV_WIDTH), ("kw", KV_WIDTH), ("vw", KV_WIDTH), ("g", V7X_LANES)]
    cols, off = {}, 0
    for name, width in sizes:
        cols[name] = (off, off + width)
        off += width

    n_sel = s // SEL_BLOCK
    n_pad = -(-n_sel // SEL_GROUP) * SEL_GROUP
    n1 = s // CMP_STRIDE
    key_blk = jnp.arange(min(s, SEL_GROUP * SEL_BLOCK)) // SEL_BLOCK
    key_onehot = (key_blk[:, None] == jnp.arange(SEL_GROUP)[None, :]).astype(MXU_DTYPE)

    inv = ROPE_THETA ** (-jnp.arange(0, HEAD_DIM, 2, dtype=F32) / HEAD_DIM)
    inv_row = jnp.tile(inv, V7X_LANES // (HEAD_DIM // 2))[None, :]

    cast = lambda w: w.astype(MXU_DTYPE)
    gain = lambda g: g[:, None, :]
    ffn1 = (gain(ffn1_pre_g), gain(ffn1_post_g), cast(ffn1_w_gate), cast(ffn1_w_up), cast(ffn1_w_down))
    ffn2 = (gain(ffn2_pre_g), gain(ffn2_post_g), cast(ffn2_w_gate), cast(ffn2_w_up), cast(ffn2_w_down))
    w_all = cast(jnp.pad(w_in, ((0, 0), (0, 0), (0, off - w_in.shape[2]))))
    w_o = cast(w_out)

    h = x
    for l in range(depth):
        h = _ffn(l, h, ffn1)
        xpl, qt, kc, vc, ks, vs_t, kw, vw_t, gates_t = _inproj(l, h, gain(mix_pre_g), w_all, cols, positions, inv_row)

        w_ri = jnp.concatenate([_block_diag(lru_w_r[l]), _block_diag(lru_w_i[l])], axis=1).astype(MXU_DTYPE)
        ypl = _poollru(xpl, _block_diag(pool_w[l]).astype(MXU_DTYPE), pool_scale[l],
                       conv_w[l], conv_b[l], w_ri, jnp.concatenate([lru_b_r[l], lru_b_i[l]]), lru_lambda[l])

        pe_rows = jnp.tile(cmp_pe[l], (1, N_KV_HEADS))[:, None, :]
        k_cmp = _compress(kc, pe_rows, _compress_weights(cmp_w_k[l]), False)
        v_cmp_t = _compress(vc, pe_rows, _compress_weights(cmp_w_v[l]), True)

        oc_t, mb_t = _cmp_attention(qt, k_cmp, v_cmp_t, n_sel, n_pad)
        y_t = _sel_win_attention(qt, mb_t, ks, key_onehot, vs_t, kw, vw_t, oc_t, gates_t)

        h = _outproj_ffn(l, h, ypl, y_t, w_o, gain(mix_post_g), ffn2)
    return h
```
